```python
import jax
import jax.numpy as jnp
from jax import lax
import numpy as np

D_MODEL = 2048
BATCH = 4
SEQ = 2048
DEPTH = 4

HEAD_DIM = 128
ROPE_THETA = 10000.0
Q_BLOCK = 128
NEG_INF = -1e30
FOX_HEADS = 6
NSA_HEADS = 4
CMP_LEN = 32
CMP_STRIDE = 16
CMP_HIDDEN = 256
SLC_LEN = 64
SLC_TOPN = 16
WIN = 512
SLC_FORCE = 1e4
DSA_HEADS = 6
DSA_KV_RANK = 256
IDX_HEADS = 16
IDX_DIM = 64
IDX_TOPK_MAX = 256
D_FF = ((8 * D_MODEL + 3 * 256 - 1) // (3 * 256)) * 256
PLE_DIM = 256
ALPHA = (2 * DEPTH) ** 0.25
BETA = (8 * DEPTH) ** -0.25

SPLITS = (
    ('fox_q', FOX_HEADS * HEAD_DIM), ('fox_k', FOX_HEADS * HEAD_DIM), ('fox_v', FOX_HEADS * HEAD_DIM), ('fox_f', FOX_HEADS),
    ('nsa_q', NSA_HEADS * HEAD_DIM), ('nsa_kc', HEAD_DIM), ('nsa_vc', HEAD_DIM), ('nsa_ks', HEAD_DIM), ('nsa_vs', HEAD_DIM),
    ('nsa_kw', HEAD_DIM), ('nsa_vw', HEAD_DIM), ('nsa_g', 3 * NSA_HEADS),
    ('dsa_q', DSA_HEADS * HEAD_DIM), ('dsa_ckv', DSA_KV_RANK), ('idx_q', IDX_HEADS * IDX_DIM), ('idx_k', IDX_DIM), ('idx_w', IDX_HEADS),
    ('gate', 3 * D_MODEL),
)
IN_COLS = sum(w for _, w in SPLITS)

kernel_name = 'hybrid_fox_nsa_dsa_deepnorm'


def split_cols(z):
    offs = [int(o) for o in np.cumsum([w for _, w in SPLITS])[:-1]]
    parts = jnp.split(z, offs, axis=-1)
    return {name: part for (name, _), part in zip(SPLITS, parts)}


def heads(t, n):
    return t.reshape(t.shape[0], t.shape[1], n, -1)


def layer_norm(x, g, b, eps=1e-5):
    xf = x.astype(jnp.float32)
    mu = jnp.mean(xf, axis=-1, keepdims=True)
    var = jnp.mean(jnp.square(xf - mu), axis=-1, keepdims=True)
    return ((xf - mu) * lax.rsqrt(var + eps)).astype(x.dtype) * g + b


def rms_norm(x, g, eps=1e-6):
    xf = x.astype(jnp.float32)
    return (xf * lax.rsqrt(jnp.mean(jnp.square(xf), axis=-1, keepdims=True) + eps)).astype(x.dtype) * g


def rope_tables(n, dim):
    inv = 1.0 / (ROPE_THETA ** (jnp.arange(0, dim, 2, dtype=jnp.float32) / dim))
    ang = jnp.arange(n, dtype=jnp.float32)[:, None] * inv[None, :]
    return jnp.cos(ang), jnp.sin(ang)


def apply_rope(x, cos, sin):
    x1, x2 = jnp.split(x, 2, axis=-1)
    c = cos[:, None, :].astype(x.dtype)
    s = sin[:, None, :].astype(x.dtype)
    return jnp.concatenate([x1 * c - x2 * s, x1 * s + x2 * c], axis=-1)


def rope1(t, cos, sin):
    return apply_rope(t[:, :, None], cos, sin)[:, :, 0]


def to_blocks(a):
    B, S = a.shape[:2]
    return jnp.moveaxis(a.reshape((B, S // Q_BLOCK, Q_BLOCK) + a.shape[2:]), 1, 0)


def from_blocks(a):
    nb, B, Q = a.shape[:3]
    return jnp.moveaxis(a, 0, 1).reshape((B, nb * Q) + a.shape[3:])


def fox_attention(q, k, v, f_logit, f_bias):
    B, S, H, dh = q.shape
    pos = jnp.arange(S)
    scale = dh ** -0.5
    logf = jax.nn.log_sigmoid(f_logit.astype(jnp.float32) + f_bias.astype(jnp.float32))
    cum = jnp.cumsum(logf, axis=1)
    cum_k = jnp.transpose(cum, (0, 2, 1))

    def block(args):
        qb, cq, qpos = args
        s = jnp.einsum('bqhd,bkhd->bhqk', qb, k).astype(jnp.float32) * scale
        s = s + jnp.transpose(cq, (0, 2, 1))[..., None] - cum_k[:, :, None, :]
        s = jnp.where(pos[None, None, None, :] <= qpos[None, None, :, None], s, NEG_INF)
        pr = jax.nn.softmax(s, axis=-1).astype(v.dtype)
        return jnp.einsum('bhqk,bkhd->bqhd', pr, v)

    out = lax.map(block, (to_blocks(q), to_blocks(cum), pos.reshape(-1, Q_BLOCK)))
    return from_blocks(out)


def nsa_attention(q, kc_raw, vc_raw, ks, vs, kw, vw, g_logit, pe_k, pe_v, wk1, wk2, wv1, wv2, cos, sin):
    B, S, H, dh = q.shape
    pos = jnp.arange(S)
    scale = dh ** -0.5
    n_cmp = (S - CMP_LEN) // CMP_STRIDE + 1
    c_start = jnp.arange(n_cmp) * CMP_STRIDE
    c_end = c_start + CMP_LEN - 1
    tok = c_start[:, None] + jnp.arange(CMP_LEN)[None, :]

    def compress(raw, pe, w1, w2):
        blk = (raw[:, tok] + pe).reshape(B, n_cmp, CMP_LEN * dh)
        return jax.nn.gelu(blk @ w1) @ w2

    k_cmp = rope1(compress(kc_raw, pe_k, wk1, wk2), cos[c_end], sin[c_end])
    v_cmp = compress(vc_raw, pe_v, wv1, wv2)
    cmask = c_end[None, :] <= pos[:, None]
    s_c = jnp.einsum('bshd,bcd->bhsc', q, k_cmp).astype(jnp.float32) * scale
    p_c = jnp.where(cmask, jax.nn.softmax(jnp.where(cmask, s_c, NEG_INF), axis=-1), 0.0)
    o_cmp = jnp.einsum('bhsc,bcd->bshd', p_c.astype(v_cmp.dtype), v_cmp)
    n_slc = S // SLC_LEN
    s_start = jnp.arange(n_slc) * SLC_LEN
    overlap = jnp.maximum(jnp.minimum(c_end[:, None], s_start[None, :] + SLC_LEN - 1)
                          - jnp.maximum(c_start[:, None], s_start[None, :]) + 1, 0).astype(jnp.float32) / CMP_LEN
    imp = jnp.einsum('bhsc,cj->bsj', p_c, overlap)
    blk_id = jnp.arange(n_slc)[None, :]
    cur = (pos // SLC_LEN)[:, None]
    forced = (blk_id == 0) | (blk_id == cur) | (blk_id == cur - 1)
    imp = jnp.where(forced, SLC_FORCE, jnp.where(s_start[None, :] <= pos[:, None], imp, -SLC_FORCE))
    n_sel = min(SLC_TOPN, n_slc)
    _, sel = lax.top_k(imp, n_sel)
    ks_blk = ks.reshape(B, n_slc, SLC_LEN, dh)
    vs_blk = vs.reshape(B, n_slc, SLC_LEN, dh)
    kw_pad = jnp.pad(kw, ((0, 0), (WIN, 0), (0, 0)))
    vw_pad = jnp.pad(vw, ((0, 0), (WIN, 0), (0, 0)))
    band = WIN + Q_BLOCK
    gather = jax.vmap(lambda kb, ib: kb[ib])

    def block(args):
        qb, selb, qpos, start = args
        gk = gather(ks_blk, selb).reshape(B, Q_BLOCK, n_sel * SLC_LEN, dh)
        gv = gather(vs_blk, selb).reshape(B, Q_BLOCK, n_sel * SLC_LEN, dh)
        kpos = (selb[..., None] * SLC_LEN + jnp.arange(SLC_LEN)).reshape(B, Q_BLOCK, n_sel * SLC_LEN)
        s = jnp.einsum('bqhd,bqkd->bhqk', qb, gk).astype(jnp.float32) * scale
        s = jnp.where((kpos <= qpos[None, :, None])[:, None], s, NEG_INF)
        o_s = jnp.einsum('bhqk,bqkd->bqhd', jax.nn.softmax(s, axis=-1).astype(gv.dtype), gv)
        kwb = lax.dynamic_slice_in_dim(kw_pad, start, band, axis=1)
        vwb = lax.dynamic_slice_in_dim(vw_pad, start, band, axis=1)
        wpos = start - WIN + jnp.arange(band)
        dist = qpos[:, None] - wpos[None, :]
        wmask = (dist >= 0) & (dist < WIN) & (wpos[None, :] >= 0)
        s_w = jnp.einsum('bqhd,bkd->bhqk', qb, kwb).astype(jnp.float32) * scale
        s_w = jnp.where(wmask[None, None], s_w, NEG_INF)
        o_w = jnp.einsum('bhqk,bkd->bqhd', jax.nn.softmax(s_w, axis=-1).astype(vwb.dtype), vwb)
        return o_s, o_w

    nb = S // Q_BLOCK
    o_slc, o_win = lax.map(block, (to_blocks(q), to_blocks(sel), pos.reshape(nb, Q_BLOCK), jnp.arange(nb) * Q_BLOCK))
    g = jax.nn.sigmoid(g_logit).reshape(B, S, H, 3)
    return g[..., 0:1] * o_cmp + g[..., 1:2] * from_blocks(o_slc) + g[..., 2:3] * from_blocks(o_win)


def dsa_attention(q, k, v, iq, ik, iw):
    B, S, H, dh = q.shape
    pos = jnp.arange(S)
    scale = dh ** -0.5
    k_top = min(IDX_TOPK_MAX, S // 4)
    iw = iw.astype(jnp.float32) * (IDX_HEADS ** -0.5 * IDX_DIM ** -0.5)
    gather = jax.vmap(lambda kb, ib: kb[ib])

    def block(args):
        qb, iqb, iwb, qpos = args
        sc = jax.nn.relu(jnp.einsum('bqhd,bkd->bqhk', iqb, ik).astype(jnp.float32))
        sc = jnp.einsum('bqhk,bqh->bqk', sc, iwb)
        sc = jnp.where(pos[None, None, :] <= qpos[None, :, None], sc, NEG_INF)
        _, idx = lax.top_k(sc, k_top)
        gk = gather(k, idx)
        gv = gather(v, idx)
        s = jnp.einsum('bqhd,bqkd->bhqk', qb, gk).astype(jnp.float32) * scale
        s = jnp.where((idx <= qpos[None, :, None])[:, None], s, NEG_INF)
        return jnp.einsum('bhqk,bqkd->bqhd', jax.nn.softmax(s, axis=-1).astype(gv.dtype), gv)

    out = lax.map(block, (to_blocks(q), to_blocks(iq), to_blocks(iw), pos.reshape(-1, Q_BLOCK)))
    return from_blocks(out)


def setup_inputs(seed: int = 0) -> dict:
    key = jax.random.key(seed)
    ks = jax.random.split(key, 24)
    L, D = DEPTH, D_MODEL
    cw = CMP_LEN * HEAD_DIM

    def nrm(k, shape, s):
        return jax.random.normal(k, shape, jnp.float32) * s

    return {
        'x': nrm(ks[0], (BATCH, SEQ, D), 1.0),
        'p': nrm(ks[1], (DEPTH, BATCH, SEQ, PLE_DIM), 1.0),
        'w_in': nrm(ks[2], (L, D, IN_COLS), D ** -0.5),
        'fox_f_bias': 3.0 + nrm(ks[3], (L, FOX_HEADS), 0.5),
        'nsa_pe_k': nrm(ks[4], (L, CMP_LEN, HEAD_DIM), 0.1),
        'nsa_pe_v': nrm(ks[5], (L, CMP_LEN, HEAD_DIM), 0.1),
        'nsa_cmp_k1': nrm(ks[6], (L, cw, CMP_HIDDEN), cw ** -0.5),
        'nsa_cmp_k2': nrm(ks[7], (L, CMP_HIDDEN, HEAD_DIM), CMP_HIDDEN ** -0.5),
        'nsa_cmp_v1': nrm(ks[8], (L, cw, CMP_HIDDEN), cw ** -0.5),
        'nsa_cmp_v2': nrm(ks[9], (L, CMP_HIDDEN, HEAD_DIM), CMP_HIDDEN ** -0.5),
        'dsa_kv_norm': 1.0 + nrm(ks[10], (L, DSA_KV_RANK), 0.02),
        'dsa_kv_up': nrm(ks[11], (L, DSA_KV_RANK, 2 * HEAD_DIM), DSA_KV_RANK ** -0.5),
        'w_br_fox': nrm(ks[12], (L, FOX_HEADS * HEAD_DIM, D), (FOX_HEADS * HEAD_DIM) ** -0.5),
        'w_br_nsa': nrm(ks[13], (L, NSA_HEADS * HEAD_DIM, D), (NSA_HEADS * HEAD_DIM) ** -0.5),
        'w_br_dsa': nrm(ks[14], (L, DSA_HEADS * HEAD_DIM, D), (DSA_HEADS * HEAD_DIM) ** -0.5),
        'w_out': nrm(ks[15], (L, D, D), BETA * D ** -0.5),
        'ln1_g': 1.0 + nrm(ks[16], (L, D), 0.02),
        'ln1_b': nrm(ks[17], (L, D), 0.02),
        'w_ffn_in': nrm(ks[18], (L, D, 2 * D_FF), BETA * D ** -0.5),
        'w_ffn_out': nrm(ks[19], (L, D_FF, D), BETA * D_FF ** -0.5),
        'ln2_g': 1.0 + nrm(ks[20], (L, D), 0.02),
        'ln2_b': nrm(ks[21], (L, D), 0.02),
        'w_ple_in': nrm(ks[22], (L, PLE_DIM, D), BETA * PLE_DIM ** -0.5),
        'w_ple_gate': nrm(ks[23], (L, D, D), D ** -0.5),
    }


def reference(x, p, w_in, fox_f_bias, nsa_pe_k, nsa_pe_v, nsa_cmp_k1, nsa_cmp_k2, nsa_cmp_v1, nsa_cmp_v2,
              dsa_kv_norm, dsa_kv_up, w_br_fox, w_br_nsa, w_br_dsa, w_out, ln1_g, ln1_b,
              w_ffn_in, w_ffn_out, ln2_g, ln2_b, w_ple_in, w_ple_gate):
    B, S, _ = x.shape
    cos, sin = rope_tables(S, HEAD_DIM)
    cos_i, sin_i = rope_tables(S, IDX_DIM)
    h = x
    for i in range(DEPTH):
        z = split_cols(h @ w_in[i])
        o_fox = fox_attention(heads(z['fox_q'], FOX_HEADS), heads(z['fox_k'], FOX_HEADS),
                              heads(z['fox_v'], FOX_HEADS), z['fox_f'], fox_f_bias[i])
        o_nsa = nsa_attention(apply_rope(heads(z['nsa_q'], NSA_HEADS), cos, sin),
                              z['nsa_kc'], z['nsa_vc'], rope1(z['nsa_ks'], cos, sin), z['nsa_vs'],
                              rope1(z['nsa_kw'], cos, sin), z['nsa_vw'], z['nsa_g'],
                              nsa_pe_k[i], nsa_pe_v[i], nsa_cmp_k1[i], nsa_cmp_k2[i], nsa_cmp_v1[i], nsa_cmp_v2[i],
                              cos, sin)
        k_d, v_d = jnp.split(rms_norm(z['dsa_ckv'], dsa_kv_norm[i]) @ dsa_kv_up[i], 2, axis=-1)
        o_dsa = dsa_attention(apply_rope(heads(z['dsa_q'], DSA_HEADS), cos, sin),
                              rope1(k_d, cos, sin), v_d,
                              apply_rope(heads(z['idx_q'], IDX_HEADS), cos_i, sin_i),
                              rope1(z['idx_k'], cos_i, sin_i), z['idx_w'])
        g_fox, g_nsa, g_dsa = jnp.split(jax.nn.sigmoid(z['gate']), 3, axis=-1)
        mixed = (g_fox * (o_fox.reshape(B, S, -1) @ w_br_fox[i])
                 + g_nsa * (o_nsa.reshape(B, S, -1) @ w_br_nsa[i])
                 + g_dsa * (o_dsa.reshape(B, S, -1) @ w_br_dsa[i]))
        h = layer_norm(ALPHA * h + mixed @ w_out[i], ln1_g[i], ln1_b[i])
        a, b = jnp.split(h @ w_ffn_in[i], 2, axis=-1)
        h = layer_norm(ALPHA * h + (jax.nn.silu(a) * b) @ w_ffn_out[i], ln2_g[i], ln2_b[i])
        h = h + (p[i] @ w_ple_in[i]) * jax.nn.sigmoid(h @ w_ple_gate[i])
    return h
```

```python
import functools

import jax
import jax.numpy as jnp
from jax import lax
from jax.experimental import pallas as pl
from jax.experimental.pallas import tpu as pltpu

F32, BF16, I32 = jnp.float32, jnp.bfloat16, jnp.int32

D_MODEL = 2048
DEPTH = 4
HEAD_DIM = 128
ROPE_THETA = 10000.0
NEG_INF = -1e30
FOX_HEADS = 6
NSA_HEADS = 4
CMP_LEN = 32
CMP_STRIDE = 16
CMP_HIDDEN = 256
SLC_LEN = 64
SLC_TOPN = 16
WIN = 512
SLC_FORCE = 1e4
DSA_HEADS = 6
DSA_KV_RANK = 256
IDX_HEADS = 16
IDX_DIM = 64
IDX_TOPK_MAX = 256
D_FF = ((8 * D_MODEL + 3 * 256 - 1) // (3 * 256)) * 256
PLE_DIM = 256
ALPHA = (2 * DEPTH) ** 0.25
SCALE = HEAD_DIM ** -0.5

LANE = 128
VMEM_LIMIT_BYTES = 48 * 1024 * 1024

_IN_SPLITS = (
    ('fox_q', FOX_HEADS * HEAD_DIM), ('fox_k', FOX_HEADS * HEAD_DIM), ('fox_v', FOX_HEADS * HEAD_DIM), ('fox_f', FOX_HEADS),
    ('nsa_q', NSA_HEADS * HEAD_DIM), ('nsa_kc', HEAD_DIM), ('nsa_vc', HEAD_DIM), ('nsa_ks', HEAD_DIM), ('nsa_vs', HEAD_DIM),
    ('nsa_kw', HEAD_DIM), ('nsa_vw', HEAD_DIM), ('nsa_g', 3 * NSA_HEADS),
    ('dsa_q', DSA_HEADS * HEAD_DIM), ('dsa_ckv', DSA_KV_RANK), ('idx_q', IDX_HEADS * IDX_DIM), ('idx_k', IDX_DIM), ('idx_w', IDX_HEADS),
    ('gate', 3 * D_MODEL),
)
_IN_OFFSETS = {}
_off = 0
for _name, _width in _IN_SPLITS:
    _IN_OFFSETS[_name] = (_off, _off + _width)
    _off += _width

MISC_FOX_F = 0
MISC_NSA_G = FOX_HEADS
MISC_IDX_W = FOX_HEADS + 3 * NSA_HEADS

TQ = 256
TK = 256
N_CMP_PAD = 128


def _dot(a, b):
    return jnp.dot(a, b, preferred_element_type=F32)


def _dot_nt(a, b):
    return lax.dot_general(a, b, (((1,), (1,)), ((), ())), preferred_element_type=F32)


def _split3(a):
    a1 = a.astype(BF16)
    r1 = a - a1.astype(F32)
    a2 = r1.astype(BF16)
    a3 = (r1 - a2.astype(F32)).astype(BF16)
    return a1, a2, a3


def _call(kernel, name, lidx, args, grid, in_specs, out_specs, out_shape, scratch_shapes=()):
    return pl.pallas_call(
        kernel,
        grid_spec=pltpu.PrefetchScalarGridSpec(
            num_scalar_prefetch=1, grid=grid, in_specs=in_specs, out_specs=out_specs,
            scratch_shapes=list(scratch_shapes)),
        out_shape=out_shape,
        compiler_params=pltpu.CompilerParams(
            dimension_semantics=("arbitrary",) * len(grid), vmem_limit_bytes=VMEM_LIMIT_BYTES),
        name=name,
    )(lidx, *args)


def _proj_kernel(l_ref, x_ref, w_ref, *refs, shifts):
    o_ref = refs[-1]
    acc = _dot(x_ref[...], w_ref[...])
    if not shifts:
        o_ref[...] = acc.astype(o_ref.dtype)
        return
    c_ref = refs[0]
    s_refs = refs[1:-1]
    for g in range(acc.shape[1] // LANE):
        xg = acc[:, g * LANE:(g + 1) * LANE]
        out = xg * c_ref[...]
        for shift, s_ref in zip(shifts, s_refs):
            out = out + pltpu.roll(xg, shift, 1) * s_ref[...]
        o_ref[:, g * LANE:(g + 1) * LANE] = out.astype(o_ref.dtype)


def _proj(name, lidx, hb, w, tn, out_dtype, seq, rope=None):
    t, d = hb.shape
    n = w.shape[2]
    tm = 1024
    in_specs = [pl.BlockSpec((tm, d), lambda i, j, l: (i, 0)),
                pl.BlockSpec((None, d, tn), lambda i, j, l: (l[0], 0, j))]
    args = [hb, w]
    shifts = ()
    if rope is not None:
        shifts, tables = rope[0], rope[1:]
        nrow = seq // tm
        for tab in tables:
            in_specs.append(pl.BlockSpec((tm, LANE), lambda i, j, l: (i % nrow, 0)))
            args.append(tab)
    return _call(functools.partial(_proj_kernel, shifts=shifts), name, lidx, args,
                 grid=(t // tm, n // tn), in_specs=in_specs,
                 out_specs=pl.BlockSpec((tm, tn), lambda i, j, l: (i, j)),
                 out_shape=jax.ShapeDtypeStruct((t, n), out_dtype))


def _cum_kernel(l_ref, f_ref, bias_ref, o_ref):
    chunk = 256
    row = lax.broadcasted_iota(I32, (chunk, chunk), 0)
    col = lax.broadcasted_iota(I32, (chunk, chunk), 1)
    tri = jnp.where(row >= col, 1.0, 0.0).astype(BF16)
    carry = jnp.zeros((1, LANE), F32)
    for c in range(f_ref.shape[0] // chunk):
        logf = jax.nn.log_sigmoid(f_ref[c * chunk:(c + 1) * chunk, :] + bias_ref[...])
        l1, l2, l3 = _split3(logf)
        cs = _dot(tri, l1) + _dot(tri, l2) + _dot(tri, l3) + carry
        o_ref[c * chunk:(c + 1) * chunk, :] = cs
        carry = cs[chunk - 1:chunk, :]


def _fox_kernel(l_ref, q_ref, k_ref, v_ref, cq_ref, ck_ref, o_ref, m_ref, l_sc, acc_ref):
    qi = pl.program_id(1)
    nchunk = ck_ref.shape[0] // 8
    row = lax.broadcasted_iota(I32, (TQ, TK), 0)
    col = lax.broadcasted_iota(I32, (TQ, TK), 1)
    for h in range(FOX_HEADS):
        hs = slice(h * HEAD_DIM, (h + 1) * HEAD_DIM)
        q = q_ref[:, hs]
        cq = cq_ref[:, h:h + 1]
        m_ref[...] = jnp.full(m_ref.shape, NEG_INF, F32)
        l_sc[...] = jnp.zeros(l_sc.shape, F32)
        acc_ref[...] = jnp.zeros(acc_ref.shape, F32)

        def chunk_step(j, diagonal):
            start = pl.multiple_of(j * TK, TK)
            kj = k_ref[pl.ds(start, TK), hs]
            vj = v_ref[pl.ds(start, TK), hs]
            ck = ck_ref[pl.ds(h * nchunk + j, 1), :]
            s = _dot_nt(q, kj) * SCALE + cq - ck
            if diagonal:
                s = jnp.where(col <= row, s, NEG_INF)
            m_old = m_ref[...]
            m_new = jnp.maximum(m_old, jnp.max(s, axis=-1, keepdims=True))
            p = jnp.exp(s - m_new)
            alpha = jnp.exp(m_old - m_new)
            l_sc[...] = alpha * l_sc[...] + jnp.sum(p, axis=-1, keepdims=True)
            acc_ref[...] = alpha * acc_ref[...] + _dot(p.astype(BF16), vj)
            m_ref[...] = m_new

        def body(j, carry):
            chunk_step(j, False)
            return carry

        lax.fori_loop(0, qi, body, 0)
        chunk_step(qi, True)
        o_ref[:, hs] = (acc_ref[...] / l_sc[...]).astype(o_ref.dtype)


def _cmp_kernel(l_ref, rk_ref, rv_ref, pek_ref, pev_ref, wk1_ref, wk2_ref, wv1_ref, wv2_ref,
                c_ref, s_ref, ko_ref, vo_ref):
    half = CMP_STRIDE * HEAD_DIM

    def compress(r_ref, pe_ref, w1_ref, w2_ref):
        r = r_ref[...]
        lo = (r + pe_ref[0:1, :]).astype(BF16)
        hi = (r + pe_ref[1:2, :]).astype(BF16)
        a = _dot(lo, w1_ref[0:half, :])
        b = _dot(hi, w1_ref[half:2 * half, :])
        hid = a + pltpu.roll(b, N_CMP_PAD - 1, 0)
        return _dot(jax.nn.gelu(hid).astype(BF16), w2_ref[...])

    kc = compress(rk_ref, pek_ref, wk1_ref, wk2_ref)
    kc = kc * c_ref[...] + pltpu.roll(kc, HEAD_DIM // 2, 1) * s_ref[...]
    ko_ref[...] = kc.astype(ko_ref.dtype)
    vo_ref[...] = compress(rv_ref, pev_ref, wv1_ref, wv2_ref).astype(vo_ref.dtype)


def _masked_softmax(s, mask):
    s = jnp.where(mask, s, NEG_INF)
    m = jnp.max(s, axis=-1, keepdims=True)
    e = jnp.where(mask, jnp.exp(s - m), 0.0)
    den = jnp.sum(e, axis=-1, keepdims=True)
    return e, jnp.where(den > 0.0, den, 1.0)


def _nsa_kernel(l_ref, q_ref, ks_ref, kw_ref, vs_ref, vw_ref, kc_ref, vc_ref, g_ref, ov_ref, et_ref,
                o_ref, m_ref, l_sc, acc_ref):
    qi = pl.program_id(1)
    nh = NSA_HEADS
    q = jnp.concatenate([q_ref[:, h * HEAD_DIM:(h + 1) * HEAD_DIM] for h in range(nh)], axis=0)
    pos = qi * TQ + lax.broadcasted_iota(I32, (TQ, 1), 0)
    lane = lax.broadcasted_iota(I32, (TQ, LANE), 1)

    cvis = (lane * CMP_STRIDE + (CMP_LEN - 1)) <= pos
    s_c = (_dot_nt(q, kc_ref[...]) * SCALE).reshape(nh, TQ, N_CMP_PAD)
    e_c, den_c = _masked_softmax(s_c, cvis[None])
    p_c = e_c / den_c
    o_cmp = _dot(p_c.reshape(nh * TQ, N_CMP_PAD).astype(BF16), vc_ref[...])

    p_sum = p_c[0]
    for h in range(1, nh):
        p_sum = p_sum + p_c[h]
    p1, p2, p3 = _split3(p_sum)
    imp = _dot(p1, ov_ref[...]) + _dot(p2, ov_ref[...]) + _dot(p3, ov_ref[...])
    n_slc = et_ref.shape[0] // SLC_LEN
    cur = lax.shift_right_logical(pos, SLC_LEN.bit_length() - 1)
    forced = (lane == 0) | (lane == cur) | (lane == cur - 1)
    imp = jnp.where(forced, SLC_FORCE, jnp.where(lane * SLC_LEN <= pos, imp, -SLC_FORCE))
    imp = jnp.where(lane < n_slc, imp, -4.0 * SLC_FORCE)
    rank = jnp.zeros((TQ, LANE), F32)
    for k in range(n_slc):
        ik = imp[:, k:k + 1]
        ahead = (ik > imp) | ((ik == imp) & (lane > k))
        rank = rank + jnp.where(ahead, 1.0, 0.0)
    sel = jnp.where(rank < float(min(SLC_TOPN, n_slc)), 1.0, 0.0).astype(BF16)

    m_ref[...] = jnp.full(m_ref.shape, NEG_INF, F32)
    l_sc[...] = jnp.zeros(l_sc.shape, F32)
    acc_ref[...] = jnp.zeros(acc_ref.shape, F32)
    row = lax.broadcasted_iota(I32, (TQ, TK), 0)
    col = lax.broadcasted_iota(I32, (TQ, TK), 1)

    def chunk_step(j, diagonal):
        start = pl.multiple_of(j * TK, TK)
        kj = ks_ref[pl.ds(start, TK), :]
        vj = vs_ref[pl.ds(start, TK), :]
        vis = _dot_nt(sel, et_ref[pl.ds(start, TK), :]) > 0.5
        if diagonal:
            vis = vis & (col <= row)
        s = (_dot_nt(q, kj) * SCALE).reshape(nh, TQ, TK)
        s = jnp.where(vis[None], s, NEG_INF).reshape(nh * TQ, TK)
        m_old = m_ref[...]
        m_new = jnp.maximum(m_old, jnp.max(s, axis=-1, keepdims=True))
        p = jnp.exp(s - m_new).reshape(nh, TQ, TK)
        p = jnp.where(vis[None], p, 0.0).reshape(nh * TQ, TK)
        alpha = jnp.exp(m_old - m_new)
        l_sc[...] = alpha * l_sc[...] + jnp.sum(p, axis=-1, keepdims=True)
        acc_ref[...] = alpha * acc_ref[...] + _dot(p.astype(BF16), vj)
        m_ref[...] = m_new

    def body(j, carry):
        chunk_step(j, False)
        return carry

    lax.fori_loop(0, qi, body, 0)
    chunk_step(qi, True)
    o_slc = acc_ref[...] / l_sc[...]

    band = WIN + TQ
    start_w = pl.multiple_of(jnp.maximum(qi * TQ - WIN, 0), TQ)
    kwb = kw_ref[pl.ds(start_w, band), :]
    vwb = vw_ref[pl.ds(start_w, band), :]
    dist = pos - (start_w + lax.broadcasted_iota(I32, (TQ, band), 1))
    wvis = (dist >= 0) & (dist < WIN)
    s_w = (_dot_nt(q, kwb) * SCALE).reshape(nh, TQ, band)
    e_w, den_w = _masked_softmax(s_w, wvis[None])
    o_win = _dot(e_w.reshape(nh * TQ, band).astype(BF16), vwb) / den_w.reshape(nh * TQ, 1)

    gate = jax.nn.sigmoid(g_ref[...])
    for h in range(nh):
        rows = slice(h * TQ, (h + 1) * TQ)
        g0 = gate[:, MISC_NSA_G + 3 * h:MISC_NSA_G + 3 * h + 1]
        g1 = gate[:, MISC_NSA_G + 3 * h + 1:MISC_NSA_G + 3 * h + 2]
        g2 = gate[:, MISC_NSA_G + 3 * h + 2:MISC_NSA_G + 3 * h + 3]
        out = g0 * o_cmp[rows] + g1 * o_slc[rows] + g2 * o_win[rows]
        o_ref[:, h * HEAD_DIM:(h + 1) * HEAD_DIM] = out.astype(o_ref.dtype)


def _dsa_kv_kernel(l_ref, x_ref, g_ref, up_ref, c_ref, s_ref, ko_ref, vo_ref):
    x = x_ref[...]
    r = x * lax.rsqrt(jnp.mean(jnp.square(x), axis=-1, keepdims=True) + 1e-6) * g_ref[...]
    kv = _dot(r.astype(BF16), up_ref[...])
    k = kv[:, :HEAD_DIM]
    k = k * c_ref[...] + pltpu.roll(k, HEAD_DIM // 2, 1) * s_ref[...]
    ko_ref[...] = k.astype(ko_ref.dtype)
    vo_ref[...] = kv[:, HEAD_DIM:].astype(vo_ref.dtype)


def _count_lanes(x):
    out = x[:, 0:LANE]
    for g in range(1, x.shape[1] // LANE):
        out = out + x[:, g * LANE:(g + 1) * LANE]
    return out


def _dsa_kernel(l_ref, q_ref, kd_ref, vd_ref, iq_ref, ika_ref, ikb_ref, w_ref, o_ref,
                key_ref, m_ref, l_sc, acc_ref):
    qi = pl.program_id(1)
    nh = DSA_HEADS
    k_top = float(IDX_TOPK_MAX)
    row = lax.broadcasted_iota(I32, (TQ, TK), 0)
    col = lax.broadcasted_iota(I32, (TQ, TK), 1)
    int_min = jnp.int32(-2 ** 31)
    idx_bits = (kd_ref.shape[0] - 1).bit_length()

    w_all = w_ref[...] * (IDX_HEADS ** -0.5 * IDX_DIM ** -0.5)

    def score_step(j, diagonal):
        start = pl.multiple_of(j * TK, TK)
        ka = ika_ref[pl.ds(start, TK), :]
        kb = ikb_ref[pl.ds(start, TK), :]
        sc = jnp.zeros((TQ, TK), F32)
        for g in range(IDX_HEADS // 2):
            qpair = iq_ref[:, g * LANE:(g + 1) * LANE]
            wa = w_all[:, MISC_IDX_W + 2 * g:MISC_IDX_W + 2 * g + 1]
            wb = w_all[:, MISC_IDX_W + 2 * g + 1:MISC_IDX_W + 2 * g + 2]
            sc = sc + jnp.maximum(_dot_nt(qpair, ka), 0.0) * wa + jnp.maximum(_dot_nt(qpair, kb), 0.0) * wb
        if diagonal:
            sc = jnp.where(col <= row, sc, NEG_INF)
        bits = lax.bitcast_convert_type(sc, I32)
        key = jnp.where(bits < 0, bits ^ jnp.int32(0x7FFFFFFF), bits)
        key_ref[j] = jnp.where(sc == 0.0, 0, key)

    def score_body(j, carry):
        score_step(j, False)
        return carry

    lax.fori_loop(0, qi, score_body, 0)
    score_step(qi, True)

    def count(pred_fn):
        def body(j, acc):
            return acc + _count_lanes(jnp.where(pred_fn(key_ref[j], j), 1.0, 0.0))
        part = lax.fori_loop(0, qi + 1, body, jnp.zeros((TQ, LANE), F32))
        return jnp.sum(part, axis=-1, keepdims=True)

    thr = jnp.where(count(lambda key, j: key >= 0) >= k_top, jnp.int32(0), int_min) + jnp.zeros((TQ, 1), I32)

    def value_bit(i, thr):
        cand = thr + (jnp.int32(1) << (30 - i))
        return jnp.where(count(lambda key, j: key >= cand) >= k_top, cand, thr)

    thr = lax.fori_loop(0, 31, value_bit, thr)
    n_gt = count(lambda key, j: key > thr)

    def index_bit(i, cut):
        cand = cut + (jnp.int32(1) << (idx_bits - 1 - i))
        n_before = count(lambda key, j: (key == thr) & (j * TK + col < cand))
        return jnp.where(n_gt + n_before < k_top, cand, cut)

    cut = lax.fori_loop(0, idx_bits, index_bit, jnp.zeros((TQ, 1), I32))

    q = jnp.concatenate([q_ref[:, h * HEAD_DIM:(h + 1) * HEAD_DIM] for h in range(nh)], axis=0)
    m_ref[...] = jnp.full(m_ref.shape, NEG_INF, F32)
    l_sc[...] = jnp.zeros(l_sc.shape, F32)
    acc_ref[...] = jnp.zeros(acc_ref.shape, F32)

    def attn_step(j, diagonal):
        start = pl.multiple_of(j * TK, TK)
        kj = kd_ref[pl.ds(start, TK), :]
        vj = vd_ref[pl.ds(start, TK), :]
        key = key_ref[j]
        vis = (key > thr) | ((key == thr) & (j * TK + col <= cut))
        if diagonal:
            vis = vis & (col <= row)
        s = (_dot_nt(q, kj) * SCALE).reshape(nh, TQ, TK)
        s = jnp.where(vis[None], s, NEG_INF).reshape(nh * TQ, TK)
        m_old = m_ref[...]
        m_new = jnp.maximum(m_old, jnp.max(s, axis=-1, keepdims=True))
        p = jnp.exp(s - m_new).reshape(nh, TQ, TK)
        p = jnp.where(vis[None], p, 0.0).reshape(nh * TQ, TK)
        alpha = jnp.exp(m_old - m_new)
        l_sc[...] = alpha * l_sc[...] + jnp.sum(p, axis=-1, keepdims=True)
        acc_ref[...] = alpha * acc_ref[...] + _dot(p.astype(BF16), vj)
        m_ref[...] = m_new

    def attn_body(j, carry):
        attn_step(j, False)
        return carry

    lax.fori_loop(0, qi, attn_body, 0)
    attn_step(qi, True)
    out = acc_ref[...] / l_sc[...]
    for h in range(nh):
        o_ref[:, h * HEAD_DIM:(h + 1) * HEAD_DIM] = out[h * TQ:(h + 1) * TQ].astype(o_ref.dtype)


def _merge_kernel(l_ref, hb_ref, of_ref, on_ref, od_ref, wg1_ref, wg2_ref, wg3_ref, wf_ref, wn_ref, wd_ref, o_ref):
    hb = hb_ref[...]
    mixed = (jax.nn.sigmoid(_dot(hb, wg1_ref[...])) * _dot(of_ref[...], wf_ref[...])
             + jax.nn.sigmoid(_dot(hb, wg2_ref[...])) * _dot(on_ref[...], wn_ref[...])
             + jax.nn.sigmoid(_dot(hb, wg3_ref[...])) * _dot(od_ref[...], wd_ref[...]))
    o_ref[...] = mixed.astype(o_ref.dtype)


def _proj_ln_kernel(l_ref, x_ref, w_ref, h_ref, g_ref, b_ref, of_ref, ob_ref, acc_ref, *, nk):
    k = pl.program_id(1)

    @pl.when(k == 0)
    def _():
        acc_ref[...] = ALPHA * h_ref[...]

    acc_ref[...] += _dot(x_ref[...], w_ref[...])

    @pl.when(k == nk - 1)
    def _():
        y = acc_ref[...]
        mu = jnp.mean(y, axis=-1, keepdims=True)
        yc = y - mu
        var = jnp.mean(jnp.square(yc), axis=-1, keepdims=True)
        out = yc * lax.rsqrt(var + 1e-5) * g_ref[...] + b_ref[...]
        of_ref[...] = out
        ob_ref[...] = out.astype(ob_ref.dtype)


def _proj_ln(name, lidx, x, w, h, g, b, tk):
    t, kdim = x.shape
    d = h.shape[1]
    tm = 512
    nk = kdim // tk
    return _call(functools.partial(_proj_ln_kernel, nk=nk), name, lidx, [x, w, h, g, b],
                 grid=(t // tm, nk),
                 in_specs=[pl.BlockSpec((tm, tk), lambda i, k, l: (i, k)),
                           pl.BlockSpec((None, tk, d), lambda i, k, l: (l[0], k, 0)),
                           pl.BlockSpec((tm, d), lambda i, k, l: (i, 0)),
                           pl.BlockSpec((None, 1, d), lambda i, k, l: (l[0], 0, 0)),
                           pl.BlockSpec((None, 1, d), lambda i, k, l: (l[0], 0, 0))],
                 out_specs=[pl.BlockSpec((tm, d), lambda i, k, l: (i, 0)),
                            pl.BlockSpec((tm, d), lambda i, k, l: (i, 0))],
                 out_shape=[jax.ShapeDtypeStruct((t, d), F32), jax.ShapeDtypeStruct((t, d), BF16)],
                 scratch_shapes=[pltpu.VMEM((tm, d), F32)])


def _swiglu_kernel(l_ref, x_ref, wa_ref, wb_ref, o_ref):
    x = x_ref[...]
    a = _dot(x, wa_ref[...])
    o_ref[...] = (jax.nn.silu(a) * _dot(x, wb_ref[...])).astype(o_ref.dtype)


def _ple_kernel(l_ref, hb_ref, p_ref, wpi_ref, wpg_ref, h_ref, of_ref, ob_ref):
    out = h_ref[...] + _dot(p_ref[...], wpi_ref[...]) * jax.nn.sigmoid(_dot(hb_ref[...], wpg_ref[...]))
    of_ref[...] = out
    ob_ref[...] = out.astype(ob_ref.dtype)


def _rope_tables(n, dim):
    inv = 1.0 / (ROPE_THETA ** (jnp.arange(0, dim, 2, dtype=F32) / dim))
    ang = jnp.arange(n, dtype=F32)[:, None] * inv[None, :]
    return jnp.cos(ang), jnp.sin(ang)


def _layer(lidx, h, hb, consts, weights, bsz, seq):
    (cos128, sin128, cos64, sin64a, sin64b, cmp_cos, cmp_sin, overlap, block_of_key) = consts
    (w_a, w_b, w_c, w_d, w_gate, fox_bias, pe_k, pe_v, wk1, wk2, wv1, wv2, kv_norm, kv_up,
     w_br_fox, w_br_nsa, w_br_dsa, w_out, ln1_g, ln1_b, w_ffn_in, w_ffn_out, ln2_g, ln2_b, p, w_ple_in, w_ple_gate) = weights
    t = bsz * seq
    nq = seq // TQ
    d = D_MODEL

    z_a = _proj("proj_plain", lidx, hb, w_a, 512, BF16, seq)
    z_b = _proj("proj_rope128", lidx, hb, w_b, 512, BF16, seq, rope=((HEAD_DIM // 2,), cos128, sin128))
    z_c = _proj("proj_rope64", lidx, hb, w_c, 640, BF16, seq,
                rope=((IDX_DIM // 2, LANE - IDX_DIM // 2), cos64, sin64a, sin64b))
    z_d = _proj("proj_f32", lidx, hb, w_d, 640, F32, seq)

    cum = _call(_cum_kernel, "fox_cum", lidx, [z_d, fox_bias], grid=(bsz,),
                in_specs=[pl.BlockSpec((seq, LANE), lambda b, l: (b, 2)),
                          pl.BlockSpec((None, 1, LANE), lambda b, l: (l[0], 0, 0))],
                out_specs=pl.BlockSpec((seq, LANE), lambda b, l: (b, 0)),
                out_shape=jax.ShapeDtypeStruct((t, LANE), F32))
    cum_rows = jnp.transpose(cum.reshape(bsz, seq, LANE)[:, :, :8], (0, 2, 1)).reshape(bsz, 8 * (seq // TK), TK)
    fq = FOX_HEADS * HEAD_DIM
    o_fox = _call(_fox_kernel, "fox_attn", lidx, [z_a, z_a, z_a, cum, cum_rows], grid=(bsz, nq),
                  in_specs=[pl.BlockSpec((TQ, fq), lambda b, i, l: (b * nq + i, 0)),
                            pl.BlockSpec((seq, fq), lambda b, i, l: (b, 1)),
                            pl.BlockSpec((seq, fq), lambda b, i, l: (b, 2)),
                            pl.BlockSpec((TQ, LANE), lambda b, i, l: (b * nq + i, 0)),
                            pl.BlockSpec((None, 8 * (seq // TK), TK), lambda b, i, l: (b, 0, 0))],
                  out_specs=pl.BlockSpec((TQ, fq), lambda b, i, l: (b * nq + i, 0)),
                  out_shape=jax.ShapeDtypeStruct((t, fq), BF16),
                  scratch_shapes=[pltpu.VMEM((TQ, 1), F32), pltpu.VMEM((TQ, 1), F32), pltpu.VMEM((TQ, HEAD_DIM), F32)])

    nchunk = seq // CMP_STRIDE
    half = CMP_STRIDE * HEAD_DIM
    r_k = z_d[:, 3 * LANE:4 * LANE].reshape(bsz * nchunk, half)
    r_v = z_d[:, 4 * LANE:5 * LANE].reshape(bsz * nchunk, half)
    k_cmp, v_cmp = _call(
        _cmp_kernel, "nsa_compress", lidx, [r_k, r_v, pe_k, pe_v, wk1, wk2, wv1, wv2, cmp_cos, cmp_sin], grid=(bsz,),
        in_specs=[pl.BlockSpec((nchunk, half), lambda b, l: (b, 0)),
                  pl.BlockSpec((nchunk, half), lambda b, l: (b, 0)),
                  pl.BlockSpec((None, 2, half), lambda b, l: (l[0], 0, 0)),
                  pl.BlockSpec((None, 2, half), lambda b, l: (l[0], 0, 0)),
                  pl.BlockSpec((None, 2 * half, CMP_HIDDEN), lambda b, l: (l[0], 0, 0)),
                  pl.BlockSpec((None, CMP_HIDDEN, HEAD_DIM), lambda b, l: (l[0], 0, 0)),
                  pl.BlockSpec((None, 2 * half, CMP_HIDDEN), lambda b, l: (l[0], 0, 0)),
                  pl.BlockSpec((None, CMP_HIDDEN, HEAD_DIM), lambda b, l: (l[0], 0, 0)),
                  pl.BlockSpec((nchunk, HEAD_DIM), lambda b, l: (0, 0)),
                  pl.BlockSpec((nchunk, HEAD_DIM), lambda b, l: (0, 0))],
        out_specs=[pl.BlockSpec((nchunk, HEAD_DIM), lambda b, l: (b, 0)),
                   pl.BlockSpec((nchunk, HEAD_DIM), lambda b, l: (b, 0))],
        out_shape=[jax.ShapeDtypeStruct((bsz * nchunk, HEAD_DIM), BF16)] * 2)
    nsq = NSA_HEADS * HEAD_DIM
    o_nsa = _call(
        _nsa_kernel, "nsa_attn", lidx, [z_b, z_b, z_b, z_a, z_a, k_cmp, v_cmp, z_d, overlap, block_of_key], grid=(bsz, nq),
        in_specs=[pl.BlockSpec((TQ, nsq), lambda b, i, l: (b * nq + i, 0)),
                  pl.BlockSpec((seq, LANE), lambda b, i, l: (b, 4)),
                  pl.BlockSpec((seq, LANE), lambda b, i, l: (b, 5)),
                  pl.BlockSpec((seq, LANE), lambda b, i, l: (b, 18)),
                  pl.BlockSpec((seq, LANE), lambda b, i, l: (b, 19)),
                  pl.BlockSpec((nchunk, HEAD_DIM), lambda b, i, l: (b, 0)),
                  pl.BlockSpec((nchunk, HEAD_DIM), lambda b, i, l: (b, 0)),
                  pl.BlockSpec((TQ, LANE), lambda b, i, l: (b * nq + i, 2)),
                  pl.BlockSpec((N_CMP_PAD, LANE), lambda b, i, l: (0, 0)),
                  pl.BlockSpec((seq, LANE), lambda b, i, l: (0, 0))],
        out_specs=pl.BlockSpec((TQ, nsq), lambda b, i, l: (b * nq + i, 0)),
        out_shape=jax.ShapeDtypeStruct((t, nsq), BF16),
        scratch_shapes=[pltpu.VMEM((NSA_HEADS * TQ, 1), F32), pltpu.VMEM((NSA_HEADS * TQ, 1), F32),
                        pltpu.VMEM((NSA_HEADS * TQ, HEAD_DIM), F32)])

    tm_kv = 1024
    k_d, v_d = _call(
        _dsa_kv_kernel, "dsa_kv", lidx, [z_d, kv_norm, kv_up, cos128, sin128], grid=(t // tm_kv,),
        in_specs=[pl.BlockSpec((tm_kv, DSA_KV_RANK), lambda i, l: (i, 0)),
                  pl.BlockSpec((None, 1, DSA_KV_RANK), lambda i, l: (l[0], 0, 0)),
                  pl.BlockSpec((None, DSA_KV_RANK, 2 * HEAD_DIM), lambda i, l: (l[0], 0, 0)),
                  pl.BlockSpec((tm_kv, LANE), lambda i, l: (i % (seq // tm_kv), 0)),
                  pl.BlockSpec((tm_kv, LANE), lambda i, l: (i % (seq // tm_kv), 0))],
        out_specs=[pl.BlockSpec((tm_kv, HEAD_DIM), lambda i, l: (i, 0)),
                   pl.BlockSpec((tm_kv, HEAD_DIM), lambda i, l: (i, 0))],
        out_shape=[jax.ShapeDtypeStruct((t, HEAD_DIM), BF16)] * 2)
    dq = DSA_HEADS * HEAD_DIM
    iqw = IDX_HEADS * IDX_DIM
    o_dsa = _call(
        _dsa_kernel, "dsa_attn", lidx, [z_b, k_d, v_d, z_c, z_c, z_c, z_d], grid=(bsz, nq),
        in_specs=[pl.BlockSpec((TQ, dq), lambda b, i, l: (b * nq + i, 1)),
                  pl.BlockSpec((seq, HEAD_DIM), lambda b, i, l: (b, 0)),
                  pl.BlockSpec((seq, HEAD_DIM), lambda b, i, l: (b, 0)),
                  pl.BlockSpec((TQ, iqw), lambda b, i, l: (b * nq + i, 0)),
                  pl.BlockSpec((seq, LANE), lambda b, i, l: (b, iqw // LANE)),
                  pl.BlockSpec((seq, LANE), lambda b, i, l: (b, iqw // LANE + 1)),
                  pl.BlockSpec((TQ, LANE), lambda b, i, l: (b * nq + i, 2))],
        out_specs=pl.BlockSpec((TQ, dq), lambda b, i, l: (b * nq + i, 0)),
        out_shape=jax.ShapeDtypeStruct((t, dq), BF16),
        scratch_shapes=[pltpu.VMEM((seq // TK, TQ, TK), I32),
                        pltpu.VMEM((DSA_HEADS * TQ, 1), F32), pltpu.VMEM((DSA_HEADS * TQ, 1), F32),
                        pltpu.VMEM((DSA_HEADS * TQ, HEAD_DIM), F32)])

    tm, tn = 1024, 512
    ncol = d // tn
    mixed = _call(
        _merge_kernel, "merge", lidx, [hb, o_fox, o_nsa, o_dsa, w_gate, w_gate, w_gate, w_br_fox, w_br_nsa, w_br_dsa],
        grid=(t // tm, ncol),
        in_specs=[pl.BlockSpec((tm, d), lambda i, j, l: (i, 0)),
                  pl.BlockSpec((tm, fq), lambda i, j, l: (i, 0)),
                  pl.BlockSpec((tm, nsq), lambda i, j, l: (i, 0)),
                  pl.BlockSpec((tm, dq), lambda i, j, l: (i, 0)),
                  pl.BlockSpec((None, d, tn), lambda i, j, l: (l[0], 0, j)),
                  pl.BlockSpec((None, d, tn), lambda i, j, l: (l[0], 0, ncol + j)),
                  pl.BlockSpec((None, d, tn), lambda i, j, l: (l[0], 0, 2 * ncol + j)),
                  pl.BlockSpec((None, fq, tn), lambda i, j, l: (l[0], 0, j)),
                  pl.BlockSpec((None, nsq, tn), lambda i, j, l: (l[0], 0, j)),
                  pl.BlockSpec((None, dq, tn), lambda i, j, l: (l[0], 0, j))],
        out_specs=pl.BlockSpec((tm, tn), lambda i, j, l: (i, j)),
        out_shape=jax.ShapeDtypeStruct((t, d), BF16))

    h, hb = _proj_ln("out_ln", lidx, mixed, w_out, h, ln1_g, ln1_b, tk=1024)

    nff = D_FF // tn
    act = _call(
        _swiglu_kernel, "swiglu", lidx, [hb, w_ffn_in, w_ffn_in], grid=(t // tm, nff),
        in_specs=[pl.BlockSpec((tm, d), lambda i, j, l: (i, 0)),
                  pl.BlockSpec((None, d, tn), lambda i, j, l: (l[0], 0, j)),
                  pl.BlockSpec((None, d, tn), lambda i, j, l: (l[0], 0, nff + j))],
        out_specs=pl.BlockSpec((tm, tn), lambda i, j, l: (i, j)),
        out_shape=jax.ShapeDtypeStruct((t, D_FF), BF16))
    h, hb = _proj_ln("ffn_ln", lidx, act, w_ffn_out, h, ln2_g, ln2_b, tk=D_FF // 4)

    h, hb = _call(
        _ple_kernel, "ple", lidx, [hb, p, w_ple_in, w_ple_gate, h], grid=(t // tm, ncol),
        in_specs=[pl.BlockSpec((tm, d), lambda i, j, l: (i, 0)),
                  pl.BlockSpec((None, tm, PLE_DIM), lambda i, j, l: (l[0], i, 0)),
                  pl.BlockSpec((None, PLE_DIM, tn), lambda i, j, l: (l[0], 0, j)),
                  pl.BlockSpec((None, d, tn), lambda i, j, l: (l[0], 0, j)),
                  pl.BlockSpec((tm, tn), lambda i, j, l: (i, j))],
        out_specs=[pl.BlockSpec((tm, tn), lambda i, j, l: (i, j)),
                   pl.BlockSpec((tm, tn), lambda i, j, l: (i, j))],
        out_shape=[jax.ShapeDtypeStruct((t, d), F32), jax.ShapeDtypeStruct((t, d), BF16)])
    return h, hb


def kernel(x, p, w_in, fox_f_bias, nsa_pe_k, nsa_pe_v, nsa_cmp_k1, nsa_cmp_k2, nsa_cmp_v1, nsa_cmp_v2, dsa_kv_norm, dsa_kv_up, w_br_fox, w_br_nsa, w_br_dsa, w_out, ln1_g, ln1_b, w_ffn_in, w_ffn_out, ln2_g, ln2_b, w_ple_in, w_ple_gate):
    bsz, seq, d = x.shape
    depth = w_in.shape[0]
    t = bsz * seq
    assert d == D_MODEL and seq % 1024 == 0 and depth == DEPTH

    def seg(name):
        a, b = _IN_OFFSETS[name]
        return w_in[:, :, a:b]

    def zeros(n):
        return jnp.zeros((depth, d, n), w_in.dtype)

    cat = functools.partial(jnp.concatenate, axis=-1)
    w_a = cat([seg('fox_q'), seg('fox_k'), seg('fox_v'), seg('nsa_vs'), seg('nsa_vw')]).astype(BF16)
    w_b = cat([seg('nsa_q'), seg('nsa_ks'), seg('nsa_kw'), seg('dsa_q')]).astype(BF16)
    w_c = cat([seg('idx_q'), seg('idx_k'), zeros(IDX_DIM), zeros(IDX_DIM), seg('idx_k')]).astype(BF16)
    n_misc = FOX_HEADS + 3 * NSA_HEADS + IDX_HEADS
    w_d = cat([seg('dsa_ckv'), seg('fox_f'), seg('nsa_g'), seg('idx_w'), zeros(LANE - n_misc),
               seg('nsa_kc'), seg('nsa_vc')]).astype(BF16)
    w_gate = seg('gate').astype(BF16)

    fox_bias = jnp.pad(fox_f_bias, ((0, 0), (0, LANE - FOX_HEADS))).reshape(depth, 1, LANE)
    half = CMP_STRIDE * HEAD_DIM
    weights = (
        w_a, w_b, w_c, w_d, w_gate, fox_bias,
        nsa_pe_k.reshape(depth, 2, half), nsa_pe_v.reshape(depth, 2, half),
        nsa_cmp_k1.astype(BF16), nsa_cmp_k2.astype(BF16), nsa_cmp_v1.astype(BF16), nsa_cmp_v2.astype(BF16),
        dsa_kv_norm.reshape(depth, 1, DSA_KV_RANK), dsa_kv_up.astype(BF16),
        w_br_fox.astype(BF16), w_br_nsa.astype(BF16), w_br_dsa.astype(BF16), w_out.astype(BF16),
        ln1_g.reshape(depth, 1, d), ln1_b.reshape(depth, 1, d),
        w_ffn_in.astype(BF16), w_ffn_out.astype(BF16),
        ln2_g.reshape(depth, 1, d), ln2_b.reshape(depth, 1, d),
        p.reshape(depth, t, PLE_DIM).astype(BF16), w_ple_in.astype(BF16), w_ple_gate.astype(BF16),
    )

    cos, sin = _rope_tables(seq, HEAD_DIM)
    cos128 = cat([cos, cos])
    sin128 = cat([-sin, sin])
    cos_i, sin_i = _rope_tables(seq, IDX_DIM)
    zi = jnp.zeros_like(sin_i)
    cos64 = cat([cos_i, cos_i, cos_i, cos_i])
    sin64a = cat([zi, sin_i, zi, sin_i])
    sin64b = cat([-sin_i, zi, -sin_i, zi])
    n_cmp = (seq - CMP_LEN) // CMP_STRIDE + 1
    c_end = jnp.minimum(jnp.arange(N_CMP_PAD) * CMP_STRIDE + CMP_LEN - 1, seq - 1)
    cmp_cos, cmp_sin = cos128[c_end], sin128[c_end]
    n_slc = seq // SLC_LEN
    c_start = jnp.arange(N_CMP_PAD) * CMP_STRIDE
    s_start = jnp.arange(LANE) * SLC_LEN
    overlap = jnp.maximum(jnp.minimum(c_start[:, None] + CMP_LEN - 1, s_start[None, :] + SLC_LEN - 1)
                          - jnp.maximum(c_start[:, None], s_start[None, :]) + 1, 0).astype(F32) / CMP_LEN
    overlap = jnp.where((jnp.arange(N_CMP_PAD)[:, None] < n_cmp) & (jnp.arange(LANE)[None, :] < n_slc), overlap, 0.0).astype(BF16)
    block_of_key = (jnp.arange(seq)[:, None] // SLC_LEN == jnp.arange(LANE)[None, :]).astype(BF16)
    consts = (cos128, sin128, cos64, sin64a, sin64b, cmp_cos, cmp_sin, overlap, block_of_key)

    h = x.reshape(t, d)
    hb = h.astype(BF16)
    for layer in range(depth):
        lidx = jnp.full((1,), layer, I32)
        h, hb = _layer(lidx, h, hb, consts, weights, bsz, seq)
    return h.reshape(bsz, seq, d)
```

```python
import functools

import jax
import jax.numpy as jnp
from jax import lax
from jax.experimental import pallas as pl
from jax.experimental.pallas import tpu as pltpu

F32, BF16, I32 = jnp.float32, jnp.bfloat16, jnp.int32

D_MODEL = 2048
DEPTH = 4
HEAD_DIM = 128
ROPE_THETA = 10000.0
NEG_INF = -1e30
FOX_HEADS = 6
NSA_HEADS = 4
CMP_LEN = 32
CMP_STRIDE = 16
CMP_HIDDEN = 256
SLC_LEN = 64
SLC_TOPN = 16
WIN = 512
SLC_FORCE = 1e4
DSA_HEADS = 6
DSA_KV_RANK = 256
IDX_HEADS = 16
IDX_DIM = 64
IDX_TOPK_MAX = 256
D_FF = ((8 * D_MODEL + 3 * 256 - 1) // (3 * 256)) * 256
PLE_DIM = 256
ALPHA = (2 * DEPTH) ** 0.25
SCALE = HEAD_DIM ** -0.5

LANE = 128
VMEM_LIMIT_BYTES = 48 * 1024 * 1024

_IN_SPLITS = (
    ('fox_q', FOX_HEADS * HEAD_DIM), ('fox_k', FOX_HEADS * HEAD_DIM), ('fox_v', FOX_HEADS * HEAD_DIM), ('fox_f', FOX_HEADS),
    ('nsa_q', NSA_HEADS * HEAD_DIM), ('nsa_kc', HEAD_DIM), ('nsa_vc', HEAD_DIM), ('nsa_ks', HEAD_DIM), ('nsa_vs', HEAD_DIM),
    ('nsa_kw', HEAD_DIM), ('nsa_vw', HEAD_DIM), ('nsa_g', 3 * NSA_HEADS),
    ('dsa_q', DSA_HEADS * HEAD_DIM), ('dsa_ckv', DSA_KV_RANK), ('idx_q', IDX_HEADS * IDX_DIM), ('idx_k', IDX_DIM), ('idx_w', IDX_HEADS),
    ('gate', 3 * D_MODEL),
)
_IN_OFFSETS = {}
_off = 0
for _name, _width in _IN_SPLITS:
    _IN_OFFSETS[_name] = (_off, _off + _width)
    _off += _width

MISC_FOX_F = 0
MISC_NSA_G = FOX_HEADS
MISC_IDX_W = FOX_HEADS + 3 * NSA_HEADS

TQ = 256
TK = 256
N_CMP_PAD = 128


def _dot(a, b):
    return jnp.dot(a, b, preferred_element_type=F32)


def _dot_nt(a, b):
    return lax.dot_general(a, b, (((1,), (1,)), ((), ())), preferred_element_type=F32)


def _split3(a):
    a1 = a.astype(BF16)
    r1 = a - a1.astype(F32)
    a2 = r1.astype(BF16)
    a3 = (r1 - a2.astype(F32)).astype(BF16)
    return a1, a2, a3


def _call(kernel, name, lidx, args, grid, in_specs, out_specs, out_shape, scratch_shapes=()):
    return pl.pallas_call(
        kernel,
        grid_spec=pltpu.PrefetchScalarGridSpec(
            num_scalar_prefetch=1, grid=grid, in_specs=in_specs, out_specs=out_specs,
            scratch_shapes=list(scratch_shapes)),
        out_shape=out_shape,
        compiler_params=pltpu.CompilerParams(
            dimension_semantics=("arbitrary",) * len(grid), vmem_limit_bytes=VMEM_LIMIT_BYTES),
        name=name,
    )(lidx, *args)


def _proj_kernel(l_ref, x_ref, w_ref, *refs, shifts):
    o_ref = refs[-1]
    acc = _dot(x_ref[...], w_ref[...])
    if not shifts:
        o_ref[...] = acc.astype(o_ref.dtype)
        return
    c_ref = refs[0]
    s_refs = refs[1:-1]
    for g in range(acc.shape[1] // LANE):
        xg = acc[:, g * LANE:(g + 1) * LANE]
        out = xg * c_ref[...]
        for shift, s_ref in zip(shifts, s_refs):
            out = out + pltpu.roll(xg, shift, 1) * s_ref[...]
        o_ref[:, g * LANE:(g + 1) * LANE] = out.astype(o_ref.dtype)


def _proj(name, lidx, hb, w, tn, out_dtype, seq, rope=None):
    t, d = hb.shape
    n = w.shape[2]
    tm = 1024
    in_specs = [pl.BlockSpec((tm, d), lambda i, j, l: (i, 0)),
                pl.BlockSpec((None, d, tn), lambda i, j, l: (l[0], 0, j))]
    args = [hb, w]
    shifts = ()
    if rope is not None:
        shifts, tables = rope[0], rope[1:]
        nrow = seq // tm
        for tab in tables:
            in_specs.append(pl.BlockSpec((tm, LANE), lambda i, j, l: (i % nrow, 0)))
            args.append(tab)
    return _call(functools.partial(_proj_kernel, shifts=shifts), name, lidx, args,
                 grid=(t // tm, n // tn), in_specs=in_specs,
                 out_specs=pl.BlockSpec((tm, tn), lambda i, j, l: (i, j)),
                 out_shape=jax.ShapeDtypeStruct((t, n), out_dtype))


def _cum_kernel(l_ref, f_ref, bias_ref, o_ref):
    chunk = 256
    row = lax.broadcasted_iota(I32, (chunk, chunk), 0)
    col = lax.broadcasted_iota(I32, (chunk, chunk), 1)
    tri = jnp.where(row >= col, 1.0, 0.0).astype(BF16)
    carry = jnp.zeros((1, LANE), F32)
    for c in range(f_ref.shape[0] // chunk):
        logf = jax.nn.log_sigmoid(f_ref[c * chunk:(c + 1) * chunk, :] + bias_ref[...])
        l1, l2, l3 = _split3(logf)
        cs = _dot(tri, l1) + _dot(tri, l2) + _dot(tri, l3) + carry
        o_ref[c * chunk:(c + 1) * chunk, :] = cs
        carry = cs[chunk - 1:chunk, :]


def _max_lanes(x):
    out = x[:, 0:LANE]
    for g in range(1, x.shape[1] // LANE):
        out = jnp.maximum(out, x[:, g * LANE:(g + 1) * LANE])
    return out


def _tile_lanes(x, n):
    return jnp.concatenate([x] * n, axis=1)


def _row_max_to_lanes(mx):
    return jnp.broadcast_to(jnp.max(mx, axis=-1, keepdims=True), mx.shape)


def _exp_pv(s, m_lanes, v):
    p = jnp.exp(s - _tile_lanes(m_lanes, s.shape[1] // LANE)).astype(BF16)
    return _dot(p, jnp.concatenate([v, jnp.ones_like(v)], axis=1))


def _fox_kernel(l_ref, q_ref, k_ref, v_ref, cq_ref, ck_ref, o_ref, s_ref, mx_ref, cqb_ref, acc_ref):
    qi = pl.program_id(1)
    nh = FOX_HEADS
    nchunk = ck_ref.shape[0] // 8
    row = lax.broadcasted_iota(I32, (TQ, TK), 0)
    col = lax.broadcasted_iota(I32, (TQ, TK), 1)
    cq_all = cq_ref[...]
    for h in range(nh):
        cqb_ref[h] = jnp.broadcast_to(cq_all[:, h:h + 1], (TQ, LANE))
    mx_ref[...] = jnp.full(mx_ref.shape, NEG_INF, F32)
    acc_ref[...] = jnp.zeros(acc_ref.shape, F32)

    def score_step(j, diagonal):
        start = pl.multiple_of(j * TK, TK)
        for h in range(nh):
            hs = slice(h * HEAD_DIM, (h + 1) * HEAD_DIM)
            s = (_dot_nt(q_ref[:, hs], k_ref[pl.ds(start, TK), hs]) * SCALE
                 + _tile_lanes(cqb_ref[h], TK // LANE) - ck_ref[pl.ds(h * nchunk + j, 1), :])
            if diagonal:
                s = jnp.where(col <= row, s, NEG_INF)
            s_ref[h, j] = s
            mx_ref[h] = jnp.maximum(mx_ref[h], _max_lanes(s))

    def score_body(j, carry):
        score_step(j, False)
        return carry

    lax.fori_loop(0, qi, score_body, 0)
    score_step(qi, True)
    for h in range(nh):
        mx_ref[h] = _row_max_to_lanes(mx_ref[h])

    def pv_body(j, carry):
        start = pl.multiple_of(j * TK, TK)
        for h in range(nh):
            hs = slice(h * HEAD_DIM, (h + 1) * HEAD_DIM)
            acc_ref[h] += _exp_pv(s_ref[h, j], mx_ref[h], v_ref[pl.ds(start, TK), hs])
        return carry

    lax.fori_loop(0, qi + 1, pv_body, 0)
    for h in range(nh):
        a = acc_ref[h]
        o_ref[:, h * HEAD_DIM:(h + 1) * HEAD_DIM] = (a[:, :HEAD_DIM] / a[:, HEAD_DIM:]).astype(o_ref.dtype)


def _cmp_kernel(l_ref, rk_ref, rv_ref, pek_ref, pev_ref, wk1_ref, wk2_ref, wv1_ref, wv2_ref,
                c_ref, s_ref, ko_ref, vo_ref):
    half = CMP_STRIDE * HEAD_DIM

    def compress(r_ref, pe_ref, w1_ref, w2_ref):
        r = r_ref[...]
        lo = (r + pe_ref[0:1, :]).astype(BF16)
        hi = (r + pe_ref[1:2, :]).astype(BF16)
        a = _dot(lo, w1_ref[0:half, :])
        b = _dot(hi, w1_ref[half:2 * half, :])
        hid = a + pltpu.roll(b, N_CMP_PAD - 1, 0)
        return _dot(jax.nn.gelu(hid).astype(BF16), w2_ref[...])

    kc = compress(rk_ref, pek_ref, wk1_ref, wk2_ref)
    kc = kc * c_ref[...] + pltpu.roll(kc, HEAD_DIM // 2, 1) * s_ref[...]
    ko_ref[...] = kc.astype(ko_ref.dtype)
    vo_ref[...] = compress(rv_ref, pev_ref, wv1_ref, wv2_ref).astype(vo_ref.dtype)


def _masked_softmax(s, mask):
    s = jnp.where(mask, s, NEG_INF)
    m = jnp.max(s, axis=-1, keepdims=True)
    e = jnp.where(mask, jnp.exp(s - m), 0.0)
    den = jnp.sum(e, axis=-1, keepdims=True)
    return e, jnp.where(den > 0.0, den, 1.0)


def _nsa_kernel(l_ref, q_ref, ks_ref, kw_ref, vs_ref, vw_ref, kc_ref, vc_ref, g_ref, ovt_ref, et_ref,
                o_ref, q_sc, s_ref, mx_ref, acc_ref):
    qi = pl.program_id(1)
    nh = NSA_HEADS
    for h in range(nh):
        q_sc[h * TQ:(h + 1) * TQ, :] = q_ref[:, h * HEAD_DIM:(h + 1) * HEAD_DIM]
    q = q_sc[...]
    pos = qi * TQ + lax.broadcasted_iota(I32, (TQ, 1), 0)
    lane = lax.broadcasted_iota(I32, (TQ, LANE), 1)

    cvis = (lane * CMP_STRIDE + (CMP_LEN - 1)) <= pos
    s_c = (_dot_nt(q, kc_ref[...]) * SCALE).reshape(nh, TQ, N_CMP_PAD)
    e_c, den_c = _masked_softmax(s_c, cvis[None])
    p_c = e_c / den_c
    o_cmp = _dot(p_c.reshape(nh * TQ, N_CMP_PAD).astype(BF16), vc_ref[...])

    p_sum = p_c[0]
    for h in range(1, nh):
        p_sum = p_sum + p_c[h]
    p1, p2, p3 = _split3(p_sum)
    ovt = ovt_ref[...]
    n_slc = et_ref.shape[0] // SLC_LEN
    imp = (_dot_nt(ovt, p1) + _dot_nt(ovt, p2) + _dot_nt(ovt, p3))[0:n_slc, :]
    blk = lax.broadcasted_iota(I32, (n_slc, TQ), 0)
    pos_t = qi * TQ + lax.broadcasted_iota(I32, (n_slc, TQ), 1)
    cur = lax.shift_right_logical(pos_t, SLC_LEN.bit_length() - 1)
    forced = (blk == 0) | (blk == cur) | (blk == cur - 1)
    imp = jnp.where(forced, SLC_FORCE, jnp.where(blk * SLC_LEN <= pos_t, imp, -SLC_FORCE))
    rank = jnp.zeros((n_slc, TQ), F32)
    for k in range(n_slc):
        ik = imp[k:k + 1, :]
        ahead = (ik > imp) | ((ik == imp) & (blk > k))
        rank = rank + jnp.where(ahead, 1.0, 0.0)
    sel_t = jnp.where(rank < float(min(SLC_TOPN, n_slc)), 1.0, 0.0)
    sel_t = jnp.concatenate([sel_t, jnp.zeros((LANE - n_slc, TQ), F32)], axis=0)
    sel = sel_t.T.astype(BF16)

    mx_ref[...] = jnp.full(mx_ref.shape, NEG_INF, F32)
    acc_ref[...] = jnp.zeros(acc_ref.shape, F32)
    row = lax.broadcasted_iota(I32, (TQ, TK), 0)
    col = lax.broadcasted_iota(I32, (TQ, TK), 1)

    def score_step(j, diagonal):
        start = pl.multiple_of(j * TK, TK)
        vis = _dot_nt(sel, et_ref[pl.ds(start, TK), :]) > 0.5
        if diagonal:
            vis = vis & (col <= row)
        s = (_dot_nt(q_sc[...], ks_ref[pl.ds(start, TK), :]) * SCALE).reshape(nh, TQ, TK)
        s = jnp.where(vis[None], s, NEG_INF).reshape(nh * TQ, TK)
        s_ref[j] = s
        mx_ref[...] = jnp.maximum(mx_ref[...], _max_lanes(s))

    def score_body(j, carry):
        score_step(j, False)
        return carry

    lax.fori_loop(0, qi, score_body, 0)
    score_step(qi, True)
    mx_ref[...] = _row_max_to_lanes(mx_ref[...])

    def pv_body(j, carry):
        start = pl.multiple_of(j * TK, TK)
        acc_ref[...] += _exp_pv(s_ref[j], mx_ref[...], vs_ref[pl.ds(start, TK), :])
        return carry

    lax.fori_loop(0, qi + 1, pv_body, 0)
    a_slc = acc_ref[...]
    o_slc = a_slc[:, :HEAD_DIM] / a_slc[:, HEAD_DIM:]

    band = WIN + TQ
    start_w = pl.multiple_of(jnp.maximum(qi * TQ - WIN, 0), TQ)
    kwb = kw_ref[pl.ds(start_w, band), :]
    vwb = vw_ref[pl.ds(start_w, band), :]
    dist = pos - (start_w + lax.broadcasted_iota(I32, (TQ, band), 1))
    wvis = (dist >= 0) & (dist < WIN)
    s_w = (_dot_nt(q, kwb) * SCALE).reshape(nh, TQ, band)
    e_w, den_w = _masked_softmax(s_w, wvis[None])
    o_win = _dot(e_w.reshape(nh * TQ, band).astype(BF16), vwb) / den_w.reshape(nh * TQ, 1)

    gate = jax.nn.sigmoid(g_ref[...])
    for h in range(nh):
        rows = slice(h * TQ, (h + 1) * TQ)
        g0 = gate[:, MISC_NSA_G + 3 * h:MISC_NSA_G + 3 * h + 1]
        g1 = gate[:, MISC_NSA_G + 3 * h + 1:MISC_NSA_G + 3 * h + 2]
        g2 = gate[:, MISC_NSA_G + 3 * h + 2:MISC_NSA_G + 3 * h + 3]
        out = g0 * o_cmp[rows] + g1 * o_slc[rows] + g2 * o_win[rows]
        o_ref[:, h * HEAD_DIM:(h + 1) * HEAD_DIM] = out.astype(o_ref.dtype)


def _dsa_kv_kernel(l_ref, x_ref, g_ref, up_ref, c_ref, s_ref, ko_ref, vo_ref):
    x = x_ref[...]
    r = x * lax.rsqrt(jnp.mean(jnp.square(x), axis=-1, keepdims=True) + 1e-6) * g_ref[...]
    kv = _dot(r.astype(BF16), up_ref[...])
    k = kv[:, :HEAD_DIM]
    k = k * c_ref[...] + pltpu.roll(k, HEAD_DIM // 2, 1) * s_ref[...]
    ko_ref[...] = k.astype(ko_ref.dtype)
    vo_ref[...] = kv[:, HEAD_DIM:].astype(vo_ref.dtype)


def _dsa_kernel(l_ref, q_ref, kd_ref, vd_ref, iq_ref, ika_ref, ikb_ref, w_ref, o_ref,
                key_ref, keyt_ref, wb_ref, sel_ref, q_sc, s_ref, mx_ref, acc_ref):
    qi = pl.program_id(1)
    nh = DSA_HEADS
    seq = kd_ref.shape[0]
    k_top = float(min(IDX_TOPK_MAX, seq // 4))
    row = lax.broadcasted_iota(I32, (TQ, TK), 0)
    col = lax.broadcasted_iota(I32, (TQ, TK), 1)
    int_min = jnp.int32(-2 ** 31)
    idx_bits = (seq - 1).bit_length()

    w_all = w_ref[...] * (IDX_HEADS ** -0.5 * IDX_DIM ** -0.5)
    for h in range(IDX_HEADS):
        wb_ref[h] = jnp.broadcast_to(w_all[:, MISC_IDX_W + h:MISC_IDX_W + h + 1], (TQ, LANE))

    def score_step(j, diagonal):
        start = pl.multiple_of(j * TK, TK)
        ka = ika_ref[pl.ds(start, TK), :]
        kb = ikb_ref[pl.ds(start, TK), :]
        sc = jnp.zeros((TQ, TK), F32)
        for g in range(IDX_HEADS // 2):
            qpair = iq_ref[:, g * LANE:(g + 1) * LANE]
            wa = _tile_lanes(wb_ref[2 * g], TK // LANE)
            wb = _tile_lanes(wb_ref[2 * g + 1], TK // LANE)
            sc = sc + jnp.maximum(_dot_nt(qpair, ka), 0.0) * wa + jnp.maximum(_dot_nt(qpair, kb), 0.0) * wb
        if diagonal:
            sc = jnp.where(col <= row, sc, NEG_INF)
        bits = lax.bitcast_convert_type(sc, I32)
        key = jnp.where(bits < 0, bits ^ jnp.int32(0x7FFFFFFF), bits)
        key = jnp.where(sc == 0.0, 0, key)
        key_ref[j] = key
        keyt_ref[j] = key.T

    def score_body(j, carry):
        score_step(j, False)
        return carry

    lax.fori_loop(0, qi, score_body, 0)
    score_step(qi, True)

    krow = lax.broadcasted_iota(I32, (TK, TQ), 0)

    def count(pred_fn):
        def body(j, acc):
            hit = jnp.where(pred_fn(keyt_ref[j], j), 1.0, 0.0)
            return acc + jnp.sum(hit.reshape(TK // 8, 8, TQ), axis=0)
        part = lax.fori_loop(0, qi + 1, body, jnp.zeros((8, TQ), F32))
        return jnp.sum(part, axis=0, keepdims=True)

    thr = jnp.where(count(lambda key, j: key >= 0) >= k_top, jnp.int32(0), int_min)

    def value_bit(i, thr):
        cand = thr + (jnp.int32(1) << (30 - i))
        return jnp.where(count(lambda key, j: key >= cand) >= k_top, cand, thr)

    thr = lax.fori_loop(0, 31, value_bit, thr)
    n_gt = count(lambda key, j: key > thr)
    n_ge = count(lambda key, j: key >= thr)

    def index_bisect():
        def index_bit(i, cut):
            cand = cut + (jnp.int32(1) << (idx_bits - 1 - i))
            n_before = count(lambda key, j: (key == thr) & (j * TK + krow < cand))
            return jnp.where(n_gt + n_before < k_top, cand, cut)
        return lax.fori_loop(0, idx_bits, index_bit, jnp.zeros((1, TQ), I32))

    def take_all_ties():
        return jnp.full((1, TQ), seq - 1, I32)

    cut = lax.cond(jnp.max(n_ge) > k_top, index_bisect, take_all_ties)
    sel_ref[0] = jnp.broadcast_to(thr, (LANE, TQ)).T
    sel_ref[1] = jnp.broadcast_to(cut, (LANE, TQ)).T

    for h in range(nh):
        q_sc[h * TQ:(h + 1) * TQ, :] = q_ref[:, h * HEAD_DIM:(h + 1) * HEAD_DIM]
    mx_ref[...] = jnp.full(mx_ref.shape, NEG_INF, F32)
    acc_ref[...] = jnp.zeros(acc_ref.shape, F32)

    def attn_score_step(j, diagonal):
        start = pl.multiple_of(j * TK, TK)
        key = key_ref[j]
        thr_t = _tile_lanes(sel_ref[0], TK // LANE)
        cut_t = _tile_lanes(sel_ref[1], TK // LANE)
        vis = (key > thr_t) | ((key == thr_t) & (j * TK + col <= cut_t))
        if diagonal:
            vis = vis & (col <= row)
        s = (_dot_nt(q_sc[...], kd_ref[pl.ds(start, TK), :]) * SCALE).reshape(nh, TQ, TK)
        s = jnp.where(vis[None], s, NEG_INF).reshape(nh * TQ, TK)
        s_ref[j] = s
        mx_ref[...] = jnp.maximum(mx_ref[...], _max_lanes(s))

    def attn_score_body(j, carry):
        attn_score_step(j, False)
        return carry

    lax.fori_loop(0, qi, attn_score_body, 0)
    attn_score_step(qi, True)
    mx_ref[...] = _row_max_to_lanes(mx_ref[...])

    def pv_body(j, carry):
        start = pl.multiple_of(j * TK, TK)
        acc_ref[...] += _exp_pv(s_ref[j], mx_ref[...], vd_ref[pl.ds(start, TK), :])
        return carry

    lax.fori_loop(0, qi + 1, pv_body, 0)
    a = acc_ref[...]
    out = a[:, :HEAD_DIM] / a[:, HEAD_DIM:]
    for h in range(nh):
        o_ref[:, h * HEAD_DIM:(h + 1) * HEAD_DIM] = out[h * TQ:(h + 1) * TQ].astype(o_ref.dtype)


def _merge_kernel(l_ref, hb_ref, of_ref, on_ref, od_ref, wg1_ref, wg2_ref, wg3_ref, wf_ref, wn_ref, wd_ref, o_ref):
    hb = hb_ref[...]
    mixed = (jax.nn.sigmoid(_dot(hb, wg1_ref[...])) * _dot(of_ref[...], wf_ref[...])
             + jax.nn.sigmoid(_dot(hb, wg2_ref[...])) * _dot(on_ref[...], wn_ref[...])
             + jax.nn.sigmoid(_dot(hb, wg3_ref[...])) * _dot(od_ref[...], wd_ref[...]))
    o_ref[...] = mixed.astype(o_ref.dtype)


def _proj_ln_kernel(l_ref, x_ref, w_ref, h_ref, g_ref, b_ref, of_ref, ob_ref, acc_ref, *, nk):
    k = pl.program_id(1)

    @pl.when(k == 0)
    def _():
        acc_ref[...] = ALPHA * h_ref[...]

    acc_ref[...] += _dot(x_ref[...], w_ref[...])

    @pl.when(k == nk - 1)
    def _():
        y = acc_ref[...]
        mu = jnp.mean(y, axis=-1, keepdims=True)
        yc = y - mu
        var = jnp.mean(jnp.square(yc), axis=-1, keepdims=True)
        out = yc * lax.rsqrt(var + 1e-5) * g_ref[...] + b_ref[...]
        of_ref[...] = out
        ob_ref[...] = out.astype(ob_ref.dtype)


def _proj_ln(name, lidx, x, w, h, g, b, tk):
    t, kdim = x.shape
    d = h.shape[1]
    tm = 512
    nk = kdim // tk
    return _call(functools.partial(_proj_ln_kernel, nk=nk), name, lidx, [x, w, h, g, b],
                 grid=(t // tm, nk),
                 in_specs=[pl.BlockSpec((tm, tk), lambda i, k, l: (i, k)),
                           pl.BlockSpec((None, tk, d), lambda i, k, l: (l[0], k, 0)),
                           pl.BlockSpec((tm, d), lambda i, k, l: (i, 0)),
                           pl.BlockSpec((None, 1, d), lambda i, k, l: (l[0], 0, 0)),
                           pl.BlockSpec((None, 1, d), lambda i, k, l: (l[0], 0, 0))],
                 out_specs=[pl.BlockSpec((tm, d), lambda i, k, l: (i, 0)),
                            pl.BlockSpec((tm, d), lambda i, k, l: (i, 0))],
                 out_shape=[jax.ShapeDtypeStruct((t, d), F32), jax.ShapeDtypeStruct((t, d), BF16)],
                 scratch_shapes=[pltpu.VMEM((tm, d), F32)])


def _swiglu_kernel(l_ref, x_ref, wa_ref, wb_ref, o_ref):
    x = x_ref[...]
    a = _dot(x, wa_ref[...])
    o_ref[...] = (jax.nn.silu(a) * _dot(x, wb_ref[...])).astype(o_ref.dtype)


def _ple_kernel(l_ref, hb_ref, p_ref, wpi_ref, wpg_ref, h_ref, of_ref, ob_ref):
    out = h_ref[...] + _dot(p_ref[...], wpi_ref[...]) * jax.nn.sigmoid(_dot(hb_ref[...], wpg_ref[...]))
    of_ref[...] = out
    ob_ref[...] = out.astype(ob_ref.dtype)


def _rope_tables(n, dim):
    inv = 1.0 / (ROPE_THETA ** (jnp.arange(0, dim, 2, dtype=F32) / dim))
    ang = jnp.arange(n, dtype=F32)[:, None] * inv[None, :]
    return jnp.cos(ang), jnp.sin(ang)


def _layer(lidx, h, hb, consts, weights, bsz, seq):
    (cos128, sin128, cos64, sin64a, sin64b, cmp_cos, cmp_sin, overlap, block_of_key) = consts
    (w_a, w_b, w_c, w_d, w_gate, fox_bias, pe_k, pe_v, wk1, wk2, wv1, wv2, kv_norm, kv_up,
     w_br_fox, w_br_nsa, w_br_dsa, w_out, ln1_g, ln1_b, w_ffn_in, w_ffn_out, ln2_g, ln2_b, p, w_ple_in, w_ple_gate) = weights
    t = bsz * seq
    nq = seq // TQ
    d = D_MODEL

    z_a = _proj("proj_plain", lidx, hb, w_a, 512, BF16, seq)
    z_b = _proj("proj_rope128", lidx, hb, w_b, 512, BF16, seq, rope=((HEAD_DIM // 2,), cos128, sin128))
    z_c = _proj("proj_rope64", lidx, hb, w_c, 640, BF16, seq,
                rope=((IDX_DIM // 2, LANE - IDX_DIM // 2), cos64, sin64a, sin64b))
    z_d = _proj("proj_f32", lidx, hb, w_d, 640, F32, seq)

    cum = _call(_cum_kernel, "fox_cum", lidx, [z_d, fox_bias], grid=(bsz,),
                in_specs=[pl.BlockSpec((seq, LANE), lambda b, l: (b, 2)),
                          pl.BlockSpec((None, 1, LANE), lambda b, l: (l[0], 0, 0))],
                out_specs=pl.BlockSpec((seq, LANE), lambda b, l: (b, 0)),
                out_shape=jax.ShapeDtypeStruct((t, LANE), F32))
    cum_rows = jnp.transpose(cum.reshape(bsz, seq, LANE)[:, :, :8], (0, 2, 1)).reshape(bsz, 8 * (seq // TK), TK)
    fq = FOX_HEADS * HEAD_DIM
    o_fox = _call(_fox_kernel, "fox_attn", lidx, [z_a, z_a, z_a, cum, cum_rows], grid=(bsz, nq),
                  in_specs=[pl.BlockSpec((TQ, fq), lambda b, i, l: (b * nq + i, 0)),
                            pl.BlockSpec((seq, fq), lambda b, i, l: (b, 1)),
                            pl.BlockSpec((seq, fq), lambda b, i, l: (b, 2)),
                            pl.BlockSpec((TQ, LANE), lambda b, i, l: (b * nq + i, 0)),
                            pl.BlockSpec((None, 8 * (seq // TK), TK), lambda b, i, l: (b, 0, 0))],
                  out_specs=pl.BlockSpec((TQ, fq), lambda b, i, l: (b * nq + i, 0)),
                  out_shape=jax.ShapeDtypeStruct((t, fq), BF16),
                  scratch_shapes=[pltpu.VMEM((FOX_HEADS, seq // TK, TQ, TK), F32),
                                  pltpu.VMEM((FOX_HEADS, TQ, LANE), F32), pltpu.VMEM((FOX_HEADS, TQ, LANE), F32),
                                  pltpu.VMEM((FOX_HEADS, TQ, 2 * HEAD_DIM), F32)])

    nchunk = seq // CMP_STRIDE
    half = CMP_STRIDE * HEAD_DIM
    r_k = z_d[:, 3 * LANE:4 * LANE].reshape(bsz * nchunk, half)
    r_v = z_d[:, 4 * LANE:5 * LANE].reshape(bsz * nchunk, half)
    k_cmp, v_cmp = _call(
        _cmp_kernel, "nsa_compress", lidx, [r_k, r_v, pe_k, pe_v, wk1, wk2, wv1, wv2, cmp_cos, cmp_sin], grid=(bsz,),
        in_specs=[pl.BlockSpec((nchunk, half), lambda b, l: (b, 0)),
                  pl.BlockSpec((nchunk, half), lambda b, l: (b, 0)),
                  pl.BlockSpec((None, 2, half), lambda b, l: (l[0], 0, 0)),
                  pl.BlockSpec((None, 2, half), lambda b, l: (l[0], 0, 0)),
                  pl.BlockSpec((None, 2 * half, CMP_HIDDEN), lambda b, l: (l[0], 0, 0)),
                  pl.BlockSpec((None, CMP_HIDDEN, HEAD_DIM), lambda b, l: (l[0], 0, 0)),
                  pl.BlockSpec((None, 2 * half, CMP_HIDDEN), lambda b, l: (l[0], 0, 0)),
                  pl.BlockSpec((None, CMP_HIDDEN, HEAD_DIM), lambda b, l: (l[0], 0, 0)),
                  pl.BlockSpec((nchunk, HEAD_DIM), lambda b, l: (0, 0)),
                  pl.BlockSpec((nchunk, HEAD_DIM), lambda b, l: (0, 0))],
        out_specs=[pl.BlockSpec((nchunk, HEAD_DIM), lambda b, l: (b, 0)),
                   pl.BlockSpec((nchunk, HEAD_DIM), lambda b, l: (b, 0))],
        out_shape=[jax.ShapeDtypeStruct((bsz * nchunk, HEAD_DIM), BF16)] * 2)
    nsq = NSA_HEADS * HEAD_DIM
    o_nsa = _call(
        _nsa_kernel, "nsa_attn", lidx, [z_b, z_b, z_b, z_a, z_a, k_cmp, v_cmp, z_d, overlap, block_of_key], grid=(bsz, nq),
        in_specs=[pl.BlockSpec((TQ, nsq), lambda b, i, l: (b * nq + i, 0)),
                  pl.BlockSpec((seq, LANE), lambda b, i, l: (b, 4)),
                  pl.BlockSpec((seq, LANE), lambda b, i, l: (b, 5)),
                  pl.BlockSpec((seq, LANE), lambda b, i, l: (b, 18)),
                  pl.BlockSpec((seq, LANE), lambda b, i, l: (b, 19)),
                  pl.BlockSpec((nchunk, HEAD_DIM), lambda b, i, l: (b, 0)),
                  pl.BlockSpec((nchunk, HEAD_DIM), lambda b, i, l: (b, 0)),
                  pl.BlockSpec((TQ, LANE), lambda b, i, l: (b * nq + i, 2)),
                  pl.BlockSpec((N_CMP_PAD, LANE), lambda b, i, l: (0, 0)),
                  pl.BlockSpec((seq, LANE), lambda b, i, l: (0, 0))],
        out_specs=pl.BlockSpec((TQ, nsq), lambda b, i, l: (b * nq + i, 0)),
        out_shape=jax.ShapeDtypeStruct((t, nsq), BF16),
        scratch_shapes=[pltpu.VMEM((NSA_HEADS * TQ, HEAD_DIM), BF16),
                        pltpu.VMEM((seq // TK, NSA_HEADS * TQ, TK), F32),
                        pltpu.VMEM((NSA_HEADS * TQ, LANE), F32),
                        pltpu.VMEM((NSA_HEADS * TQ, 2 * HEAD_DIM), F32)])

    tm_kv = 1024
    k_d, v_d = _call(
        _dsa_kv_kernel, "dsa_kv", lidx, [z_d, kv_norm, kv_up, cos128, sin128], grid=(t // tm_kv,),
        in_specs=[pl.BlockSpec((tm_kv, DSA_KV_RANK), lambda i, l: (i, 0)),
                  pl.BlockSpec((None, 1, DSA_KV_RANK), lambda i, l: (l[0], 0, 0)),
                  pl.BlockSpec((None, DSA_KV_RANK, 2 * HEAD_DIM), lambda i, l: (l[0], 0, 0)),
                  pl.BlockSpec((tm_kv, LANE), lambda i, l: (i % (seq // tm_kv), 0)),
                  pl.BlockSpec((tm_kv, LANE), lambda i, l: (i % (seq // tm_kv), 0))],
        out_specs=[pl.BlockSpec((tm_kv, HEAD_DIM), lambda i, l: (i, 0)),
                   pl.BlockSpec((tm_kv, HEAD_DIM), lambda i, l: (i, 0))],
        out_shape=[jax.ShapeDtypeStruct((t, HEAD_DIM), BF16)] * 2)
    dq = DSA_HEADS * HEAD_DIM
    iqw = IDX_HEADS * IDX_DIM
    o_dsa = _call(
        _dsa_kernel, "dsa_attn", lidx, [z_b, k_d, v_d, z_c, z_c, z_c, z_d], grid=(bsz, nq),
        in_specs=[pl.BlockSpec((TQ, dq), lambda b, i, l: (b * nq + i, 1)),
                  pl.BlockSpec((seq, HEAD_DIM), lambda b, i, l: (b, 0)),
                  pl.BlockSpec((seq, HEAD_DIM), lambda b, i, l: (b, 0)),
                  pl.BlockSpec((TQ, iqw), lambda b, i, l: (b * nq + i, 0)),
                  pl.BlockSpec((seq, LANE), lambda b, i, l: (b, iqw // LANE)),
                  pl.BlockSpec((seq, LANE), lambda b, i, l: (b, iqw // LANE + 1)),
                  pl.BlockSpec((TQ, LANE), lambda b, i, l: (b * nq + i, 2))],
        out_specs=pl.BlockSpec((TQ, dq), lambda b, i, l: (b * nq + i, 0)),
        out_shape=jax.ShapeDtypeStruct((t, dq), BF16),
        scratch_shapes=[pltpu.VMEM((seq // TK, TQ, TK), I32), pltpu.VMEM((seq // TK, TK, TQ), I32),
                        pltpu.VMEM((IDX_HEADS, TQ, LANE), F32), pltpu.VMEM((2, TQ, LANE), I32),
                        pltpu.VMEM((DSA_HEADS * TQ, HEAD_DIM), BF16),
                        pltpu.VMEM((seq // TK, DSA_HEADS * TQ, TK), F32),
                        pltpu.VMEM((DSA_HEADS * TQ, LANE), F32),
                        pltpu.VMEM((DSA_HEADS * TQ, 2 * HEAD_DIM), F32)])

    tm, tn = 1024, 512
    ncol = d // tn
    mixed = _call(
        _merge_kernel, "merge", lidx, [hb, o_fox, o_nsa, o_dsa, w_gate, w_gate, w_gate, w_br_fox, w_br_nsa, w_br_dsa],
        grid=(t // tm, ncol),
        in_specs=[pl.BlockSpec((tm, d), lambda i, j, l: (i, 0)),
                  pl.BlockSpec((tm, fq), lambda i, j, l: (i, 0)),
                  pl.BlockSpec((tm, nsq), lambda i, j, l: (i, 0)),
                  pl.BlockSpec((tm, dq), lambda i, j, l: (i, 0)),
                  pl.BlockSpec((None, d, tn), lambda i, j, l: (l[0], 0, j)),
                  pl.BlockSpec((None, d, tn), lambda i, j, l: (l[0], 0, ncol + j)),
                  pl.BlockSpec((None, d, tn), lambda i, j, l: (l[0], 0, 2 * ncol + j)),
                  pl.BlockSpec((None, fq, tn), lambda i, j, l: (l[0], 0, j)),
                  pl.BlockSpec((None, nsq, tn), lambda i, j, l: (l[0], 0, j)),
                  pl.BlockSpec((None, dq, tn), lambda i, j, l: (l[0], 0, j))],
        out_specs=pl.BlockSpec((tm, tn), lambda i, j, l: (i, j)),
        out_shape=jax.ShapeDtypeStruct((t, d), BF16))

    h, hb = _proj_ln("out_ln", lidx, mixed, w_out, h, ln1_g, ln1_b, tk=1024)

    nff = D_FF // tn
    act = _call(
        _swiglu_kernel, "swiglu", lidx, [hb, w_ffn_in, w_ffn_in], grid=(t // tm, nff),
        in_specs=[pl.BlockSpec((tm, d), lambda i, j, l: (i, 0)),
                  pl.BlockSpec((None, d, tn), lambda i, j, l: (l[0], 0, j)),
                  pl.BlockSpec((None, d, tn), lambda i, j, l: (l[0], 0, nff + j))],
        out_specs=pl.BlockSpec((tm, tn), lambda i, j, l: (i, j)),
        out_shape=jax.ShapeDtypeStruct((t, D_FF), BF16))
    h, hb = _proj_ln("ffn_ln", lidx, act, w_ffn_out, h, ln2_g, ln2_b, tk=D_FF // 4)

    h, hb = _call(
        _ple_kernel, "ple", lidx, [hb, p, w_ple_in, w_ple_gate, h], grid=(t // tm, ncol),
        in_specs=[pl.BlockSpec((tm, d), lambda i, j, l: (i, 0)),
                  pl.BlockSpec((None, tm, PLE_DIM), lambda i, j, l: (l[0], i, 0)),
                  pl.BlockSpec((None, PLE_DIM, tn), lambda i, j, l: (l[0], 0, j)),
                  pl.BlockSpec((None, d, tn), lambda i, j, l: (l[0], 0, j)),
                  pl.BlockSpec((tm, tn), lambda i, j, l: (i, j))],
        out_specs=[pl.BlockSpec((tm, tn), lambda i, j, l: (i, j)),
                   pl.BlockSpec((tm, tn), lambda i, j, l: (i, j))],
        out_shape=[jax.ShapeDtypeStruct((t, d), F32), jax.ShapeDtypeStruct((t, d), BF16)])
    return h, hb


def kernel(x, p, w_in, fox_f_bias, nsa_pe_k, nsa_pe_v, nsa_cmp_k1, nsa_cmp_k2, nsa_cmp_v1, nsa_cmp_v2, dsa_kv_norm, dsa_kv_up, w_br_fox, w_br_nsa, w_br_dsa, w_out, ln1_g, ln1_b, w_ffn_in, w_ffn_out, ln2_g, ln2_b, w_ple_in, w_ple_gate):
    bsz, seq, d = x.shape
    depth = w_in.shape[0]
    t = bsz * seq
    assert d == D_MODEL and seq % 1024 == 0 and depth == DEPTH

    def seg(name):
        a, b = _IN_OFFSETS[name]
        return w_in[:, :, a:b]

    def zeros(n):
        return jnp.zeros((depth, d, n), w_in.dtype)

    cat = functools.partial(jnp.concatenate, axis=-1)
    w_a = cat([seg('fox_q'), seg('fox_k'), seg('fox_v'), seg('nsa_vs'), seg('nsa_vw')]).astype(BF16)
    w_b = cat([seg('nsa_q'), seg('nsa_ks'), seg('nsa_kw'), seg('dsa_q')]).astype(BF16)
    w_c = cat([seg('idx_q'), seg('idx_k'), zeros(IDX_DIM), zeros(IDX_DIM), seg('idx_k')]).astype(BF16)
    n_misc = FOX_HEADS + 3 * NSA_HEADS + IDX_HEADS
    w_d = cat([seg('dsa_ckv'), seg('fox_f'), seg('nsa_g'), seg('idx_w'), zeros(LANE - n_misc),
               seg('nsa_kc'), seg('nsa_vc')]).astype(BF16)
    w_gate = seg('gate').astype(BF16)

    fox_bias = jnp.pad(fox_f_bias, ((0, 0), (0, LANE - FOX_HEADS))).reshape(depth, 1, LANE)
    half = CMP_STRIDE * HEAD_DIM
    weights = (
        w_a, w_b, w_c, w_d, w_gate, fox_bias,
        nsa_pe_k.reshape(depth, 2, half), nsa_pe_v.reshape(depth, 2, half),
        nsa_cmp_k1.astype(BF16), nsa_cmp_k2.astype(BF16), nsa_cmp_v1.astype(BF16), nsa_cmp_v2.astype(BF16),
        dsa_kv_norm.reshape(depth, 1, DSA_KV_RANK), dsa_kv_up.astype(BF16),
        w_br_fox.astype(BF16), w_br_nsa.astype(BF16), w_br_dsa.astype(BF16), w_out.astype(BF16),
        ln1_g.reshape(depth, 1, d), ln1_b.reshape(depth, 1, d),
        w_ffn_in.astype(BF16), w_ffn_out.astype(BF16),
        ln2_g.reshape(depth, 1, d), ln2_b.reshape(depth, 1, d),
        p.reshape(depth, t, PLE_DIM).astype(BF16), w_ple_in.astype(BF16), w_ple_gate.astype(BF16),
    )

    cos, sin = _rope_tables(seq, HEAD_DIM)
    cos128 = cat([cos, cos])
    sin128 = cat([-sin, sin])
    cos_i, sin_i = _rope_tables(seq, IDX_DIM)
    zi = jnp.zeros_like(sin_i)
    cos64 = cat([cos_i, cos_i, cos_i, cos_i])
    sin64a = cat([zi, sin_i, zi, sin_i])
    sin64b = cat([-sin_i, zi, -sin_i, zi])
    n_cmp = (seq - CMP_LEN) // CMP_STRIDE + 1
    c_end = jnp.minimum(jnp.arange(N_CMP_PAD) * CMP_STRIDE + CMP_LEN - 1, seq - 1)
    cmp_cos, cmp_sin = cos128[c_end], sin128[c_end]
    n_slc = seq // SLC_LEN
    c_start = jnp.arange(N_CMP_PAD) * CMP_STRIDE
    s_start = jnp.arange(LANE) * SLC_LEN
    overlap = jnp.maximum(jnp.minimum(c_start[:, None] + CMP_LEN - 1, s_start[None, :] + SLC_LEN - 1)
                          - jnp.maximum(c_start[:, None], s_start[None, :]) + 1, 0).astype(F32) / CMP_LEN
    overlap = jnp.where((jnp.arange(N_CMP_PAD)[:, None] < n_cmp) & (jnp.arange(LANE)[None, :] < n_slc), overlap, 0.0)
    overlap = overlap.T.astype(BF16)
    block_of_key = (jnp.arange(seq)[:, None] // SLC_LEN == jnp.arange(LANE)[None, :]).astype(BF16)
    consts = (cos128, sin128, cos64, sin64a, sin64b, cmp_cos, cmp_sin, overlap, block_of_key)

    h = x.reshape(t, d)
    hb = h.astype(BF16)
    for layer in range(depth):
        lidx = jnp.full((1,), layer, I32)
        h, hb = _layer(lidx, h, hb, consts, weights, bsz, seq)
    return h.reshape(bsz, seq, d)
```

```python
import functools

import jax
import jax.numpy as jnp
from jax import lax
from jax.experimental import pallas as pl
from jax.experimental.pallas import tpu as pltpu

F32, BF16, I32 = jnp.float32, jnp.bfloat16, jnp.int32

D_MODEL = 2048
DEPTH = 4
HEAD_DIM = 128
ROPE_THETA = 10000.0
NEG_INF = -1e30
FOX_HEADS = 6
NSA_HEADS = 4
CMP_LEN = 32
CMP_STRIDE = 16
CMP_HIDDEN = 256
SLC_LEN = 64
SLC_TOPN = 16
WIN = 512
SLC_FORCE = 1e4
DSA_HEADS = 6
DSA_KV_RANK = 256
IDX_HEADS = 16
IDX_DIM = 64
IDX_TOPK_MAX = 256
D_FF = ((8 * D_MODEL + 3 * 256 - 1) // (3 * 256)) * 256
PLE_DIM = 256
ALPHA = (2 * DEPTH) ** 0.25
SCALE = HEAD_DIM ** -0.5

LANE = 128
VMEM_LIMIT_BYTES = 48 * 1024 * 1024
VMEM_LIMIT_LARGE_BYTES = 56 * 1024 * 1024

_IN_SPLITS = (
    ('fox_q', FOX_HEADS * HEAD_DIM), ('fox_k', FOX_HEADS * HEAD_DIM), ('fox_v', FOX_HEADS * HEAD_DIM), ('fox_f', FOX_HEADS),
    ('nsa_q', NSA_HEADS * HEAD_DIM), ('nsa_kc', HEAD_DIM), ('nsa_vc', HEAD_DIM), ('nsa_ks', HEAD_DIM), ('nsa_vs', HEAD_DIM),
    ('nsa_kw', HEAD_DIM), ('nsa_vw', HEAD_DIM), ('nsa_g', 3 * NSA_HEADS),
    ('dsa_q', DSA_HEADS * HEAD_DIM), ('dsa_ckv', DSA_KV_RANK), ('idx_q', IDX_HEADS * IDX_DIM), ('idx_k', IDX_DIM), ('idx_w', IDX_HEADS),
    ('gate', 3 * D_MODEL),
)
_IN_OFFSETS = {}
_off = 0
for _name, _width in _IN_SPLITS:
    _IN_OFFSETS[_name] = (_off, _off + _width)
    _off += _width

MISC_FOX_F = 0
MISC_NSA_G = FOX_HEADS
MISC_IDX_W = FOX_HEADS + 3 * NSA_HEADS

TQ = 256
TK = 256
N_CMP_PAD = 128


def _dot(a, b):
    return jnp.dot(a, b, preferred_element_type=F32)


def _dot_nt(a, b):
    return lax.dot_general(a, b, (((1,), (1,)), ((), ())), preferred_element_type=F32)


def _split3(a):
    a1 = a.astype(BF16)
    r1 = a - a1.astype(F32)
    a2 = r1.astype(BF16)
    a3 = (r1 - a2.astype(F32)).astype(BF16)
    return a1, a2, a3


def _call(kernel, name, lidx, args, grid, in_specs, out_specs, out_shape, scratch_shapes=(),
          vmem_limit_bytes=VMEM_LIMIT_BYTES):
    return pl.pallas_call(
        kernel,
        grid_spec=pltpu.PrefetchScalarGridSpec(
            num_scalar_prefetch=1, grid=grid, in_specs=in_specs, out_specs=out_specs,
            scratch_shapes=list(scratch_shapes)),
        out_shape=out_shape,
        compiler_params=pltpu.CompilerParams(
            dimension_semantics=("arbitrary",) * len(grid), vmem_limit_bytes=vmem_limit_bytes),
        name=name,
    )(lidx, *args)


def _proj_kernel(l_ref, x_ref, w_ref, *refs, shifts):
    o_ref = refs[-1]
    acc = _dot(x_ref[...], w_ref[...].astype(BF16))
    if not shifts:
        o_ref[...] = acc.astype(o_ref.dtype)
        return
    c_ref = refs[0]
    s_refs = refs[1:-1]
    for g in range(acc.shape[1] // LANE):
        xg = acc[:, g * LANE:(g + 1) * LANE]
        out = xg * c_ref[...]
        for shift, s_ref in zip(shifts, s_refs):
            out = out + pltpu.roll(xg, shift, 1) * s_ref[...]
        o_ref[:, g * LANE:(g + 1) * LANE] = out.astype(o_ref.dtype)


def _proj(name, lidx, hb, w, tm, tn, out_dtype, seq, rope=None):
    t, d = hb.shape
    n = w.shape[2]
    in_specs = [pl.BlockSpec((tm, d), lambda i, j, l: (i, 0)),
                pl.BlockSpec((None, d, tn), lambda i, j, l: (l[0], 0, j))]
    args = [hb, w]
    shifts = ()
    if rope is not None:
        shifts, tables = rope[0], rope[1:]
        nrow = seq // tm
        for tab in tables:
            in_specs.append(pl.BlockSpec((tm, LANE), lambda i, j, l: (i % nrow, 0)))
            args.append(tab)
    return _call(functools.partial(_proj_kernel, shifts=shifts), name, lidx, args,
                 grid=(t // tm, n // tn), in_specs=in_specs,
                 out_specs=pl.BlockSpec((tm, tn), lambda i, j, l: (i, j)),
                 out_shape=jax.ShapeDtypeStruct((t, n), out_dtype))


def _cum_kernel(l_ref, f_ref, bias_ref, o_ref):
    chunk = 256
    row = lax.broadcasted_iota(I32, (chunk, chunk), 0)
    col = lax.broadcasted_iota(I32, (chunk, chunk), 1)
    tri = jnp.where(row >= col, 1.0, 0.0).astype(BF16)
    carry = jnp.zeros((1, LANE), F32)
    for c in range(f_ref.shape[0] // chunk):
        logf = jax.nn.log_sigmoid(f_ref[c * chunk:(c + 1) * chunk, :] + bias_ref[...])
        l1, l2, l3 = _split3(logf)
        cs = _dot(tri, l1) + _dot(tri, l2) + _dot(tri, l3) + carry
        o_ref[c * chunk:(c + 1) * chunk, :] = cs
        carry = cs[chunk - 1:chunk, :]


def _max_lanes(x):
    out = x[:, 0:LANE]
    for g in range(1, x.shape[1] // LANE):
        out = jnp.maximum(out, x[:, g * LANE:(g + 1) * LANE])
    return out


def _tile_lanes(x, n):
    return jnp.concatenate([x] * n, axis=1)


def _row_max_to_lanes(mx):
    return jnp.broadcast_to(jnp.max(mx, axis=-1, keepdims=True), mx.shape)


def _exp_pv(s, m_lanes, v):
    p = jnp.exp(s - _tile_lanes(m_lanes, s.shape[1] // LANE)).astype(BF16)
    return _dot(p, jnp.concatenate([v, jnp.ones_like(v)], axis=1))


def _fox_kernel(l_ref, q_ref, k_ref, v_ref, cq_ref, ck_ref, o_ref, s_ref, mx_ref, cqb_ref, acc_ref):
    qi = pl.program_id(1)
    nh = FOX_HEADS
    nchunk = ck_ref.shape[0] // 8
    row = lax.broadcasted_iota(I32, (TQ, TK), 0)
    col = lax.broadcasted_iota(I32, (TQ, TK), 1)
    cq_all = cq_ref[...]
    for h in range(nh):
        cqb_ref[h] = jnp.broadcast_to(cq_all[:, h:h + 1], (TQ, LANE))
    mx_ref[...] = jnp.full(mx_ref.shape, NEG_INF, F32)
    acc_ref[...] = jnp.zeros(acc_ref.shape, F32)

    def score_step(j, diagonal):
        start = pl.multiple_of(j * TK, TK)
        for h in range(nh):
            hs = slice(h * HEAD_DIM, (h + 1) * HEAD_DIM)
            s = (_dot_nt(q_ref[:, hs], k_ref[pl.ds(start, TK), hs]) * SCALE
                 + _tile_lanes(cqb_ref[h], TK // LANE) - ck_ref[pl.ds(h * nchunk + j, 1), :])
            if diagonal:
                s = jnp.where(col <= row, s, NEG_INF)
            s_ref[h, j] = s
            mx_ref[h] = jnp.maximum(mx_ref[h], _max_lanes(s))

    def score_body(j, carry):
        score_step(j, False)
        return carry

    lax.fori_loop(0, qi, score_body, 0)
    score_step(qi, True)
    for h in range(nh):
        mx_ref[h] = _row_max_to_lanes(mx_ref[h])

    def pv_body(j, carry):
        start = pl.multiple_of(j * TK, TK)
        for h in range(nh):
            hs = slice(h * HEAD_DIM, (h + 1) * HEAD_DIM)
            acc_ref[h] += _exp_pv(s_ref[h, j], mx_ref[h], v_ref[pl.ds(start, TK), hs])
        return carry

    lax.fori_loop(0, qi + 1, pv_body, 0)
    for h in range(nh):
        a = acc_ref[h]
        o_ref[:, h * HEAD_DIM:(h + 1) * HEAD_DIM] = (a[:, :HEAD_DIM] / a[:, HEAD_DIM:]).astype(o_ref.dtype)


def _cmp_kernel(l_ref, rk_ref, rv_ref, pek_ref, pev_ref, wk1_ref, wk2_ref, wv1_ref, wv2_ref,
                c_ref, s_ref, ko_ref, vo_ref):
    half = CMP_STRIDE * HEAD_DIM

    def compress(r_ref, pe_ref, w1_ref, w2_ref):
        r = r_ref[...]
        lo = (r + pe_ref[0:1, :]).astype(BF16)
        hi = (r + pe_ref[1:2, :]).astype(BF16)
        a = _dot(lo, w1_ref[0:half, :])
        b = _dot(hi, w1_ref[half:2 * half, :])
        hid = a + pltpu.roll(b, N_CMP_PAD - 1, 0)
        return _dot(jax.nn.gelu(hid).astype(BF16), w2_ref[...])

    kc = compress(rk_ref, pek_ref, wk1_ref, wk2_ref)
    kc = kc * c_ref[...] + pltpu.roll(kc, HEAD_DIM // 2, 1) * s_ref[...]
    ko_ref[...] = kc.astype(ko_ref.dtype)
    vo_ref[...] = compress(rv_ref, pev_ref, wv1_ref, wv2_ref).astype(vo_ref.dtype)


def _masked_softmax(s, mask):
    s = jnp.where(mask, s, NEG_INF)
    m = jnp.max(s, axis=-1, keepdims=True)
    e = jnp.where(mask, jnp.exp(s - m), 0.0)
    den = jnp.sum(e, axis=-1, keepdims=True)
    return e, jnp.where(den > 0.0, den, 1.0)


def _nsa_kernel(l_ref, q_ref, ks_ref, kw_ref, vs_ref, vw_ref, kc_ref, vc_ref, g_ref, ovt_ref, et_ref,
                o_ref, q_sc, s_ref, mx_ref, acc_ref):
    qi = pl.program_id(1)
    nh = NSA_HEADS
    for h in range(nh):
        q_sc[h * TQ:(h + 1) * TQ, :] = q_ref[:, h * HEAD_DIM:(h + 1) * HEAD_DIM]
    q = q_sc[...]
    pos = qi * TQ + lax.broadcasted_iota(I32, (TQ, 1), 0)
    lane = lax.broadcasted_iota(I32, (TQ, LANE), 1)

    cvis = (lane * CMP_STRIDE + (CMP_LEN - 1)) <= pos
    s_c = (_dot_nt(q, kc_ref[...]) * SCALE).reshape(nh, TQ, N_CMP_PAD)
    e_c, den_c = _masked_softmax(s_c, cvis[None])
    p_c = e_c / den_c
    o_cmp = _dot(p_c.reshape(nh * TQ, N_CMP_PAD).astype(BF16), vc_ref[...])

    p_sum = p_c[0]
    for h in range(1, nh):
        p_sum = p_sum + p_c[h]
    p1, p2, p3 = _split3(p_sum)
    ovt = ovt_ref[...]
    n_slc = et_ref.shape[0] // SLC_LEN
    imp = (_dot_nt(ovt, p1) + _dot_nt(ovt, p2) + _dot_nt(ovt, p3))[0:n_slc, :]
    blk = lax.broadcasted_iota(I32, (n_slc, TQ), 0)
    pos_t = qi * TQ + lax.broadcasted_iota(I32, (n_slc, TQ), 1)
    cur = lax.shift_right_logical(pos_t, SLC_LEN.bit_length() - 1)
    forced = (blk == 0) | (blk == cur) | (blk == cur - 1)
    imp = jnp.where(forced, SLC_FORCE, jnp.where(blk * SLC_LEN <= pos_t, imp, -SLC_FORCE))
    rank = jnp.zeros((n_slc, TQ), F32)
    for k in range(n_slc):
        ik = imp[k:k + 1, :]
        ahead = (ik > imp) | ((ik == imp) & (blk > k))
        rank = rank + jnp.where(ahead, 1.0, 0.0)
    sel_t = jnp.where(rank < float(min(SLC_TOPN, n_slc)), 1.0, 0.0)
    sel_t = jnp.concatenate([sel_t, jnp.zeros((LANE - n_slc, TQ), F32)], axis=0)
    sel = sel_t.T.astype(BF16)

    mx_ref[...] = jnp.full(mx_ref.shape, NEG_INF, F32)
    acc_ref[...] = jnp.zeros(acc_ref.shape, F32)
    row = lax.broadcasted_iota(I32, (TQ, TK), 0)
    col = lax.broadcasted_iota(I32, (TQ, TK), 1)

    def score_step(j, diagonal):
        start = pl.multiple_of(j * TK, TK)
        vis = _dot_nt(sel, et_ref[pl.ds(start, TK), :]) > 0.5
        if diagonal:
            vis = vis & (col <= row)
        s = (_dot_nt(q_sc[...], ks_ref[pl.ds(start, TK), :]) * SCALE).reshape(nh, TQ, TK)
        s = jnp.where(vis[None], s, NEG_INF).reshape(nh * TQ, TK)
        s_ref[j] = s
        mx_ref[...] = jnp.maximum(mx_ref[...], _max_lanes(s))

    def score_body(j, carry):
        score_step(j, False)
        return carry

    lax.fori_loop(0, qi, score_body, 0)
    score_step(qi, True)
    mx_ref[...] = _row_max_to_lanes(mx_ref[...])

    def pv_body(j, carry):
        start = pl.multiple_of(j * TK, TK)
        acc_ref[...] += _exp_pv(s_ref[j], mx_ref[...], vs_ref[pl.ds(start, TK), :])
        return carry

    lax.fori_loop(0, qi + 1, pv_body, 0)
    a_slc = acc_ref[...]
    o_slc = a_slc[:, :HEAD_DIM] / a_slc[:, HEAD_DIM:]

    band = WIN + TQ
    start_w = pl.multiple_of(jnp.maximum(qi * TQ - WIN, 0), TQ)
    kwb = kw_ref[pl.ds(start_w, band), :]
    vwb = vw_ref[pl.ds(start_w, band), :]
    dist = pos - (start_w + lax.broadcasted_iota(I32, (TQ, band), 1))
    wvis = (dist >= 0) & (dist < WIN)
    s_w = (_dot_nt(q, kwb) * SCALE).reshape(nh, TQ, band)
    e_w, den_w = _masked_softmax(s_w, wvis[None])
    o_win = _dot(e_w.reshape(nh * TQ, band).astype(BF16), vwb) / den_w.reshape(nh * TQ, 1)

    gate = jax.nn.sigmoid(g_ref[...])
    for h in range(nh):
        rows = slice(h * TQ, (h + 1) * TQ)
        g0 = gate[:, MISC_NSA_G + 3 * h:MISC_NSA_G + 3 * h + 1]
        g1 = gate[:, MISC_NSA_G + 3 * h + 1:MISC_NSA_G + 3 * h + 2]
        g2 = gate[:, MISC_NSA_G + 3 * h + 2:MISC_NSA_G + 3 * h + 3]
        out = g0 * o_cmp[rows] + g1 * o_slc[rows] + g2 * o_win[rows]
        o_ref[:, h * HEAD_DIM:(h + 1) * HEAD_DIM] = out.astype(o_ref.dtype)


def _dsa_kv_kernel(l_ref, x_ref, g_ref, up_ref, c_ref, s_ref, ko_ref, vo_ref):
    x = x_ref[...]
    r = x * lax.rsqrt(jnp.mean(jnp.square(x), axis=-1, keepdims=True) + 1e-6) * g_ref[...]
    kv = _dot(r.astype(BF16), up_ref[...])
    k = kv[:, :HEAD_DIM]
    k = k * c_ref[...] + pltpu.roll(k, HEAD_DIM // 2, 1) * s_ref[...]
    ko_ref[...] = k.astype(ko_ref.dtype)
    vo_ref[...] = kv[:, HEAD_DIM:].astype(vo_ref.dtype)


def _dsa_kernel(l_ref, q_ref, kd_ref, vd_ref, iq_ref, ika_ref, ikb_ref, w_ref, o_ref,
                key_ref, keyt_ref, wb_ref, sel_ref, q_sc, s_ref, mx_ref, acc_ref):
    qi = pl.program_id(1)
    nh = DSA_HEADS
    seq = kd_ref.shape[0]
    k_top = float(min(IDX_TOPK_MAX, seq // 4))
    row = lax.broadcasted_iota(I32, (TQ, TK), 0)
    col = lax.broadcasted_iota(I32, (TQ, TK), 1)
    int_min = jnp.int32(-2 ** 31)
    idx_bits = (seq - 1).bit_length()

    w_all = w_ref[...] * (IDX_HEADS ** -0.5 * IDX_DIM ** -0.5)
    for h in range(IDX_HEADS):
        wb_ref[h] = jnp.broadcast_to(w_all[:, MISC_IDX_W + h:MISC_IDX_W + h + 1], (TQ, LANE))

    def score_step(j, diagonal):
        start = pl.multiple_of(j * TK, TK)
        ka = ika_ref[pl.ds(start, TK), :]
        kb = ikb_ref[pl.ds(start, TK), :]
        sc = jnp.zeros((TQ, TK), F32)
        for g in range(IDX_HEADS // 2):
            qpair = iq_ref[:, g * LANE:(g + 1) * LANE]
            wa = _tile_lanes(wb_ref[2 * g], TK // LANE)
            wb = _tile_lanes(wb_ref[2 * g + 1], TK // LANE)
            sc = sc + jnp.maximum(_dot_nt(qpair, ka), 0.0) * wa + jnp.maximum(_dot_nt(qpair, kb), 0.0) * wb
        if diagonal:
            sc = jnp.where(col <= row, sc, NEG_INF)
        bits = lax.bitcast_convert_type(sc, I32)
        key = jnp.where(bits < 0, bits ^ jnp.int32(0x7FFFFFFF), bits)
        key = jnp.where(sc == 0.0, 0, key)
        key_ref[j] = key
        keyt_ref[j] = key.T

    def score_body(j, carry):
        score_step(j, False)
        return carry

    lax.fori_loop(0, qi, score_body, 0)
    score_step(qi, True)

    krow = lax.broadcasted_iota(I32, (TK, TQ), 0)

    def count(pred_fn):
        def body(j, acc):
            hit = jnp.where(pred_fn(keyt_ref[j], j), 1.0, 0.0)
            return acc + jnp.sum(hit.reshape(TK // 8, 8, TQ), axis=0)
        part = lax.fori_loop(0, qi + 1, body, jnp.zeros((8, TQ), F32))
        return jnp.sum(part, axis=0, keepdims=True)

    thr = jnp.where(count(lambda key, j: key >= 0) >= k_top, jnp.int32(0), int_min)

    def value_bit(i, thr):
        cand = thr + (jnp.int32(1) << (30 - i))
        return jnp.where(count(lambda key, j: key >= cand) >= k_top, cand, thr)

    thr = lax.fori_loop(0, 31, value_bit, thr)
    n_gt = count(lambda key, j: key > thr)
    n_ge = count(lambda key, j: key >= thr)

    def index_bisect():
        def index_bit(i, cut):
            cand = cut + (jnp.int32(1) << (idx_bits - 1 - i))
            n_before = count(lambda key, j: (key == thr) & (j * TK + krow < cand))
            return jnp.where(n_gt + n_before < k_top, cand, cut)
        return lax.fori_loop(0, idx_bits, index_bit, jnp.zeros((1, TQ), I32))

    def take_all_ties():
        return jnp.full((1, TQ), seq - 1, I32)

    cut = lax.cond(jnp.max(n_ge) > k_top, index_bisect, take_all_ties)
    sel_ref[0] = jnp.broadcast_to(thr, (LANE, TQ)).T
    sel_ref[1] = jnp.broadcast_to(cut, (LANE, TQ)).T

    for h in range(nh):
        q_sc[h * TQ:(h + 1) * TQ, :] = q_ref[:, h * HEAD_DIM:(h + 1) * HEAD_DIM]
    mx_ref[...] = jnp.full(mx_ref.shape, NEG_INF, F32)
    acc_ref[...] = jnp.zeros(acc_ref.shape, F32)

    def attn_score_step(j, diagonal):
        start = pl.multiple_of(j * TK, TK)
        key = key_ref[j]
        thr_t = _tile_lanes(sel_ref[0], TK // LANE)
        cut_t = _tile_lanes(sel_ref[1], TK // LANE)
        vis = (key > thr_t) | ((key == thr_t) & (j * TK + col <= cut_t))
        if diagonal:
            vis = vis & (col <= row)
        s = (_dot_nt(q_sc[...], kd_ref[pl.ds(start, TK), :]) * SCALE).reshape(nh, TQ, TK)
        s = jnp.where(vis[None], s, NEG_INF).reshape(nh * TQ, TK)
        s_ref[j] = s
        mx_ref[...] = jnp.maximum(mx_ref[...], _max_lanes(s))

    def attn_score_body(j, carry):
        attn_score_step(j, False)
        return carry

    lax.fori_loop(0, qi, attn_score_body, 0)
    attn_score_step(qi, True)
    mx_ref[...] = _row_max_to_lanes(mx_ref[...])

    def pv_body(j, carry):
        start = pl.multiple_of(j * TK, TK)
        acc_ref[...] += _exp_pv(s_ref[j], mx_ref[...], vd_ref[pl.ds(start, TK), :])
        return carry

    lax.fori_loop(0, qi + 1, pv_body, 0)
    a = acc_ref[...]
    out = a[:, :HEAD_DIM] / a[:, HEAD_DIM:]
    for h in range(nh):
        o_ref[:, h * HEAD_DIM:(h + 1) * HEAD_DIM] = out[h * TQ:(h + 1) * TQ].astype(o_ref.dtype)


def _merge_kernel(l_ref, hb_ref, of_ref, on_ref, od_ref, wg1_ref, wg2_ref, wg3_ref, wf_ref, wn_ref, wd_ref, o_ref,
                  wgb_ref):
    @pl.when(pl.program_id(1) == 0)
    def _():
        wgb_ref[0] = wg1_ref[...].astype(BF16)
        wgb_ref[1] = wg2_ref[...].astype(BF16)
        wgb_ref[2] = wg3_ref[...].astype(BF16)

    hb = hb_ref[...]
    mixed = (jax.nn.sigmoid(_dot(hb, wgb_ref[0])) * _dot(of_ref[...], wf_ref[...])
             + jax.nn.sigmoid(_dot(hb, wgb_ref[1])) * _dot(on_ref[...], wn_ref[...])
             + jax.nn.sigmoid(_dot(hb, wgb_ref[2])) * _dot(od_ref[...], wd_ref[...]))
    o_ref[...] = mixed.astype(o_ref.dtype)


LN_SUB_ROWS = 256


def _layer_norm_store(y, g_ref, b_ref, of_ref, ob_ref, rows):
    mu = jnp.mean(y, axis=-1, keepdims=True)
    yc = y - mu
    var = jnp.mean(jnp.square(yc), axis=-1, keepdims=True)
    out = yc * lax.rsqrt(var + 1e-5) * g_ref[...] + b_ref[...]
    of_ref[rows, :] = out
    ob_ref[rows, :] = out.astype(ob_ref.dtype)


def _out_ln_kernel(l_ref, x_ref, w_ref, h_ref, g_ref, b_ref, of_ref, ob_ref):
    w = w_ref[...]
    for r in range(x_ref.shape[0] // LN_SUB_ROWS):
        rows = slice(r * LN_SUB_ROWS, (r + 1) * LN_SUB_ROWS)
        y = ALPHA * h_ref[rows, :] + _dot(x_ref[rows, :], w)
        _layer_norm_store(y, g_ref, b_ref, of_ref, ob_ref, rows)


def _out_ln(name, lidx, x, w, h, g, b):
    t, kdim = x.shape
    d = h.shape[1]
    tm = 512
    return _call(_out_ln_kernel, name, lidx, [x, w, h, g, b], grid=(t // tm,),
                 in_specs=[pl.BlockSpec((tm, kdim), lambda i, l: (i, 0)),
                           pl.BlockSpec((None, kdim, d), lambda i, l: (l[0], 0, 0)),
                           pl.BlockSpec((tm, d), lambda i, l: (i, 0)),
                           pl.BlockSpec((None, 1, d), lambda i, l: (l[0], 0, 0)),
                           pl.BlockSpec((None, 1, d), lambda i, l: (l[0], 0, 0))],
                 out_specs=[pl.BlockSpec((tm, d), lambda i, l: (i, 0)),
                            pl.BlockSpec((tm, d), lambda i, l: (i, 0))],
                 out_shape=[jax.ShapeDtypeStruct((t, d), F32), jax.ShapeDtypeStruct((t, d), BF16)])


def _ffn_ln_kernel(l_ref, x_ref, w_ref, h_ref, g_ref, b_ref, of_ref, ob_ref, *, nk):
    k = pl.program_id(1)

    @pl.when(k == 0)
    def _():
        of_ref[...] = ALPHA * h_ref[...] + _dot(x_ref[...], w_ref[...])

    @pl.when((k > 0) & (k < nk - 1))
    def _():
        of_ref[...] += _dot(x_ref[...], w_ref[...])

    @pl.when(k == nk - 1)
    def _():
        w = w_ref[...]
        for r in range(x_ref.shape[0] // LN_SUB_ROWS):
            rows = slice(r * LN_SUB_ROWS, (r + 1) * LN_SUB_ROWS)
            y = of_ref[rows, :] + _dot(x_ref[rows, :], w)
            _layer_norm_store(y, g_ref, b_ref, of_ref, ob_ref, rows)


def _ffn_ln(name, lidx, x, w, h, g, b, tk):
    t, kdim = x.shape
    d = h.shape[1]
    tm = 1024
    nk = kdim // tk
    assert nk >= 2
    return _call(functools.partial(_ffn_ln_kernel, nk=nk), name, lidx, [x, w, h, g, b],
                 grid=(t // tm, nk),
                 in_specs=[pl.BlockSpec((tm, tk), lambda i, k, l: (i, k)),
                           pl.BlockSpec((None, tk, d), lambda i, k, l: (l[0], k, 0)),
                           pl.BlockSpec((tm, d), lambda i, k, l: (i, 0)),
                           pl.BlockSpec((None, 1, d), lambda i, k, l: (l[0], 0, 0)),
                           pl.BlockSpec((None, 1, d), lambda i, k, l: (l[0], 0, 0))],
                 out_specs=[pl.BlockSpec((tm, d), lambda i, k, l: (i, 0)),
                            pl.BlockSpec((tm, d), lambda i, k, l: (i, 0))],
                 out_shape=[jax.ShapeDtypeStruct((t, d), F32), jax.ShapeDtypeStruct((t, d), BF16)],
                 vmem_limit_bytes=VMEM_LIMIT_LARGE_BYTES)


def _swiglu_kernel(l_ref, x_ref, wa_ref, wb_ref, o_ref, wab_ref):
    @pl.when(pl.program_id(1) == 0)
    def _():
        wab_ref[0] = wa_ref[...].astype(BF16)
        wab_ref[1] = wb_ref[...].astype(BF16)

    x = x_ref[...]
    a = _dot(x, wab_ref[0])
    o_ref[...] = (jax.nn.silu(a) * _dot(x, wab_ref[1])).astype(o_ref.dtype)


def _ple_kernel(l_ref, hb_ref, p_ref, wpi_ref, wpg_ref, h_ref, of_ref, ob_ref):
    out = h_ref[...] + _dot(p_ref[...], wpi_ref[...]) * jax.nn.sigmoid(_dot(hb_ref[...], wpg_ref[...]))
    of_ref[...] = out
    ob_ref[...] = out.astype(ob_ref.dtype)


def _rope_tables(n, dim):
    inv = 1.0 / (ROPE_THETA ** (jnp.arange(0, dim, 2, dtype=F32) / dim))
    ang = jnp.arange(n, dtype=F32)[:, None] * inv[None, :]
    return jnp.cos(ang), jnp.sin(ang)


def _layer(lidx, h, hb, consts, weights, bsz, seq):
    (cos128, sin128, cos64, sin64a, sin64b, cmp_cos, cmp_sin, overlap, block_of_key) = consts
    (w_a, w_b, w_c, w_d, w_gate, fox_bias, pe_k, pe_v, wk1, wk2, wv1, wv2, kv_norm, kv_up,
     w_br_fox, w_br_nsa, w_br_dsa, w_out, ln1_g, ln1_b, w_ffn_in, w_ffn_out, ln2_g, ln2_b, p, w_ple_in, w_ple_gate) = weights
    t = bsz * seq
    nq = seq // TQ
    d = D_MODEL

    z_a = _proj("proj_plain", lidx, hb, w_a, 2048, 512, BF16, seq)
    z_b = _proj("proj_rope128", lidx, hb, w_b, 2048, 512, BF16, seq, rope=((HEAD_DIM // 2,), cos128, sin128))
    z_c = _proj("proj_rope64", lidx, hb, w_c, 2048, 640, BF16, seq,
                rope=((IDX_DIM // 2, LANE - IDX_DIM // 2), cos64, sin64a, sin64b))
    z_d = _proj("proj_f32", lidx, hb, w_d, 1024, 640, F32, seq)

    cum = _call(_cum_kernel, "fox_cum", lidx, [z_d, fox_bias], grid=(bsz,),
                in_specs=[pl.BlockSpec((seq, LANE), lambda b, l: (b, 2)),
                          pl.BlockSpec((None, 1, LANE), lambda b, l: (l[0], 0, 0))],
                out_specs=pl.BlockSpec((seq, LANE), lambda b, l: (b, 0)),
                out_shape=jax.ShapeDtypeStruct((t, LANE), F32))
    cum_rows = jnp.transpose(cum.reshape(bsz, seq, LANE)[:, :, :8], (0, 2, 1)).reshape(bsz, 8 * (seq // TK), TK)
    fq = FOX_HEADS * HEAD_DIM
    o_fox = _call(_fox_kernel, "fox_attn", lidx, [z_a, z_a, z_a, cum, cum_rows], grid=(bsz, nq),
                  in_specs=[pl.BlockSpec((TQ, fq), lambda b, i, l: (b * nq + i, 0)),
                            pl.BlockSpec((seq, fq), lambda b, i, l: (b, 1)),
                            pl.BlockSpec((seq, fq), lambda b, i, l: (b, 2)),
                            pl.BlockSpec((TQ, LANE), lambda b, i, l: (b * nq + i, 0)),
                            pl.BlockSpec((None, 8 * (seq // TK), TK), lambda b, i, l: (b, 0, 0))],
                  out_specs=pl.BlockSpec((TQ, fq), lambda b, i, l: (b * nq + i, 0)),
                  out_shape=jax.ShapeDtypeStruct((t, fq), BF16),
                  scratch_shapes=[pltpu.VMEM((FOX_HEADS, seq // TK, TQ, TK), F32),
                                  pltpu.VMEM((FOX_HEADS, TQ, LANE), F32), pltpu.VMEM((FOX_HEADS, TQ, LANE), F32),
                                  pltpu.VMEM((FOX_HEADS, TQ, 2 * HEAD_DIM), F32)])

    nchunk = seq // CMP_STRIDE
    half = CMP_STRIDE * HEAD_DIM
    r_k = z_d[:, 3 * LANE:4 * LANE].reshape(bsz * nchunk, half)
    r_v = z_d[:, 4 * LANE:5 * LANE].reshape(bsz * nchunk, half)
    k_cmp, v_cmp = _call(
        _cmp_kernel, "nsa_compress", lidx, [r_k, r_v, pe_k, pe_v, wk1, wk2, wv1, wv2, cmp_cos, cmp_sin], grid=(bsz,),
        in_specs=[pl.BlockSpec((nchunk, half), lambda b, l: (b, 0)),
                  pl.BlockSpec((nchunk, half), lambda b, l: (b, 0)),
                  pl.BlockSpec((None, 2, half), lambda b, l: (l[0], 0, 0)),
                  pl.BlockSpec((None, 2, half), lambda b, l: (l[0], 0, 0)),
                  pl.BlockSpec((None, 2 * half, CMP_HIDDEN), lambda b, l: (l[0], 0, 0)),
                  pl.BlockSpec((None, CMP_HIDDEN, HEAD_DIM), lambda b, l: (l[0], 0, 0)),
                  pl.BlockSpec((None, 2 * half, CMP_HIDDEN), lambda b, l: (l[0], 0, 0)),
                  pl.BlockSpec((None, CMP_HIDDEN, HEAD_DIM), lambda b, l: (l[0], 0, 0)),
                  pl.BlockSpec((nchunk, HEAD_DIM), lambda b, l: (0, 0)),
                  pl.BlockSpec((nchunk, HEAD_DIM), lambda b, l: (0, 0))],
        out_specs=[pl.BlockSpec((nchunk, HEAD_DIM), lambda b, l: (b, 0)),
                   pl.BlockSpec((nchunk, HEAD_DIM), lambda b, l: (b, 0))],
        out_shape=[jax.ShapeDtypeStruct((bsz * nchunk, HEAD_DIM), BF16)] * 2)
    nsq = NSA_HEADS * HEAD_DIM
    o_nsa = _call(
        _nsa_kernel, "nsa_attn", lidx, [z_b, z_b, z_b, z_a, z_a, k_cmp, v_cmp, z_d, overlap, block_of_key], grid=(bsz, nq),
        in_specs=[pl.BlockSpec((TQ, nsq), lambda b, i, l: (b * nq + i, 0)),
                  pl.BlockSpec((seq, LANE), lambda b, i, l: (b, 4)),
                  pl.BlockSpec((seq, LANE), lambda b, i, l: (b, 5)),
                  pl.BlockSpec((seq, LANE), lambda b, i, l: (b, 18)),
                  pl.BlockSpec((seq, LANE), lambda b, i, l: (b, 19)),
                  pl.BlockSpec((nchunk, HEAD_DIM), lambda b, i, l: (b, 0)),
                  pl.BlockSpec((nchunk, HEAD_DIM), lambda b, i, l: (b, 0)),
                  pl.BlockSpec((TQ, LANE), lambda b, i, l: (b * nq + i, 2)),
                  pl.BlockSpec((N_CMP_PAD, LANE), lambda b, i, l: (0, 0)),
                  pl.BlockSpec((seq, LANE), lambda b, i, l: (0, 0))],
        out_specs=pl.BlockSpec((TQ, nsq), lambda b, i, l: (b * nq + i, 0)),
        out_shape=jax.ShapeDtypeStruct((t, nsq), BF16),
        scratch_shapes=[pltpu.VMEM((NSA_HEADS * TQ, HEAD_DIM), BF16),
                        pltpu.VMEM((seq // TK, NSA_HEADS * TQ, TK), F32),
                        pltpu.VMEM((NSA_HEADS * TQ, LANE), F32),
                        pltpu.VMEM((NSA_HEADS * TQ, 2 * HEAD_DIM), F32)])

    tm_kv = 1024
    k_d, v_d = _call(
        _dsa_kv_kernel, "dsa_kv", lidx, [z_d, kv_norm, kv_up, cos128, sin128], grid=(t // tm_kv,),
        in_specs=[pl.BlockSpec((tm_kv, DSA_KV_RANK), lambda i, l: (i, 0)),
                  pl.BlockSpec((None, 1, DSA_KV_RANK), lambda i, l: (l[0], 0, 0)),
                  pl.BlockSpec((None, DSA_KV_RANK, 2 * HEAD_DIM), lambda i, l: (l[0], 0, 0)),
                  pl.BlockSpec((tm_kv, LANE), lambda i, l: (i % (seq // tm_kv), 0)),
                  pl.BlockSpec((tm_kv, LANE), lambda i, l: (i % (seq // tm_kv), 0))],
        out_specs=[pl.BlockSpec((tm_kv, HEAD_DIM), lambda i, l: (i, 0)),
                   pl.BlockSpec((tm_kv, HEAD_DIM), lambda i, l: (i, 0))],
        out_shape=[jax.ShapeDtypeStruct((t, HEAD_DIM), BF16)] * 2)
    dq = DSA_HEADS * HEAD_DIM
    iqw = IDX_HEADS * IDX_DIM
    o_dsa = _call(
        _dsa_kernel, "dsa_attn", lidx, [z_b, k_d, v_d, z_c, z_c, z_c, z_d], grid=(bsz, nq),
        in_specs=[pl.BlockSpec((TQ, dq), lambda b, i, l: (b * nq + i, 1)),
                  pl.BlockSpec((seq, HEAD_DIM), lambda b, i, l: (b, 0)),
                  pl.BlockSpec((seq, HEAD_DIM), lambda b, i, l: (b, 0)),
                  pl.BlockSpec((TQ, iqw), lambda b, i, l: (b * nq + i, 0)),
                  pl.BlockSpec((seq, LANE), lambda b, i, l: (b, iqw // LANE)),
                  pl.BlockSpec((seq, LANE), lambda b, i, l: (b, iqw // LANE + 1)),
                  pl.BlockSpec((TQ, LANE), lambda b, i, l: (b * nq + i, 2))],
        out_specs=pl.BlockSpec((TQ, dq), lambda b, i, l: (b * nq + i, 0)),
        out_shape=jax.ShapeDtypeStruct((t, dq), BF16),
        scratch_shapes=[pltpu.VMEM((seq // TK, TQ, TK), I32), pltpu.VMEM((seq // TK, TK, TQ), I32),
                        pltpu.VMEM((IDX_HEADS, TQ, LANE), F32), pltpu.VMEM((2, TQ, LANE), I32),
                        pltpu.VMEM((DSA_HEADS * TQ, HEAD_DIM), BF16),
                        pltpu.VMEM((seq // TK, DSA_HEADS * TQ, TK), F32),
                        pltpu.VMEM((DSA_HEADS * TQ, LANE), F32),
                        pltpu.VMEM((DSA_HEADS * TQ, 2 * HEAD_DIM), F32)])

    tm, tn = 1024, 512
    ncol = d // tn
    tng = 256
    ncg = d // tng
    mixed = _call(
        _merge_kernel, "merge", lidx, [hb, o_fox, o_nsa, o_dsa, w_gate, w_gate, w_gate, w_br_fox, w_br_nsa, w_br_dsa],
        grid=(ncg, t // tm),
        in_specs=[pl.BlockSpec((tm, d), lambda j, i, l: (i, 0)),
                  pl.BlockSpec((tm, fq), lambda j, i, l: (i, 0)),
                  pl.BlockSpec((tm, nsq), lambda j, i, l: (i, 0)),
                  pl.BlockSpec((tm, dq), lambda j, i, l: (i, 0)),
                  pl.BlockSpec((None, d, tng), lambda j, i, l: (l[0], 0, j)),
                  pl.BlockSpec((None, d, tng), lambda j, i, l: (l[0], 0, ncg + j)),
                  pl.BlockSpec((None, d, tng), lambda j, i, l: (l[0], 0, 2 * ncg + j)),
                  pl.BlockSpec((None, fq, tng), lambda j, i, l: (l[0], 0, j)),
                  pl.BlockSpec((None, nsq, tng), lambda j, i, l: (l[0], 0, j)),
                  pl.BlockSpec((None, dq, tng), lambda j, i, l: (l[0], 0, j))],
        out_specs=pl.BlockSpec((tm, tng), lambda j, i, l: (i, j)),
        out_shape=jax.ShapeDtypeStruct((t, d), BF16),
        scratch_shapes=[pltpu.VMEM((3, d, tng), BF16)])

    h, hb = _out_ln("out_ln", lidx, mixed, w_out, h, ln1_g, ln1_b)

    nff = D_FF // tn
    act = _call(
        _swiglu_kernel, "swiglu", lidx, [hb, w_ffn_in, w_ffn_in], grid=(nff, t // tm),
        in_specs=[pl.BlockSpec((tm, d), lambda j, i, l: (i, 0)),
                  pl.BlockSpec((None, d, tn), lambda j, i, l: (l[0], 0, j)),
                  pl.BlockSpec((None, d, tn), lambda j, i, l: (l[0], 0, nff + j))],
        out_specs=pl.BlockSpec((tm, tn), lambda j, i, l: (i, j)),
        out_shape=jax.ShapeDtypeStruct((t, D_FF), BF16),
        scratch_shapes=[pltpu.VMEM((2, d, tn), BF16)])
    h, hb = _ffn_ln("ffn_ln", lidx, act, w_ffn_out, h, ln2_g, ln2_b, tk=512)

    h, hb = _call(
        _ple_kernel, "ple", lidx, [hb, p, w_ple_in, w_ple_gate, h], grid=(t // tm, ncol),
        in_specs=[pl.BlockSpec((tm, d), lambda i, j, l: (i, 0)),
                  pl.BlockSpec((None, tm, PLE_DIM), lambda i, j, l: (l[0], i, 0)),
                  pl.BlockSpec((None, PLE_DIM, tn), lambda i, j, l: (l[0], 0, j)),
                  pl.BlockSpec((None, d, tn), lambda i, j, l: (l[0], 0, j)),
                  pl.BlockSpec((tm, tn), lambda i, j, l: (i, j))],
        out_specs=[pl.BlockSpec((tm, tn), lambda i, j, l: (i, j)),
                   pl.BlockSpec((tm, tn), lambda i, j, l: (i, j))],
        out_shape=[jax.ShapeDtypeStruct((t, d), F32), jax.ShapeDtypeStruct((t, d), BF16)])
    return h, hb


def kernel(x, p, w_in, fox_f_bias, nsa_pe_k, nsa_pe_v, nsa_cmp_k1, nsa_cmp_k2, nsa_cmp_v1, nsa_cmp_v2, dsa_kv_norm, dsa_kv_up, w_br_fox, w_br_nsa, w_br_dsa, w_out, ln1_g, ln1_b, w_ffn_in, w_ffn_out, ln2_g, ln2_b, w_ple_in, w_ple_gate):
    bsz, seq, d = x.shape
    depth = w_in.shape[0]
    t = bsz * seq
    assert d == D_MODEL and seq % 1024 == 0 and depth == DEPTH

    def seg(name):
        a, b = _IN_OFFSETS[name]
        return w_in[:, :, a:b]

    def zeros(n):
        return jnp.zeros((depth, d, n), w_in.dtype)

    cat = functools.partial(jnp.concatenate, axis=-1)
    w_a = cat([seg('fox_q'), seg('fox_k'), seg('fox_v'), seg('nsa_vs'), seg('nsa_vw')])
    w_b = cat([seg('nsa_q'), seg('nsa_ks'), seg('nsa_kw'), seg('dsa_q')])
    w_c = cat([seg('idx_q'), seg('idx_k'), zeros(IDX_DIM), zeros(IDX_DIM), seg('idx_k')])
    n_misc = FOX_HEADS + 3 * NSA_HEADS + IDX_HEADS
    w_d = cat([seg('dsa_ckv'), seg('fox_f'), seg('nsa_g'), seg('idx_w'), zeros(LANE - n_misc),
               seg('nsa_kc'), seg('nsa_vc')])
    w_gate = seg('gate')

    fox_bias = jnp.pad(fox_f_bias, ((0, 0), (0, LANE - FOX_HEADS))).reshape(depth, 1, LANE)
    half = CMP_STRIDE * HEAD_DIM
    weights = (
        w_a, w_b, w_c, w_d, w_gate, fox_bias,
        nsa_pe_k.reshape(depth, 2, half), nsa_pe_v.reshape(depth, 2, half),
        nsa_cmp_k1.astype(BF16), nsa_cmp_k2.astype(BF16), nsa_cmp_v1.astype(BF16), nsa_cmp_v2.astype(BF16),
        dsa_kv_norm.reshape(depth, 1, DSA_KV_RANK), dsa_kv_up.astype(BF16),
        w_br_fox.astype(BF16), w_br_nsa.astype(BF16), w_br_dsa.astype(BF16), w_out.astype(BF16),
        ln1_g.reshape(depth, 1, d), ln1_b.reshape(depth, 1, d),
        w_ffn_in, w_ffn_out.astype(BF16),
        ln2_g.reshape(depth, 1, d), ln2_b.reshape(depth, 1, d),
        p.reshape(depth, t, PLE_DIM).astype(BF16), w_ple_in.astype(BF16), w_ple_gate.astype(BF16),
    )

    cos, sin = _rope_tables(seq, HEAD_DIM)
    cos128 = cat([cos, cos])
    sin128 = cat([-sin, sin])
    cos_i, sin_i = _rope_tables(seq, IDX_DIM)
    zi = jnp.zeros_like(sin_i)
    cos64 = cat([cos_i, cos_i, cos_i, cos_i])
    sin64a = cat([zi, sin_i, zi, sin_i])
    sin64b = cat([-sin_i, zi, -sin_i, zi])
    n_cmp = (seq - CMP_LEN) // CMP_STRIDE + 1
    c_end = jnp.minimum(jnp.arange(N_CMP_PAD) * CMP_STRIDE + CMP_LEN - 1, seq - 1)
    cmp_cos, cmp_sin = cos128[c_end], sin128[c_end]
    n_slc = seq // SLC_LEN
    c_start = jnp.arange(N_CMP_PAD) * CMP_STRIDE
    s_start = jnp.arange(LANE) * SLC_LEN
    overlap = jnp.maximum(jnp.minimum(c_start[:, None] + CMP_LEN - 1, s_start[None, :] + SLC_LEN - 1)
                          - jnp.maximum(c_start[:, None], s_start[None, :]) + 1, 0).astype(F32) / CMP_LEN
    overlap = jnp.where((jnp.arange(N_CMP_PAD)[:, None] < n_cmp) & (jnp.arange(LANE)[None, :] < n_slc), overlap, 0.0)
    overlap = overlap.T.astype(BF16)
    block_of_key = (jnp.arange(seq)[:, None] // SLC_LEN == jnp.arange(LANE)[None, :]).astype(BF16)
    consts = (cos128, sin128, cos64, sin64a, sin64b, cmp_cos, cmp_sin, overlap, block_of_key)

    h = x.reshape(t, d)
    hb = h.astype(BF16)
    for layer in range(depth):
        lidx = jnp.full((1,), layer, I32)
        h, hb = _layer(lidx, h, hb, consts, weights, bsz, seq)
    return h.reshape(bsz, seq, d)
```

```python
import functools

import jax
import jax.numpy as jnp
from jax import lax
from jax.experimental import pallas as pl
from jax.experimental.pallas import tpu as pltpu

F32, BF16, I32 = jnp.float32, jnp.bfloat16, jnp.int32

D_MODEL = 2048
DEPTH = 4
HEAD_DIM = 128
ROPE_THETA = 10000.0
NEG_INF = -1e30
FOX_HEADS = 6
NSA_HEADS = 4
CMP_LEN = 32
CMP_STRIDE = 16
CMP_HIDDEN = 256
SLC_LEN = 64
SLC_TOPN = 16
WIN = 512
SLC_FORCE = 1e4
DSA_HEADS = 6
DSA_KV_RANK = 256
IDX_HEADS = 16
IDX_DIM = 64
IDX_TOPK_MAX = 256
D_FF = ((8 * D_MODEL + 3 * 256 - 1) // (3 * 256)) * 256
PLE_DIM = 256
ALPHA = (2 * DEPTH) ** 0.25
SCALE = HEAD_DIM ** -0.5

LANE = 128
VMEM_LIMIT_BYTES = 48 * 1024 * 1024
VMEM_LIMIT_LARGE_BYTES = 56 * 1024 * 1024

_IN_SPLITS = (
    ('fox_q', FOX_HEADS * HEAD_DIM), ('fox_k', FOX_HEADS * HEAD_DIM), ('fox_v', FOX_HEADS * HEAD_DIM), ('fox_f', FOX_HEADS),
    ('nsa_q', NSA_HEADS * HEAD_DIM), ('nsa_kc', HEAD_DIM), ('nsa_vc', HEAD_DIM), ('nsa_ks', HEAD_DIM), ('nsa_vs', HEAD_DIM),
    ('nsa_kw', HEAD_DIM), ('nsa_vw', HEAD_DIM), ('nsa_g', 3 * NSA_HEADS),
    ('dsa_q', DSA_HEADS * HEAD_DIM), ('dsa_ckv', DSA_KV_RANK), ('idx_q', IDX_HEADS * IDX_DIM), ('idx_k', IDX_DIM), ('idx_w', IDX_HEADS),
    ('gate', 3 * D_MODEL),
)
_IN_OFFSETS = {}
_off = 0
for _name, _width in _IN_SPLITS:
    _IN_OFFSETS[_name] = (_off, _off + _width)
    _off += _width

MISC_FOX_F = 0
MISC_NSA_G = FOX_HEADS
MISC_IDX_W = FOX_HEADS + 3 * NSA_HEADS

TQ = 256
TK = 256
N_CMP_PAD = 128


def _dot(a, b):
    return jnp.dot(a, b, preferred_element_type=F32)


def _dot_nt(a, b):
    return lax.dot_general(a, b, (((1,), (1,)), ((), ())), preferred_element_type=F32)


def _split3(a):
    a1 = a.astype(BF16)
    r1 = a - a1.astype(F32)
    a2 = r1.astype(BF16)
    a3 = (r1 - a2.astype(F32)).astype(BF16)
    return a1, a2, a3


def _call(kernel, name, lidx, args, grid, in_specs, out_specs, out_shape, scratch_shapes=(),
          vmem_limit_bytes=VMEM_LIMIT_BYTES):
    return pl.pallas_call(
        kernel,
        grid_spec=pltpu.PrefetchScalarGridSpec(
            num_scalar_prefetch=1, grid=grid, in_specs=in_specs, out_specs=out_specs,
            scratch_shapes=list(scratch_shapes)),
        out_shape=out_shape,
        compiler_params=pltpu.CompilerParams(
            dimension_semantics=("arbitrary",) * len(grid), vmem_limit_bytes=vmem_limit_bytes),
        name=name,
    )(lidx, *args)


def _regroup_plan():
    plan = []

    def whole(name):
        a, b = _IN_OFFSETS[name]
        for g in range((b - a) // LANE):
            plan.append(((a + g * LANE, 0, LANE),))

    for name in ('fox_q', 'fox_k', 'fox_v', 'nsa_vs', 'nsa_vw'):
        whole(name)
    for name in ('nsa_q', 'nsa_ks', 'nsa_kw', 'dsa_q'):
        whole(name)
    whole('gate')
    whole('idx_q')
    ik = _IN_OFFSETS['idx_k'][0]
    plan.append(((ik, 0, IDX_DIM),))
    plan.append(((ik - IDX_DIM, IDX_DIM, LANE),))
    whole('dsa_ckv')
    misc, lane0 = [], 0
    for name in ('fox_f', 'nsa_g', 'idx_w'):
        a, b = _IN_OFFSETS[name]
        misc.append((a - lane0, lane0, lane0 + b - a))
        lane0 += b - a
    plan.append(tuple(misc))
    whole('nsa_kc')
    whole('nsa_vc')
    return tuple(plan)


_REGROUP_PLAN = _regroup_plan()
COL_PLAIN = 0
N_PLAIN = (3 * FOX_HEADS + 2) * HEAD_DIM
COL_ROPE128 = COL_PLAIN + N_PLAIN
N_ROPE128 = (NSA_HEADS + 2 + DSA_HEADS) * HEAD_DIM
COL_GATE = COL_ROPE128 + N_ROPE128
COL_ROPE64 = COL_GATE + 3 * D_MODEL
N_ROPE64 = IDX_HEADS * IDX_DIM + 2 * LANE
COL_F32 = COL_ROPE64 + N_ROPE64
N_F32 = DSA_KV_RANK + 3 * LANE
N_REGROUPED = COL_F32 + N_F32
assert N_REGROUPED == len(_REGROUP_PLAN) * LANE


def _regroup_kernel(l_ref, x_ref, o_ref):
    lane = lax.broadcasted_iota(I32, (x_ref.shape[0], LANE), 1)
    for g, pieces in enumerate(_REGROUP_PLAN):
        out = None
        for src0, lo, hi in pieces:
            win = x_ref[:, src0:src0 + LANE]
            if (lo, hi) != (0, LANE):
                win = jnp.where((lane >= lo) & (lane < hi), win, 0.0)
            out = win if out is None else out + win
        o_ref[:, g * LANE:(g + 1) * LANE] = out.astype(o_ref.dtype)


def _regroup(w_in):
    depth, d, n_in = w_in.shape
    tr = 256
    return _call(_regroup_kernel, "regroup_w_in", jnp.zeros((1,), I32), [w_in], grid=(depth, d // tr),
                 in_specs=[pl.BlockSpec((None, tr, n_in), lambda li, r, l: (li, r, 0))],
                 out_specs=pl.BlockSpec((None, tr, N_REGROUPED), lambda li, r, l: (li, r, 0)),
                 out_shape=jax.ShapeDtypeStruct((depth, d, N_REGROUPED), BF16))


def _proj_kernel(l_ref, x_ref, w_ref, *refs, shifts):
    o_ref = refs[-1]
    acc = _dot(x_ref[...], w_ref[...])
    if not shifts:
        o_ref[...] = acc.astype(o_ref.dtype)
        return
    c_ref = refs[0]
    s_refs = refs[1:-1]
    for g in range(acc.shape[1] // LANE):
        xg = acc[:, g * LANE:(g + 1) * LANE]
        out = xg * c_ref[...]
        for shift, s_ref in zip(shifts, s_refs):
            out = out + pltpu.roll(xg, shift, 1) * s_ref[...]
        o_ref[:, g * LANE:(g + 1) * LANE] = out.astype(o_ref.dtype)


def _proj(name, lidx, hb, w, col0, n, tm, tn, out_dtype, seq, rope=None):
    t, d = hb.shape
    jb = col0 // tn
    assert jb * tn == col0
    in_specs = [pl.BlockSpec((tm, d), lambda i, j, l: (i, 0)),
                pl.BlockSpec((None, d, tn), lambda i, j, l: (l[0], 0, jb + j))]
    args = [hb, w]
    shifts = ()
    if rope is not None:
        shifts, tables = rope[0], rope[1:]
        nrow = seq // tm
        for tab in tables:
            in_specs.append(pl.BlockSpec((tm, LANE), lambda i, j, l: (i % nrow, 0)))
            args.append(tab)
    return _call(functools.partial(_proj_kernel, shifts=shifts), name, lidx, args,
                 grid=(t // tm, n // tn), in_specs=in_specs,
                 out_specs=pl.BlockSpec((tm, tn), lambda i, j, l: (i, j)),
                 out_shape=jax.ShapeDtypeStruct((t, n), out_dtype))


def _cum_kernel(l_ref, f_ref, bias_ref, o_ref):
    chunk = 256
    row = lax.broadcasted_iota(I32, (chunk, chunk), 0)
    col = lax.broadcasted_iota(I32, (chunk, chunk), 1)
    tri = jnp.where(row >= col, 1.0, 0.0).astype(BF16)
    carry = jnp.zeros((1, LANE), F32)
    for c in range(f_ref.shape[0] // chunk):
        logf = jax.nn.log_sigmoid(f_ref[c * chunk:(c + 1) * chunk, :] + bias_ref[...])
        l1, l2, l3 = _split3(logf)
        cs = _dot(tri, l1) + _dot(tri, l2) + _dot(tri, l3) + carry
        o_ref[c * chunk:(c + 1) * chunk, :] = cs
        carry = cs[chunk - 1:chunk, :]


def _max_lanes(x):
    out = x[:, 0:LANE]
    for g in range(1, x.shape[1] // LANE):
        out = jnp.maximum(out, x[:, g * LANE:(g + 1) * LANE])
    return out


def _tile_lanes(x, n):
    return jnp.concatenate([x] * n, axis=1)


def _row_max_to_lanes(mx):
    return jnp.broadcast_to(jnp.max(mx, axis=-1, keepdims=True), mx.shape)


def _exp_pv(s, m_lanes, v):
    p = jnp.exp(s - _tile_lanes(m_lanes, s.shape[1] // LANE)).astype(BF16)
    return _dot(p, jnp.concatenate([v, jnp.ones_like(v)], axis=1))


def _fox_kernel(l_ref, q_ref, k_ref, v_ref, cq_ref, ck_ref, o_ref, s_ref, mx_ref, cqb_ref, acc_ref):
    qi = pl.program_id(1)
    nh = FOX_HEADS
    nchunk = ck_ref.shape[0] // 8
    row = lax.broadcasted_iota(I32, (TQ, TK), 0)
    col = lax.broadcasted_iota(I32, (TQ, TK), 1)
    cq_all = cq_ref[...]
    for h in range(nh):
        cqb_ref[h] = jnp.broadcast_to(cq_all[:, h:h + 1], (TQ, LANE))
    mx_ref[...] = jnp.full(mx_ref.shape, NEG_INF, F32)
    acc_ref[...] = jnp.zeros(acc_ref.shape, F32)

    def score_step(j, diagonal):
        start = pl.multiple_of(j * TK, TK)
        for h in range(nh):
            hs = slice(h * HEAD_DIM, (h + 1) * HEAD_DIM)
            s = (_dot_nt(q_ref[:, hs], k_ref[pl.ds(start, TK), hs]) * SCALE
                 + _tile_lanes(cqb_ref[h], TK // LANE) - ck_ref[pl.ds(h * nchunk + j, 1), :])
            if diagonal:
                s = jnp.where(col <= row, s, NEG_INF)
            s_ref[h, j] = s
            mx_ref[h] = jnp.maximum(mx_ref[h], _max_lanes(s))

    def score_body(j, carry):
        score_step(j, False)
        return carry

    lax.fori_loop(0, qi, score_body, 0)
    score_step(qi, True)
    for h in range(nh):
        mx_ref[h] = _row_max_to_lanes(mx_ref[h])

    def pv_body(j, carry):
        start = pl.multiple_of(j * TK, TK)
        for h in range(nh):
            hs = slice(h * HEAD_DIM, (h + 1) * HEAD_DIM)
            acc_ref[h] += _exp_pv(s_ref[h, j], mx_ref[h], v_ref[pl.ds(start, TK), hs])
        return carry

    lax.fori_loop(0, qi + 1, pv_body, 0)
    for h in range(nh):
        a = acc_ref[h]
        o_ref[:, h * HEAD_DIM:(h + 1) * HEAD_DIM] = (a[:, :HEAD_DIM] / a[:, HEAD_DIM:]).astype(o_ref.dtype)


def _cmp_kernel(l_ref, rk_ref, rv_ref, pek_ref, pev_ref, wk1_ref, wk2_ref, wv1_ref, wv2_ref,
                c_ref, s_ref, ko_ref, vo_ref):
    half = CMP_STRIDE * HEAD_DIM

    def compress(r_ref, pe_ref, w1_ref, w2_ref):
        r = r_ref[...]
        lo = (r + pe_ref[0:1, :]).astype(BF16)
        hi = (r + pe_ref[1:2, :]).astype(BF16)
        a = _dot(lo, w1_ref[0:half, :])
        b = _dot(hi, w1_ref[half:2 * half, :])
        hid = a + pltpu.roll(b, N_CMP_PAD - 1, 0)
        return _dot(jax.nn.gelu(hid).astype(BF16), w2_ref[...])

    kc = compress(rk_ref, pek_ref, wk1_ref, wk2_ref)
    kc = kc * c_ref[...] + pltpu.roll(kc, HEAD_DIM // 2, 1) * s_ref[...]
    ko_ref[...] = kc.astype(ko_ref.dtype)
    vo_ref[...] = compress(rv_ref, pev_ref, wv1_ref, wv2_ref).astype(vo_ref.dtype)


def _masked_softmax(s, mask):
    s = jnp.where(mask, s, NEG_INF)
    m = jnp.max(s, axis=-1, keepdims=True)
    e = jnp.where(mask, jnp.exp(s - m), 0.0)
    den = jnp.sum(e, axis=-1, keepdims=True)
    return e, jnp.where(den > 0.0, den, 1.0)


def _nsa_kernel(l_ref, q_ref, ks_ref, kw_ref, vs_ref, vw_ref, kc_ref, vc_ref, g_ref, ovt_ref, et_ref,
                o_ref, q_sc, s_ref, mx_ref, acc_ref):
    qi = pl.program_id(1)
    nh = NSA_HEADS
    for h in range(nh):
        q_sc[h * TQ:(h + 1) * TQ, :] = q_ref[:, h * HEAD_DIM:(h + 1) * HEAD_DIM]
    q = q_sc[...]
    pos = qi * TQ + lax.broadcasted_iota(I32, (TQ, 1), 0)
    lane = lax.broadcasted_iota(I32, (TQ, LANE), 1)

    cvis = (lane * CMP_STRIDE + (CMP_LEN - 1)) <= pos
    s_c = (_dot_nt(q, kc_ref[...]) * SCALE).reshape(nh, TQ, N_CMP_PAD)
    e_c, den_c = _masked_softmax(s_c, cvis[None])
    p_c = e_c / den_c
    o_cmp = _dot(p_c.reshape(nh * TQ, N_CMP_PAD).astype(BF16), vc_ref[...])

    p_sum = p_c[0]
    for h in range(1, nh):
        p_sum = p_sum + p_c[h]
    p1, p2, p3 = _split3(p_sum)
    ovt = ovt_ref[...]
    n_slc = et_ref.shape[0] // SLC_LEN
    imp = (_dot_nt(ovt, p1) + _dot_nt(ovt, p2) + _dot_nt(ovt, p3))[0:n_slc, :]
    blk = lax.broadcasted_iota(I32, (n_slc, TQ), 0)
    pos_t = qi * TQ + lax.broadcasted_iota(I32, (n_slc, TQ), 1)
    cur = lax.shift_right_logical(pos_t, SLC_LEN.bit_length() - 1)
    forced = (blk == 0) | (blk == cur) | (blk == cur - 1)
    imp = jnp.where(forced, SLC_FORCE, jnp.where(blk * SLC_LEN <= pos_t, imp, -SLC_FORCE))
    rank = jnp.zeros((n_slc, TQ), F32)
    for k in range(n_slc):
        ik = imp[k:k + 1, :]
        ahead = (ik > imp) | ((ik == imp) & (blk > k))
        rank = rank + jnp.where(ahead, 1.0, 0.0)
    sel_t = jnp.where(rank < float(min(SLC_TOPN, n_slc)), 1.0, 0.0)
    sel_t = jnp.concatenate([sel_t, jnp.zeros((LANE - n_slc, TQ), F32)], axis=0)
    sel = sel_t.T.astype(BF16)

    mx_ref[...] = jnp.full(mx_ref.shape, NEG_INF, F32)
    acc_ref[...] = jnp.zeros(acc_ref.shape, F32)
    row = lax.broadcasted_iota(I32, (TQ, TK), 0)
    col = lax.broadcasted_iota(I32, (TQ, TK), 1)

    def score_step(j, diagonal):
        start = pl.multiple_of(j * TK, TK)
        vis = _dot_nt(sel, et_ref[pl.ds(start, TK), :]) > 0.5
        if diagonal:
            vis = vis & (col <= row)
        s = (_dot_nt(q_sc[...], ks_ref[pl.ds(start, TK), :]) * SCALE).reshape(nh, TQ, TK)
        s = jnp.where(vis[None], s, NEG_INF).reshape(nh * TQ, TK)
        s_ref[j] = s
        mx_ref[...] = jnp.maximum(mx_ref[...], _max_lanes(s))

    def score_body(j, carry):
        score_step(j, False)
        return carry

    lax.fori_loop(0, qi, score_body, 0)
    score_step(qi, True)
    mx_ref[...] = _row_max_to_lanes(mx_ref[...])

    def pv_body(j, carry):
        start = pl.multiple_of(j * TK, TK)
        acc_ref[...] += _exp_pv(s_ref[j], mx_ref[...], vs_ref[pl.ds(start, TK), :])
        return carry

    lax.fori_loop(0, qi + 1, pv_body, 0)
    a_slc = acc_ref[...]
    o_slc = a_slc[:, :HEAD_DIM] / a_slc[:, HEAD_DIM:]

    band = WIN + TQ
    start_w = pl.multiple_of(jnp.maximum(qi * TQ - WIN, 0), TQ)
    kwb = kw_ref[pl.ds(start_w, band), :]
    vwb = vw_ref[pl.ds(start_w, band), :]
    dist = pos - (start_w + lax.broadcasted_iota(I32, (TQ, band), 1))
    wvis = (dist >= 0) & (dist < WIN)
    s_w = (_dot_nt(q, kwb) * SCALE).reshape(nh, TQ, band)
    e_w, den_w = _masked_softmax(s_w, wvis[None])
    o_win = _dot(e_w.reshape(nh * TQ, band).astype(BF16), vwb) / den_w.reshape(nh * TQ, 1)

    gate = jax.nn.sigmoid(g_ref[...])
    for h in range(nh):
        rows = slice(h * TQ, (h + 1) * TQ)
        g0 = gate[:, MISC_NSA_G + 3 * h:MISC_NSA_G + 3 * h + 1]
        g1 = gate[:, MISC_NSA_G + 3 * h + 1:MISC_NSA_G + 3 * h + 2]
        g2 = gate[:, MISC_NSA_G + 3 * h + 2:MISC_NSA_G + 3 * h + 3]
        out = g0 * o_cmp[rows] + g1 * o_slc[rows] + g2 * o_win[rows]
        o_ref[:, h * HEAD_DIM:(h + 1) * HEAD_DIM] = out.astype(o_ref.dtype)


def _dsa_kv_kernel(l_ref, x_ref, g_ref, up_ref, c_ref, s_ref, ko_ref, vo_ref):
    x = x_ref[...]
    r = x * lax.rsqrt(jnp.mean(jnp.square(x), axis=-1, keepdims=True) + 1e-6) * g_ref[...]
    kv = _dot(r.astype(BF16), up_ref[...])
    k = kv[:, :HEAD_DIM]
    k = k * c_ref[...] + pltpu.roll(k, HEAD_DIM // 2, 1) * s_ref[...]
    ko_ref[...] = k.astype(ko_ref.dtype)
    vo_ref[...] = kv[:, HEAD_DIM:].astype(vo_ref.dtype)


def _dsa_kernel(l_ref, q_ref, kd_ref, vd_ref, iq_ref, ika_ref, ikb_ref, w_ref, o_ref,
                key_ref, keyt_ref, wb_ref, sel_ref, q_sc, s_ref, mx_ref, acc_ref):
    qi = pl.program_id(1)
    nh = DSA_HEADS
    seq = kd_ref.shape[0]
    k_top = float(min(IDX_TOPK_MAX, seq // 4))
    row = lax.broadcasted_iota(I32, (TQ, TK), 0)
    col = lax.broadcasted_iota(I32, (TQ, TK), 1)
    int_min = jnp.int32(-2 ** 31)
    idx_bits = (seq - 1).bit_length()

    w_all = w_ref[...] * (IDX_HEADS ** -0.5 * IDX_DIM ** -0.5)
    for h in range(IDX_HEADS):
        wb_ref[h] = jnp.broadcast_to(w_all[:, MISC_IDX_W + h:MISC_IDX_W + h + 1], (TQ, LANE))

    def score_step(j, diagonal):
        start = pl.multiple_of(j * TK, TK)
        ka = ika_ref[pl.ds(start, TK), :]
        kb = ikb_ref[pl.ds(start, TK), :]
        sc = jnp.zeros((TQ, TK), F32)
        for g in range(IDX_HEADS // 2):
            qpair = iq_ref[:, g * LANE:(g + 1) * LANE]
            wa = _tile_lanes(wb_ref[2 * g], TK // LANE)
            wb = _tile_lanes(wb_ref[2 * g + 1], TK // LANE)
            sc = sc + jnp.maximum(_dot_nt(qpair, ka), 0.0) * wa + jnp.maximum(_dot_nt(qpair, kb), 0.0) * wb
        if diagonal:
            sc = jnp.where(col <= row, sc, NEG_INF)
        bits = lax.bitcast_convert_type(sc, I32)
        key = jnp.where(bits < 0, bits ^ jnp.int32(0x7FFFFFFF), bits)
        key = jnp.where(sc == 0.0, 0, key)
        key_ref[j] = key
        keyt_ref[j] = key.T

    def score_body(j, carry):
        score_step(j, False)
        return carry

    lax.fori_loop(0, qi, score_body, 0)
    score_step(qi, True)

    krow = lax.broadcasted_iota(I32, (TK, TQ), 0)

    def count(pred_fn):
        def body(j, acc):
            hit = jnp.where(pred_fn(keyt_ref[j], j), 1.0, 0.0)
            return acc + jnp.sum(hit.reshape(TK // 8, 8, TQ), axis=0)
        part = lax.fori_loop(0, qi + 1, body, jnp.zeros((8, TQ), F32))
        return jnp.sum(part, axis=0, keepdims=True)

    thr = jnp.where(count(lambda key, j: key >= 0) >= k_top, jnp.int32(0), int_min)

    def value_bit(i, thr):
        cand = thr + (jnp.int32(1) << (30 - i))
        return jnp.where(count(lambda key, j: key >= cand) >= k_top, cand, thr)

    thr = lax.fori_loop(0, 31, value_bit, thr)
    n_gt = count(lambda key, j: key > thr)
    n_ge = count(lambda key, j: key >= thr)

    def index_bisect():
        def index_bit(i, cut):
            cand = cut + (jnp.int32(1) << (idx_bits - 1 - i))
            n_before = count(lambda key, j: (key == thr) & (j * TK + krow < cand))
            return jnp.where(n_gt + n_before < k_top, cand, cut)
        return lax.fori_loop(0, idx_bits, index_bit, jnp.zeros((1, TQ), I32))

    def take_all_ties():
        return jnp.full((1, TQ), seq - 1, I32)

    cut = lax.cond(jnp.max(n_ge) > k_top, index_bisect, take_all_ties)
    sel_ref[0] = jnp.broadcast_to(thr, (LANE, TQ)).T
    sel_ref[1] = jnp.broadcast_to(cut, (LANE, TQ)).T

    for h in range(nh):
        q_sc[h * TQ:(h + 1) * TQ, :] = q_ref[:, h * HEAD_DIM:(h + 1) * HEAD_DIM]
    mx_ref[...] = jnp.full(mx_ref.shape, NEG_INF, F32)
    acc_ref[...] = jnp.zeros(acc_ref.shape, F32)

    def attn_score_step(j, diagonal):
        start = pl.multiple_of(j * TK, TK)
        key = key_ref[j]
        thr_t = _tile_lanes(sel_ref[0], TK // LANE)
        cut_t = _tile_lanes(sel_ref[1], TK // LANE)
        vis = (key > thr_t) | ((key == thr_t) & (j * TK + col <= cut_t))
        if diagonal:
            vis = vis & (col <= row)
        s = (_dot_nt(q_sc[...], kd_ref[pl.ds(start, TK), :]) * SCALE).reshape(nh, TQ, TK)
        s = jnp.where(vis[None], s, NEG_INF).reshape(nh * TQ, TK)
        s_ref[j] = s
        mx_ref[...] = jnp.maximum(mx_ref[...], _max_lanes(s))

    def attn_score_body(j, carry):
        attn_score_step(j, False)
        return carry

    lax.fori_loop(0, qi, attn_score_body, 0)
    attn_score_step(qi, True)
    mx_ref[...] = _row_max_to_lanes(mx_ref[...])

    def pv_body(j, carry):
        start = pl.multiple_of(j * TK, TK)
        acc_ref[...] += _exp_pv(s_ref[j], mx_ref[...], vd_ref[pl.ds(start, TK), :])
        return carry

    lax.fori_loop(0, qi + 1, pv_body, 0)
    a = acc_ref[...]
    out = a[:, :HEAD_DIM] / a[:, HEAD_DIM:]
    for h in range(nh):
        o_ref[:, h * HEAD_DIM:(h + 1) * HEAD_DIM] = out[h * TQ:(h + 1) * TQ].astype(o_ref.dtype)


def _merge_kernel(l_ref, hb_ref, of_ref, on_ref, od_ref, wg1_ref, wg2_ref, wg3_ref, wf_ref, wn_ref, wd_ref, o_ref):
    hb = hb_ref[...]
    mixed = (jax.nn.sigmoid(_dot(hb, wg1_ref[...])) * _dot(of_ref[...], wf_ref[...])
             + jax.nn.sigmoid(_dot(hb, wg2_ref[...])) * _dot(on_ref[...], wn_ref[...])
             + jax.nn.sigmoid(_dot(hb, wg3_ref[...])) * _dot(od_ref[...], wd_ref[...]))
    o_ref[...] = mixed.astype(o_ref.dtype)


LN_SUB_ROWS = 256


def _layer_norm_store(y, g_ref, b_ref, of_ref, ob_ref, rows):
    mu = jnp.mean(y, axis=-1, keepdims=True)
    yc = y - mu
    var = jnp.mean(jnp.square(yc), axis=-1, keepdims=True)
    out = yc * lax.rsqrt(var + 1e-5) * g_ref[...] + b_ref[...]
    of_ref[rows, :] = out
    ob_ref[rows, :] = out.astype(ob_ref.dtype)


def _out_ln_kernel(l_ref, x_ref, w_ref, h_ref, g_ref, b_ref, of_ref, ob_ref):
    w = w_ref[...]
    for r in range(x_ref.shape[0] // LN_SUB_ROWS):
        rows = slice(r * LN_SUB_ROWS, (r + 1) * LN_SUB_ROWS)
        y = ALPHA * h_ref[rows, :] + _dot(x_ref[rows, :], w)
        _layer_norm_store(y, g_ref, b_ref, of_ref, ob_ref, rows)


def _out_ln(name, lidx, x, w, h, g, b):
    t, kdim = x.shape
    d = h.shape[1]
    tm = 512
    return _call(_out_ln_kernel, name, lidx, [x, w, h, g, b], grid=(t // tm,),
                 in_specs=[pl.BlockSpec((tm, kdim), lambda i, l: (i, 0)),
                           pl.BlockSpec((None, kdim, d), lambda i, l: (l[0], 0, 0)),
                           pl.BlockSpec((tm, d), lambda i, l: (i, 0)),
                           pl.BlockSpec((None, 1, d), lambda i, l: (l[0], 0, 0)),
                           pl.BlockSpec((None, 1, d), lambda i, l: (l[0], 0, 0))],
                 out_specs=[pl.BlockSpec((tm, d), lambda i, l: (i, 0)),
                            pl.BlockSpec((tm, d), lambda i, l: (i, 0))],
                 out_shape=[jax.ShapeDtypeStruct((t, d), F32), jax.ShapeDtypeStruct((t, d), BF16)])


def _ffn_ln_kernel(l_ref, x_ref, w_ref, h_ref, g_ref, b_ref, of_ref, ob_ref, *, nk):
    k = pl.program_id(1)

    @pl.when(k == 0)
    def _():
        of_ref[...] = ALPHA * h_ref[...] + _dot(x_ref[...], w_ref[...].astype(BF16))

    @pl.when((k > 0) & (k < nk - 1))
    def _():
        of_ref[...] += _dot(x_ref[...], w_ref[...].astype(BF16))

    @pl.when(k == nk - 1)
    def _():
        w = w_ref[...].astype(BF16)
        for r in range(x_ref.shape[0] // LN_SUB_ROWS):
            rows = slice(r * LN_SUB_ROWS, (r + 1) * LN_SUB_ROWS)
            y = of_ref[rows, :] + _dot(x_ref[rows, :], w)
            _layer_norm_store(y, g_ref, b_ref, of_ref, ob_ref, rows)


def _ffn_ln(name, lidx, x, w, h, g, b, tk):
    t, kdim = x.shape
    d = h.shape[1]
    tm = 1024
    nk = kdim // tk
    assert nk >= 2
    return _call(functools.partial(_ffn_ln_kernel, nk=nk), name, lidx, [x, w, h, g, b],
                 grid=(t // tm, nk),
                 in_specs=[pl.BlockSpec((tm, tk), lambda i, k, l: (i, k)),
                           pl.BlockSpec((None, tk, d), lambda i, k, l: (l[0], k, 0)),
                           pl.BlockSpec((tm, d), lambda i, k, l: (i, 0)),
                           pl.BlockSpec((None, 1, d), lambda i, k, l: (l[0], 0, 0)),
                           pl.BlockSpec((None, 1, d), lambda i, k, l: (l[0], 0, 0))],
                 out_specs=[pl.BlockSpec((tm, d), lambda i, k, l: (i, 0)),
                            pl.BlockSpec((tm, d), lambda i, k, l: (i, 0))],
                 out_shape=[jax.ShapeDtypeStruct((t, d), F32), jax.ShapeDtypeStruct((t, d), BF16)],
                 vmem_limit_bytes=VMEM_LIMIT_LARGE_BYTES)


def _swiglu_kernel(l_ref, x_ref, wa_ref, wb_ref, o_ref, wab_ref):
    @pl.when(pl.program_id(1) == 0)
    def _():
        wab_ref[0] = wa_ref[...].astype(BF16)
        wab_ref[1] = wb_ref[...].astype(BF16)

    x = x_ref[...]
    a = _dot(x, wab_ref[0])
    o_ref[...] = (jax.nn.silu(a) * _dot(x, wab_ref[1])).astype(o_ref.dtype)


def _ple_kernel(l_ref, hb_ref, p_ref, wpi_ref, wpg_ref, h_ref, of_ref, ob_ref):
    p_in = _dot(p_ref[...].astype(BF16), wpi_ref[...].astype(BF16))
    out = h_ref[...] + p_in * jax.nn.sigmoid(_dot(hb_ref[...], wpg_ref[...].astype(BF16)))
    of_ref[...] = out
    ob_ref[...] = out.astype(ob_ref.dtype)


def _rope_tables(n, dim):
    inv = 1.0 / (ROPE_THETA ** (jnp.arange(0, dim, 2, dtype=F32) / dim))
    ang = jnp.arange(n, dtype=F32)[:, None] * inv[None, :]
    return jnp.cos(ang), jnp.sin(ang)


def _layer(lidx, h, hb, consts, weights, bsz, seq):
    (cos128, sin128, cos64, sin64a, sin64b, cmp_cos, cmp_sin, overlap, block_of_key) = consts
    (w_proj, fox_bias, pe_k, pe_v, wk1, wk2, wv1, wv2, kv_norm, kv_up,
     w_br_fox, w_br_nsa, w_br_dsa, w_out, ln1_g, ln1_b, w_ffn_in, w_ffn_out, ln2_g, ln2_b, p, w_ple_in, w_ple_gate) = weights
    t = bsz * seq
    nq = seq // TQ
    d = D_MODEL

    z_a = _proj("proj_plain", lidx, hb, w_proj, COL_PLAIN, N_PLAIN, 2048, 512, BF16, seq)
    z_b = _proj("proj_rope128", lidx, hb, w_proj, COL_ROPE128, N_ROPE128, 2048, 512, BF16, seq,
                rope=((HEAD_DIM // 2,), cos128, sin128))
    z_c = _proj("proj_rope64", lidx, hb, w_proj, COL_ROPE64, N_ROPE64, 2048, 640, BF16, seq,
                rope=((IDX_DIM // 2, LANE - IDX_DIM // 2), cos64, sin64a, sin64b))
    z_d = _proj("proj_f32", lidx, hb, w_proj, COL_F32, N_F32, 1024, 640, F32, seq)

    cum = _call(_cum_kernel, "fox_cum", lidx, [z_d, fox_bias], grid=(bsz,),
                in_specs=[pl.BlockSpec((seq, LANE), lambda b, l: (b, 2)),
                          pl.BlockSpec((None, 1, LANE), lambda b, l: (l[0], 0, 0))],
                out_specs=pl.BlockSpec((seq, LANE), lambda b, l: (b, 0)),
                out_shape=jax.ShapeDtypeStruct((t, LANE), F32))
    cum_rows = jnp.transpose(cum.reshape(bsz, seq, LANE)[:, :, :8], (0, 2, 1)).reshape(bsz, 8 * (seq // TK), TK)
    fq = FOX_HEADS * HEAD_DIM
    o_fox = _call(_fox_kernel, "fox_attn", lidx, [z_a, z_a, z_a, cum, cum_rows], grid=(bsz, nq),
                  in_specs=[pl.BlockSpec((TQ, fq), lambda b, i, l: (b * nq + i, 0)),
                            pl.BlockSpec((seq, fq), lambda b, i, l: (b, 1)),
                            pl.BlockSpec((seq, fq), lambda b, i, l: (b, 2)),
                            pl.BlockSpec((TQ, LANE), lambda b, i, l: (b * nq + i, 0)),
                            pl.BlockSpec((None, 8 * (seq // TK), TK), lambda b, i, l: (b, 0, 0))],
                  out_specs=pl.BlockSpec((TQ, fq), lambda b, i, l: (b * nq + i, 0)),
                  out_shape=jax.ShapeDtypeStruct((t, fq), BF16),
                  scratch_shapes=[pltpu.VMEM((FOX_HEADS, seq // TK, TQ, TK), F32),
                                  pltpu.VMEM((FOX_HEADS, TQ, LANE), F32), pltpu.VMEM((FOX_HEADS, TQ, LANE), F32),
                                  pltpu.VMEM((FOX_HEADS, TQ, 2 * HEAD_DIM), F32)])

    nchunk = seq // CMP_STRIDE
    half = CMP_STRIDE * HEAD_DIM
    r_k = z_d[:, 3 * LANE:4 * LANE].reshape(bsz * nchunk, half)
    r_v = z_d[:, 4 * LANE:5 * LANE].reshape(bsz * nchunk, half)
    k_cmp, v_cmp = _call(
        _cmp_kernel, "nsa_compress", lidx, [r_k, r_v, pe_k, pe_v, wk1, wk2, wv1, wv2, cmp_cos, cmp_sin], grid=(bsz,),
        in_specs=[pl.BlockSpec((nchunk, half), lambda b, l: (b, 0)),
                  pl.BlockSpec((nchunk, half), lambda b, l: (b, 0)),
                  pl.BlockSpec((None, 2, half), lambda b, l: (l[0], 0, 0)),
                  pl.BlockSpec((None, 2, half), lambda b, l: (l[0], 0, 0)),
                  pl.BlockSpec((None, 2 * half, CMP_HIDDEN), lambda b, l: (l[0], 0, 0)),
                  pl.BlockSpec((None, CMP_HIDDEN, HEAD_DIM), lambda b, l: (l[0], 0, 0)),
                  pl.BlockSpec((None, 2 * half, CMP_HIDDEN), lambda b, l: (l[0], 0, 0)),
                  pl.BlockSpec((None, CMP_HIDDEN, HEAD_DIM), lambda b, l: (l[0], 0, 0)),
                  pl.BlockSpec((nchunk, HEAD_DIM), lambda b, l: (0, 0)),
                  pl.BlockSpec((nchunk, HEAD_DIM), lambda b, l: (0, 0))],
        out_specs=[pl.BlockSpec((nchunk, HEAD_DIM), lambda b, l: (b, 0)),
                   pl.BlockSpec((nchunk, HEAD_DIM), lambda b, l: (b, 0))],
        out_shape=[jax.ShapeDtypeStruct((bsz * nchunk, HEAD_DIM), BF16)] * 2)
    nsq = NSA_HEADS * HEAD_DIM
    o_nsa = _call(
        _nsa_kernel, "nsa_attn", lidx, [z_b, z_b, z_b, z_a, z_a, k_cmp, v_cmp, z_d, overlap, block_of_key], grid=(bsz, nq),
        in_specs=[pl.BlockSpec((TQ, nsq), lambda b, i, l: (b * nq + i, 0)),
                  pl.BlockSpec((seq, LANE), lambda b, i, l: (b, 4)),
                  pl.BlockSpec((seq, LANE), lambda b, i, l: (b, 5)),
                  pl.BlockSpec((seq, LANE), lambda b, i, l: (b, 18)),
                  pl.BlockSpec((seq, LANE), lambda b, i, l: (b, 19)),
                  pl.BlockSpec((nchunk, HEAD_DIM), lambda b, i, l: (b, 0)),
                  pl.BlockSpec((nchunk, HEAD_DIM), lambda b, i, l: (b, 0)),
                  pl.BlockSpec((TQ, LANE), lambda b, i, l: (b * nq + i, 2)),
                  pl.BlockSpec((N_CMP_PAD, LANE), lambda b, i, l: (0, 0)),
                  pl.BlockSpec((seq, LANE), lambda b, i, l: (0, 0))],
        out_specs=pl.BlockSpec((TQ, nsq), lambda b, i, l: (b * nq + i, 0)),
        out_shape=jax.ShapeDtypeStruct((t, nsq), BF16),
        scratch_shapes=[pltpu.VMEM((NSA_HEADS * TQ, HEAD_DIM), BF16),
                        pltpu.VMEM((seq // TK, NSA_HEADS * TQ, TK), F32),
                        pltpu.VMEM((NSA_HEADS * TQ, LANE), F32),
                        pltpu.VMEM((NSA_HEADS * TQ, 2 * HEAD_DIM), F32)])

    tm_kv = 1024
    k_d, v_d = _call(
        _dsa_kv_kernel, "dsa_kv", lidx, [z_d, kv_norm, kv_up, cos128, sin128], grid=(t // tm_kv,),
        in_specs=[pl.BlockSpec((tm_kv, DSA_KV_RANK), lambda i, l: (i, 0)),
                  pl.BlockSpec((None, 1, DSA_KV_RANK), lambda i, l: (l[0], 0, 0)),
                  pl.BlockSpec((None, DSA_KV_RANK, 2 * HEAD_DIM), lambda i, l: (l[0], 0, 0)),
                  pl.BlockSpec((tm_kv, LANE), lambda i, l: (i % (seq // tm_kv), 0)),
                  pl.BlockSpec((tm_kv, LANE), lambda i, l: (i % (seq // tm_kv), 0))],
        out_specs=[pl.BlockSpec((tm_kv, HEAD_DIM), lambda i, l: (i, 0)),
                   pl.BlockSpec((tm_kv, HEAD_DIM), lambda i, l: (i, 0))],
        out_shape=[jax.ShapeDtypeStruct((t, HEAD_DIM), BF16)] * 2)
    dq = DSA_HEADS * HEAD_DIM
    iqw = IDX_HEADS * IDX_DIM
    o_dsa = _call(
        _dsa_kernel, "dsa_attn", lidx, [z_b, k_d, v_d, z_c, z_c, z_c, z_d], grid=(bsz, nq),
        in_specs=[pl.BlockSpec((TQ, dq), lambda b, i, l: (b * nq + i, 1)),
                  pl.BlockSpec((seq, HEAD_DIM), lambda b, i, l: (b, 0)),
                  pl.BlockSpec((seq, HEAD_DIM), lambda b, i, l: (b, 0)),
                  pl.BlockSpec((TQ, iqw), lambda b, i, l: (b * nq + i, 0)),
                  pl.BlockSpec((seq, LANE), lambda b, i, l: (b, iqw // LANE)),
                  pl.BlockSpec((seq, LANE), lambda b, i, l: (b, iqw // LANE + 1)),
                  pl.BlockSpec((TQ, LANE), lambda b, i, l: (b * nq + i, 2))],
        out_specs=pl.BlockSpec((TQ, dq), lambda b, i, l: (b * nq + i, 0)),
        out_shape=jax.ShapeDtypeStruct((t, dq), BF16),
        scratch_shapes=[pltpu.VMEM((seq // TK, TQ, TK), I32), pltpu.VMEM((seq // TK, TK, TQ), I32),
                        pltpu.VMEM((IDX_HEADS, TQ, LANE), F32), pltpu.VMEM((2, TQ, LANE), I32),
                        pltpu.VMEM((DSA_HEADS * TQ, HEAD_DIM), BF16),
                        pltpu.VMEM((seq // TK, DSA_HEADS * TQ, TK), F32),
                        pltpu.VMEM((DSA_HEADS * TQ, LANE), F32),
                        pltpu.VMEM((DSA_HEADS * TQ, 2 * HEAD_DIM), F32)])

    tm, tn = 1024, 512
    ncol = d // tn
    gate0 = COL_GATE // tn
    assert gate0 * tn == COL_GATE
    mixed = _call(
        _merge_kernel, "merge", lidx, [hb, o_fox, o_nsa, o_dsa, w_proj, w_proj, w_proj, w_br_fox, w_br_nsa, w_br_dsa],
        grid=(t // tm, ncol),
        in_specs=[pl.BlockSpec((tm, d), lambda i, j, l: (i, 0)),
                  pl.BlockSpec((tm, fq), lambda i, j, l: (i, 0)),
                  pl.BlockSpec((tm, nsq), lambda i, j, l: (i, 0)),
                  pl.BlockSpec((tm, dq), lambda i, j, l: (i, 0)),
                  pl.BlockSpec((None, d, tn), lambda i, j, l: (l[0], 0, gate0 + j)),
                  pl.BlockSpec((None, d, tn), lambda i, j, l: (l[0], 0, gate0 + ncol + j)),
                  pl.BlockSpec((None, d, tn), lambda i, j, l: (l[0], 0, gate0 + 2 * ncol + j)),
                  pl.BlockSpec((None, fq, tn), lambda i, j, l: (l[0], 0, j)),
                  pl.BlockSpec((None, nsq, tn), lambda i, j, l: (l[0], 0, j)),
                  pl.BlockSpec((None, dq, tn), lambda i, j, l: (l[0], 0, j))],
        out_specs=pl.BlockSpec((tm, tn), lambda i, j, l: (i, j)),
        out_shape=jax.ShapeDtypeStruct((t, d), BF16))

    h, hb = _out_ln("out_ln", lidx, mixed, w_out, h, ln1_g, ln1_b)

    nff = D_FF // tn
    act = _call(
        _swiglu_kernel, "swiglu", lidx, [hb, w_ffn_in, w_ffn_in], grid=(nff, t // tm),
        in_specs=[pl.BlockSpec((tm, d), lambda j, i, l: (i, 0)),
                  pl.BlockSpec((None, d, tn), lambda j, i, l: (l[0], 0, j)),
                  pl.BlockSpec((None, d, tn), lambda j, i, l: (l[0], 0, nff + j))],
        out_specs=pl.BlockSpec((tm, tn), lambda j, i, l: (i, j)),
        out_shape=jax.ShapeDtypeStruct((t, D_FF), BF16),
        scratch_shapes=[pltpu.VMEM((2, d, tn), BF16)])
    h, hb = _ffn_ln("ffn_ln", lidx, act, w_ffn_out, h, ln2_g, ln2_b, tk=512)

    h, hb = _call(
        _ple_kernel, "ple", lidx, [hb, p, w_ple_in, w_ple_gate, h], grid=(t // tm, ncol),
        in_specs=[pl.BlockSpec((tm, d), lambda i, j, l: (i, 0)),
                  pl.BlockSpec((None, tm, PLE_DIM), lambda i, j, l: (l[0], i, 0)),
                  pl.BlockSpec((None, PLE_DIM, tn), lambda i, j, l: (l[0], 0, j)),
                  pl.BlockSpec((None, d, tn), lambda i, j, l: (l[0], 0, j)),
                  pl.BlockSpec((tm, tn), lambda i, j, l: (i, j))],
        out_specs=[pl.BlockSpec((tm, tn), lambda i, j, l: (i, j)),
                   pl.BlockSpec((tm, tn), lambda i, j, l: (i, j))],
        out_shape=[jax.ShapeDtypeStruct((t, d), F32), jax.ShapeDtypeStruct((t, d), BF16)])
    return h, hb


def kernel(x, p, w_in, fox_f_bias, nsa_pe_k, nsa_pe_v, nsa_cmp_k1, nsa_cmp_k2, nsa_cmp_v1, nsa_cmp_v2, dsa_kv_norm, dsa_kv_up, w_br_fox, w_br_nsa, w_br_dsa, w_out, ln1_g, ln1_b, w_ffn_in, w_ffn_out, ln2_g, ln2_b, w_ple_in, w_ple_gate):
    bsz, seq, d = x.shape
    depth = w_in.shape[0]
    t = bsz * seq
    assert d == D_MODEL and seq % 1024 == 0 and depth == DEPTH

    cat = functools.partial(jnp.concatenate, axis=-1)
    w_proj = _regroup(w_in)

    fox_bias = jnp.pad(fox_f_bias, ((0, 0), (0, LANE - FOX_HEADS))).reshape(depth, 1, LANE)
    half = CMP_STRIDE * HEAD_DIM
    weights = (
        w_proj, fox_bias,
        nsa_pe_k.reshape(depth, 2, half), nsa_pe_v.reshape(depth, 2, half),
        nsa_cmp_k1.astype(BF16), nsa_cmp_k2.astype(BF16), nsa_cmp_v1.astype(BF16), nsa_cmp_v2.astype(BF16),
        dsa_kv_norm.reshape(depth, 1, DSA_KV_RANK), dsa_kv_up.astype(BF16),
        w_br_fox.astype(BF16), w_br_nsa.astype(BF16), w_br_dsa.astype(BF16), w_out.astype(BF16),
        ln1_g.reshape(depth, 1, d), ln1_b.reshape(depth, 1, d),
        w_ffn_in, w_ffn_out,
        ln2_g.reshape(depth, 1, d), ln2_b.reshape(depth, 1, d),
        p.reshape(depth, t, PLE_DIM), w_ple_in, w_ple_gate,
    )

    cos, sin = _rope_tables(seq, HEAD_DIM)
    cos128 = cat([cos, cos])
    sin128 = cat([-sin, sin])
    cos_i, sin_i = _rope_tables(seq, IDX_DIM)
    zi = jnp.zeros_like(sin_i)
    cos64 = cat([cos_i, cos_i, cos_i, cos_i])
    sin64a = cat([zi, sin_i, zi, sin_i])
    sin64b = cat([-sin_i, zi, -sin_i, zi])
    n_cmp = (seq - CMP_LEN) // CMP_STRIDE + 1
    c_end = jnp.minimum(jnp.arange(N_CMP_PAD) * CMP_STRIDE + CMP_LEN - 1, seq - 1)
    cmp_cos, cmp_sin = cos128[c_end], sin128[c_end]
    n_slc = seq // SLC_LEN
    c_start = jnp.arange(N_CMP_PAD) * CMP_STRIDE
    s_start = jnp.arange(LANE) * SLC_LEN
    overlap = jnp.maximum(jnp.minimum(c_start[:, None] + CMP_LEN - 1, s_start[None, :] + SLC_LEN - 1)
                          - jnp.maximum(c_start[:, None], s_start[None, :]) + 1, 0).astype(F32) / CMP_LEN
    overlap = jnp.where((jnp.arange(N_CMP_PAD)[:, None] < n_cmp) & (jnp.arange(LANE)[None, :] < n_slc), overlap, 0.0)
    overlap = overlap.T.astype(BF16)
    block_of_key = (jnp.arange(seq)[:, None] // SLC_LEN == jnp.arange(LANE)[None, :]).astype(BF16)
    consts = (cos128, sin128, cos64, sin64a, sin64b, cmp_cos, cmp_sin, overlap, block_of_key)

    h = x.reshape(t, d)
    hb = h.astype(BF16)
    for layer in range(depth):
        lidx = jnp.full((1,), layer, I32)
        h, hb = _layer(lidx, h, hb, consts, weights, bsz, seq)
    return h.reshape(bsz, seq, d)
```

```python
import functools

import jax
import jax.numpy as jnp
from jax import lax
from jax.experimental import pallas as pl
from jax.experimental.pallas import tpu as pltpu

F32, BF16, I32, I16 = jnp.float32, jnp.bfloat16, jnp.int32, jnp.int16

D_MODEL = 2048
DEPTH = 4
HEAD_DIM = 128
ROPE_THETA = 10000.0
NEG_INF = -1e30
FOX_HEADS = 6
NSA_HEADS = 4
CMP_LEN = 32
CMP_STRIDE = 16
CMP_HIDDEN = 256
SLC_LEN = 64
SLC_TOPN = 16
WIN = 512
SLC_FORCE = 1e4
DSA_HEADS = 6
DSA_KV_RANK = 256
IDX_HEADS = 16
IDX_DIM = 64
IDX_TOPK_MAX = 256
D_FF = ((8 * D_MODEL + 3 * 256 - 1) // (3 * 256)) * 256
PLE_DIM = 256
ALPHA = (2 * DEPTH) ** 0.25
SCALE = HEAD_DIM ** -0.5

LANE = 128
VMEM_LIMIT_BYTES = 48 * 1024 * 1024
VMEM_LIMIT_LARGE_BYTES = 56 * 1024 * 1024

_IN_SPLITS = (
    ('fox_q', FOX_HEADS * HEAD_DIM), ('fox_k', FOX_HEADS * HEAD_DIM), ('fox_v', FOX_HEADS * HEAD_DIM), ('fox_f', FOX_HEADS),
    ('nsa_q', NSA_HEADS * HEAD_DIM), ('nsa_kc', HEAD_DIM), ('nsa_vc', HEAD_DIM), ('nsa_ks', HEAD_DIM), ('nsa_vs', HEAD_DIM),
    ('nsa_kw', HEAD_DIM), ('nsa_vw', HEAD_DIM), ('nsa_g', 3 * NSA_HEADS),
    ('dsa_q', DSA_HEADS * HEAD_DIM), ('dsa_ckv', DSA_KV_RANK), ('idx_q', IDX_HEADS * IDX_DIM), ('idx_k', IDX_DIM), ('idx_w', IDX_HEADS),
    ('gate', 3 * D_MODEL),
)
_IN_OFFSETS = {}
_off = 0
for _name, _width in _IN_SPLITS:
    _IN_OFFSETS[_name] = (_off, _off + _width)
    _off += _width

MISC_FOX_F = 0
MISC_NSA_G = FOX_HEADS
MISC_IDX_W = FOX_HEADS + 3 * NSA_HEADS

TQ = 256
TK = 256
N_CMP_PAD = 128


def _dot(a, b):
    return jnp.dot(a, b, preferred_element_type=F32)


def _dot_nt(a, b):
    return lax.dot_general(a, b, (((1,), (1,)), ((), ())), preferred_element_type=F32)


def _split3(a):
    a1 = a.astype(BF16)
    r1 = a - a1.astype(F32)
    a2 = r1.astype(BF16)
    a3 = (r1 - a2.astype(F32)).astype(BF16)
    return a1, a2, a3


def _call(kernel, name, lidx, args, grid, in_specs, out_specs, out_shape, scratch_shapes=(),
          vmem_limit_bytes=VMEM_LIMIT_BYTES):
    return pl.pallas_call(
        kernel,
        grid_spec=pltpu.PrefetchScalarGridSpec(
            num_scalar_prefetch=1, grid=grid, in_specs=in_specs, out_specs=out_specs,
            scratch_shapes=list(scratch_shapes)),
        out_shape=out_shape,
        compiler_params=pltpu.CompilerParams(
            dimension_semantics=("arbitrary",) * len(grid), vmem_limit_bytes=vmem_limit_bytes),
        name=name,
    )(lidx, *args)


def _regroup_plan():
    plan = []

    def whole(name):
        a, b = _IN_OFFSETS[name]
        for g in range((b - a) // LANE):
            plan.append(((a + g * LANE, 0, LANE),))

    for name in ('fox_q', 'fox_k', 'fox_v', 'nsa_vs', 'nsa_vw'):
        whole(name)
    for name in ('nsa_q', 'nsa_ks', 'nsa_kw', 'dsa_q'):
        whole(name)
    whole('gate')
    whole('idx_q')
    ik = _IN_OFFSETS['idx_k'][0]
    plan.append(((ik, 0, IDX_DIM),))
    plan.append(((ik - IDX_DIM, IDX_DIM, LANE),))
    whole('dsa_ckv')
    misc, lane0 = [], 0
    for name in ('fox_f', 'nsa_g', 'idx_w'):
        a, b = _IN_OFFSETS[name]
        misc.append((a - lane0, lane0, lane0 + b - a))
        lane0 += b - a
    plan.append(tuple(misc))
    whole('nsa_kc')
    whole('nsa_vc')
    return tuple(plan)


_REGROUP_PLAN = _regroup_plan()
COL_PLAIN = 0
N_PLAIN = (3 * FOX_HEADS + 2) * HEAD_DIM
COL_ROPE128 = COL_PLAIN + N_PLAIN
N_ROPE128 = (NSA_HEADS + 2 + DSA_HEADS) * HEAD_DIM
COL_GATE = COL_ROPE128 + N_ROPE128
COL_ROPE64 = COL_GATE + 3 * D_MODEL
N_ROPE64 = IDX_HEADS * IDX_DIM + 2 * LANE
COL_F32 = COL_ROPE64 + N_ROPE64
N_F32 = DSA_KV_RANK + 3 * LANE
N_REGROUPED = COL_F32 + N_F32
assert N_REGROUPED == len(_REGROUP_PLAN) * LANE


def _regroup_kernel(l_ref, x_ref, o_ref):
    lane = lax.broadcasted_iota(I32, (x_ref.shape[0], LANE), 1)
    for g, pieces in enumerate(_REGROUP_PLAN):
        out = None
        for src0, lo, hi in pieces:
            win = x_ref[:, src0:src0 + LANE]
            if (lo, hi) != (0, LANE):
                win = jnp.where((lane >= lo) & (lane < hi), win, 0.0)
            out = win if out is None else out + win
        o_ref[:, g * LANE:(g + 1) * LANE] = out.astype(o_ref.dtype)


def _regroup(w_in):
    depth, d, n_in = w_in.shape
    tr = 256
    return _call(_regroup_kernel, "regroup_w_in", jnp.zeros((1,), I32), [w_in], grid=(depth, d // tr),
                 in_specs=[pl.BlockSpec((None, tr, n_in), lambda li, r, l: (li, r, 0))],
                 out_specs=pl.BlockSpec((None, tr, N_REGROUPED), lambda li, r, l: (li, r, 0)),
                 out_shape=jax.ShapeDtypeStruct((depth, d, N_REGROUPED), BF16))


def _proj_kernel(l_ref, x_ref, w_ref, *refs, shifts):
    o_ref = refs[-1]
    acc = _dot(x_ref[...], w_ref[...])
    if not shifts:
        o_ref[...] = acc.astype(o_ref.dtype)
        return
    c_ref = refs[0]
    s_refs = refs[1:-1]
    for g in range(acc.shape[1] // LANE):
        xg = acc[:, g * LANE:(g + 1) * LANE]
        out = xg * c_ref[...]
        for shift, s_ref in zip(shifts, s_refs):
            out = out + pltpu.roll(xg, shift, 1) * s_ref[...]
        o_ref[:, g * LANE:(g + 1) * LANE] = out.astype(o_ref.dtype)


def _proj(name, lidx, hb, w, col0, n, tm, tn, out_dtype, seq, rope=None):
    t, d = hb.shape
    jb = col0 // tn
    assert jb * tn == col0
    in_specs = [pl.BlockSpec((tm, d), lambda i, j, l: (i, 0)),
                pl.BlockSpec((None, d, tn), lambda i, j, l: (l[0], 0, jb + j))]
    args = [hb, w]
    shifts = ()
    if rope is not None:
        shifts, tables = rope[0], rope[1:]
        nrow = seq // tm
        for tab in tables:
            in_specs.append(pl.BlockSpec((tm, LANE), lambda i, j, l: (i % nrow, 0)))
            args.append(tab)
    return _call(functools.partial(_proj_kernel, shifts=shifts), name, lidx, args,
                 grid=(t // tm, n // tn), in_specs=in_specs,
                 out_specs=pl.BlockSpec((tm, tn), lambda i, j, l: (i, j)),
                 out_shape=jax.ShapeDtypeStruct((t, n), out_dtype))


def _cum_kernel(l_ref, f_ref, bias_ref, o_ref):
    chunk = 256
    row = lax.broadcasted_iota(I32, (chunk, chunk), 0)
    col = lax.broadcasted_iota(I32, (chunk, chunk), 1)
    tri = jnp.where(row >= col, 1.0, 0.0).astype(BF16)
    carry = jnp.zeros((1, LANE), F32)
    for c in range(f_ref.shape[0] // chunk):
        logf = jax.nn.log_sigmoid(f_ref[c * chunk:(c + 1) * chunk, :] + bias_ref[...])
        l1, l2, l3 = _split3(logf)
        cs = _dot(tri, l1) + _dot(tri, l2) + _dot(tri, l3) + carry
        o_ref[c * chunk:(c + 1) * chunk, :] = cs
        carry = cs[chunk - 1:chunk, :]


def _max_lanes(x):
    out = x[:, 0:LANE]
    for g in range(1, x.shape[1] // LANE):
        out = jnp.maximum(out, x[:, g * LANE:(g + 1) * LANE])
    return out


def _tile_lanes(x, n):
    return jnp.concatenate([x] * n, axis=1)


def _row_max_to_lanes(mx):
    return jnp.broadcast_to(jnp.max(mx, axis=-1, keepdims=True), mx.shape)


def _exp_pv(s, m_lanes, v):
    p = jnp.exp(s - _tile_lanes(m_lanes, s.shape[1] // LANE)).astype(BF16)
    return _dot(p, jnp.concatenate([v, jnp.ones_like(v)], axis=1))


def _fox_kernel(l_ref, q_ref, k_ref, v_ref, cq_ref, ck_ref, o_ref, s_ref, mx_ref, cqb_ref, acc_ref):
    qi = pl.program_id(1)
    nh = FOX_HEADS
    nchunk = ck_ref.shape[0] // 8
    row = lax.broadcasted_iota(I32, (TQ, TK), 0)
    col = lax.broadcasted_iota(I32, (TQ, TK), 1)
    cq_all = cq_ref[...]
    for h in range(nh):
        cqb_ref[h] = jnp.broadcast_to(cq_all[:, h:h + 1], (TQ, LANE))
    mx_ref[...] = jnp.full(mx_ref.shape, NEG_INF, F32)
    acc_ref[...] = jnp.zeros(acc_ref.shape, F32)

    def score_step(j, diagonal):
        start = pl.multiple_of(j * TK, TK)
        for h in range(nh):
            hs = slice(h * HEAD_DIM, (h + 1) * HEAD_DIM)
            s = (_dot_nt(q_ref[:, hs], k_ref[pl.ds(start, TK), hs]) * SCALE
                 + _tile_lanes(cqb_ref[h], TK // LANE) - ck_ref[pl.ds(h * nchunk + j, 1), :])
            if diagonal:
                s = jnp.where(col <= row, s, NEG_INF)
            s_ref[h, j] = s
            mx_ref[h] = jnp.maximum(mx_ref[h], _max_lanes(s))

    def score_body(j, carry):
        score_step(j, False)
        return carry

    lax.fori_loop(0, qi, score_body, 0)
    score_step(qi, True)
    for h in range(nh):
        mx_ref[h] = _row_max_to_lanes(mx_ref[h])

    def pv_body(j, carry):
        start = pl.multiple_of(j * TK, TK)
        for h in range(nh):
            hs = slice(h * HEAD_DIM, (h + 1) * HEAD_DIM)
            acc_ref[h] += _exp_pv(s_ref[h, j], mx_ref[h], v_ref[pl.ds(start, TK), hs])
        return carry

    lax.fori_loop(0, qi + 1, pv_body, 0)
    for h in range(nh):
        a = acc_ref[h]
        o_ref[:, h * HEAD_DIM:(h + 1) * HEAD_DIM] = (a[:, :HEAD_DIM] / a[:, HEAD_DIM:]).astype(o_ref.dtype)


def _cmp_kernel(l_ref, rk_ref, rv_ref, pek_ref, pev_ref, wk1_ref, wk2_ref, wv1_ref, wv2_ref,
                c_ref, s_ref, ko_ref, vo_ref):
    half = CMP_STRIDE * HEAD_DIM

    def compress(r_ref, pe_ref, w1_ref, w2_ref):
        r = r_ref[...]
        lo = (r + pe_ref[0:1, :]).astype(BF16)
        hi = (r + pe_ref[1:2, :]).astype(BF16)
        a = _dot(lo, w1_ref[0:half, :])
        b = _dot(hi, w1_ref[half:2 * half, :])
        hid = a + pltpu.roll(b, N_CMP_PAD - 1, 0)
        return _dot(jax.nn.gelu(hid).astype(BF16), w2_ref[...])

    kc = compress(rk_ref, pek_ref, wk1_ref, wk2_ref)
    kc = kc * c_ref[...] + pltpu.roll(kc, HEAD_DIM // 2, 1) * s_ref[...]
    ko_ref[...] = kc.astype(ko_ref.dtype)
    vo_ref[...] = compress(rv_ref, pev_ref, wv1_ref, wv2_ref).astype(vo_ref.dtype)


def _masked_softmax(s, mask):
    s = jnp.where(mask, s, NEG_INF)
    m = jnp.max(s, axis=-1, keepdims=True)
    e = jnp.where(mask, jnp.exp(s - m), 0.0)
    den = jnp.sum(e, axis=-1, keepdims=True)
    return e, jnp.where(den > 0.0, den, 1.0)


def _nsa_kernel(l_ref, q_ref, ks_ref, kw_ref, vs_ref, vw_ref, kc_ref, vc_ref, g_ref, ovt_ref, et_ref,
                o_ref, q_sc, s_ref, mx_ref, acc_ref):
    qi = pl.program_id(1)
    nh = NSA_HEADS
    for h in range(nh):
        q_sc[h * TQ:(h + 1) * TQ, :] = q_ref[:, h * HEAD_DIM:(h + 1) * HEAD_DIM]
    q = q_sc[...]
    pos = qi * TQ + lax.broadcasted_iota(I32, (TQ, 1), 0)
    lane = lax.broadcasted_iota(I32, (TQ, LANE), 1)

    cvis = (lane * CMP_STRIDE + (CMP_LEN - 1)) <= pos
    s_c = (_dot_nt(q, kc_ref[...]) * SCALE).reshape(nh, TQ, N_CMP_PAD)
    e_c, den_c = _masked_softmax(s_c, cvis[None])
    p_c = e_c / den_c
    o_cmp = _dot(p_c.reshape(nh * TQ, N_CMP_PAD).astype(BF16), vc_ref[...])

    p_sum = p_c[0]
    for h in range(1, nh):
        p_sum = p_sum + p_c[h]
    p1, p2, p3 = _split3(p_sum)
    ovt = ovt_ref[...]
    n_slc = et_ref.shape[0] // SLC_LEN
    imp = (_dot_nt(ovt, p1) + _dot_nt(ovt, p2) + _dot_nt(ovt, p3))[0:n_slc, :]
    blk = lax.broadcasted_iota(I32, (n_slc, TQ), 0)
    pos_t = qi * TQ + lax.broadcasted_iota(I32, (n_slc, TQ), 1)
    cur = lax.shift_right_logical(pos_t, SLC_LEN.bit_length() - 1)
    forced = (blk == 0) | (blk == cur) | (blk == cur - 1)
    imp = jnp.where(forced, SLC_FORCE, jnp.where(blk * SLC_LEN <= pos_t, imp, -SLC_FORCE))
    rank = jnp.zeros((n_slc, TQ), F32)
    for k in range(n_slc):
        ik = imp[k:k + 1, :]
        ahead = (ik > imp) | ((ik == imp) & (blk > k))
        rank = rank + jnp.where(ahead, 1.0, 0.0)
    sel_t = jnp.where(rank < float(min(SLC_TOPN, n_slc)), 1.0, 0.0)
    sel_t = jnp.concatenate([sel_t, jnp.zeros((LANE - n_slc, TQ), F32)], axis=0)
    sel = sel_t.T.astype(BF16)

    mx_ref[...] = jnp.full(mx_ref.shape, NEG_INF, F32)
    acc_ref[...] = jnp.zeros(acc_ref.shape, F32)
    row = lax.broadcasted_iota(I32, (TQ, TK), 0)
    col = lax.broadcasted_iota(I32, (TQ, TK), 1)

    def score_step(j, diagonal):
        start = pl.multiple_of(j * TK, TK)
        vis = _dot_nt(sel, et_ref[pl.ds(start, TK), :]) > 0.5
        if diagonal:
            vis = vis & (col <= row)
        s = (_dot_nt(q_sc[...], ks_ref[pl.ds(start, TK), :]) * SCALE).reshape(nh, TQ, TK)
        s = jnp.where(vis[None], s, NEG_INF).reshape(nh * TQ, TK)
        s_ref[j] = s
        mx_ref[...] = jnp.maximum(mx_ref[...], _max_lanes(s))

    def score_body(j, carry):
        score_step(j, False)
        return carry

    lax.fori_loop(0, qi, score_body, 0)
    score_step(qi, True)
    mx_ref[...] = _row_max_to_lanes(mx_ref[...])

    def pv_body(j, carry):
        start = pl.multiple_of(j * TK, TK)
        acc_ref[...] += _exp_pv(s_ref[j], mx_ref[...], vs_ref[pl.ds(start, TK), :])
        return carry

    lax.fori_loop(0, qi + 1, pv_body, 0)
    a_slc = acc_ref[...]
    o_slc = a_slc[:, :HEAD_DIM] / a_slc[:, HEAD_DIM:]

    band = WIN + TQ
    start_w = pl.multiple_of(jnp.maximum(qi * TQ - WIN, 0), TQ)
    kwb = kw_ref[pl.ds(start_w, band), :]
    vwb = vw_ref[pl.ds(start_w, band), :]
    dist = pos - (start_w + lax.broadcasted_iota(I32, (TQ, band), 1))
    wvis = (dist >= 0) & (dist < WIN)
    s_w = (_dot_nt(q, kwb) * SCALE).reshape(nh, TQ, band)
    e_w, den_w = _masked_softmax(s_w, wvis[None])
    o_win = _dot(e_w.reshape(nh * TQ, band).astype(BF16), vwb) / den_w.reshape(nh * TQ, 1)

    gate = jax.nn.sigmoid(g_ref[...])
    for h in range(nh):
        rows = slice(h * TQ, (h + 1) * TQ)
        g0 = gate[:, MISC_NSA_G + 3 * h:MISC_NSA_G + 3 * h + 1]
        g1 = gate[:, MISC_NSA_G + 3 * h + 1:MISC_NSA_G + 3 * h + 2]
        g2 = gate[:, MISC_NSA_G + 3 * h + 2:MISC_NSA_G + 3 * h + 3]
        out = g0 * o_cmp[rows] + g1 * o_slc[rows] + g2 * o_win[rows]
        o_ref[:, h * HEAD_DIM:(h + 1) * HEAD_DIM] = out.astype(o_ref.dtype)


def _dsa_kv_kernel(l_ref, x_ref, g_ref, up_ref, c_ref, s_ref, ko_ref, vo_ref):
    x = x_ref[...]
    r = x * lax.rsqrt(jnp.mean(jnp.square(x), axis=-1, keepdims=True) + 1e-6) * g_ref[...]
    kv = _dot(r.astype(BF16), up_ref[...])
    k = kv[:, :HEAD_DIM]
    k = k * c_ref[...] + pltpu.roll(k, HEAD_DIM // 2, 1) * s_ref[...]
    ko_ref[...] = k.astype(ko_ref.dtype)
    vo_ref[...] = kv[:, HEAD_DIM:].astype(vo_ref.dtype)


def _dsa_kernel(l_ref, q_ref, kd_ref, vd_ref, iq_ref, ika_ref, ikb_ref, w_ref, o_ref,
                key_ref, keyt_ref, hi_ref, lo_ref, wb_ref, sel_ref, q_sc, s_ref, mx_ref, acc_ref):
    qi = pl.program_id(1)
    nh = DSA_HEADS
    seq = kd_ref.shape[0]
    k_top = float(min(IDX_TOPK_MAX, seq // 4))
    row = lax.broadcasted_iota(I32, (TQ, TK), 0)
    col = lax.broadcasted_iota(I32, (TQ, TK), 1)
    int_min = jnp.int32(-2 ** 31)
    idx_bits = (seq - 1).bit_length()

    w_all = w_ref[...] * (IDX_HEADS ** -0.5 * IDX_DIM ** -0.5)
    for h in range(IDX_HEADS):
        wb_ref[h] = jnp.broadcast_to(w_all[:, MISC_IDX_W + h:MISC_IDX_W + h + 1], (TQ, LANE))

    def score_step(j, diagonal):
        start = pl.multiple_of(j * TK, TK)
        ka = ika_ref[pl.ds(start, TK), :]
        kb = ikb_ref[pl.ds(start, TK), :]
        sc = jnp.zeros((TQ, TK), F32)
        for g in range(IDX_HEADS // 2):
            qpair = iq_ref[:, g * LANE:(g + 1) * LANE]
            wa = _tile_lanes(wb_ref[2 * g], TK // LANE)
            wb = _tile_lanes(wb_ref[2 * g + 1], TK // LANE)
            sc = sc + jnp.maximum(_dot_nt(qpair, ka), 0.0) * wa + jnp.maximum(_dot_nt(qpair, kb), 0.0) * wb
        if diagonal:
            sc = jnp.where(col <= row, sc, NEG_INF)
        bits = lax.bitcast_convert_type(sc, I32)
        key = jnp.where(bits < 0, bits ^ jnp.int32(0x7FFFFFFF), bits)
        key = jnp.where(sc == 0.0, 0, key)
        key_ref[j] = key
        key_t = key.T
        keyt_ref[j] = key_t
        hi_ref[j] = (key_t >> 16).astype(I16)
        lo_ref[j] = ((key_t & 0xFFFF) - 32768).astype(I16)

    def score_body(j, carry):
        score_step(j, False)
        return carry

    lax.fori_loop(0, qi, score_body, 0)
    score_step(qi, True)

    krow = lax.broadcasted_iota(I32, (TK, TQ), 0)

    def count(pred_fn):
        def body(j, acc):
            hit = jnp.where(pred_fn(keyt_ref[j], j), 1.0, 0.0)
            return acc + jnp.sum(hit.reshape(TK // 8, 8, TQ), axis=0)
        part = lax.fori_loop(0, qi + 1, body, jnp.zeros((8, TQ), F32))
        return jnp.sum(part, axis=0, keepdims=True)

    def count16(ref, cand, strict):
        cand16 = cand.astype(I16)

        def body(j, acc):
            x = ref[j]
            hit = jnp.where(x > cand16 if strict else x >= cand16, jnp.int16(1), jnp.int16(0))
            for g in range(TK // 16):
                acc = acc + hit[g * 16:(g + 1) * 16, :]
            return acc
        part = lax.fori_loop(0, qi + 1, body, jnp.zeros((16, TQ), I16))
        return jnp.sum(part.astype(F32), axis=0, keepdims=True)

    def bisect16(ref, need):
        zero = jnp.zeros((1, TQ), I32)
        v = jnp.where(count16(ref, zero, False) >= need, zero, zero - 32768)

        def bit(i, v):
            cand = v + (jnp.int32(1) << (14 - i))
            return jnp.where(count16(ref, cand, False) >= need, cand, v)
        return lax.fori_loop(0, 15, bit, v)

    hi_thr = bisect16(hi_ref, k_top)
    need_lo = k_top - count16(hi_ref, hi_thr, True)
    hi_thr16 = hi_thr.astype(I16)

    def mask_lo(j, carry):
        lo_ref[j] = jnp.where(hi_ref[j] == hi_thr16, lo_ref[j], jnp.int16(-32768))
        return carry

    lax.fori_loop(0, qi + 1, mask_lo, 0)
    lo_thr = bisect16(lo_ref, need_lo)
    thr = lax.shift_left(hi_thr, jnp.int32(16)) + (lo_thr + 32768)
    n_gt = count(lambda key, j: key > thr)
    n_ge = count(lambda key, j: key >= thr)

    def index_bisect():
        def index_bit(i, cut):
            cand = cut + (jnp.int32(1) << (idx_bits - 1 - i))
            n_before = count(lambda key, j: (key == thr) & (j * TK + krow < cand))
            return jnp.where(n_gt + n_before < k_top, cand, cut)
        return lax.fori_loop(0, idx_bits, index_bit, jnp.zeros((1, TQ), I32))

    def take_all_ties():
        return jnp.full((1, TQ), seq - 1, I32)

    cut = lax.cond(jnp.max(n_ge) > k_top, index_bisect, take_all_ties)
    sel_ref[0] = jnp.broadcast_to(thr, (LANE, TQ)).T
    sel_ref[1] = jnp.broadcast_to(cut, (LANE, TQ)).T

    for h in range(nh):
        q_sc[h * TQ:(h + 1) * TQ, :] = q_ref[:, h * HEAD_DIM:(h + 1) * HEAD_DIM]
    mx_ref[...] = jnp.full(mx_ref.shape, NEG_INF, F32)
    acc_ref[...] = jnp.zeros(acc_ref.shape, F32)

    def attn_score_step(j, diagonal):
        start = pl.multiple_of(j * TK, TK)
        key = key_ref[j]
        thr_t = _tile_lanes(sel_ref[0], TK // LANE)
        cut_t = _tile_lanes(sel_ref[1], TK // LANE)
        vis = (key > thr_t) | ((key == thr_t) & (j * TK + col <= cut_t))
        if diagonal:
            vis = vis & (col <= row)
        s = (_dot_nt(q_sc[...], kd_ref[pl.ds(start, TK), :]) * SCALE).reshape(nh, TQ, TK)
        s = jnp.where(vis[None], s, NEG_INF).reshape(nh * TQ, TK)
        s_ref[j] = s
        mx_ref[...] = jnp.maximum(mx_ref[...], _max_lanes(s))

    def attn_score_body(j, carry):
        attn_score_step(j, False)
        return carry

    lax.fori_loop(0, qi, attn_score_body, 0)
    attn_score_step(qi, True)
    mx_ref[...] = _row_max_to_lanes(mx_ref[...])

    def pv_body(j, carry):
        start = pl.multiple_of(j * TK, TK)
        acc_ref[...] += _exp_pv(s_ref[j], mx_ref[...], vd_ref[pl.ds(start, TK), :])
        return carry

    lax.fori_loop(0, qi + 1, pv_body, 0)
    a = acc_ref[...]
    out = a[:, :HEAD_DIM] / a[:, HEAD_DIM:]
    for h in range(nh):
        o_ref[:, h * HEAD_DIM:(h + 1) * HEAD_DIM] = out[h * TQ:(h + 1) * TQ].astype(o_ref.dtype)


def _merge_kernel(l_ref, hb_ref, of_ref, on_ref, od_ref, wg1_ref, wg2_ref, wg3_ref, wf_ref, wn_ref, wd_ref, o_ref):
    hb = hb_ref[...]
    mixed = (jax.nn.sigmoid(_dot(hb, wg1_ref[...])) * _dot(of_ref[...], wf_ref[...])
             + jax.nn.sigmoid(_dot(hb, wg2_ref[...])) * _dot(on_ref[...], wn_ref[...])
             + jax.nn.sigmoid(_dot(hb, wg3_ref[...])) * _dot(od_ref[...], wd_ref[...]))
    o_ref[...] = mixed.astype(o_ref.dtype)


LN_SUB_ROWS = 256


def _layer_norm_store(y, g_ref, b_ref, of_ref, ob_ref, rows):
    mu = jnp.mean(y, axis=-1, keepdims=True)
    yc = y - mu
    var = jnp.mean(jnp.square(yc), axis=-1, keepdims=True)
    out = yc * lax.rsqrt(var + 1e-5) * g_ref[...] + b_ref[...]
    of_ref[rows, :] = out
    ob_ref[rows, :] = out.astype(ob_ref.dtype)


def _out_ln_kernel(l_ref, x_ref, w_ref, h_ref, g_ref, b_ref, of_ref, ob_ref):
    w = w_ref[...]
    for r in range(x_ref.shape[0] // LN_SUB_ROWS):
        rows = slice(r * LN_SUB_ROWS, (r + 1) * LN_SUB_ROWS)
        y = ALPHA * h_ref[rows, :] + _dot(x_ref[rows, :], w)
        _layer_norm_store(y, g_ref, b_ref, of_ref, ob_ref, rows)


def _out_ln(name, lidx, x, w, h, g, b):
    t, kdim = x.shape
    d = h.shape[1]
    tm = 512
    return _call(_out_ln_kernel, name, lidx, [x, w, h, g, b], grid=(t // tm,),
                 in_specs=[pl.BlockSpec((tm, kdim), lambda i, l: (i, 0)),
                           pl.BlockSpec((None, kdim, d), lambda i, l: (l[0], 0, 0)),
                           pl.BlockSpec((tm, d), lambda i, l: (i, 0)),
                           pl.BlockSpec((None, 1, d), lambda i, l: (l[0], 0, 0)),
                           pl.BlockSpec((None, 1, d), lambda i, l: (l[0], 0, 0))],
                 out_specs=[pl.BlockSpec((tm, d), lambda i, l: (i, 0)),
                            pl.BlockSpec((tm, d), lambda i, l: (i, 0))],
                 out_shape=[jax.ShapeDtypeStruct((t, d), F32), jax.ShapeDtypeStruct((t, d), BF16)])


def _ffn_ln_kernel(l_ref, x_ref, w_ref, h_ref, g_ref, b_ref, of_ref, ob_ref, *, nk):
    k = pl.program_id(1)

    @pl.when(k == 0)
    def _():
        of_ref[...] = ALPHA * h_ref[...] + _dot(x_ref[...], w_ref[...])

    @pl.when((k > 0) & (k < nk - 1))
    def _():
        of_ref[...] += _dot(x_ref[...], w_ref[...])

    @pl.when(k == nk - 1)
    def _():
        w = w_ref[...]
        for r in range(x_ref.shape[0] // LN_SUB_ROWS):
            rows = slice(r * LN_SUB_ROWS, (r + 1) * LN_SUB_ROWS)
            y = of_ref[rows, :] + _dot(x_ref[rows, :], w)
            _layer_norm_store(y, g_ref, b_ref, of_ref, ob_ref, rows)


def _ffn_ln(name, lidx, x, w, h, g, b, tk):
    t, kdim = x.shape
    d = h.shape[1]
    tm = 1024
    nk = kdim // tk
    assert nk >= 2
    return _call(functools.partial(_ffn_ln_kernel, nk=nk), name, lidx, [x, w, h, g, b],
                 grid=(t // tm, nk),
                 in_specs=[pl.BlockSpec((tm, tk), lambda i, k, l: (i, k)),
                           pl.BlockSpec((None, tk, d), lambda i, k, l: (l[0], k, 0)),
                           pl.BlockSpec((tm, d), lambda i, k, l: (i, 0)),
                           pl.BlockSpec((None, 1, d), lambda i, k, l: (l[0], 0, 0)),
                           pl.BlockSpec((None, 1, d), lambda i, k, l: (l[0], 0, 0))],
                 out_specs=[pl.BlockSpec((tm, d), lambda i, k, l: (i, 0)),
                            pl.BlockSpec((tm, d), lambda i, k, l: (i, 0))],
                 out_shape=[jax.ShapeDtypeStruct((t, d), F32), jax.ShapeDtypeStruct((t, d), BF16)],
                 vmem_limit_bytes=VMEM_LIMIT_LARGE_BYTES)


def _swiglu_kernel(l_ref, x_ref, wa_ref, wb_ref, o_ref, wab_ref):
    @pl.when(pl.program_id(1) == 0)
    def _():
        wab_ref[0] = wa_ref[...].astype(BF16)
        wab_ref[1] = wb_ref[...].astype(BF16)

    x = x_ref[...]
    a = _dot(x, wab_ref[0])
    o_ref[...] = (jax.nn.silu(a) * _dot(x, wab_ref[1])).astype(o_ref.dtype)


def _ple_kernel(l_ref, hb_ref, p_ref, wpi_ref, wpg_ref, h_ref, of_ref, ob_ref):
    p_in = _dot(p_ref[...].astype(BF16), wpi_ref[...].astype(BF16))
    out = h_ref[...] + p_in * jax.nn.sigmoid(_dot(hb_ref[...], wpg_ref[...].astype(BF16)))
    of_ref[...] = out
    ob_ref[...] = out.astype(ob_ref.dtype)


def _rope_tables(n, dim):
    inv = 1.0 / (ROPE_THETA ** (jnp.arange(0, dim, 2, dtype=F32) / dim))
    ang = jnp.arange(n, dtype=F32)[:, None] * inv[None, :]
    return jnp.cos(ang), jnp.sin(ang)


def _layer(lidx, h, hb, consts, weights, bsz, seq):
    (cos128, sin128, cos64, sin64a, sin64b, cmp_cos, cmp_sin, overlap, block_of_key) = consts
    (w_proj, fox_bias, pe_k, pe_v, wk1, wk2, wv1, wv2, kv_norm, kv_up,
     w_br_fox, w_br_nsa, w_br_dsa, w_out, ln1_g, ln1_b, w_ffn_in, w_ffn_out, ln2_g, ln2_b, p, w_ple_in, w_ple_gate) = weights
    t = bsz * seq
    nq = seq // TQ
    d = D_MODEL

    z_a = _proj("proj_plain", lidx, hb, w_proj, COL_PLAIN, N_PLAIN, 2048, 512, BF16, seq)
    z_b = _proj("proj_rope128", lidx, hb, w_proj, COL_ROPE128, N_ROPE128, 2048, 512, BF16, seq,
                rope=((HEAD_DIM // 2,), cos128, sin128))
    z_c = _proj("proj_rope64", lidx, hb, w_proj, COL_ROPE64, N_ROPE64, 2048, 640, BF16, seq,
                rope=((IDX_DIM // 2, LANE - IDX_DIM // 2), cos64, sin64a, sin64b))
    z_d = _proj("proj_f32", lidx, hb, w_proj, COL_F32, N_F32, 1024, 640, F32, seq)

    cum = _call(_cum_kernel, "fox_cum", lidx, [z_d, fox_bias], grid=(bsz,),
                in_specs=[pl.BlockSpec((seq, LANE), lambda b, l: (b, 2)),
                          pl.BlockSpec((None, 1, LANE), lambda b, l: (l[0], 0, 0))],
                out_specs=pl.BlockSpec((seq, LANE), lambda b, l: (b, 0)),
                out_shape=jax.ShapeDtypeStruct((t, LANE), F32))
    cum_rows = jnp.transpose(cum.reshape(bsz, seq, LANE)[:, :, :8], (0, 2, 1)).reshape(bsz, 8 * (seq // TK), TK)
    fq = FOX_HEADS * HEAD_DIM
    o_fox = _call(_fox_kernel, "fox_attn", lidx, [z_a, z_a, z_a, cum, cum_rows], grid=(bsz, nq),
                  in_specs=[pl.BlockSpec((TQ, fq), lambda b, i, l: (b * nq + i, 0)),
                            pl.BlockSpec((seq, fq), lambda b, i, l: (b, 1)),
                            pl.BlockSpec((seq, fq), lambda b, i, l: (b, 2)),
                            pl.BlockSpec((TQ, LANE), lambda b, i, l: (b * nq + i, 0)),
                            pl.BlockSpec((None, 8 * (seq // TK), TK), lambda b, i, l: (b, 0, 0))],
                  out_specs=pl.BlockSpec((TQ, fq), lambda b, i, l: (b * nq + i, 0)),
                  out_shape=jax.ShapeDtypeStruct((t, fq), BF16),
                  scratch_shapes=[pltpu.VMEM((FOX_HEADS, seq // TK, TQ, TK), F32),
                                  pltpu.VMEM((FOX_HEADS, TQ, LANE), F32), pltpu.VMEM((FOX_HEADS, TQ, LANE), F32),
                                  pltpu.VMEM((FOX_HEADS, TQ, 2 * HEAD_DIM), F32)])

    nchunk = seq // CMP_STRIDE
    half = CMP_STRIDE * HEAD_DIM
    r_k = z_d[:, 3 * LANE:4 * LANE].reshape(bsz * nchunk, half)
    r_v = z_d[:, 4 * LANE:5 * LANE].reshape(bsz * nchunk, half)
    k_cmp, v_cmp = _call(
        _cmp_kernel, "nsa_compress", lidx, [r_k, r_v, pe_k, pe_v, wk1, wk2, wv1, wv2, cmp_cos, cmp_sin], grid=(bsz,),
        in_specs=[pl.BlockSpec((nchunk, half), lambda b, l: (b, 0)),
                  pl.BlockSpec((nchunk, half), lambda b, l: (b, 0)),
                  pl.BlockSpec((None, 2, half), lambda b, l: (l[0], 0, 0)),
                  pl.BlockSpec((None, 2, half), lambda b, l: (l[0], 0, 0)),
                  pl.BlockSpec((None, 2 * half, CMP_HIDDEN), lambda b, l: (l[0], 0, 0)),
                  pl.BlockSpec((None, CMP_HIDDEN, HEAD_DIM), lambda b, l: (l[0], 0, 0)),
                  pl.BlockSpec((None, 2 * half, CMP_HIDDEN), lambda b, l: (l[0], 0, 0)),
                  pl.BlockSpec((None, CMP_HIDDEN, HEAD_DIM), lambda b, l: (l[0], 0, 0)),
                  pl.BlockSpec((nchunk, HEAD_DIM), lambda b, l: (0, 0)),
                  pl.BlockSpec((nchunk, HEAD_DIM), lambda b, l: (0, 0))],
        out_specs=[pl.BlockSpec((nchunk, HEAD_DIM), lambda b, l: (b, 0)),
                   pl.BlockSpec((nchunk, HEAD_DIM), lambda b, l: (b, 0))],
        out_shape=[jax.ShapeDtypeStruct((bsz * nchunk, HEAD_DIM), BF16)] * 2)
    nsq = NSA_HEADS * HEAD_DIM
    o_nsa = _call(
        _nsa_kernel, "nsa_attn", lidx, [z_b, z_b, z_b, z_a, z_a, k_cmp, v_cmp, z_d, overlap, block_of_key], grid=(bsz, nq),
        in_specs=[pl.BlockSpec((TQ, nsq), lambda b, i, l: (b * nq + i, 0)),
                  pl.BlockSpec((seq, LANE), lambda b, i, l: (b, 4)),
                  pl.BlockSpec((seq, LANE), lambda b, i, l: (b, 5)),
                  pl.BlockSpec((seq, LANE), lambda b, i, l: (b, 18)),
                  pl.BlockSpec((seq, LANE), lambda b, i, l: (b, 19)),
                  pl.BlockSpec((nchunk, HEAD_DIM), lambda b, i, l: (b, 0)),
                  pl.BlockSpec((nchunk, HEAD_DIM), lambda b, i, l: (b, 0)),
                  pl.BlockSpec((TQ, LANE), lambda b, i, l: (b * nq + i, 2)),
                  pl.BlockSpec((N_CMP_PAD, LANE), lambda b, i, l: (0, 0)),
                  pl.BlockSpec((seq, LANE), lambda b, i, l: (0, 0))],
        out_specs=pl.BlockSpec((TQ, nsq), lambda b, i, l: (b * nq + i, 0)),
        out_shape=jax.ShapeDtypeStruct((t, nsq), BF16),
        scratch_shapes=[pltpu.VMEM((NSA_HEADS * TQ, HEAD_DIM), BF16),
                        pltpu.VMEM((seq // TK, NSA_HEADS * TQ, TK), F32),
                        pltpu.VMEM((NSA_HEADS * TQ, LANE), F32),
                        pltpu.VMEM((NSA_HEADS * TQ, 2 * HEAD_DIM), F32)])

    tm_kv = 1024
    k_d, v_d = _call(
        _dsa_kv_kernel, "dsa_kv", lidx, [z_d, kv_norm, kv_up, cos128, sin128], grid=(t // tm_kv,),
        in_specs=[pl.BlockSpec((tm_kv, DSA_KV_RANK), lambda i, l: (i, 0)),
                  pl.BlockSpec((None, 1, DSA_KV_RANK), lambda i, l: (l[0], 0, 0)),
                  pl.BlockSpec((None, DSA_KV_RANK, 2 * HEAD_DIM), lambda i, l: (l[0], 0, 0)),
                  pl.BlockSpec((tm_kv, LANE), lambda i, l: (i % (seq // tm_kv), 0)),
                  pl.BlockSpec((tm_kv, LANE), lambda i, l: (i % (seq // tm_kv), 0))],
        out_specs=[pl.BlockSpec((tm_kv, HEAD_DIM), lambda i, l: (i, 0)),
                   pl.BlockSpec((tm_kv, HEAD_DIM), lambda i, l: (i, 0))],
        out_shape=[jax.ShapeDtypeStruct((t, HEAD_DIM), BF16)] * 2)
    dq = DSA_HEADS * HEAD_DIM
    iqw = IDX_HEADS * IDX_DIM
    o_dsa = _call(
        _dsa_kernel, "dsa_attn", lidx, [z_b, k_d, v_d, z_c, z_c, z_c, z_d], grid=(bsz, nq),
        in_specs=[pl.BlockSpec((TQ, dq), lambda b, i, l: (b * nq + i, 1)),
                  pl.BlockSpec((seq, HEAD_DIM), lambda b, i, l: (b, 0)),
                  pl.BlockSpec((seq, HEAD_DIM), lambda b, i, l: (b, 0)),
                  pl.BlockSpec((TQ, iqw), lambda b, i, l: (b * nq + i, 0)),
                  pl.BlockSpec((seq, LANE), lambda b, i, l: (b, iqw // LANE)),
                  pl.BlockSpec((seq, LANE), lambda b, i, l: (b, iqw // LANE + 1)),
                  pl.BlockSpec((TQ, LANE), lambda b, i, l: (b * nq + i, 2))],
        out_specs=pl.BlockSpec((TQ, dq), lambda b, i, l: (b * nq + i, 0)),
        out_shape=jax.ShapeDtypeStruct((t, dq), BF16),
        scratch_shapes=[pltpu.VMEM((seq // TK, TQ, TK), I32), pltpu.VMEM((seq // TK, TK, TQ), I32),
                        pltpu.VMEM((seq // TK, TK, TQ), I16), pltpu.VMEM((seq // TK, TK, TQ), I16),
                        pltpu.VMEM((IDX_HEADS, TQ, LANE), F32), pltpu.VMEM((2, TQ, LANE), I32),
                        pltpu.VMEM((DSA_HEADS * TQ, HEAD_DIM), BF16),
                        pltpu.VMEM((seq // TK, DSA_HEADS * TQ, TK), F32),
                        pltpu.VMEM((DSA_HEADS * TQ, LANE), F32),
                        pltpu.VMEM((DSA_HEADS * TQ, 2 * HEAD_DIM), F32)])

    tm, tn = 1024, 512
    ncol = d // tn
    gate0 = COL_GATE // tn
    assert gate0 * tn == COL_GATE
    mixed = _call(
        _merge_kernel, "merge", lidx, [hb, o_fox, o_nsa, o_dsa, w_proj, w_proj, w_proj, w_br_fox, w_br_nsa, w_br_dsa],
        grid=(t // tm, ncol),
        in_specs=[pl.BlockSpec((tm, d), lambda i, j, l: (i, 0)),
                  pl.BlockSpec((tm, fq), lambda i, j, l: (i, 0)),
                  pl.BlockSpec((tm, nsq), lambda i, j, l: (i, 0)),
                  pl.BlockSpec((tm, dq), lambda i, j, l: (i, 0)),
                  pl.BlockSpec((None, d, tn), lambda i, j, l: (l[0], 0, gate0 + j)),
                  pl.BlockSpec((None, d, tn), lambda i, j, l: (l[0], 0, gate0 + ncol + j)),
                  pl.BlockSpec((None, d, tn), lambda i, j, l: (l[0], 0, gate0 + 2 * ncol + j)),
                  pl.BlockSpec((None, fq, tn), lambda i, j, l: (l[0], 0, j)),
                  pl.BlockSpec((None, nsq, tn), lambda i, j, l: (l[0], 0, j)),
                  pl.BlockSpec((None, dq, tn), lambda i, j, l: (l[0], 0, j))],
        out_specs=pl.BlockSpec((tm, tn), lambda i, j, l: (i, j)),
        out_shape=jax.ShapeDtypeStruct((t, d), BF16))

    h, hb = _out_ln("out_ln", lidx, mixed, w_out, h, ln1_g, ln1_b)

    nff = D_FF // tn
    act = _call(
        _swiglu_kernel, "swiglu", lidx, [hb, w_ffn_in, w_ffn_in], grid=(nff, t // tm),
        in_specs=[pl.BlockSpec((tm, d), lambda j, i, l: (i, 0)),
                  pl.BlockSpec((None, d, tn), lambda j, i, l: (l[0], 0, j)),
                  pl.BlockSpec((None, d, tn), lambda j, i, l: (l[0], 0, nff + j))],
        out_specs=pl.BlockSpec((tm, tn), lambda j, i, l: (i, j)),
        out_shape=jax.ShapeDtypeStruct((t, D_FF), BF16),
        scratch_shapes=[pltpu.VMEM((2, d, tn), BF16)])
    h, hb = _ffn_ln("ffn_ln", lidx, act, w_ffn_out, h, ln2_g, ln2_b, tk=512)

    h, hb = _call(
        _ple_kernel, "ple", lidx, [hb, p, w_ple_in, w_ple_gate, h], grid=(t // tm, ncol),
        in_specs=[pl.BlockSpec((tm, d), lambda i, j, l: (i, 0)),
                  pl.BlockSpec((None, tm, PLE_DIM), lambda i, j, l: (l[0], i, 0)),
                  pl.BlockSpec((None, PLE_DIM, tn), lambda i, j, l: (l[0], 0, j)),
                  pl.BlockSpec((None, d, tn), lambda i, j, l: (l[0], 0, j)),
                  pl.BlockSpec((tm, tn), lambda i, j, l: (i, j))],
        out_specs=[pl.BlockSpec((tm, tn), lambda i, j, l: (i, j)),
                   pl.BlockSpec((tm, tn), lambda i, j, l: (i, j))],
        out_shape=[jax.ShapeDtypeStruct((t, d), F32), jax.ShapeDtypeStruct((t, d), BF16)])
    return h, hb


def kernel(x, p, w_in, fox_f_bias, nsa_pe_k, nsa_pe_v, nsa_cmp_k1, nsa_cmp_k2, nsa_cmp_v1, nsa_cmp_v2, dsa_kv_norm, dsa_kv_up, w_br_fox, w_br_nsa, w_br_dsa, w_out, ln1_g, ln1_b, w_ffn_in, w_ffn_out, ln2_g, ln2_b, w_ple_in, w_ple_gate):
    bsz, seq, d = x.shape
    depth = w_in.shape[0]
    t = bsz * seq
    assert d == D_MODEL and seq % 1024 == 0 and depth == DEPTH

    cat = functools.partial(jnp.concatenate, axis=-1)
    w_proj = _regroup(w_in)

    fox_bias = jnp.pad(fox_f_bias, ((0, 0), (0, LANE - FOX_HEADS))).reshape(depth, 1, LANE)
    half = CMP_STRIDE * HEAD_DIM
    weights = (
        w_proj, fox_bias,
        nsa_pe_k.reshape(depth, 2, half), nsa_pe_v.reshape(depth, 2, half),
        nsa_cmp_k1.astype(BF16), nsa_cmp_k2.astype(BF16), nsa_cmp_v1.astype(BF16), nsa_cmp_v2.astype(BF16),
        dsa_kv_norm.reshape(depth, 1, DSA_KV_RANK), dsa_kv_up.astype(BF16),
        w_br_fox.astype(BF16), w_br_nsa.astype(BF16), w_br_dsa.astype(BF16), w_out.astype(BF16),
        ln1_g.reshape(depth, 1, d), ln1_b.reshape(depth, 1, d),
        w_ffn_in, w_ffn_out.astype(BF16),
        ln2_g.reshape(depth, 1, d), ln2_b.reshape(depth, 1, d),
        p.reshape(depth, t, PLE_DIM), w_ple_in, w_ple_gate,
    )

    cos, sin = _rope_tables(seq, HEAD_DIM)
    cos128 = cat([cos, cos])
    sin128 = cat([-sin, sin])
    cos_i, sin_i = _rope_tables(seq, IDX_DIM)
    zi = jnp.zeros_like(sin_i)
    cos64 = cat([cos_i, cos_i, cos_i, cos_i])
    sin64a = cat([zi, sin_i, zi, sin_i])
    sin64b = cat([-sin_i, zi, -sin_i, zi])
    n_cmp = (seq - CMP_LEN) // CMP_STRIDE + 1
    c_end = jnp.minimum(jnp.arange(N_CMP_PAD) * CMP_STRIDE + CMP_LEN - 1, seq - 1)
    cmp_cos, cmp_sin = cos128[c_end], sin128[c_end]
    n_slc = seq // SLC_LEN
    c_start = jnp.arange(N_CMP_PAD) * CMP_STRIDE
    s_start = jnp.arange(LANE) * SLC_LEN
    overlap = jnp.maximum(jnp.minimum(c_start[:, None] + CMP_LEN - 1, s_start[None, :] + SLC_LEN - 1)
                          - jnp.maximum(c_start[:, None], s_start[None, :]) + 1, 0).astype(F32) / CMP_LEN
    overlap = jnp.where((jnp.arange(N_CMP_PAD)[:, None] < n_cmp) & (jnp.arange(LANE)[None, :] < n_slc), overlap, 0.0)
    overlap = overlap.T.astype(BF16)
    block_of_key = (jnp.arange(seq)[:, None] // SLC_LEN == jnp.arange(LANE)[None, :]).astype(BF16)
    consts = (cos128, sin128, cos64, sin64a, sin64b, cmp_cos, cmp_sin, overlap, block_of_key)

    h = x.reshape(t, d)
    hb = h.astype(BF16)
    for layer in range(depth):
        lidx = jnp.full((1,), layer, I32)
        h, hb = _layer(lidx, h, hb, consts, weights, bsz, seq)
    return h.reshape(bsz, seq, d)
```

```python
import functools

import jax
import jax.numpy as jnp
from jax import lax
from jax.experimental import pallas as pl
from jax.experimental.pallas import tpu as pltpu

F32, BF16, I32, I16 = jnp.float32, jnp.bfloat16, jnp.int32, jnp.int16

D_MODEL = 2048
DEPTH = 4
HEAD_DIM = 128
ROPE_THETA = 10000.0
NEG_INF = -1e30
FOX_HEADS = 6
NSA_HEADS = 4
CMP_LEN = 32
CMP_STRIDE = 16
CMP_HIDDEN = 256
SLC_LEN = 64
SLC_TOPN = 16
WIN = 512
SLC_FORCE = 1e4
DSA_HEADS = 6
DSA_KV_RANK = 256
IDX_HEADS = 16
IDX_DIM = 64
IDX_TOPK_MAX = 256
D_FF = ((8 * D_MODEL + 3 * 256 - 1) // (3 * 256)) * 256
PLE_DIM = 256
ALPHA = (2 * DEPTH) ** 0.25
SCALE = HEAD_DIM ** -0.5
LOG2E = 1.4426950408889634
SCALE_LOG2E = SCALE * LOG2E

LANE = 128
VMEM_LIMIT_BYTES = 48 * 1024 * 1024
VMEM_LIMIT_LARGE_BYTES = 56 * 1024 * 1024

_IN_SPLITS = (
    ('fox_q', FOX_HEADS * HEAD_DIM), ('fox_k', FOX_HEADS * HEAD_DIM), ('fox_v', FOX_HEADS * HEAD_DIM), ('fox_f', FOX_HEADS),
    ('nsa_q', NSA_HEADS * HEAD_DIM), ('nsa_kc', HEAD_DIM), ('nsa_vc', HEAD_DIM), ('nsa_ks', HEAD_DIM), ('nsa_vs', HEAD_DIM),
    ('nsa_kw', HEAD_DIM), ('nsa_vw', HEAD_DIM), ('nsa_g', 3 * NSA_HEADS),
    ('dsa_q', DSA_HEADS * HEAD_DIM), ('dsa_ckv', DSA_KV_RANK), ('idx_q', IDX_HEADS * IDX_DIM), ('idx_k', IDX_DIM), ('idx_w', IDX_HEADS),
    ('gate', 3 * D_MODEL),
)
_IN_OFFSETS = {}
_off = 0
for _name, _width in _IN_SPLITS:
    _IN_OFFSETS[_name] = (_off, _off + _width)
    _off += _width

MISC_FOX_F = 0
MISC_NSA_G = FOX_HEADS
MISC_IDX_W = FOX_HEADS + 3 * NSA_HEADS

TQ = 256
TK = 256
N_CMP_PAD = 128


def _dot(a, b):
    return jnp.dot(a, b, preferred_element_type=F32)


def _dot_nt(a, b):
    return lax.dot_general(a, b, (((1,), (1,)), ((), ())), preferred_element_type=F32)


def _split3(a):
    a1 = a.astype(BF16)
    r1 = a - a1.astype(F32)
    a2 = r1.astype(BF16)
    a3 = (r1 - a2.astype(F32)).astype(BF16)
    return a1, a2, a3


def _call(kernel, name, lidx, args, grid, in_specs, out_specs, out_shape, scratch_shapes=(),
          vmem_limit_bytes=VMEM_LIMIT_BYTES):
    return pl.pallas_call(
        kernel,
        grid_spec=pltpu.PrefetchScalarGridSpec(
            num_scalar_prefetch=1, grid=grid, in_specs=in_specs, out_specs=out_specs,
            scratch_shapes=list(scratch_shapes)),
        out_shape=out_shape,
        compiler_params=pltpu.CompilerParams(
            dimension_semantics=("arbitrary",) * len(grid), vmem_limit_bytes=vmem_limit_bytes),
        name=name,
    )(lidx, *args)


def _regroup_plan():
    plan = []

    def whole(name):
        a, b = _IN_OFFSETS[name]
        for g in range((b - a) // LANE):
            plan.append((a + g * LANE, 0, LANE))

    whole('dsa_ckv')
    lane0 = {}
    start = 0
    for name in ('fox_f', 'nsa_g', 'idx_w'):
        a, b = _IN_OFFSETS[name]
        lane0[name] = (a - start, start, start + b - a)
        start += b - a
    plan.append(lane0['fox_f'])
    whole('nsa_kc')
    whole('nsa_vc')
    plan.append(lane0['nsa_g'])
    plan.append(lane0['idx_w'])
    plan.append((0, 0, 0))
    for name in ('fox_q', 'fox_k', 'fox_v', 'nsa_vs', 'nsa_vw'):
        whole(name)
    for name in ('nsa_q', 'nsa_ks', 'nsa_kw', 'dsa_q'):
        whole(name)
    whole('gate')
    whole('idx_q')
    ik = _IN_OFFSETS['idx_k'][0]
    plan.append((ik, 0, IDX_DIM))
    plan.append((ik - IDX_DIM, IDX_DIM, LANE))
    return tuple(plan)


_REGROUP_PLAN = _regroup_plan()
COL_F32 = 0
N_F32 = DSA_KV_RANK + 6 * LANE
ZD_FOX_F, ZD_NSA_KC, ZD_NSA_VC, ZD_NSA_G, ZD_IDX_W = 2, 3, 4, 5, 6
COL_PLAIN = COL_F32 + N_F32
N_PLAIN = (3 * FOX_HEADS + 2) * HEAD_DIM
COL_ROPE128 = COL_PLAIN + N_PLAIN
N_ROPE128 = (NSA_HEADS + 2 + DSA_HEADS) * HEAD_DIM
COL_GATE = COL_ROPE128 + N_ROPE128
COL_ROPE64 = COL_GATE + 3 * D_MODEL
N_ROPE64 = IDX_HEADS * IDX_DIM + 2 * LANE
N_REGROUPED = COL_ROPE64 + N_ROPE64
assert N_REGROUPED == len(_REGROUP_PLAN) * LANE


def _regroup_kernel(src_ref, lo_ref, hi_ref, x_ref, o_ref):
    g = pl.program_id(0)
    col = lax.broadcasted_iota(I32, (LANE, x_ref.shape[2]), 0)
    keep = (col >= lo_ref[g]) & (col < hi_ref[g])
    for layer in range(x_ref.shape[1]):
        o_ref[layer] = jnp.where(keep, x_ref[:, layer, :], 0.0).T.astype(o_ref.dtype)


def _regroup(w_in):
    depth, d, n_in = w_in.shape
    w_t = jnp.transpose(w_in, (2, 0, 1))
    src = jnp.array([p[0] for p in _REGROUP_PLAN], I32)
    lo = jnp.array([p[1] for p in _REGROUP_PLAN], I32)
    hi = jnp.array([p[2] for p in _REGROUP_PLAN], I32)
    return pl.pallas_call(
        _regroup_kernel,
        grid_spec=pltpu.PrefetchScalarGridSpec(
            num_scalar_prefetch=3, grid=(len(_REGROUP_PLAN),),
            in_specs=[pl.BlockSpec((pl.Element(LANE), pl.Element(depth), pl.Element(d)),
                                   lambda g, src, lo, hi: (src[g], 0, 0))],
            out_specs=pl.BlockSpec((depth, d, LANE), lambda g, src, lo, hi: (0, 0, g))),
        out_shape=jax.ShapeDtypeStruct((depth, d, N_REGROUPED), BF16),
        compiler_params=pltpu.CompilerParams(dimension_semantics=("arbitrary",), vmem_limit_bytes=VMEM_LIMIT_BYTES),
        name="regroup_w_in",
    )(src, lo, hi, w_t)


def _proj_kernel(l_ref, x_ref, w_ref, *refs, shifts):
    o_ref = refs[-1]
    acc = _dot(x_ref[...], w_ref[...])
    if not shifts:
        o_ref[...] = acc.astype(o_ref.dtype)
        return
    c_ref = refs[0]
    s_refs = refs[1:-1]
    for g in range(acc.shape[1] // LANE):
        xg = acc[:, g * LANE:(g + 1) * LANE]
        out = xg * c_ref[...]
        for shift, s_ref in zip(shifts, s_refs):
            out = out + pltpu.roll(xg, shift, 1) * s_ref[...]
        o_ref[:, g * LANE:(g + 1) * LANE] = out.astype(o_ref.dtype)


def _proj(name, lidx, hb, w, col0, n, tm, tn, out_dtype, seq, rope=None):
    t, d = hb.shape
    jb = col0 // tn
    assert jb * tn == col0
    in_specs = [pl.BlockSpec((tm, d), lambda i, j, l: (i, 0)),
                pl.BlockSpec((None, d, tn), lambda i, j, l: (l[0], 0, jb + j))]
    args = [hb, w]
    shifts = ()
    if rope is not None:
        shifts, tables = rope[0], rope[1:]
        nrow = seq // tm
        for tab in tables:
            in_specs.append(pl.BlockSpec((tm, LANE), lambda i, j, l: (i % nrow, 0)))
            args.append(tab)
    return _call(functools.partial(_proj_kernel, shifts=shifts), name, lidx, args,
                 grid=(t // tm, n // tn), in_specs=in_specs,
                 out_specs=pl.BlockSpec((tm, tn), lambda i, j, l: (i, j)),
                 out_shape=jax.ShapeDtypeStruct((t, n), out_dtype))


def _cum_kernel(l_ref, f_ref, bias_ref, o_ref):
    chunk = 256
    row = lax.broadcasted_iota(I32, (chunk, chunk), 0)
    col = lax.broadcasted_iota(I32, (chunk, chunk), 1)
    tri = jnp.where(row >= col, 1.0, 0.0).astype(BF16)
    carry = jnp.zeros((1, LANE), F32)
    for c in range(f_ref.shape[0] // chunk):
        logf = jax.nn.log_sigmoid(f_ref[c * chunk:(c + 1) * chunk, :] + bias_ref[...])
        l1, l2, l3 = _split3(logf)
        cs = _dot(tri, l1) + _dot(tri, l2) + _dot(tri, l3) + carry
        o_ref[c * chunk:(c + 1) * chunk, :] = cs
        carry = cs[chunk - 1:chunk, :]


def _max_lanes(x):
    out = x[:, 0:LANE]
    for g in range(1, x.shape[1] // LANE):
        out = jnp.maximum(out, x[:, g * LANE:(g + 1) * LANE])
    return out


def _tile_lanes(x, n):
    return jnp.concatenate([x] * n, axis=1)


def _row_max_to_lanes(mx):
    return jnp.broadcast_to(jnp.max(mx, axis=-1, keepdims=True), mx.shape)


def _exp_pv(s, m_lanes, v):
    p = jnp.exp2(s - _tile_lanes(m_lanes, s.shape[1] // LANE)).astype(BF16)
    return _dot(p, jnp.concatenate([v, jnp.ones_like(v)], axis=1))


def _fox_kernel(l_ref, q_ref, k_ref, v_ref, cq_ref, ck_ref, o_ref, s_ref, mx_ref, cqb_ref, acc_ref):
    qi = pl.program_id(1)
    nh = FOX_HEADS
    nchunk = ck_ref.shape[0] // 8
    row = lax.broadcasted_iota(I32, (TQ, TK), 0)
    col = lax.broadcasted_iota(I32, (TQ, TK), 1)
    cq_all = cq_ref[...]
    for h in range(nh):
        cqb_ref[h] = jnp.broadcast_to(cq_all[:, h:h + 1] * LOG2E, (TQ, LANE))
    mx_ref[...] = jnp.full(mx_ref.shape, NEG_INF, F32)
    acc_ref[...] = jnp.zeros(acc_ref.shape, F32)

    def score_step(j, diagonal):
        start = pl.multiple_of(j * TK, TK)
        for h in range(nh):
            hs = slice(h * HEAD_DIM, (h + 1) * HEAD_DIM)
            s = (_dot_nt(q_ref[:, hs], k_ref[pl.ds(start, TK), hs]) * SCALE_LOG2E
                 + _tile_lanes(cqb_ref[h], TK // LANE) - ck_ref[pl.ds(h * nchunk + j, 1), :] * LOG2E)
            if diagonal:
                s = jnp.where(col <= row, s, NEG_INF)
            s_ref[h, j] = s
            mx_ref[h] = jnp.maximum(mx_ref[h], _max_lanes(s))

    def score_body(j, carry):
        score_step(j, False)
        return carry

    lax.fori_loop(0, qi, score_body, 0)
    score_step(qi, True)
    for h in range(nh):
        mx_ref[h] = _row_max_to_lanes(mx_ref[h])

    def pv_body(j, carry):
        start = pl.multiple_of(j * TK, TK)
        for h in range(nh):
            hs = slice(h * HEAD_DIM, (h + 1) * HEAD_DIM)
            acc_ref[h] += _exp_pv(s_ref[h, j], mx_ref[h], v_ref[pl.ds(start, TK), hs])
        return carry

    lax.fori_loop(0, qi + 1, pv_body, 0)
    for h in range(nh):
        a = acc_ref[h]
        o_ref[:, h * HEAD_DIM:(h + 1) * HEAD_DIM] = (a[:, :HEAD_DIM] / a[:, HEAD_DIM:]).astype(o_ref.dtype)


def _cmp_kernel(l_ref, rk_ref, rv_ref, pek_ref, pev_ref, wk1_ref, wk2_ref, wv1_ref, wv2_ref,
                c_ref, s_ref, ko_ref, vo_ref):
    half = CMP_STRIDE * HEAD_DIM

    def compress(r_ref, pe_ref, w1_ref, w2_ref):
        r = r_ref[...]
        lo = (r + pe_ref[0:1, :]).astype(BF16)
        hi = (r + pe_ref[1:2, :]).astype(BF16)
        a = _dot(lo, w1_ref[0:half, :])
        b = _dot(hi, w1_ref[half:2 * half, :])
        hid = a + pltpu.roll(b, N_CMP_PAD - 1, 0)
        return _dot(jax.nn.gelu(hid).astype(BF16), w2_ref[...])

    kc = compress(rk_ref, pek_ref, wk1_ref, wk2_ref)
    kc = kc * c_ref[...] + pltpu.roll(kc, HEAD_DIM // 2, 1) * s_ref[...]
    ko_ref[...] = kc.astype(ko_ref.dtype)
    vo_ref[...] = compress(rv_ref, pev_ref, wv1_ref, wv2_ref).astype(vo_ref.dtype)


def _masked_softmax(s, mask):
    s = jnp.where(mask, s, NEG_INF)
    m = jnp.max(s, axis=-1, keepdims=True)
    e = jnp.where(mask, jnp.exp(s - m), 0.0)
    den = jnp.sum(e, axis=-1, keepdims=True)
    return e, jnp.where(den > 0.0, den, 1.0)


def _nsa_kernel(l_ref, q_ref, ks_ref, kw_ref, vs_ref, vw_ref, kc_ref, vc_ref, g_ref, ovt_ref, et_ref,
                o_ref, q_sc, s_ref, mx_ref, acc_ref):
    qi = pl.program_id(1)
    nh = NSA_HEADS
    for h in range(nh):
        q_sc[h * TQ:(h + 1) * TQ, :] = q_ref[:, h * HEAD_DIM:(h + 1) * HEAD_DIM]
    q = q_sc[...]
    pos = qi * TQ + lax.broadcasted_iota(I32, (TQ, 1), 0)
    lane = lax.broadcasted_iota(I32, (TQ, LANE), 1)

    cvis = (lane * CMP_STRIDE + (CMP_LEN - 1)) <= pos
    s_c = (_dot_nt(q, kc_ref[...]) * SCALE).reshape(nh, TQ, N_CMP_PAD)
    e_c, den_c = _masked_softmax(s_c, cvis[None])
    p_c = e_c / den_c
    o_cmp = _dot(p_c.reshape(nh * TQ, N_CMP_PAD).astype(BF16), vc_ref[...])

    p_sum = p_c[0]
    for h in range(1, nh):
        p_sum = p_sum + p_c[h]
    p1, p2, p3 = _split3(p_sum)
    ovt = ovt_ref[...]
    n_slc = et_ref.shape[0] // SLC_LEN
    imp = (_dot_nt(ovt, p1) + _dot_nt(ovt, p2) + _dot_nt(ovt, p3))[0:n_slc, :]
    blk = lax.broadcasted_iota(I32, (n_slc, TQ), 0)
    pos_t = qi * TQ + lax.broadcasted_iota(I32, (n_slc, TQ), 1)
    cur = lax.shift_right_logical(pos_t, SLC_LEN.bit_length() - 1)
    forced = (blk == 0) | (blk == cur) | (blk == cur - 1)
    imp = jnp.where(forced, SLC_FORCE, jnp.where(blk * SLC_LEN <= pos_t, imp, -SLC_FORCE))
    rank = jnp.zeros((n_slc, TQ), F32)
    for k in range(n_slc):
        ik = imp[k:k + 1, :]
        ahead = (ik > imp) | ((ik == imp) & (blk > k))
        rank = rank + jnp.where(ahead, 1.0, 0.0)
    sel_t = jnp.where(rank < float(min(SLC_TOPN, n_slc)), 1.0, 0.0)
    sel_t = jnp.concatenate([sel_t, jnp.zeros((LANE - n_slc, TQ), F32)], axis=0)
    sel = sel_t.T.astype(BF16)

    mx_ref[...] = jnp.full(mx_ref.shape, NEG_INF, F32)
    acc_ref[...] = jnp.zeros(acc_ref.shape, F32)
    row = lax.broadcasted_iota(I32, (TQ, TK), 0)
    col = lax.broadcasted_iota(I32, (TQ, TK), 1)

    def score_step(j, diagonal):
        start = pl.multiple_of(j * TK, TK)
        vis = _dot_nt(sel, et_ref[pl.ds(start, TK), :]) > 0.5
        if diagonal:
            vis = vis & (col <= row)
        s = (_dot_nt(q_sc[...], ks_ref[pl.ds(start, TK), :]) * SCALE_LOG2E).reshape(nh, TQ, TK)
        s = jnp.where(vis[None], s, NEG_INF).reshape(nh * TQ, TK)
        s_ref[j] = s
        mx_ref[...] = jnp.maximum(mx_ref[...], _max_lanes(s))

    def score_body(j, carry):
        score_step(j, False)
        return carry

    lax.fori_loop(0, qi, score_body, 0)
    score_step(qi, True)
    mx_ref[...] = _row_max_to_lanes(mx_ref[...])

    def pv_body(j, carry):
        start = pl.multiple_of(j * TK, TK)
        acc_ref[...] += _exp_pv(s_ref[j], mx_ref[...], vs_ref[pl.ds(start, TK), :])
        return carry

    lax.fori_loop(0, qi + 1, pv_body, 0)
    a_slc = acc_ref[...]
    o_slc = a_slc[:, :HEAD_DIM] / a_slc[:, HEAD_DIM:]

    band = WIN + TQ
    start_w = pl.multiple_of(jnp.maximum(qi * TQ - WIN, 0), TQ)
    kwb = kw_ref[pl.ds(start_w, band), :]
    vwb = vw_ref[pl.ds(start_w, band), :]
    dist = pos - (start_w + lax.broadcasted_iota(I32, (TQ, band), 1))
    wvis = (dist >= 0) & (dist < WIN)
    s_w = (_dot_nt(q, kwb) * SCALE_LOG2E).reshape(nh, TQ, band)
    s_w = jnp.where(wvis[None], s_w, NEG_INF).reshape(nh * TQ, band)
    a_win = _exp_pv(s_w, _row_max_to_lanes(_max_lanes(s_w)), vwb)
    o_win = a_win[:, :HEAD_DIM] / a_win[:, HEAD_DIM:]

    gate = jax.nn.sigmoid(g_ref[...])
    for h in range(nh):
        rows = slice(h * TQ, (h + 1) * TQ)
        g0 = gate[:, MISC_NSA_G + 3 * h:MISC_NSA_G + 3 * h + 1]
        g1 = gate[:, MISC_NSA_G + 3 * h + 1:MISC_NSA_G + 3 * h + 2]
        g2 = gate[:, MISC_NSA_G + 3 * h + 2:MISC_NSA_G + 3 * h + 3]
        out = g0 * o_cmp[rows] + g1 * o_slc[rows] + g2 * o_win[rows]
        o_ref[:, h * HEAD_DIM:(h + 1) * HEAD_DIM] = out.astype(o_ref.dtype)


def _dsa_kv_kernel(l_ref, x_ref, g_ref, up_ref, c_ref, s_ref, ko_ref, vo_ref):
    x = x_ref[...]
    r = x * lax.rsqrt(jnp.mean(jnp.square(x), axis=-1, keepdims=True) + 1e-6) * g_ref[...]
    kv = _dot(r.astype(BF16), up_ref[...])
    k = kv[:, :HEAD_DIM]
    k = k * c_ref[...] + pltpu.roll(k, HEAD_DIM // 2, 1) * s_ref[...]
    ko_ref[...] = k.astype(ko_ref.dtype)
    vo_ref[...] = kv[:, HEAD_DIM:].astype(vo_ref.dtype)


def _dsa_kernel(l_ref, q_ref, kd_ref, vd_ref, iq_ref, ika_ref, ikb_ref, w_ref, o_ref,
                key_ref, keyt_ref, hi_ref, lo_ref, wb_ref, sel_ref, q_sc, s_ref, mx_ref, acc_ref):
    qi = pl.program_id(1)
    nh = DSA_HEADS
    seq = kd_ref.shape[0]
    k_top = float(min(IDX_TOPK_MAX, seq // 4))
    row = lax.broadcasted_iota(I32, (TQ, TK), 0)
    col = lax.broadcasted_iota(I32, (TQ, TK), 1)
    int_min = jnp.int32(-2 ** 31)
    idx_bits = (seq - 1).bit_length()

    w_all = w_ref[...] * (IDX_HEADS ** -0.5 * IDX_DIM ** -0.5)
    for h in range(IDX_HEADS):
        wb_ref[h] = jnp.broadcast_to(w_all[:, MISC_IDX_W + h:MISC_IDX_W + h + 1], (TQ, LANE))

    def score_step(j, diagonal):
        start = pl.multiple_of(j * TK, TK)
        ka = ika_ref[pl.ds(start, TK), :]
        kb = ikb_ref[pl.ds(start, TK), :]
        sc = jnp.zeros((TQ, TK), F32)
        for g in range(IDX_HEADS // 2):
            qpair = iq_ref[:, g * LANE:(g + 1) * LANE]
            wa = _tile_lanes(wb_ref[2 * g], TK // LANE)
            wb = _tile_lanes(wb_ref[2 * g + 1], TK // LANE)
            sc = sc + jnp.maximum(_dot_nt(qpair, ka), 0.0) * wa + jnp.maximum(_dot_nt(qpair, kb), 0.0) * wb
        if diagonal:
            sc = jnp.where(col <= row, sc, NEG_INF)
        bits = lax.bitcast_convert_type(sc, I32)
        key = jnp.where(bits < 0, bits ^ jnp.int32(0x7FFFFFFF), bits)
        key = jnp.where(sc == 0.0, 0, key)
        key_ref[j] = key
        key_t = key.T
        keyt_ref[j] = key_t
        hi_ref[j] = (key_t >> 16).astype(I16)
        lo_ref[j] = ((key_t & 0xFFFF) - 32768).astype(I16)

    def score_body(j, carry):
        score_step(j, False)
        return carry

    lax.fori_loop(0, qi, score_body, 0)
    score_step(qi, True)

    krow = lax.broadcasted_iota(I32, (TK, TQ), 0)

    def count(pred_fn):
        def body(j, acc):
            hit = jnp.where(pred_fn(keyt_ref[j], j), 1.0, 0.0)
            return acc + jnp.sum(hit.reshape(TK // 8, 8, TQ), axis=0)
        part = lax.fori_loop(0, qi + 1, body, jnp.zeros((8, TQ), F32))
        return jnp.sum(part, axis=0, keepdims=True)

    def count16(ref, cand, strict):
        cand16 = cand.astype(I16)

        def body(j, acc):
            x = ref[j]
            hit = jnp.where(x > cand16 if strict else x >= cand16, jnp.int16(1), jnp.int16(0))
            for g in range(TK // 16):
                acc = acc + hit[g * 16:(g + 1) * 16, :]
            return acc
        part = lax.fori_loop(0, qi + 1, body, jnp.zeros((16, TQ), I16))
        return jnp.sum(part.astype(F32), axis=0, keepdims=True)

    def bisect16(ref, need):
        zero = jnp.zeros((1, TQ), I32)
        v = jnp.where(count16(ref, zero, False) >= need, zero, zero - 32768)

        def bit(i, v):
            cand = v + (jnp.int32(1) << (14 - i))
            return jnp.where(count16(ref, cand, False) >= need, cand, v)
        return lax.fori_loop(0, 15, bit, v)

    hi_thr = bisect16(hi_ref, k_top)
    need_lo = k_top - count16(hi_ref, hi_thr, True)
    hi_thr16 = hi_thr.astype(I16)

    def mask_lo(j, carry):
        lo_ref[j] = jnp.where(hi_ref[j] == hi_thr16, lo_ref[j], jnp.int16(-32768))
        return carry

    lax.fori_loop(0, qi + 1, mask_lo, 0)
    lo_thr = bisect16(lo_ref, need_lo)
    thr = lax.shift_left(hi_thr, jnp.int32(16)) + (lo_thr + 32768)
    n_gt = count(lambda key, j: key > thr)
    n_ge = count(lambda key, j: key >= thr)

    def index_bisect():
        def index_bit(i, cut):
            cand = cut + (jnp.int32(1) << (idx_bits - 1 - i))
            n_before = count(lambda key, j: (key == thr) & (j * TK + krow < cand))
            return jnp.where(n_gt + n_before < k_top, cand, cut)
        return lax.fori_loop(0, idx_bits, index_bit, jnp.zeros((1, TQ), I32))

    def take_all_ties():
        return jnp.full((1, TQ), seq - 1, I32)

    cut = lax.cond(jnp.max(n_ge) > k_top, index_bisect, take_all_ties)
    sel_ref[0] = jnp.broadcast_to(thr, (LANE, TQ)).T
    sel_ref[1] = jnp.broadcast_to(cut, (LANE, TQ)).T

    for h in range(nh):
        q_sc[h * TQ:(h + 1) * TQ, :] = q_ref[:, h * HEAD_DIM:(h + 1) * HEAD_DIM]
    mx_ref[...] = jnp.full(mx_ref.shape, NEG_INF, F32)
    acc_ref[...] = jnp.zeros(acc_ref.shape, F32)

    def attn_score_step(j, diagonal):
        start = pl.multiple_of(j * TK, TK)
        key = key_ref[j]
        thr_t = _tile_lanes(sel_ref[0], TK // LANE)
        cut_t = _tile_lanes(sel_ref[1], TK // LANE)
        vis = (key > thr_t) | ((key == thr_t) & (j * TK + col <= cut_t))
        if diagonal:
            vis = vis & (col <= row)
        s = (_dot_nt(q_sc[...], kd_ref[pl.ds(start, TK), :]) * SCALE_LOG2E).reshape(nh, TQ, TK)
        s = jnp.where(vis[None], s, NEG_INF).reshape(nh * TQ, TK)
        s_ref[j] = s
        mx_ref[...] = jnp.maximum(mx_ref[...], _max_lanes(s))

    def attn_score_body(j, carry):
        attn_score_step(j, False)
        return carry

    lax.fori_loop(0, qi, attn_score_body, 0)
    attn_score_step(qi, True)
    mx_ref[...] = _row_max_to_lanes(mx_ref[...])

    def pv_body(j, carry):
        start = pl.multiple_of(j * TK, TK)
        acc_ref[...] += _exp_pv(s_ref[j], mx_ref[...], vd_ref[pl.ds(start, TK), :])
        return carry

    lax.fori_loop(0, qi + 1, pv_body, 0)
    a = acc_ref[...]
    out = a[:, :HEAD_DIM] / a[:, HEAD_DIM:]
    for h in range(nh):
        o_ref[:, h * HEAD_DIM:(h + 1) * HEAD_DIM] = out[h * TQ:(h + 1) * TQ].astype(o_ref.dtype)


def _merge_kernel(l_ref, hb_ref, of_ref, on_ref, od_ref, wg1_ref, wg2_ref, wg3_ref, wf_ref, wn_ref, wd_ref, o_ref):
    hb = hb_ref[...]
    mixed = (jax.nn.sigmoid(_dot(hb, wg1_ref[...])) * _dot(of_ref[...], wf_ref[...])
             + jax.nn.sigmoid(_dot(hb, wg2_ref[...])) * _dot(on_ref[...], wn_ref[...])
             + jax.nn.sigmoid(_dot(hb, wg3_ref[...])) * _dot(od_ref[...], wd_ref[...]))
    o_ref[...] = mixed.astype(o_ref.dtype)


LN_SUB_ROWS = 256


def _layer_norm_store(y, g_ref, b_ref, of_ref, ob_ref, rows):
    mu = jnp.mean(y, axis=-1, keepdims=True)
    yc = y - mu
    var = jnp.mean(jnp.square(yc), axis=-1, keepdims=True)
    out = yc * lax.rsqrt(var + 1e-5) * g_ref[...] + b_ref[...]
    of_ref[rows, :] = out
    ob_ref[rows, :] = out.astype(ob_ref.dtype)


def _out_ln_kernel(l_ref, x_ref, w_ref, h_ref, g_ref, b_ref, of_ref, ob_ref):
    w = w_ref[...]
    for r in range(x_ref.shape[0] // LN_SUB_ROWS):
        rows = slice(r * LN_SUB_ROWS, (r + 1) * LN_SUB_ROWS)
        y = ALPHA * h_ref[rows, :] + _dot(x_ref[rows, :], w)
        _layer_norm_store(y, g_ref, b_ref, of_ref, ob_ref, rows)


def _out_ln(name, lidx, x, w, h, g, b):
    t, kdim = x.shape
    d = h.shape[1]
    tm = 512
    return _call(_out_ln_kernel, name, lidx, [x, w, h, g, b], grid=(t // tm,),
                 in_specs=[pl.BlockSpec((tm, kdim), lambda i, l: (i, 0)),
                           pl.BlockSpec((None, kdim, d), lambda i, l: (l[0], 0, 0)),
                           pl.BlockSpec((tm, d), lambda i, l: (i, 0)),
                           pl.BlockSpec((None, 1, d), lambda i, l: (l[0], 0, 0)),
                           pl.BlockSpec((None, 1, d), lambda i, l: (l[0], 0, 0))],
                 out_specs=[pl.BlockSpec((tm, d), lambda i, l: (i, 0)),
                            pl.BlockSpec((tm, d), lambda i, l: (i, 0))],
                 out_shape=[jax.ShapeDtypeStruct((t, d), F32), jax.ShapeDtypeStruct((t, d), BF16)])


def _ffn_ln_kernel(l_ref, x_ref, w_ref, h_ref, g_ref, b_ref, of_ref, ob_ref, *, nk):
    k = pl.program_id(1)

    @pl.when(k == 0)
    def _():
        of_ref[...] = ALPHA * h_ref[...] + _dot(x_ref[...], w_ref[...])

    @pl.when((k > 0) & (k < nk - 1))
    def _():
        of_ref[...] += _dot(x_ref[...], w_ref[...])

    @pl.when(k == nk - 1)
    def _():
        w = w_ref[...]
        for r in range(x_ref.shape[0] // LN_SUB_ROWS):
            rows = slice(r * LN_SUB_ROWS, (r + 1) * LN_SUB_ROWS)
            y = of_ref[rows, :] + _dot(x_ref[rows, :], w)
            _layer_norm_store(y, g_ref, b_ref, of_ref, ob_ref, rows)


def _ffn_ln(name, lidx, x, w, h, g, b, tk):
    t, kdim = x.shape
    d = h.shape[1]
    tm = 1024
    nk = kdim // tk
    assert nk >= 2
    return _call(functools.partial(_ffn_ln_kernel, nk=nk), name, lidx, [x, w, h, g, b],
                 grid=(t // tm, nk),
                 in_specs=[pl.BlockSpec((tm, tk), lambda i, k, l: (i, k)),
                           pl.BlockSpec((None, tk, d), lambda i, k, l: (l[0], k, 0)),
                           pl.BlockSpec((tm, d), lambda i, k, l: (i, 0)),
                           pl.BlockSpec((None, 1, d), lambda i, k, l: (l[0], 0, 0)),
                           pl.BlockSpec((None, 1, d), lambda i, k, l: (l[0], 0, 0))],
                 out_specs=[pl.BlockSpec((tm, d), lambda i, k, l: (i, 0)),
                            pl.BlockSpec((tm, d), lambda i, k, l: (i, 0))],
                 out_shape=[jax.ShapeDtypeStruct((t, d), F32), jax.ShapeDtypeStruct((t, d), BF16)],
                 vmem_limit_bytes=VMEM_LIMIT_LARGE_BYTES)


def _swiglu_kernel(l_ref, x_ref, wa_ref, wb_ref, o_ref, wab_ref):
    @pl.when(pl.program_id(1) == 0)
    def _():
        wab_ref[0] = wa_ref[...].astype(BF16)
        wab_ref[1] = wb_ref[...].astype(BF16)

    x = x_ref[...]
    a = _dot(x, wab_ref[0])
    o_ref[...] = (jax.nn.silu(a) * _dot(x, wab_ref[1])).astype(o_ref.dtype)


def _ple_kernel(l_ref, hb_ref, p_ref, wpi_ref, wpg_ref, h_ref, of_ref, ob_ref):
    p_in = _dot(p_ref[...].astype(BF16), wpi_ref[...].astype(BF16))
    out = h_ref[...] + p_in * jax.nn.sigmoid(_dot(hb_ref[...], wpg_ref[...].astype(BF16)))
    of_ref[...] = out
    ob_ref[...] = out.astype(ob_ref.dtype)


def _rope_tables(n, dim):
    inv = 1.0 / (ROPE_THETA ** (jnp.arange(0, dim, 2, dtype=F32) / dim))
    ang = jnp.arange(n, dtype=F32)[:, None] * inv[None, :]
    return jnp.cos(ang), jnp.sin(ang)


def _layer(lidx, h, hb, consts, weights, bsz, seq):
    (cos128, sin128, cos64, sin64a, sin64b, cmp_cos, cmp_sin, overlap, block_of_key) = consts
    (w_proj, fox_bias, pe_k, pe_v, wk1, wk2, wv1, wv2, kv_norm, kv_up,
     w_br_fox, w_br_nsa, w_br_dsa, w_out, ln1_g, ln1_b, w_ffn_in, w_ffn_out, ln2_g, ln2_b, p, w_ple_in, w_ple_gate) = weights
    t = bsz * seq
    nq = seq // TQ
    d = D_MODEL

    z_a = _proj("proj_plain", lidx, hb, w_proj, COL_PLAIN, N_PLAIN, 2048, 512, BF16, seq)
    z_b = _proj("proj_rope128", lidx, hb, w_proj, COL_ROPE128, N_ROPE128, 2048, 512, BF16, seq,
                rope=((HEAD_DIM // 2,), cos128, sin128))
    z_c = _proj("proj_rope64", lidx, hb, w_proj, COL_ROPE64, N_ROPE64, 2048, 256, BF16, seq,
                rope=((IDX_DIM // 2, LANE - IDX_DIM // 2), cos64, sin64a, sin64b))
    z_d = _proj("proj_f32", lidx, hb, w_proj, COL_F32, N_F32, 1024, N_F32, F32, seq)

    cum = _call(_cum_kernel, "fox_cum", lidx, [z_d, fox_bias], grid=(bsz,),
                in_specs=[pl.BlockSpec((seq, LANE), lambda b, l: (b, ZD_FOX_F)),
                          pl.BlockSpec((None, 1, LANE), lambda b, l: (l[0], 0, 0))],
                out_specs=pl.BlockSpec((seq, LANE), lambda b, l: (b, 0)),
                out_shape=jax.ShapeDtypeStruct((t, LANE), F32))
    cum_rows = jnp.transpose(cum.reshape(bsz, seq, LANE)[:, :, :8], (0, 2, 1)).reshape(bsz, 8 * (seq // TK), TK)
    fq = FOX_HEADS * HEAD_DIM
    o_fox = _call(_fox_kernel, "fox_attn", lidx, [z_a, z_a, z_a, cum, cum_rows], grid=(bsz, nq),
                  in_specs=[pl.BlockSpec((TQ, fq), lambda b, i, l: (b * nq + i, 0)),
                            pl.BlockSpec((seq, fq), lambda b, i, l: (b, 1)),
                            pl.BlockSpec((seq, fq), lambda b, i, l: (b, 2)),
                            pl.BlockSpec((TQ, LANE), lambda b, i, l: (b * nq + i, 0)),
                            pl.BlockSpec((None, 8 * (seq // TK), TK), lambda b, i, l: (b, 0, 0))],
                  out_specs=pl.BlockSpec((TQ, fq), lambda b, i, l: (b * nq + i, 0)),
                  out_shape=jax.ShapeDtypeStruct((t, fq), BF16),
                  scratch_shapes=[pltpu.VMEM((FOX_HEADS, seq // TK, TQ, TK), F32),
                                  pltpu.VMEM((FOX_HEADS, TQ, LANE), F32), pltpu.VMEM((FOX_HEADS, TQ, LANE), F32),
                                  pltpu.VMEM((FOX_HEADS, TQ, 2 * HEAD_DIM), F32)])

    nchunk = seq // CMP_STRIDE
    half = CMP_STRIDE * HEAD_DIM
    r_k = z_d[:, ZD_NSA_KC * LANE:(ZD_NSA_KC + 1) * LANE].reshape(bsz * nchunk, half)
    r_v = z_d[:, ZD_NSA_VC * LANE:(ZD_NSA_VC + 1) * LANE].reshape(bsz * nchunk, half)
    k_cmp, v_cmp = _call(
        _cmp_kernel, "nsa_compress", lidx, [r_k, r_v, pe_k, pe_v, wk1, wk2, wv1, wv2, cmp_cos, cmp_sin], grid=(bsz,),
        in_specs=[pl.BlockSpec((nchunk, half), lambda b, l: (b, 0)),
                  pl.BlockSpec((nchunk, half), lambda b, l: (b, 0)),
                  pl.BlockSpec((None, 2, half), lambda b, l: (l[0], 0, 0)),
                  pl.BlockSpec((None, 2, half), lambda b, l: (l[0], 0, 0)),
                  pl.BlockSpec((None, 2 * half, CMP_HIDDEN), lambda b, l: (l[0], 0, 0)),
                  pl.BlockSpec((None, CMP_HIDDEN, HEAD_DIM), lambda b, l: (l[0], 0, 0)),
                  pl.BlockSpec((None, 2 * half, CMP_HIDDEN), lambda b, l: (l[0], 0, 0)),
                  pl.BlockSpec((None, CMP_HIDDEN, HEAD_DIM), lambda b, l: (l[0], 0, 0)),
                  pl.BlockSpec((nchunk, HEAD_DIM), lambda b, l: (0, 0)),
                  pl.BlockSpec((nchunk, HEAD_DIM), lambda b, l: (0, 0))],
        out_specs=[pl.BlockSpec((nchunk, HEAD_DIM), lambda b, l: (b, 0)),
                   pl.BlockSpec((nchunk, HEAD_DIM), lambda b, l: (b, 0))],
        out_shape=[jax.ShapeDtypeStruct((bsz * nchunk, HEAD_DIM), BF16)] * 2)
    nsq = NSA_HEADS * HEAD_DIM
    o_nsa = _call(
        _nsa_kernel, "nsa_attn", lidx, [z_b, z_b, z_b, z_a, z_a, k_cmp, v_cmp, z_d, overlap, block_of_key], grid=(bsz, nq),
        in_specs=[pl.BlockSpec((TQ, nsq), lambda b, i, l: (b * nq + i, 0)),
                  pl.BlockSpec((seq, LANE), lambda b, i, l: (b, 4)),
                  pl.BlockSpec((seq, LANE), lambda b, i, l: (b, 5)),
                  pl.BlockSpec((seq, LANE), lambda b, i, l: (b, 18)),
                  pl.BlockSpec((seq, LANE), lambda b, i, l: (b, 19)),
                  pl.BlockSpec((nchunk, HEAD_DIM), lambda b, i, l: (b, 0)),
                  pl.BlockSpec((nchunk, HEAD_DIM), lambda b, i, l: (b, 0)),
                  pl.BlockSpec((TQ, LANE), lambda b, i, l: (b * nq + i, ZD_NSA_G)),
                  pl.BlockSpec((N_CMP_PAD, LANE), lambda b, i, l: (0, 0)),
                  pl.BlockSpec((seq, LANE), lambda b, i, l: (0, 0))],
        out_specs=pl.BlockSpec((TQ, nsq), lambda b, i, l: (b * nq + i, 0)),
        out_shape=jax.ShapeDtypeStruct((t, nsq), BF16),
        scratch_shapes=[pltpu.VMEM((NSA_HEADS * TQ, HEAD_DIM), BF16),
                        pltpu.VMEM((seq // TK, NSA_HEADS * TQ, TK), F32),
                        pltpu.VMEM((NSA_HEADS * TQ, LANE), F32),
                        pltpu.VMEM((NSA_HEADS * TQ, 2 * HEAD_DIM), F32)])

    tm_kv = 1024
    k_d, v_d = _call(
        _dsa_kv_kernel, "dsa_kv", lidx, [z_d, kv_norm, kv_up, cos128, sin128], grid=(t // tm_kv,),
        in_specs=[pl.BlockSpec((tm_kv, DSA_KV_RANK), lambda i, l: (i, 0)),
                  pl.BlockSpec((None, 1, DSA_KV_RANK), lambda i, l: (l[0], 0, 0)),
                  pl.BlockSpec((None, DSA_KV_RANK, 2 * HEAD_DIM), lambda i, l: (l[0], 0, 0)),
                  pl.BlockSpec((tm_kv, LANE), lambda i, l: (i % (seq // tm_kv), 0)),
                  pl.BlockSpec((tm_kv, LANE), lambda i, l: (i % (seq // tm_kv), 0))],
        out_specs=[pl.BlockSpec((tm_kv, HEAD_DIM), lambda i, l: (i, 0)),
                   pl.BlockSpec((tm_kv, HEAD_DIM), lambda i, l: (i, 0))],
        out_shape=[jax.ShapeDtypeStruct((t, HEAD_DIM), BF16)] * 2)
    dq = DSA_HEADS * HEAD_DIM
    iqw = IDX_HEADS * IDX_DIM
    o_dsa = _call(
        _dsa_kernel, "dsa_attn", lidx, [z_b, k_d, v_d, z_c, z_c, z_c, z_d], grid=(bsz, nq),
        in_specs=[pl.BlockSpec((TQ, dq), lambda b, i, l: (b * nq + i, 1)),
                  pl.BlockSpec((seq, HEAD_DIM), lambda b, i, l: (b, 0)),
                  pl.BlockSpec((seq, HEAD_DIM), lambda b, i, l: (b, 0)),
                  pl.BlockSpec((TQ, iqw), lambda b, i, l: (b * nq + i, 0)),
                  pl.BlockSpec((seq, LANE), lambda b, i, l: (b, iqw // LANE)),
                  pl.BlockSpec((seq, LANE), lambda b, i, l: (b, iqw // LANE + 1)),
                  pl.BlockSpec((TQ, LANE), lambda b, i, l: (b * nq + i, ZD_IDX_W))],
        out_specs=pl.BlockSpec((TQ, dq), lambda b, i, l: (b * nq + i, 0)),
        out_shape=jax.ShapeDtypeStruct((t, dq), BF16),
        scratch_shapes=[pltpu.VMEM((seq // TK, TQ, TK), I32), pltpu.VMEM((seq // TK, TK, TQ), I32),
                        pltpu.VMEM((seq // TK, TK, TQ), I16), pltpu.VMEM((seq // TK, TK, TQ), I16),
                        pltpu.VMEM((IDX_HEADS, TQ, LANE), F32), pltpu.VMEM((2, TQ, LANE), I32),
                        pltpu.VMEM((DSA_HEADS * TQ, HEAD_DIM), BF16),
                        pltpu.VMEM((seq // TK, DSA_HEADS * TQ, TK), F32),
                        pltpu.VMEM((DSA_HEADS * TQ, LANE), F32),
                        pltpu.VMEM((DSA_HEADS * TQ, 2 * HEAD_DIM), F32)])

    tm, tn = 1024, 512
    ncol = d // tn
    gate0 = COL_GATE // tn
    assert gate0 * tn == COL_GATE
    mixed = _call(
        _merge_kernel, "merge", lidx, [hb, o_fox, o_nsa, o_dsa, w_proj, w_proj, w_proj, w_br_fox, w_br_nsa, w_br_dsa],
        grid=(t // tm, ncol),
        in_specs=[pl.BlockSpec((tm, d), lambda i, j, l: (i, 0)),
                  pl.BlockSpec((tm, fq), lambda i, j, l: (i, 0)),
                  pl.BlockSpec((tm, nsq), lambda i, j, l: (i, 0)),
                  pl.BlockSpec((tm, dq), lambda i, j, l: (i, 0)),
                  pl.BlockSpec((None, d, tn), lambda i, j, l: (l[0], 0, gate0 + j)),
                  pl.BlockSpec((None, d, tn), lambda i, j, l: (l[0], 0, gate0 + ncol + j)),
                  pl.BlockSpec((None, d, tn), lambda i, j, l: (l[0], 0, gate0 + 2 * ncol + j)),
                  pl.BlockSpec((None, fq, tn), lambda i, j, l: (l[0], 0, j)),
                  pl.BlockSpec((None, nsq, tn), lambda i, j, l: (l[0], 0, j)),
                  pl.BlockSpec((None, dq, tn), lambda i, j, l: (l[0], 0, j))],
        out_specs=pl.BlockSpec((tm, tn), lambda i, j, l: (i, j)),
        out_shape=jax.ShapeDtypeStruct((t, d), BF16))

    h, hb = _out_ln("out_ln", lidx, mixed, w_out, h, ln1_g, ln1_b)

    nff = D_FF // tn
    act = _call(
        _swiglu_kernel, "swiglu", lidx, [hb, w_ffn_in, w_ffn_in], grid=(nff, t // tm),
        in_specs=[pl.BlockSpec((tm, d), lambda j, i, l: (i, 0)),
                  pl.BlockSpec((None, d, tn), lambda j, i, l: (l[0], 0, j)),
                  pl.BlockSpec((None, d, tn), lambda j, i, l: (l[0], 0, nff + j))],
        out_specs=pl.BlockSpec((tm, tn), lambda j, i, l: (i, j)),
        out_shape=jax.ShapeDtypeStruct((t, D_FF), BF16),
        scratch_shapes=[pltpu.VMEM((2, d, tn), BF16)])
    h, hb = _ffn_ln("ffn_ln", lidx, act, w_ffn_out, h, ln2_g, ln2_b, tk=512)

    h, hb = _call(
        _ple_kernel, "ple", lidx, [hb, p, w_ple_in, w_ple_gate, h], grid=(t // tm, ncol),
        in_specs=[pl.BlockSpec((tm, d), lambda i, j, l: (i, 0)),
                  pl.BlockSpec((None, tm, PLE_DIM), lambda i, j, l: (l[0], i, 0)),
                  pl.BlockSpec((None, PLE_DIM, tn), lambda i, j, l: (l[0], 0, j)),
                  pl.BlockSpec((None, d, tn), lambda i, j, l: (l[0], 0, j)),
                  pl.BlockSpec((tm, tn), lambda i, j, l: (i, j))],
        out_specs=[pl.BlockSpec((tm, tn), lambda i, j, l: (i, j)),
                   pl.BlockSpec((tm, tn), lambda i, j, l: (i, j))],
        out_shape=[jax.ShapeDtypeStruct((t, d), F32), jax.ShapeDtypeStruct((t, d), BF16)])
    return h, hb


def kernel(x, p, w_in, fox_f_bias, nsa_pe_k, nsa_pe_v, nsa_cmp_k1, nsa_cmp_k2, nsa_cmp_v1, nsa_cmp_v2, dsa_kv_norm, dsa_kv_up, w_br_fox, w_br_nsa, w_br_dsa, w_out, ln1_g, ln1_b, w_ffn_in, w_ffn_out, ln2_g, ln2_b, w_ple_in, w_ple_gate):
    bsz, seq, d = x.shape
    depth = w_in.shape[0]
    t = bsz * seq
    assert d == D_MODEL and seq % 1024 == 0 and depth == DEPTH

    cat = functools.partial(jnp.concatenate, axis=-1)
    w_proj = _regroup(w_in)

    fox_bias = jnp.pad(fox_f_bias, ((0, 0), (0, LANE - FOX_HEADS))).reshape(depth, 1, LANE)
    half = CMP_STRIDE * HEAD_DIM
    weights = (
        w_proj, fox_bias,
        nsa_pe_k.reshape(depth, 2, half), nsa_pe_v.reshape(depth, 2, half),
        nsa_cmp_k1.astype(BF16), nsa_cmp_k2.astype(BF16), nsa_cmp_v1.astype(BF16), nsa_cmp_v2.astype(BF16),
        dsa_kv_norm.reshape(depth, 1, DSA_KV_RANK), dsa_kv_up.astype(BF16),
        w_br_fox.astype(BF16), w_br_nsa.astype(BF16), w_br_dsa.astype(BF16), w_out.astype(BF16),
        ln1_g.reshape(depth, 1, d), ln1_b.reshape(depth, 1, d),
        w_ffn_in, w_ffn_out.astype(BF16),
        ln2_g.reshape(depth, 1, d), ln2_b.reshape(depth, 1, d),
        p.reshape(depth, t, PLE_DIM), w_ple_in, w_ple_gate,
    )

    cos, sin = _rope_tables(seq, HEAD_DIM)
    cos128 = cat([cos, cos])
    sin128 = cat([-sin, sin])
    cos_i, sin_i = _rope_tables(seq, IDX_DIM)
    zi = jnp.zeros_like(sin_i)
    cos64 = cat([cos_i, cos_i, cos_i, cos_i])
    sin64a = cat([zi, sin_i, zi, sin_i])
    sin64b = cat([-sin_i, zi, -sin_i, zi])
    n_cmp = (seq - CMP_LEN) // CMP_STRIDE + 1
    c_end = jnp.minimum(jnp.arange(N_CMP_PAD) * CMP_STRIDE + CMP_LEN - 1, seq - 1)
    cmp_cos, cmp_sin = cos128[c_end], sin128[c_end]
    n_slc = seq // SLC_LEN
    c_start = jnp.arange(N_CMP_PAD) * CMP_STRIDE
    s_start = jnp.arange(LANE) * SLC_LEN
    overlap = jnp.maximum(jnp.minimum(c_start[:, None] + CMP_LEN - 1, s_start[None, :] + SLC_LEN - 1)
                          - jnp.maximum(c_start[:, None], s_start[None, :]) + 1, 0).astype(F32) / CMP_LEN
    overlap = jnp.where((jnp.arange(N_CMP_PAD)[:, None] < n_cmp) & (jnp.arange(LANE)[None, :] < n_slc), overlap, 0.0)
    overlap = overlap.T.astype(BF16)
    block_of_key = (jnp.arange(seq)[:, None] // SLC_LEN == jnp.arange(LANE)[None, :]).astype(BF16)
    consts = (cos128, sin128, cos64, sin64a, sin64b, cmp_cos, cmp_sin, overlap, block_of_key)

    h = x.reshape(t, d)
    hb = h.astype(BF16)
    for layer in range(depth):
        lidx = jnp.full((1,), layer, I32)
        h, hb = _layer(lidx, h, hb, consts, weights, bsz, seq)
    return h.reshape(bsz, seq, d)
```

```python
import functools

import jax
import jax.numpy as jnp
from jax import lax
from jax.experimental import pallas as pl
from jax.experimental.pallas import tpu as pltpu

F32, BF16, I32, I16 = jnp.float32, jnp.bfloat16, jnp.int32, jnp.int16

D_MODEL = 2048
DEPTH = 4
HEAD_DIM = 128
ROPE_THETA = 10000.0
NEG_INF = -1e30
FOX_HEADS = 6
NSA_HEADS = 4
CMP_LEN = 32
CMP_STRIDE = 16
CMP_HIDDEN = 256
SLC_LEN = 64
SLC_TOPN = 16
WIN = 512
SLC_FORCE = 1e4
DSA_HEADS = 6
DSA_KV_RANK = 256
IDX_HEADS = 16
IDX_DIM = 64
IDX_TOPK_MAX = 256
D_FF = ((8 * D_MODEL + 3 * 256 - 1) // (3 * 256)) * 256
PLE_DIM = 256
ALPHA = (2 * DEPTH) ** 0.25
SCALE = HEAD_DIM ** -0.5
LOG2E = 1.4426950408889634
SCALE_LOG2E = SCALE * LOG2E

LANE = 128
VMEM_LIMIT_BYTES = 48 * 1024 * 1024
VMEM_LIMIT_LARGE_BYTES = 56 * 1024 * 1024

_IN_SPLITS = (
    ('fox_q', FOX_HEADS * HEAD_DIM), ('fox_k', FOX_HEADS * HEAD_DIM), ('fox_v', FOX_HEADS * HEAD_DIM), ('fox_f', FOX_HEADS),
    ('nsa_q', NSA_HEADS * HEAD_DIM), ('nsa_kc', HEAD_DIM), ('nsa_vc', HEAD_DIM), ('nsa_ks', HEAD_DIM), ('nsa_vs', HEAD_DIM),
    ('nsa_kw', HEAD_DIM), ('nsa_vw', HEAD_DIM), ('nsa_g', 3 * NSA_HEADS),
    ('dsa_q', DSA_HEADS * HEAD_DIM), ('dsa_ckv', DSA_KV_RANK), ('idx_q', IDX_HEADS * IDX_DIM), ('idx_k', IDX_DIM), ('idx_w', IDX_HEADS),
    ('gate', 3 * D_MODEL),
)
_IN_OFFSETS = {}
_off = 0
for _name, _width in _IN_SPLITS:
    _IN_OFFSETS[_name] = (_off, _off + _width)
    _off += _width

MISC_FOX_F = 0
MISC_NSA_G = FOX_HEADS
MISC_IDX_W = FOX_HEADS + 3 * NSA_HEADS

TQ = 256
TK = 256
N_CMP_PAD = 128


def _dot(a, b):
    return jnp.dot(a, b, preferred_element_type=F32)


def _dot_nt(a, b):
    return lax.dot_general(a, b, (((1,), (1,)), ((), ())), preferred_element_type=F32)


def _split3(a):
    a1 = a.astype(BF16)
    r1 = a - a1.astype(F32)
    a2 = r1.astype(BF16)
    a3 = (r1 - a2.astype(F32)).astype(BF16)
    return a1, a2, a3


def _call(kernel, name, lidx, args, grid, in_specs, out_specs, out_shape, scratch_shapes=(),
          vmem_limit_bytes=VMEM_LIMIT_BYTES):
    return pl.pallas_call(
        kernel,
        grid_spec=pltpu.PrefetchScalarGridSpec(
            num_scalar_prefetch=1, grid=grid, in_specs=in_specs, out_specs=out_specs,
            scratch_shapes=list(scratch_shapes)),
        out_shape=out_shape,
        compiler_params=pltpu.CompilerParams(
            dimension_semantics=("arbitrary",) * len(grid), vmem_limit_bytes=vmem_limit_bytes),
        name=name,
    )(lidx, *args)


def _regroup_plan():
    plan = []

    def whole(name):
        a, b = _IN_OFFSETS[name]
        for g in range((b - a) // LANE):
            plan.append((a + g * LANE, 0, LANE))

    whole('dsa_ckv')
    lane0 = {}
    start = 0
    for name in ('fox_f', 'nsa_g', 'idx_w'):
        a, b = _IN_OFFSETS[name]
        lane0[name] = (a - start, start, start + b - a)
        start += b - a
    plan.append(lane0['fox_f'])
    whole('nsa_kc')
    whole('nsa_vc')
    plan.append(lane0['nsa_g'])
    plan.append(lane0['idx_w'])
    plan.append((0, 0, 0))
    for name in ('fox_q', 'fox_k', 'fox_v', 'nsa_vs', 'nsa_vw'):
        whole(name)
    for name in ('nsa_q', 'nsa_ks', 'nsa_kw', 'dsa_q'):
        whole(name)
    whole('gate')
    whole('idx_q')
    ik = _IN_OFFSETS['idx_k'][0]
    plan.append((ik, 0, IDX_DIM))
    plan.append((ik - IDX_DIM, IDX_DIM, LANE))
    return tuple(plan)


_REGROUP_PLAN = _regroup_plan()
COL_F32 = 0
N_F32 = DSA_KV_RANK + 6 * LANE
ZD_FOX_F, ZD_NSA_KC, ZD_NSA_VC, ZD_NSA_G, ZD_IDX_W = 2, 3, 4, 5, 6
COL_PLAIN = COL_F32 + N_F32
N_PLAIN = (3 * FOX_HEADS + 2) * HEAD_DIM
COL_ROPE128 = COL_PLAIN + N_PLAIN
N_ROPE128 = (NSA_HEADS + 2 + DSA_HEADS) * HEAD_DIM
COL_GATE = COL_ROPE128 + N_ROPE128
COL_ROPE64 = COL_GATE + 3 * D_MODEL
N_ROPE64 = IDX_HEADS * IDX_DIM + 2 * LANE
N_REGROUPED = COL_ROPE64 + N_ROPE64
assert N_REGROUPED == len(_REGROUP_PLAN) * LANE


def _regroup_kernel(src_ref, lo_ref, hi_ref, x_ref, o_ref):
    g = pl.program_id(0)
    col = lax.broadcasted_iota(I32, (LANE, x_ref.shape[2]), 0)
    keep = (col >= lo_ref[g]) & (col < hi_ref[g])
    for layer in range(x_ref.shape[1]):
        o_ref[layer] = jnp.where(keep, x_ref[:, layer, :], 0.0).T.astype(o_ref.dtype)


def _regroup(w_in):
    depth, d, n_in = w_in.shape
    w_t = jnp.transpose(w_in, (2, 0, 1))
    src = jnp.array([p[0] for p in _REGROUP_PLAN], I32)
    lo = jnp.array([p[1] for p in _REGROUP_PLAN], I32)
    hi = jnp.array([p[2] for p in _REGROUP_PLAN], I32)
    return pl.pallas_call(
        _regroup_kernel,
        grid_spec=pltpu.PrefetchScalarGridSpec(
            num_scalar_prefetch=3, grid=(len(_REGROUP_PLAN),),
            in_specs=[pl.BlockSpec((pl.Element(LANE), pl.Element(depth), pl.Element(d)),
                                   lambda g, src, lo, hi: (src[g], 0, 0))],
            out_specs=pl.BlockSpec((depth, d, LANE), lambda g, src, lo, hi: (0, 0, g))),
        out_shape=jax.ShapeDtypeStruct((depth, d, N_REGROUPED), BF16),
        compiler_params=pltpu.CompilerParams(dimension_semantics=("arbitrary",), vmem_limit_bytes=VMEM_LIMIT_BYTES),
        name="regroup_w_in",
    )(src, lo, hi, w_t)


def _proj_kernel(l_ref, x_ref, w_ref, *refs, shifts):
    o_ref = refs[-1]
    acc = _dot(x_ref[...], w_ref[...])
    if not shifts:
        o_ref[...] = acc.astype(o_ref.dtype)
        return
    c_ref = refs[0]
    s_refs = refs[1:-1]
    for g in range(acc.shape[1] // LANE):
        xg = acc[:, g * LANE:(g + 1) * LANE]
        out = xg * c_ref[...]
        for shift, s_ref in zip(shifts, s_refs):
            out = out + pltpu.roll(xg, shift, 1) * s_ref[...]
        o_ref[:, g * LANE:(g + 1) * LANE] = out.astype(o_ref.dtype)


def _proj(name, lidx, hb, w, col0, n, tm, tn, out_dtype, seq, rope=None):
    t, d = hb.shape
    jb = col0 // tn
    assert jb * tn == col0
    in_specs = [pl.BlockSpec((tm, d), lambda i, j, l: (i, 0)),
                pl.BlockSpec((None, d, tn), lambda i, j, l: (l[0], 0, jb + j))]
    args = [hb, w]
    shifts = ()
    if rope is not None:
        shifts, tables = rope[0], rope[1:]
        nrow = seq // tm
        for tab in tables:
            in_specs.append(pl.BlockSpec((tm, LANE), lambda i, j, l: (i % nrow, 0)))
            args.append(tab)
    return _call(functools.partial(_proj_kernel, shifts=shifts), name, lidx, args,
                 grid=(t // tm, n // tn), in_specs=in_specs,
                 out_specs=pl.BlockSpec((tm, tn), lambda i, j, l: (i, j)),
                 out_shape=jax.ShapeDtypeStruct((t, n), out_dtype))


def _cum_kernel(l_ref, f_ref, bias_ref, o_ref):
    chunk = 256
    row = lax.broadcasted_iota(I32, (chunk, chunk), 0)
    col = lax.broadcasted_iota(I32, (chunk, chunk), 1)
    tri = jnp.where(row >= col, 1.0, 0.0).astype(BF16)
    carry = jnp.zeros((1, LANE), F32)
    for c in range(f_ref.shape[0] // chunk):
        logf = jax.nn.log_sigmoid(f_ref[c * chunk:(c + 1) * chunk, :] + bias_ref[...])
        l1, l2, l3 = _split3(logf)
        cs = _dot(tri, l1) + _dot(tri, l2) + _dot(tri, l3) + carry
        o_ref[c * chunk:(c + 1) * chunk, :] = cs
        carry = cs[chunk - 1:chunk, :]


def _max_lanes(x):
    out = x[:, 0:LANE]
    for g in range(1, x.shape[1] // LANE):
        out = jnp.maximum(out, x[:, g * LANE:(g + 1) * LANE])
    return out


def _tile_lanes(x, n):
    return jnp.concatenate([x] * n, axis=1)


def _row_max_to_lanes(mx):
    return jnp.broadcast_to(jnp.max(mx, axis=-1, keepdims=True), mx.shape)


def _exp_pv(s, m_lanes, v):
    p = jnp.exp2(s - _tile_lanes(m_lanes, s.shape[1] // LANE)).astype(BF16)
    return _dot(p, jnp.concatenate([v, jnp.ones_like(v)], axis=1))


def _fox_kernel(l_ref, q_ref, k_ref, v_ref, cq_ref, ck_ref, o_ref, s_ref, mx_ref, cqb_ref, acc_ref):
    qi = pl.program_id(1)
    nh = FOX_HEADS
    nchunk = ck_ref.shape[0] // 8
    row = lax.broadcasted_iota(I32, (TQ, TK), 0)
    col = lax.broadcasted_iota(I32, (TQ, TK), 1)
    cq_all = cq_ref[...]
    for h in range(nh):
        cqb_ref[h] = jnp.broadcast_to(cq_all[:, h:h + 1] * LOG2E, (TQ, LANE))
    mx_ref[...] = jnp.full(mx_ref.shape, NEG_INF, F32)
    acc_ref[...] = jnp.zeros(acc_ref.shape, F32)

    def score_step(j, diagonal):
        start = pl.multiple_of(j * TK, TK)
        for h in range(nh):
            hs = slice(h * HEAD_DIM, (h + 1) * HEAD_DIM)
            s = (_dot_nt(q_ref[:, hs], k_ref[pl.ds(start, TK), hs]) * SCALE_LOG2E
                 + _tile_lanes(cqb_ref[h], TK // LANE) - ck_ref[pl.ds(h * nchunk + j, 1), :] * LOG2E)
            if diagonal:
                s = jnp.where(col <= row, s, NEG_INF)
            s_ref[h, j] = s
            mx_ref[h] = jnp.maximum(mx_ref[h], _max_lanes(s))

    def score_body(j, carry):
        score_step(j, False)
        return carry

    lax.fori_loop(0, qi, score_body, 0)
    score_step(qi, True)
    for h in range(nh):
        mx_ref[h] = _row_max_to_lanes(mx_ref[h])

    def pv_body(j, carry):
        start = pl.multiple_of(j * TK, TK)
        for h in range(nh):
            hs = slice(h * HEAD_DIM, (h + 1) * HEAD_DIM)
            acc_ref[h] += _exp_pv(s_ref[h, j], mx_ref[h], v_ref[pl.ds(start, TK), hs])
        return carry

    lax.fori_loop(0, qi + 1, pv_body, 0)
    for h in range(nh):
        a = acc_ref[h]
        o_ref[:, h * HEAD_DIM:(h + 1) * HEAD_DIM] = (a[:, :HEAD_DIM] / a[:, HEAD_DIM:]).astype(o_ref.dtype)


def _cmp_kernel(l_ref, rk_ref, rv_ref, pek_ref, pev_ref, wk1_ref, wk2_ref, wv1_ref, wv2_ref,
                c_ref, s_ref, ko_ref, vo_ref):
    half = CMP_STRIDE * HEAD_DIM

    def compress(r_ref, pe_ref, w1_ref, w2_ref):
        r = r_ref[...]
        lo = (r + pe_ref[0:1, :]).astype(BF16)
        hi = (r + pe_ref[1:2, :]).astype(BF16)
        a = _dot(lo, w1_ref[0:half, :])
        b = _dot(hi, w1_ref[half:2 * half, :])
        hid = a + pltpu.roll(b, N_CMP_PAD - 1, 0)
        return _dot(jax.nn.gelu(hid).astype(BF16), w2_ref[...])

    kc = compress(rk_ref, pek_ref, wk1_ref, wk2_ref)
    kc = kc * c_ref[...] + pltpu.roll(kc, HEAD_DIM // 2, 1) * s_ref[...]
    ko_ref[...] = kc.astype(ko_ref.dtype)
    vo_ref[...] = compress(rv_ref, pev_ref, wv1_ref, wv2_ref).astype(vo_ref.dtype)


def _masked_softmax(s, mask):
    s = jnp.where(mask, s, NEG_INF)
    m = jnp.max(s, axis=-1, keepdims=True)
    e = jnp.where(mask, jnp.exp(s - m), 0.0)
    den = jnp.sum(e, axis=-1, keepdims=True)
    return e, jnp.where(den > 0.0, den, 1.0)


def _nsa_kernel(l_ref, q_ref, ks_ref, kw_ref, vs_ref, vw_ref, kc_ref, vc_ref, g_ref, ovt_ref, et_ref,
                o_ref, q_sc, s_ref, mx_ref, acc_ref):
    qi = pl.program_id(1)
    nh = NSA_HEADS
    for h in range(nh):
        q_sc[h * TQ:(h + 1) * TQ, :] = q_ref[:, h * HEAD_DIM:(h + 1) * HEAD_DIM]
    q = q_sc[...]
    pos = qi * TQ + lax.broadcasted_iota(I32, (TQ, 1), 0)
    lane = lax.broadcasted_iota(I32, (TQ, LANE), 1)

    cvis = (lane * CMP_STRIDE + (CMP_LEN - 1)) <= pos
    s_c = (_dot_nt(q, kc_ref[...]) * SCALE).reshape(nh, TQ, N_CMP_PAD)
    e_c, den_c = _masked_softmax(s_c, cvis[None])
    p_c = e_c / den_c
    o_cmp = _dot(p_c.reshape(nh * TQ, N_CMP_PAD).astype(BF16), vc_ref[...])

    p_sum = p_c[0]
    for h in range(1, nh):
        p_sum = p_sum + p_c[h]
    p1, p2, p3 = _split3(p_sum)
    ovt = ovt_ref[...]
    n_slc = et_ref.shape[0] // SLC_LEN
    imp = (_dot_nt(ovt, p1) + _dot_nt(ovt, p2) + _dot_nt(ovt, p3))[0:n_slc, :]
    blk = lax.broadcasted_iota(I32, (n_slc, TQ), 0)
    pos_t = qi * TQ + lax.broadcasted_iota(I32, (n_slc, TQ), 1)
    cur = lax.shift_right_logical(pos_t, SLC_LEN.bit_length() - 1)
    forced = (blk == 0) | (blk == cur) | (blk == cur - 1)
    imp = jnp.where(forced, SLC_FORCE, jnp.where(blk * SLC_LEN <= pos_t, imp, -SLC_FORCE))
    rank = jnp.zeros((n_slc, TQ), F32)
    for k in range(n_slc):
        ik = imp[k:k + 1, :]
        ahead = (ik > imp) | ((ik == imp) & (blk > k))
        rank = rank + jnp.where(ahead, 1.0, 0.0)
    sel_t = jnp.where(rank < float(min(SLC_TOPN, n_slc)), 1.0, 0.0)
    sel_t = jnp.concatenate([sel_t, jnp.zeros((LANE - n_slc, TQ), F32)], axis=0)
    sel = sel_t.T.astype(BF16)

    mx_ref[...] = jnp.full(mx_ref.shape, NEG_INF, F32)
    acc_ref[...] = jnp.zeros(acc_ref.shape, F32)
    row = lax.broadcasted_iota(I32, (TQ, TK), 0)
    col = lax.broadcasted_iota(I32, (TQ, TK), 1)

    def score_step(j, diagonal):
        start = pl.multiple_of(j * TK, TK)
        vis = _dot_nt(sel, et_ref[pl.ds(start, TK), :]) > 0.5
        if diagonal:
            vis = vis & (col <= row)
        s = (_dot_nt(q_sc[...], ks_ref[pl.ds(start, TK), :]) * SCALE_LOG2E).reshape(nh, TQ, TK)
        s = jnp.where(vis[None], s, NEG_INF).reshape(nh * TQ, TK)
        s_ref[j] = s
        mx_ref[...] = jnp.maximum(mx_ref[...], _max_lanes(s))

    def score_body(j, carry):
        score_step(j, False)
        return carry

    lax.fori_loop(0, qi, score_body, 0)
    score_step(qi, True)
    mx_ref[...] = _row_max_to_lanes(mx_ref[...])

    def pv_body(j, carry):
        start = pl.multiple_of(j * TK, TK)
        acc_ref[...] += _exp_pv(s_ref[j], mx_ref[...], vs_ref[pl.ds(start, TK), :])
        return carry

    lax.fori_loop(0, qi + 1, pv_body, 0)
    a_slc = acc_ref[...]
    o_slc = a_slc[:, :HEAD_DIM] / a_slc[:, HEAD_DIM:]

    band = WIN + TQ
    start_w = pl.multiple_of(jnp.maximum(qi * TQ - WIN, 0), TQ)
    kwb = kw_ref[pl.ds(start_w, band), :]
    vwb = vw_ref[pl.ds(start_w, band), :]
    dist = pos - (start_w + lax.broadcasted_iota(I32, (TQ, band), 1))
    wvis = (dist >= 0) & (dist < WIN)
    s_w = (_dot_nt(q, kwb) * SCALE_LOG2E).reshape(nh, TQ, band)
    s_w = jnp.where(wvis[None], s_w, NEG_INF).reshape(nh * TQ, band)
    a_win = _exp_pv(s_w, _row_max_to_lanes(_max_lanes(s_w)), vwb)
    o_win = a_win[:, :HEAD_DIM] / a_win[:, HEAD_DIM:]

    gate = jax.nn.sigmoid(g_ref[...])
    for h in range(nh):
        rows = slice(h * TQ, (h + 1) * TQ)
        g0 = gate[:, MISC_NSA_G + 3 * h:MISC_NSA_G + 3 * h + 1]
        g1 = gate[:, MISC_NSA_G + 3 * h + 1:MISC_NSA_G + 3 * h + 2]
        g2 = gate[:, MISC_NSA_G + 3 * h + 2:MISC_NSA_G + 3 * h + 3]
        out = g0 * o_cmp[rows] + g1 * o_slc[rows] + g2 * o_win[rows]
        o_ref[:, h * HEAD_DIM:(h + 1) * HEAD_DIM] = out.astype(o_ref.dtype)


def _dsa_kv_kernel(l_ref, x_ref, g_ref, up_ref, c_ref, s_ref, ko_ref, vo_ref):
    x = x_ref[...]
    r = x * lax.rsqrt(jnp.mean(jnp.square(x), axis=-1, keepdims=True) + 1e-6) * g_ref[...]
    kv = _dot(r.astype(BF16), up_ref[...])
    k = kv[:, :HEAD_DIM]
    k = k * c_ref[...] + pltpu.roll(k, HEAD_DIM // 2, 1) * s_ref[...]
    ko_ref[...] = k.astype(ko_ref.dtype)
    vo_ref[...] = kv[:, HEAD_DIM:].astype(vo_ref.dtype)


def _dsa_kernel(l_ref, q_ref, kd_ref, vd_ref, iq_ref, ika_ref, ikb_ref, w_ref, o_ref,
                key_ref, keyt_ref, hi_ref, lo_ref, wb_ref, sel_ref, q_sc, s_ref, mx_ref, acc_ref):
    qi = pl.program_id(1)
    nh = DSA_HEADS
    seq = kd_ref.shape[0]
    k_top = float(min(IDX_TOPK_MAX, seq // 4))
    row = lax.broadcasted_iota(I32, (TQ, TK), 0)
    col = lax.broadcasted_iota(I32, (TQ, TK), 1)
    int_min = jnp.int32(-2 ** 31)
    idx_bits = (seq - 1).bit_length()

    w_all = w_ref[...] * (IDX_HEADS ** -0.5 * IDX_DIM ** -0.5)
    for h in range(IDX_HEADS):
        wb_ref[h] = jnp.broadcast_to(w_all[:, MISC_IDX_W + h:MISC_IDX_W + h + 1], (TQ, LANE))

    def score_step(j, diagonal):
        start = pl.multiple_of(j * TK, TK)
        ka = ika_ref[pl.ds(start, TK), :]
        kb = ikb_ref[pl.ds(start, TK), :]
        sc = jnp.zeros((TQ, TK), F32)
        for g in range(IDX_HEADS // 2):
            qpair = iq_ref[:, g * LANE:(g + 1) * LANE]
            wa = _tile_lanes(wb_ref[2 * g], TK // LANE)
            wb = _tile_lanes(wb_ref[2 * g + 1], TK // LANE)
            sc = sc + jnp.maximum(_dot_nt(qpair, ka), 0.0) * wa + jnp.maximum(_dot_nt(qpair, kb), 0.0) * wb
        if diagonal:
            sc = jnp.where(col <= row, sc, NEG_INF)
        bits = lax.bitcast_convert_type(sc, I32)
        key = jnp.where(bits < 0, bits ^ jnp.int32(0x7FFFFFFF), bits)
        key = jnp.where(sc == 0.0, 0, key)
        key_ref[j] = key
        key_t = key.T
        keyt_ref[j] = key_t
        hi_ref[j] = (key_t >> 16).astype(I16)
        lo_ref[j] = ((key_t & 0xFFFF) - 32768).astype(I16)

    def score_body(j, carry):
        score_step(j, False)
        return carry

    lax.fori_loop(0, qi, score_body, 0)
    score_step(qi, True)

    krow = lax.broadcasted_iota(I32, (TK, TQ), 0)

    def count(pred_fn):
        def body(j, acc):
            hit = jnp.where(pred_fn(keyt_ref[j], j), 1.0, 0.0)
            return acc + jnp.sum(hit.reshape(TK // 8, 8, TQ), axis=0)
        part = lax.fori_loop(0, qi + 1, body, jnp.zeros((8, TQ), F32))
        return jnp.sum(part, axis=0, keepdims=True)

    def count16(ref, cand, strict):
        cand16 = cand.astype(I16)

        def body(j, acc):
            x = ref[j]
            hit = jnp.where(x > cand16 if strict else x >= cand16, jnp.int16(1), jnp.int16(0))
            for g in range(TK // 16):
                acc = acc + hit[g * 16:(g + 1) * 16, :]
            return acc
        part = lax.fori_loop(0, qi + 1, body, jnp.zeros((16, TQ), I16))
        return jnp.sum(part.astype(F32), axis=0, keepdims=True)

    def bisect16(ref, need):
        zero = jnp.zeros((1, TQ), I32)
        v = jnp.where(count16(ref, zero, False) >= need, zero, zero - 32768)

        def bit(i, v):
            cand = v + (jnp.int32(1) << (14 - i))
            return jnp.where(count16(ref, cand, False) >= need, cand, v)
        return lax.fori_loop(0, 15, bit, v)

    hi_thr = bisect16(hi_ref, k_top)
    need_lo = k_top - count16(hi_ref, hi_thr, True)
    hi_thr16 = hi_thr.astype(I16)

    def mask_lo(j, carry):
        lo_ref[j] = jnp.where(hi_ref[j] == hi_thr16, lo_ref[j], jnp.int16(-32768))
        return carry

    lax.fori_loop(0, qi + 1, mask_lo, 0)
    lo_thr = bisect16(lo_ref, need_lo)
    thr = lax.shift_left(hi_thr, jnp.int32(16)) + (lo_thr + 32768)
    n_gt = count(lambda key, j: key > thr)
    n_ge = count(lambda key, j: key >= thr)

    def index_bisect():
        def index_bit(i, cut):
            cand = cut + (jnp.int32(1) << (idx_bits - 1 - i))
            n_before = count(lambda key, j: (key == thr) & (j * TK + krow < cand))
            return jnp.where(n_gt + n_before < k_top, cand, cut)
        return lax.fori_loop(0, idx_bits, index_bit, jnp.zeros((1, TQ), I32))

    def take_all_ties():
        return jnp.full((1, TQ), seq - 1, I32)

    cut = lax.cond(jnp.max(n_ge) > k_top, index_bisect, take_all_ties)
    sel_ref[0] = jnp.broadcast_to(thr, (LANE, TQ)).T
    sel_ref[1] = jnp.broadcast_to(cut, (LANE, TQ)).T

    for h in range(nh):
        q_sc[h * TQ:(h + 1) * TQ, :] = q_ref[:, h * HEAD_DIM:(h + 1) * HEAD_DIM]
    mx_ref[...] = jnp.full(mx_ref.shape, NEG_INF, F32)
    acc_ref[...] = jnp.zeros(acc_ref.shape, F32)

    def attn_score_step(j, diagonal):
        start = pl.multiple_of(j * TK, TK)
        key = key_ref[j]
        thr_t = _tile_lanes(sel_ref[0], TK // LANE)
        cut_t = _tile_lanes(sel_ref[1], TK // LANE)
        vis = (key > thr_t) | ((key == thr_t) & (j * TK + col <= cut_t))
        if diagonal:
            vis = vis & (col <= row)
        s = (_dot_nt(q_sc[...], kd_ref[pl.ds(start, TK), :]) * SCALE_LOG2E).reshape(nh, TQ, TK)
        s = jnp.where(vis[None], s, NEG_INF).reshape(nh * TQ, TK)
        s_ref[j] = s
        mx_ref[...] = jnp.maximum(mx_ref[...], _max_lanes(s))

    def attn_score_body(j, carry):
        attn_score_step(j, False)
        return carry

    lax.fori_loop(0, qi, attn_score_body, 0)
    attn_score_step(qi, True)
    mx_ref[...] = _row_max_to_lanes(mx_ref[...])

    def pv_body(j, carry):
        start = pl.multiple_of(j * TK, TK)
        acc_ref[...] += _exp_pv(s_ref[j], mx_ref[...], vd_ref[pl.ds(start, TK), :])
        return carry

    lax.fori_loop(0, qi + 1, pv_body, 0)
    a = acc_ref[...]
    out = a[:, :HEAD_DIM] / a[:, HEAD_DIM:]
    for h in range(nh):
        o_ref[:, h * HEAD_DIM:(h + 1) * HEAD_DIM] = out[h * TQ:(h + 1) * TQ].astype(o_ref.dtype)


def _merge_kernel(l_ref, hb_ref, of_ref, on_ref, od_ref, wg1_ref, wg2_ref, wg3_ref, wf_ref, wn_ref, wd_ref, o_ref):
    hb = hb_ref[...]
    mixed = (jax.nn.sigmoid(_dot(hb, wg1_ref[...])) * _dot(of_ref[...], wf_ref[...])
             + jax.nn.sigmoid(_dot(hb, wg2_ref[...])) * _dot(on_ref[...], wn_ref[...])
             + jax.nn.sigmoid(_dot(hb, wg3_ref[...])) * _dot(od_ref[...], wd_ref[...]))
    o_ref[...] = mixed.astype(o_ref.dtype)


LN_SUB_ROWS = 256


def _layer_norm_store(y, g_ref, b_ref, of_ref, ob_ref, rows):
    mu = jnp.mean(y, axis=-1, keepdims=True)
    yc = y - mu
    var = jnp.mean(jnp.square(yc), axis=-1, keepdims=True)
    out = yc * lax.rsqrt(var + 1e-5) * g_ref[...] + b_ref[...]
    of_ref[rows, :] = out
    ob_ref[rows, :] = out.astype(ob_ref.dtype)


def _out_ln_kernel(l_ref, x_ref, w_ref, h_ref, g_ref, b_ref, of_ref, ob_ref):
    w = w_ref[...]
    for r in range(x_ref.shape[0] // LN_SUB_ROWS):
        rows = slice(r * LN_SUB_ROWS, (r + 1) * LN_SUB_ROWS)
        y = ALPHA * h_ref[rows, :] + _dot(x_ref[rows, :], w)
        _layer_norm_store(y, g_ref, b_ref, of_ref, ob_ref, rows)


def _out_ln(name, lidx, x, w, h, g, b):
    t, kdim = x.shape
    d = h.shape[1]
    tm = 512
    return _call(_out_ln_kernel, name, lidx, [x, w, h, g, b], grid=(t // tm,),
                 in_specs=[pl.BlockSpec((tm, kdim), lambda i, l: (i, 0)),
                           pl.BlockSpec((None, kdim, d), lambda i, l: (l[0], 0, 0)),
                           pl.BlockSpec((tm, d), lambda i, l: (i, 0)),
                           pl.BlockSpec((None, 1, d), lambda i, l: (l[0], 0, 0)),
                           pl.BlockSpec((None, 1, d), lambda i, l: (l[0], 0, 0))],
                 out_specs=[pl.BlockSpec((tm, d), lambda i, l: (i, 0)),
                            pl.BlockSpec((tm, d), lambda i, l: (i, 0))],
                 out_shape=[jax.ShapeDtypeStruct((t, d), F32), jax.ShapeDtypeStruct((t, d), BF16)])


def _ffn_ple_kernel(l_ref, x_ref, w_ref, h_ref, g_ref, b_ref, p_ref, wpi_ref, wpg_ref, of_ref, ob_ref, *, nk):
    k = pl.program_id(1)

    @pl.when(k == 0)
    def _():
        of_ref[...] = ALPHA * h_ref[...] + _dot(x_ref[...], w_ref[...])

    @pl.when((k > 0) & (k < nk - 1))
    def _():
        of_ref[...] += _dot(x_ref[...], w_ref[...])

    @pl.when(k == nk - 1)
    def _():
        w = w_ref[...]
        for r in range(x_ref.shape[0] // LN_SUB_ROWS):
            rows = slice(r * LN_SUB_ROWS, (r + 1) * LN_SUB_ROWS)
            y = of_ref[rows, :] + _dot(x_ref[rows, :], w)
            mu = jnp.mean(y, axis=-1, keepdims=True)
            yc = y - mu
            var = jnp.mean(jnp.square(yc), axis=-1, keepdims=True)
            h2 = yc * lax.rsqrt(var + 1e-5) * g_ref[...] + b_ref[...]
            gate = jax.nn.sigmoid(_dot(h2.astype(BF16), wpg_ref[...]))
            out = h2 + _dot(p_ref[rows, :].astype(BF16), wpi_ref[...]) * gate
            of_ref[rows, :] = out
            ob_ref[rows, :] = out.astype(ob_ref.dtype)


def _ffn_ple(name, lidx, x, w, h, g, b, p, wpi, wpg, tk):
    t, kdim = x.shape
    d = h.shape[1]
    tm = 512
    nk = kdim // tk
    assert nk >= 2
    return _call(functools.partial(_ffn_ple_kernel, nk=nk), name, lidx, [x, w, h, g, b, p, wpi, wpg],
                 grid=(t // tm, nk),
                 in_specs=[pl.BlockSpec((tm, tk), lambda i, k, l: (i, k)),
                           pl.BlockSpec((None, tk, d), lambda i, k, l: (l[0], k, 0)),
                           pl.BlockSpec((tm, d), lambda i, k, l: (i, 0)),
                           pl.BlockSpec((None, 1, d), lambda i, k, l: (l[0], 0, 0)),
                           pl.BlockSpec((None, 1, d), lambda i, k, l: (l[0], 0, 0)),
                           pl.BlockSpec((None, tm, PLE_DIM), lambda i, k, l: (l[0], i, 0)),
                           pl.BlockSpec((None, PLE_DIM, d), lambda i, k, l: (l[0], 0, 0)),
                           pl.BlockSpec((None, d, d), lambda i, k, l: (l[0], 0, 0))],
                 out_specs=[pl.BlockSpec((tm, d), lambda i, k, l: (i, 0)),
                            pl.BlockSpec((tm, d), lambda i, k, l: (i, 0))],
                 out_shape=[jax.ShapeDtypeStruct((t, d), F32), jax.ShapeDtypeStruct((t, d), BF16)],
                 vmem_limit_bytes=VMEM_LIMIT_LARGE_BYTES)


def _swiglu_kernel(l_ref, x_ref, wa_ref, wb_ref, o_ref, wab_ref):
    @pl.when(pl.program_id(1) == 0)
    def _():
        wab_ref[0] = wa_ref[...].astype(BF16)
        wab_ref[1] = wb_ref[...].astype(BF16)

    x = x_ref[...]
    a = _dot(x, wab_ref[0])
    o_ref[...] = (jax.nn.silu(a) * _dot(x, wab_ref[1])).astype(o_ref.dtype)


def _rope_tables(n, dim):
    inv = 1.0 / (ROPE_THETA ** (jnp.arange(0, dim, 2, dtype=F32) / dim))
    ang = jnp.arange(n, dtype=F32)[:, None] * inv[None, :]
    return jnp.cos(ang), jnp.sin(ang)


def _layer(lidx, h, hb, consts, weights, bsz, seq):
    (cos128, sin128, cos64, sin64a, sin64b, cmp_cos, cmp_sin, overlap, block_of_key) = consts
    (w_proj, fox_bias, pe_k, pe_v, wk1, wk2, wv1, wv2, kv_norm, kv_up,
     w_br_fox, w_br_nsa, w_br_dsa, w_out, ln1_g, ln1_b, w_ffn_in, w_ffn_out, ln2_g, ln2_b, p, w_ple_in, w_ple_gate) = weights
    t = bsz * seq
    nq = seq // TQ
    d = D_MODEL

    z_a = _proj("proj_plain", lidx, hb, w_proj, COL_PLAIN, N_PLAIN, 2048, 512, BF16, seq)
    z_b = _proj("proj_rope128", lidx, hb, w_proj, COL_ROPE128, N_ROPE128, 2048, 512, BF16, seq,
                rope=((HEAD_DIM // 2,), cos128, sin128))
    z_c = _proj("proj_rope64", lidx, hb, w_proj, COL_ROPE64, N_ROPE64, 2048, 256, BF16, seq,
                rope=((IDX_DIM // 2, LANE - IDX_DIM // 2), cos64, sin64a, sin64b))
    z_d = _proj("proj_f32", lidx, hb, w_proj, COL_F32, N_F32, 1024, N_F32, F32, seq)

    cum = _call(_cum_kernel, "fox_cum", lidx, [z_d, fox_bias], grid=(bsz,),
                in_specs=[pl.BlockSpec((seq, LANE), lambda b, l: (b, ZD_FOX_F)),
                          pl.BlockSpec((None, 1, LANE), lambda b, l: (l[0], 0, 0))],
                out_specs=pl.BlockSpec((seq, LANE), lambda b, l: (b, 0)),
                out_shape=jax.ShapeDtypeStruct((t, LANE), F32))
    cum_rows = jnp.transpose(cum.reshape(bsz, seq, LANE)[:, :, :8], (0, 2, 1)).reshape(bsz, 8 * (seq // TK), TK)
    fq = FOX_HEADS * HEAD_DIM
    o_fox = _call(_fox_kernel, "fox_attn", lidx, [z_a, z_a, z_a, cum, cum_rows], grid=(bsz, nq),
                  in_specs=[pl.BlockSpec((TQ, fq), lambda b, i, l: (b * nq + i, 0)),
                            pl.BlockSpec((seq, fq), lambda b, i, l: (b, 1)),
                            pl.BlockSpec((seq, fq), lambda b, i, l: (b, 2)),
                            pl.BlockSpec((TQ, LANE), lambda b, i, l: (b * nq + i, 0)),
                            pl.BlockSpec((None, 8 * (seq // TK), TK), lambda b, i, l: (b, 0, 0))],
                  out_specs=pl.BlockSpec((TQ, fq), lambda b, i, l: (b * nq + i, 0)),
                  out_shape=jax.ShapeDtypeStruct((t, fq), BF16),
                  scratch_shapes=[pltpu.VMEM((FOX_HEADS, seq // TK, TQ, TK), F32),
                                  pltpu.VMEM((FOX_HEADS, TQ, LANE), F32), pltpu.VMEM((FOX_HEADS, TQ, LANE), F32),
                                  pltpu.VMEM((FOX_HEADS, TQ, 2 * HEAD_DIM), F32)])

    nchunk = seq // CMP_STRIDE
    half = CMP_STRIDE * HEAD_DIM
    r_k = z_d[:, ZD_NSA_KC * LANE:(ZD_NSA_KC + 1) * LANE].reshape(bsz * nchunk, half)
    r_v = z_d[:, ZD_NSA_VC * LANE:(ZD_NSA_VC + 1) * LANE].reshape(bsz * nchunk, half)
    k_cmp, v_cmp = _call(
        _cmp_kernel, "nsa_compress", lidx, [r_k, r_v, pe_k, pe_v, wk1, wk2, wv1, wv2, cmp_cos, cmp_sin], grid=(bsz,),
        in_specs=[pl.BlockSpec((nchunk, half), lambda b, l: (b, 0)),
                  pl.BlockSpec((nchunk, half), lambda b, l: (b, 0)),
                  pl.BlockSpec((None, 2, half), lambda b, l: (l[0], 0, 0)),
                  pl.BlockSpec((None, 2, half), lambda b, l: (l[0], 0, 0)),
                  pl.BlockSpec((None, 2 * half, CMP_HIDDEN), lambda b, l: (l[0], 0, 0)),
                  pl.BlockSpec((None, CMP_HIDDEN, HEAD_DIM), lambda b, l: (l[0], 0, 0)),
                  pl.BlockSpec((None, 2 * half, CMP_HIDDEN), lambda b, l: (l[0], 0, 0)),
                  pl.BlockSpec((None, CMP_HIDDEN, HEAD_DIM), lambda b, l: (l[0], 0, 0)),
                  pl.BlockSpec((nchunk, HEAD_DIM), lambda b, l: (0, 0)),
                  pl.BlockSpec((nchunk, HEAD_DIM), lambda b, l: (0, 0))],
        out_specs=[pl.BlockSpec((nchunk, HEAD_DIM), lambda b, l: (b, 0)),
                   pl.BlockSpec((nchunk, HEAD_DIM), lambda b, l: (b, 0))],
        out_shape=[jax.ShapeDtypeStruct((bsz * nchunk, HEAD_DIM), BF16)] * 2)
    nsq = NSA_HEADS * HEAD_DIM
    o_nsa = _call(
        _nsa_kernel, "nsa_attn", lidx, [z_b, z_b, z_b, z_a, z_a, k_cmp, v_cmp, z_d, overlap, block_of_key], grid=(bsz, nq),
        in_specs=[pl.BlockSpec((TQ, nsq), lambda b, i, l: (b * nq + i, 0)),
                  pl.BlockSpec((seq, LANE), lambda b, i, l: (b, 4)),
                  pl.BlockSpec((seq, LANE), lambda b, i, l: (b, 5)),
                  pl.BlockSpec((seq, LANE), lambda b, i, l: (b, 18)),
                  pl.BlockSpec((seq, LANE), lambda b, i, l: (b, 19)),
                  pl.BlockSpec((nchunk, HEAD_DIM), lambda b, i, l: (b, 0)),
                  pl.BlockSpec((nchunk, HEAD_DIM), lambda b, i, l: (b, 0)),
                  pl.BlockSpec((TQ, LANE), lambda b, i, l: (b * nq + i, ZD_NSA_G)),
                  pl.BlockSpec((N_CMP_PAD, LANE), lambda b, i, l: (0, 0)),
                  pl.BlockSpec((seq, LANE), lambda b, i, l: (0, 0))],
        out_specs=pl.BlockSpec((TQ, nsq), lambda b, i, l: (b * nq + i, 0)),
        out_shape=jax.ShapeDtypeStruct((t, nsq), BF16),
        scratch_shapes=[pltpu.VMEM((NSA_HEADS * TQ, HEAD_DIM), BF16),
                        pltpu.VMEM((seq // TK, NSA_HEADS * TQ, TK), F32),
                        pltpu.VMEM((NSA_HEADS * TQ, LANE), F32),
                        pltpu.VMEM((NSA_HEADS * TQ, 2 * HEAD_DIM), F32)])

    tm_kv = 1024
    k_d, v_d = _call(
        _dsa_kv_kernel, "dsa_kv", lidx, [z_d, kv_norm, kv_up, cos128, sin128], grid=(t // tm_kv,),
        in_specs=[pl.BlockSpec((tm_kv, DSA_KV_RANK), lambda i, l: (i, 0)),
                  pl.BlockSpec((None, 1, DSA_KV_RANK), lambda i, l: (l[0], 0, 0)),
                  pl.BlockSpec((None, DSA_KV_RANK, 2 * HEAD_DIM), lambda i, l: (l[0], 0, 0)),
                  pl.BlockSpec((tm_kv, LANE), lambda i, l: (i % (seq // tm_kv), 0)),
                  pl.BlockSpec((tm_kv, LANE), lambda i, l: (i % (seq // tm_kv), 0))],
        out_specs=[pl.BlockSpec((tm_kv, HEAD_DIM), lambda i, l: (i, 0)),
                   pl.BlockSpec((tm_kv, HEAD_DIM), lambda i, l: (i, 0))],
        out_shape=[jax.ShapeDtypeStruct((t, HEAD_DIM), BF16)] * 2)
    dq = DSA_HEADS * HEAD_DIM
    iqw = IDX_HEADS * IDX_DIM
    o_dsa = _call(
        _dsa_kernel, "dsa_attn", lidx, [z_b, k_d, v_d, z_c, z_c, z_c, z_d], grid=(bsz, nq),
        in_specs=[pl.BlockSpec((TQ, dq), lambda b, i, l: (b * nq + i, 1)),
                  pl.BlockSpec((seq, HEAD_DIM), lambda b, i, l: (b, 0)),
                  pl.BlockSpec((seq, HEAD_DIM), lambda b, i, l: (b, 0)),
                  pl.BlockSpec((TQ, iqw), lambda b, i, l: (b * nq + i, 0)),
                  pl.BlockSpec((seq, LANE), lambda b, i, l: (b, iqw // LANE)),
                  pl.BlockSpec((seq, LANE), lambda b, i, l: (b, iqw // LANE + 1)),
                  pl.BlockSpec((TQ, LANE), lambda b, i, l: (b * nq + i, ZD_IDX_W))],
        out_specs=pl.BlockSpec((TQ, dq), lambda b, i, l: (b * nq + i, 0)),
        out_shape=jax.ShapeDtypeStruct((t, dq), BF16),
        scratch_shapes=[pltpu.VMEM((seq // TK, TQ, TK), I32), pltpu.VMEM((seq // TK, TK, TQ), I32),
                        pltpu.VMEM((seq // TK, TK, TQ), I16), pltpu.VMEM((seq // TK, TK, TQ), I16),
                        pltpu.VMEM((IDX_HEADS, TQ, LANE), F32), pltpu.VMEM((2, TQ, LANE), I32),
                        pltpu.VMEM((DSA_HEADS * TQ, HEAD_DIM), BF16),
                        pltpu.VMEM((seq // TK, DSA_HEADS * TQ, TK), F32),
                        pltpu.VMEM((DSA_HEADS * TQ, LANE), F32),
                        pltpu.VMEM((DSA_HEADS * TQ, 2 * HEAD_DIM), F32)])

    tm, tn = 1024, 512
    ncol = d // tn
    gate0 = COL_GATE // tn
    assert gate0 * tn == COL_GATE
    mixed = _call(
        _merge_kernel, "merge", lidx, [hb, o_fox, o_nsa, o_dsa, w_proj, w_proj, w_proj, w_br_fox, w_br_nsa, w_br_dsa],
        grid=(t // tm, ncol),
        in_specs=[pl.BlockSpec((tm, d), lambda i, j, l: (i, 0)),
                  pl.BlockSpec((tm, fq), lambda i, j, l: (i, 0)),
                  pl.BlockSpec((tm, nsq), lambda i, j, l: (i, 0)),
                  pl.BlockSpec((tm, dq), lambda i, j, l: (i, 0)),
                  pl.BlockSpec((None, d, tn), lambda i, j, l: (l[0], 0, gate0 + j)),
                  pl.BlockSpec((None, d, tn), lambda i, j, l: (l[0], 0, gate0 + ncol + j)),
                  pl.BlockSpec((None, d, tn), lambda i, j, l: (l[0], 0, gate0 + 2 * ncol + j)),
                  pl.BlockSpec((None, fq, tn), lambda i, j, l: (l[0], 0, j)),
                  pl.BlockSpec((None, nsq, tn), lambda i, j, l: (l[0], 0, j)),
                  pl.BlockSpec((None, dq, tn), lambda i, j, l: (l[0], 0, j))],
        out_specs=pl.BlockSpec((tm, tn), lambda i, j, l: (i, j)),
        out_shape=jax.ShapeDtypeStruct((t, d), BF16))

    h, hb = _out_ln("out_ln", lidx, mixed, w_out, h, ln1_g, ln1_b)

    nff = D_FF // tn
    act = _call(
        _swiglu_kernel, "swiglu", lidx, [hb, w_ffn_in, w_ffn_in], grid=(nff, t // tm),
        in_specs=[pl.BlockSpec((tm, d), lambda j, i, l: (i, 0)),
                  pl.BlockSpec((None, d, tn), lambda j, i, l: (l[0], 0, j)),
                  pl.BlockSpec((None, d, tn), lambda j, i, l: (l[0], 0, nff + j))],
        out_specs=pl.BlockSpec((tm, tn), lambda j, i, l: (i, j)),
        out_shape=jax.ShapeDtypeStruct((t, D_FF), BF16),
        scratch_shapes=[pltpu.VMEM((2, d, tn), BF16)])
    h, hb = _ffn_ple("ffn_ln_ple", lidx, act, w_ffn_out, h, ln2_g, ln2_b, p, w_ple_in, w_ple_gate, tk=512)
    return h, hb


def kernel(x, p, w_in, fox_f_bias, nsa_pe_k, nsa_pe_v, nsa_cmp_k1, nsa_cmp_k2, nsa_cmp_v1, nsa_cmp_v2, dsa_kv_norm, dsa_kv_up, w_br_fox, w_br_nsa, w_br_dsa, w_out, ln1_g, ln1_b, w_ffn_in, w_ffn_out, ln2_g, ln2_b, w_ple_in, w_ple_gate):
    bsz, seq, d = x.shape
    depth = w_in.shape[0]
    t = bsz * seq
    assert d == D_MODEL and seq % 1024 == 0 and depth == DEPTH

    cat = functools.partial(jnp.concatenate, axis=-1)
    w_proj = _regroup(w_in)

    fox_bias = jnp.pad(fox_f_bias, ((0, 0), (0, LANE - FOX_HEADS))).reshape(depth, 1, LANE)
    half = CMP_STRIDE * HEAD_DIM
    weights = (
        w_proj, fox_bias,
        nsa_pe_k.reshape(depth, 2, half), nsa_pe_v.reshape(depth, 2, half),
        nsa_cmp_k1.astype(BF16), nsa_cmp_k2.astype(BF16), nsa_cmp_v1.astype(BF16), nsa_cmp_v2.astype(BF16),
        dsa_kv_norm.reshape(depth, 1, DSA_KV_RANK), dsa_kv_up.astype(BF16),
        w_br_fox.astype(BF16), w_br_nsa.astype(BF16), w_br_dsa.astype(BF16), w_out.astype(BF16),
        ln1_g.reshape(depth, 1, d), ln1_b.reshape(depth, 1, d),
        w_ffn_in, w_ffn_out.astype(BF16),
        ln2_g.reshape(depth, 1, d), ln2_b.reshape(depth, 1, d),
        p.reshape(depth, t, PLE_DIM), w_ple_in.astype(BF16), w_ple_gate.astype(BF16),
    )

    cos, sin = _rope_tables(seq, HEAD_DIM)
    cos128 = cat([cos, cos])
    sin128 = cat([-sin, sin])
    cos_i, sin_i = _rope_tables(seq, IDX_DIM)
    zi = jnp.zeros_like(sin_i)
    cos64 = cat([cos_i, cos_i, cos_i, cos_i])
    sin64a = cat([zi, sin_i, zi, sin_i])
    sin64b = cat([-sin_i, zi, -sin_i, zi])
    n_cmp = (seq - CMP_LEN) // CMP_STRIDE + 1
    c_end = jnp.minimum(jnp.arange(N_CMP_PAD) * CMP_STRIDE + CMP_LEN - 1, seq - 1)
    cmp_cos, cmp_sin = cos128[c_end], sin128[c_end]
    n_slc = seq // SLC_LEN
    c_start = jnp.arange(N_CMP_PAD) * CMP_STRIDE
    s_start = jnp.arange(LANE) * SLC_LEN
    overlap = jnp.maximum(jnp.minimum(c_start[:, None] + CMP_LEN - 1, s_start[None, :] + SLC_LEN - 1)
                          - jnp.maximum(c_start[:, None], s_start[None, :]) + 1, 0).astype(F32) / CMP_LEN
    overlap = jnp.where((jnp.arange(N_CMP_PAD)[:, None] < n_cmp) & (jnp.arange(LANE)[None, :] < n_slc), overlap, 0.0)
    overlap = overlap.T.astype(BF16)
    block_of_key = (jnp.arange(seq)[:, None] // SLC_LEN == jnp.arange(LANE)[None, :]).astype(BF16)
    consts = (cos128, sin128, cos64, sin64a, sin64b, cmp_cos, cmp_sin, overlap, block_of_key)

    h = x.reshape(t, d)
    hb = h.astype(BF16)
    for layer in range(depth):
        lidx = jnp.full((1,), layer, I32)
        h, hb = _layer(lidx, h, hb, consts, weights, bsz, seq)
    return h.reshape(bsz, seq, d)
```

```python
import functools

import jax
import jax.numpy as jnp
from jax import lax
from jax.experimental import pallas as pl
from jax.experimental.pallas import tpu as pltpu

F32, BF16, I32, I16 = jnp.float32, jnp.bfloat16, jnp.int32, jnp.int16

D_MODEL = 2048
DEPTH = 4
HEAD_DIM = 128
ROPE_THETA = 10000.0
NEG_INF = -1e30
FOX_HEADS = 6
NSA_HEADS = 4
CMP_LEN = 32
CMP_STRIDE = 16
CMP_HIDDEN = 256
SLC_LEN = 64
SLC_TOPN = 16
WIN = 512
SLC_FORCE = 1e4
DSA_HEADS = 6
DSA_KV_RANK = 256
IDX_HEADS = 16
IDX_DIM = 64
IDX_TOPK_MAX = 256
D_FF = ((8 * D_MODEL + 3 * 256 - 1) // (3 * 256)) * 256
PLE_DIM = 256
ALPHA = (2 * DEPTH) ** 0.25
SCALE = HEAD_DIM ** -0.5
LOG2E = 1.4426950408889634
SCALE_LOG2E = SCALE * LOG2E

LANE = 128
VMEM_LIMIT_BYTES = 48 * 1024 * 1024
VMEM_LIMIT_LARGE_BYTES = 56 * 1024 * 1024

_IN_SPLITS = (
    ('fox_q', FOX_HEADS * HEAD_DIM), ('fox_k', FOX_HEADS * HEAD_DIM), ('fox_v', FOX_HEADS * HEAD_DIM), ('fox_f', FOX_HEADS),
    ('nsa_q', NSA_HEADS * HEAD_DIM), ('nsa_kc', HEAD_DIM), ('nsa_vc', HEAD_DIM), ('nsa_ks', HEAD_DIM), ('nsa_vs', HEAD_DIM),
    ('nsa_kw', HEAD_DIM), ('nsa_vw', HEAD_DIM), ('nsa_g', 3 * NSA_HEADS),
    ('dsa_q', DSA_HEADS * HEAD_DIM), ('dsa_ckv', DSA_KV_RANK), ('idx_q', IDX_HEADS * IDX_DIM), ('idx_k', IDX_DIM), ('idx_w', IDX_HEADS),
    ('gate', 3 * D_MODEL),
)
_IN_OFFSETS = {}
_off = 0
for _name, _width in _IN_SPLITS:
    _IN_OFFSETS[_name] = (_off, _off + _width)
    _off += _width

MISC_FOX_F = 0
MISC_NSA_G = FOX_HEADS
MISC_IDX_W = FOX_HEADS + 3 * NSA_HEADS

TQ = 256
TK = 256
N_CMP_PAD = 128


def _dot(a, b):
    return jnp.dot(a, b, preferred_element_type=F32)


def _dot_nt(a, b):
    return lax.dot_general(a, b, (((1,), (1,)), ((), ())), preferred_element_type=F32)


def _fori_pairs(n, body, init):
    def two(jj, carry):
        return body(2 * jj + 1, body(2 * jj, carry))
    carry = lax.fori_loop(0, lax.shift_right_logical(n, jnp.int32(1)), two, init)
    return lax.cond((n & 1) == 1, lambda c: body(n - 1, c), lambda c: c, carry)


def _split3(a):
    a1 = a.astype(BF16)
    r1 = a - a1.astype(F32)
    a2 = r1.astype(BF16)
    a3 = (r1 - a2.astype(F32)).astype(BF16)
    return a1, a2, a3


def _call(kernel, name, lidx, args, grid, in_specs, out_specs, out_shape, scratch_shapes=(),
          vmem_limit_bytes=VMEM_LIMIT_BYTES):
    return pl.pallas_call(
        kernel,
        grid_spec=pltpu.PrefetchScalarGridSpec(
            num_scalar_prefetch=1, grid=grid, in_specs=in_specs, out_specs=out_specs,
            scratch_shapes=list(scratch_shapes)),
        out_shape=out_shape,
        compiler_params=pltpu.CompilerParams(
            dimension_semantics=("arbitrary",) * len(grid), vmem_limit_bytes=vmem_limit_bytes),
        name=name,
    )(lidx, *args)


def _regroup_plan():
    plan = []

    def whole(name):
        a, b = _IN_OFFSETS[name]
        for g in range((b - a) // LANE):
            plan.append((a + g * LANE, 0, LANE))

    whole('dsa_ckv')
    lane0 = {}
    start = 0
    for name in ('fox_f', 'nsa_g', 'idx_w'):
        a, b = _IN_OFFSETS[name]
        lane0[name] = (a - start, start, start + b - a)
        start += b - a
    plan.append(lane0['fox_f'])
    whole('nsa_kc')
    whole('nsa_vc')
    plan.append(lane0['nsa_g'])
    plan.append(lane0['idx_w'])
    plan.append((0, 0, 0))
    for name in ('fox_q', 'fox_k', 'fox_v', 'nsa_vs', 'nsa_vw'):
        whole(name)
    for name in ('nsa_q', 'nsa_ks', 'nsa_kw', 'dsa_q'):
        whole(name)
    whole('gate')
    whole('idx_q')
    ik = _IN_OFFSETS['idx_k'][0]
    plan.append((ik, 0, IDX_DIM))
    plan.append((ik - IDX_DIM, IDX_DIM, LANE))
    return tuple(plan)


_REGROUP_PLAN = _regroup_plan()
COL_F32 = 0
N_F32 = DSA_KV_RANK + 6 * LANE
ZD_FOX_F, ZD_NSA_KC, ZD_NSA_VC, ZD_NSA_G, ZD_IDX_W = 2, 3, 4, 5, 6
COL_PLAIN = COL_F32 + N_F32
N_PLAIN = (3 * FOX_HEADS + 2) * HEAD_DIM
COL_ROPE128 = COL_PLAIN + N_PLAIN
N_ROPE128 = (NSA_HEADS + 2 + DSA_HEADS) * HEAD_DIM
COL_GATE = COL_ROPE128 + N_ROPE128
COL_ROPE64 = COL_GATE + 3 * D_MODEL
N_ROPE64 = IDX_HEADS * IDX_DIM + 2 * LANE
N_REGROUPED = COL_ROPE64 + N_ROPE64
assert N_REGROUPED == len(_REGROUP_PLAN) * LANE


def _regroup_kernel(src_ref, lo_ref, hi_ref, x_ref, o_ref):
    g = pl.program_id(0)
    col = lax.broadcasted_iota(I32, (LANE, x_ref.shape[2]), 0)
    keep = (col >= lo_ref[g]) & (col < hi_ref[g])
    for layer in range(x_ref.shape[1]):
        o_ref[layer] = jnp.where(keep, x_ref[:, layer, :], 0.0).T.astype(o_ref.dtype)


def _regroup(w_in):
    depth, d, n_in = w_in.shape
    w_t = jnp.transpose(w_in, (2, 0, 1))
    src = jnp.array([p[0] for p in _REGROUP_PLAN], I32)
    lo = jnp.array([p[1] for p in _REGROUP_PLAN], I32)
    hi = jnp.array([p[2] for p in _REGROUP_PLAN], I32)
    return pl.pallas_call(
        _regroup_kernel,
        grid_spec=pltpu.PrefetchScalarGridSpec(
            num_scalar_prefetch=3, grid=(len(_REGROUP_PLAN),),
            in_specs=[pl.BlockSpec((pl.Element(LANE), pl.Element(depth), pl.Element(d)),
                                   lambda g, src, lo, hi: (src[g], 0, 0))],
            out_specs=pl.BlockSpec((depth, d, LANE), lambda g, src, lo, hi: (0, 0, g))),
        out_shape=jax.ShapeDtypeStruct((depth, d, N_REGROUPED), BF16),
        compiler_params=pltpu.CompilerParams(dimension_semantics=("arbitrary",), vmem_limit_bytes=VMEM_LIMIT_BYTES),
        name="regroup_w_in",
    )(src, lo, hi, w_t)


def _proj_kernel(l_ref, x_ref, w_ref, *refs, shifts):
    o_ref = refs[-1]
    acc = _dot(x_ref[...], w_ref[...])
    if not shifts:
        o_ref[...] = acc.astype(o_ref.dtype)
        return
    c_ref = refs[0]
    s_refs = refs[1:-1]
    for g in range(acc.shape[1] // LANE):
        xg = acc[:, g * LANE:(g + 1) * LANE]
        out = xg * c_ref[...]
        for shift, s_ref in zip(shifts, s_refs):
            out = out + pltpu.roll(xg, shift, 1) * s_ref[...]
        o_ref[:, g * LANE:(g + 1) * LANE] = out.astype(o_ref.dtype)


def _proj(name, lidx, hb, w, col0, n, tm, tn, out_dtype, seq, rope=None):
    t, d = hb.shape
    jb = col0 // tn
    assert jb * tn == col0
    in_specs = [pl.BlockSpec((tm, d), lambda i, j, l: (i, 0)),
                pl.BlockSpec((None, d, tn), lambda i, j, l: (l[0], 0, jb + j))]
    args = [hb, w]
    shifts = ()
    if rope is not None:
        shifts, tables = rope[0], rope[1:]
        nrow = seq // tm
        for tab in tables:
            in_specs.append(pl.BlockSpec((tm, LANE), lambda i, j, l: (i % nrow, 0)))
            args.append(tab)
    return _call(functools.partial(_proj_kernel, shifts=shifts), name, lidx, args,
                 grid=(t // tm, n // tn), in_specs=in_specs,
                 out_specs=pl.BlockSpec((tm, tn), lambda i, j, l: (i, j)),
                 out_shape=jax.ShapeDtypeStruct((t, n), out_dtype))


def _cum_kernel(l_ref, f_ref, bias_ref, o_ref):
    chunk = 256
    row = lax.broadcasted_iota(I32, (chunk, chunk), 0)
    col = lax.broadcasted_iota(I32, (chunk, chunk), 1)
    tri = jnp.where(row >= col, 1.0, 0.0).astype(BF16)
    carry = jnp.zeros((1, LANE), F32)
    for c in range(f_ref.shape[0] // chunk):
        logf = jax.nn.log_sigmoid(f_ref[c * chunk:(c + 1) * chunk, :] + bias_ref[...])
        l1, l2, l3 = _split3(logf)
        cs = _dot(tri, l1) + _dot(tri, l2) + _dot(tri, l3) + carry
        o_ref[c * chunk:(c + 1) * chunk, :] = cs
        carry = cs[chunk - 1:chunk, :]


def _max_lanes(x):
    out = x[:, 0:LANE]
    for g in range(1, x.shape[1] // LANE):
        out = jnp.maximum(out, x[:, g * LANE:(g + 1) * LANE])
    return out


def _tile_lanes(x, n):
    return jnp.concatenate([x] * n, axis=1)


def _row_max_to_lanes(mx):
    return jnp.broadcast_to(jnp.max(mx, axis=-1, keepdims=True), mx.shape)


def _exp_pv(s, m_lanes, v):
    p = jnp.exp2(s - _tile_lanes(m_lanes, s.shape[1] // LANE)).astype(BF16)
    return _dot(p, jnp.concatenate([v, jnp.ones_like(v)], axis=1))


def _fox_kernel(l_ref, q_ref, k_ref, v_ref, cq_ref, ck_ref, o_ref, s_ref, mx_ref, cqb_ref, acc_ref):
    qi = pl.program_id(1)
    nh = FOX_HEADS
    nchunk = ck_ref.shape[0] // 8
    row = lax.broadcasted_iota(I32, (TQ, TK), 0)
    col = lax.broadcasted_iota(I32, (TQ, TK), 1)
    cq_all = cq_ref[...]
    for h in range(nh):
        cqb_ref[h] = jnp.broadcast_to(cq_all[:, h:h + 1] * LOG2E, (TQ, LANE))
    mx_ref[...] = jnp.full(mx_ref.shape, NEG_INF, F32)
    acc_ref[...] = jnp.zeros(acc_ref.shape, F32)

    def score_step(j, diagonal):
        start = pl.multiple_of(j * TK, TK)
        for h in range(nh):
            hs = slice(h * HEAD_DIM, (h + 1) * HEAD_DIM)
            s = (_dot_nt(q_ref[:, hs], k_ref[pl.ds(start, TK), hs]) * SCALE_LOG2E
                 + _tile_lanes(cqb_ref[h], TK // LANE) - ck_ref[pl.ds(h * nchunk + j, 1), :] * LOG2E)
            if diagonal:
                s = jnp.where(col <= row, s, NEG_INF)
            s_ref[h, j] = s
            mx_ref[h] = jnp.maximum(mx_ref[h], _max_lanes(s))

    def score_body(j, carry):
        score_step(j, False)
        return carry

    _fori_pairs(qi, score_body, 0)
    score_step(qi, True)
    for h in range(nh):
        mx_ref[h] = _row_max_to_lanes(mx_ref[h])

    def pv_body(j, carry):
        start = pl.multiple_of(j * TK, TK)
        for h in range(nh):
            hs = slice(h * HEAD_DIM, (h + 1) * HEAD_DIM)
            acc_ref[h] += _exp_pv(s_ref[h, j], mx_ref[h], v_ref[pl.ds(start, TK), hs])
        return carry

    _fori_pairs(qi + 1, pv_body, 0)
    for h in range(nh):
        a = acc_ref[h]
        o_ref[:, h * HEAD_DIM:(h + 1) * HEAD_DIM] = (a[:, :HEAD_DIM] / a[:, HEAD_DIM:]).astype(o_ref.dtype)


def _cmp_kernel(l_ref, rk_ref, rv_ref, pek_ref, pev_ref, wk1_ref, wk2_ref, wv1_ref, wv2_ref,
                c_ref, s_ref, ko_ref, vo_ref):
    half = CMP_STRIDE * HEAD_DIM

    def compress(r_ref, pe_ref, w1_ref, w2_ref):
        r = r_ref[...]
        lo = (r + pe_ref[0:1, :]).astype(BF16)
        hi = (r + pe_ref[1:2, :]).astype(BF16)
        a = _dot(lo, w1_ref[0:half, :])
        b = _dot(hi, w1_ref[half:2 * half, :])
        hid = a + pltpu.roll(b, N_CMP_PAD - 1, 0)
        return _dot(jax.nn.gelu(hid).astype(BF16), w2_ref[...])

    kc = compress(rk_ref, pek_ref, wk1_ref, wk2_ref)
    kc = kc * c_ref[...] + pltpu.roll(kc, HEAD_DIM // 2, 1) * s_ref[...]
    ko_ref[...] = kc.astype(ko_ref.dtype)
    vo_ref[...] = compress(rv_ref, pev_ref, wv1_ref, wv2_ref).astype(vo_ref.dtype)


def _masked_softmax(s, mask):
    s = jnp.where(mask, s, NEG_INF)
    m = jnp.max(s, axis=-1, keepdims=True)
    e = jnp.where(mask, jnp.exp(s - m), 0.0)
    den = jnp.sum(e, axis=-1, keepdims=True)
    return e, jnp.where(den > 0.0, den, 1.0)


def _nsa_kernel(l_ref, q_ref, ks_ref, kw_ref, vs_ref, vw_ref, kc_ref, vc_ref, g_ref, ovt_ref, et_ref,
                o_ref, q_sc, s_ref, mx_ref, acc_ref):
    qi = pl.program_id(1)
    nh = NSA_HEADS
    for h in range(nh):
        q_sc[h * TQ:(h + 1) * TQ, :] = q_ref[:, h * HEAD_DIM:(h + 1) * HEAD_DIM]
    q = q_sc[...]
    pos = qi * TQ + lax.broadcasted_iota(I32, (TQ, 1), 0)
    lane = lax.broadcasted_iota(I32, (TQ, LANE), 1)

    cvis = (lane * CMP_STRIDE + (CMP_LEN - 1)) <= pos
    s_c = (_dot_nt(q, kc_ref[...]) * SCALE).reshape(nh, TQ, N_CMP_PAD)
    e_c, den_c = _masked_softmax(s_c, cvis[None])
    p_c = e_c / den_c
    o_cmp = _dot(p_c.reshape(nh * TQ, N_CMP_PAD).astype(BF16), vc_ref[...])

    p_sum = p_c[0]
    for h in range(1, nh):
        p_sum = p_sum + p_c[h]
    p1, p2, p3 = _split3(p_sum)
    ovt = ovt_ref[...]
    n_slc = et_ref.shape[0] // SLC_LEN
    imp = (_dot_nt(ovt, p1) + _dot_nt(ovt, p2) + _dot_nt(ovt, p3))[0:n_slc, :]
    blk = lax.broadcasted_iota(I32, (n_slc, TQ), 0)
    pos_t = qi * TQ + lax.broadcasted_iota(I32, (n_slc, TQ), 1)
    cur = lax.shift_right_logical(pos_t, SLC_LEN.bit_length() - 1)
    forced = (blk == 0) | (blk == cur) | (blk == cur - 1)
    imp = jnp.where(forced, SLC_FORCE, jnp.where(blk * SLC_LEN <= pos_t, imp, -SLC_FORCE))
    rank = jnp.zeros((n_slc, TQ), F32)
    for k in range(n_slc):
        ik = imp[k:k + 1, :]
        ahead = (ik > imp) | ((ik == imp) & (blk > k))
        rank = rank + jnp.where(ahead, 1.0, 0.0)
    sel_t = jnp.where(rank < float(min(SLC_TOPN, n_slc)), 1.0, 0.0)
    sel_t = jnp.concatenate([sel_t, jnp.zeros((LANE - n_slc, TQ), F32)], axis=0)
    sel = sel_t.T.astype(BF16)

    mx_ref[...] = jnp.full(mx_ref.shape, NEG_INF, F32)
    acc_ref[...] = jnp.zeros(acc_ref.shape, F32)
    row = lax.broadcasted_iota(I32, (TQ, TK), 0)
    col = lax.broadcasted_iota(I32, (TQ, TK), 1)

    def score_step(j, diagonal):
        start = pl.multiple_of(j * TK, TK)
        vis = _dot_nt(sel, et_ref[pl.ds(start, TK), :]) > 0.5
        if diagonal:
            vis = vis & (col <= row)
        s = (_dot_nt(q_sc[...], ks_ref[pl.ds(start, TK), :]) * SCALE_LOG2E).reshape(nh, TQ, TK)
        s = jnp.where(vis[None], s, NEG_INF).reshape(nh * TQ, TK)
        s_ref[j] = s
        mx_ref[...] = jnp.maximum(mx_ref[...], _max_lanes(s))

    def score_body(j, carry):
        score_step(j, False)
        return carry

    _fori_pairs(qi, score_body, 0)
    score_step(qi, True)
    mx_ref[...] = _row_max_to_lanes(mx_ref[...])

    def pv_body(j, carry):
        start = pl.multiple_of(j * TK, TK)
        acc_ref[...] += _exp_pv(s_ref[j], mx_ref[...], vs_ref[pl.ds(start, TK), :])
        return carry

    _fori_pairs(qi + 1, pv_body, 0)
    a_slc = acc_ref[...]
    o_slc = a_slc[:, :HEAD_DIM] / a_slc[:, HEAD_DIM:]

    band = WIN + TQ
    start_w = pl.multiple_of(jnp.maximum(qi * TQ - WIN, 0), TQ)
    kwb = kw_ref[pl.ds(start_w, band), :]
    vwb = vw_ref[pl.ds(start_w, band), :]
    dist = pos - (start_w + lax.broadcasted_iota(I32, (TQ, band), 1))
    wvis = (dist >= 0) & (dist < WIN)
    s_w = (_dot_nt(q, kwb) * SCALE_LOG2E).reshape(nh, TQ, band)
    s_w = jnp.where(wvis[None], s_w, NEG_INF).reshape(nh * TQ, band)
    a_win = _exp_pv(s_w, _row_max_to_lanes(_max_lanes(s_w)), vwb)
    o_win = a_win[:, :HEAD_DIM] / a_win[:, HEAD_DIM:]

    gate = jax.nn.sigmoid(g_ref[...])
    for h in range(nh):
        rows = slice(h * TQ, (h + 1) * TQ)
        g0 = gate[:, MISC_NSA_G + 3 * h:MISC_NSA_G + 3 * h + 1]
        g1 = gate[:, MISC_NSA_G + 3 * h + 1:MISC_NSA_G + 3 * h + 2]
        g2 = gate[:, MISC_NSA_G + 3 * h + 2:MISC_NSA_G + 3 * h + 3]
        out = g0 * o_cmp[rows] + g1 * o_slc[rows] + g2 * o_win[rows]
        o_ref[:, h * HEAD_DIM:(h + 1) * HEAD_DIM] = out.astype(o_ref.dtype)


def _dsa_kv_kernel(l_ref, x_ref, g_ref, up_ref, c_ref, s_ref, ko_ref, vo_ref):
    x = x_ref[...]
    r = x * lax.rsqrt(jnp.mean(jnp.square(x), axis=-1, keepdims=True) + 1e-6) * g_ref[...]
    kv = _dot(r.astype(BF16), up_ref[...])
    k = kv[:, :HEAD_DIM]
    k = k * c_ref[...] + pltpu.roll(k, HEAD_DIM // 2, 1) * s_ref[...]
    ko_ref[...] = k.astype(ko_ref.dtype)
    vo_ref[...] = kv[:, HEAD_DIM:].astype(vo_ref.dtype)


def _dsa_kernel(l_ref, q_ref, kd_ref, vd_ref, iq_ref, ika_ref, ikb_ref, w_ref, o_ref,
                key_ref, keyt_ref, hi_ref, lo_ref, wb_ref, sel_ref, q_sc, s_ref, mx_ref, acc_ref):
    qi = pl.program_id(1)
    nh = DSA_HEADS
    seq = kd_ref.shape[0]
    k_top = float(min(IDX_TOPK_MAX, seq // 4))
    row = lax.broadcasted_iota(I32, (TQ, TK), 0)
    col = lax.broadcasted_iota(I32, (TQ, TK), 1)
    int_min = jnp.int32(-2 ** 31)
    idx_bits = (seq - 1).bit_length()

    w_all = w_ref[...] * (IDX_HEADS ** -0.5 * IDX_DIM ** -0.5)
    for h in range(IDX_HEADS):
        wb_ref[h] = jnp.broadcast_to(w_all[:, MISC_IDX_W + h:MISC_IDX_W + h + 1], (TQ, LANE))

    def score_step(j, diagonal):
        start = pl.multiple_of(j * TK, TK)
        ka = ika_ref[pl.ds(start, TK), :]
        kb = ikb_ref[pl.ds(start, TK), :]
        sc = jnp.zeros((TQ, TK), F32)
        for g in range(IDX_HEADS // 2):
            qpair = iq_ref[:, g * LANE:(g + 1) * LANE]
            wa = _tile_lanes(wb_ref[2 * g], TK // LANE)
            wb = _tile_lanes(wb_ref[2 * g + 1], TK // LANE)
            sc = sc + jnp.maximum(_dot_nt(qpair, ka), 0.0) * wa + jnp.maximum(_dot_nt(qpair, kb), 0.0) * wb
        if diagonal:
            sc = jnp.where(col <= row, sc, NEG_INF)
        bits = lax.bitcast_convert_type(sc, I32)
        key = jnp.where(bits < 0, bits ^ jnp.int32(0x7FFFFFFF), bits)
        key = jnp.where(sc == 0.0, 0, key)
        key_ref[j] = key
        key_t = key.T
        keyt_ref[j] = key_t
        hi_ref[j] = (key_t >> 16).astype(I16)
        lo_ref[j] = ((key_t & 0xFFFF) - 32768).astype(I16)

    def score_body(j, carry):
        score_step(j, False)
        return carry

    _fori_pairs(qi, score_body, 0)
    score_step(qi, True)

    krow = lax.broadcasted_iota(I32, (TK, TQ), 0)

    def count(pred_fn):
        def body(j, acc):
            hit = jnp.where(pred_fn(keyt_ref[j], j), 1.0, 0.0)
            return acc + jnp.sum(hit.reshape(TK // 8, 8, TQ), axis=0)
        part = _fori_pairs(qi + 1, body, jnp.zeros((8, TQ), F32))
        return jnp.sum(part, axis=0, keepdims=True)

    def count16(ref, cand, strict):
        cand16 = cand.astype(I16)

        def body(j, acc):
            x = ref[j]
            hit = jnp.where(x > cand16 if strict else x >= cand16, jnp.int16(1), jnp.int16(0))
            for g in range(TK // 16):
                acc = acc + hit[g * 16:(g + 1) * 16, :]
            return acc
        part = _fori_pairs(qi + 1, body, jnp.zeros((16, TQ), I16))
        return jnp.sum(part.astype(F32), axis=0, keepdims=True)

    def bisect16(ref, need):
        zero = jnp.zeros((1, TQ), I32)
        v = jnp.where(count16(ref, zero, False) >= need, zero, zero - 32768)

        def bit(i, v):
            cand = v + (jnp.int32(1) << (14 - i))
            return jnp.where(count16(ref, cand, False) >= need, cand, v)
        return lax.fori_loop(0, 15, bit, v)

    hi_thr = bisect16(hi_ref, k_top)
    need_lo = k_top - count16(hi_ref, hi_thr, True)
    hi_thr16 = hi_thr.astype(I16)

    def mask_lo(j, carry):
        lo_ref[j] = jnp.where(hi_ref[j] == hi_thr16, lo_ref[j], jnp.int16(-32768))
        return carry

    _fori_pairs(qi + 1, mask_lo, 0)
    lo_thr = bisect16(lo_ref, need_lo)
    thr = lax.shift_left(hi_thr, jnp.int32(16)) + (lo_thr + 32768)
    n_gt = count(lambda key, j: key > thr)
    n_ge = count(lambda key, j: key >= thr)

    def index_bisect():
        def index_bit(i, cut):
            cand = cut + (jnp.int32(1) << (idx_bits - 1 - i))
            n_before = count(lambda key, j: (key == thr) & (j * TK + krow < cand))
            return jnp.where(n_gt + n_before < k_top, cand, cut)
        return lax.fori_loop(0, idx_bits, index_bit, jnp.zeros((1, TQ), I32))

    def take_all_ties():
        return jnp.full((1, TQ), seq - 1, I32)

    cut = lax.cond(jnp.max(n_ge) > k_top, index_bisect, take_all_ties)
    sel_ref[0] = jnp.broadcast_to(thr, (LANE, TQ)).T
    sel_ref[1] = jnp.broadcast_to(cut, (LANE, TQ)).T

    for h in range(nh):
        q_sc[h * TQ:(h + 1) * TQ, :] = q_ref[:, h * HEAD_DIM:(h + 1) * HEAD_DIM]
    mx_ref[...] = jnp.full(mx_ref.shape, NEG_INF, F32)
    acc_ref[...] = jnp.zeros(acc_ref.shape, F32)

    def attn_score_step(j, diagonal):
        start = pl.multiple_of(j * TK, TK)
        key = key_ref[j]
        thr_t = _tile_lanes(sel_ref[0], TK // LANE)
        cut_t = _tile_lanes(sel_ref[1], TK // LANE)
        vis = (key > thr_t) | ((key == thr_t) & (j * TK + col <= cut_t))
        if diagonal:
            vis = vis & (col <= row)
        s = (_dot_nt(q_sc[...], kd_ref[pl.ds(start, TK), :]) * SCALE_LOG2E).reshape(nh, TQ, TK)
        s = jnp.where(vis[None], s, NEG_INF).reshape(nh * TQ, TK)
        s_ref[j] = s
        mx_ref[...] = jnp.maximum(mx_ref[...], _max_lanes(s))

    def attn_score_body(j, carry):
        attn_score_step(j, False)
        return carry

    _fori_pairs(qi, attn_score_body, 0)
    attn_score_step(qi, True)
    mx_ref[...] = _row_max_to_lanes(mx_ref[...])

    def pv_body(j, carry):
        start = pl.multiple_of(j * TK, TK)
        acc_ref[...] += _exp_pv(s_ref[j], mx_ref[...], vd_ref[pl.ds(start, TK), :])
        return carry

    _fori_pairs(qi + 1, pv_body, 0)
    a = acc_ref[...]
    out = a[:, :HEAD_DIM] / a[:, HEAD_DIM:]
    for h in range(nh):
        o_ref[:, h * HEAD_DIM:(h + 1) * HEAD_DIM] = out[h * TQ:(h + 1) * TQ].astype(o_ref.dtype)


def _merge_kernel(l_ref, hb_ref, of_ref, on_ref, od_ref, wg1_ref, wg2_ref, wg3_ref, wf_ref, wn_ref, wd_ref, o_ref):
    hb = hb_ref[...]
    mixed = (jax.nn.sigmoid(_dot(hb, wg1_ref[...])) * _dot(of_ref[...], wf_ref[...])
             + jax.nn.sigmoid(_dot(hb, wg2_ref[...])) * _dot(on_ref[...], wn_ref[...])
             + jax.nn.sigmoid(_dot(hb, wg3_ref[...])) * _dot(od_ref[...], wd_ref[...]))
    o_ref[...] = mixed.astype(o_ref.dtype)


LN_SUB_ROWS = 256


def _layer_norm_store(y, g_ref, b_ref, of_ref, ob_ref, rows):
    mu = jnp.mean(y, axis=-1, keepdims=True)
    yc = y - mu
    var = jnp.mean(jnp.square(yc), axis=-1, keepdims=True)
    out = yc * lax.rsqrt(var + 1e-5) * g_ref[...] + b_ref[...]
    of_ref[rows, :] = out
    ob_ref[rows, :] = out.astype(ob_ref.dtype)


def _out_ln_kernel(l_ref, x_ref, w_ref, h_ref, g_ref, b_ref, of_ref, ob_ref):
    w = w_ref[...]
    for r in range(x_ref.shape[0] // LN_SUB_ROWS):
        rows = slice(r * LN_SUB_ROWS, (r + 1) * LN_SUB_ROWS)
        y = ALPHA * h_ref[rows, :] + _dot(x_ref[rows, :], w)
        _layer_norm_store(y, g_ref, b_ref, of_ref, ob_ref, rows)


def _out_ln(name, lidx, x, w, h, g, b):
    t, kdim = x.shape
    d = h.shape[1]
    tm = 512
    return _call(_out_ln_kernel, name, lidx, [x, w, h, g, b], grid=(t // tm,),
                 in_specs=[pl.BlockSpec((tm, kdim), lambda i, l: (i, 0)),
                           pl.BlockSpec((None, kdim, d), lambda i, l: (l[0], 0, 0)),
                           pl.BlockSpec((tm, d), lambda i, l: (i, 0)),
                           pl.BlockSpec((None, 1, d), lambda i, l: (l[0], 0, 0)),
                           pl.BlockSpec((None, 1, d), lambda i, l: (l[0], 0, 0))],
                 out_specs=[pl.BlockSpec((tm, d), lambda i, l: (i, 0)),
                            pl.BlockSpec((tm, d), lambda i, l: (i, 0))],
                 out_shape=[jax.ShapeDtypeStruct((t, d), F32), jax.ShapeDtypeStruct((t, d), BF16)])


def _ffn_ln_kernel(l_ref, x_ref, w_ref, h_ref, g_ref, b_ref, of_ref, ob_ref, *, nk):
    k = pl.program_id(1)

    @pl.when(k == 0)
    def _():
        of_ref[...] = ALPHA * h_ref[...] + _dot(x_ref[...], w_ref[...])

    @pl.when((k > 0) & (k < nk - 1))
    def _():
        of_ref[...] += _dot(x_ref[...], w_ref[...])

    @pl.when(k == nk - 1)
    def _():
        w = w_ref[...]
        for r in range(x_ref.shape[0] // LN_SUB_ROWS):
            rows = slice(r * LN_SUB_ROWS, (r + 1) * LN_SUB_ROWS)
            y = of_ref[rows, :] + _dot(x_ref[rows, :], w)
            _layer_norm_store(y, g_ref, b_ref, of_ref, ob_ref, rows)


def _ffn_ln(name, lidx, x, w, h, g, b, tk):
    t, kdim = x.shape
    d = h.shape[1]
    tm = 1024
    nk = kdim // tk
    assert nk >= 2
    return _call(functools.partial(_ffn_ln_kernel, nk=nk), name, lidx, [x, w, h, g, b],
                 grid=(t // tm, nk),
                 in_specs=[pl.BlockSpec((tm, tk), lambda i, k, l: (i, k)),
                           pl.BlockSpec((None, tk, d), lambda i, k, l: (l[0], k, 0)),
                           pl.BlockSpec((tm, d), lambda i, k, l: (i, 0)),
                           pl.BlockSpec((None, 1, d), lambda i, k, l: (l[0], 0, 0)),
                           pl.BlockSpec((None, 1, d), lambda i, k, l: (l[0], 0, 0))],
                 out_specs=[pl.BlockSpec((tm, d), lambda i, k, l: (i, 0)),
                            pl.BlockSpec((tm, d), lambda i, k, l: (i, 0))],
                 out_shape=[jax.ShapeDtypeStruct((t, d), F32), jax.ShapeDtypeStruct((t, d), BF16)],
                 vmem_limit_bytes=VMEM_LIMIT_LARGE_BYTES)


def _swiglu_kernel(l_ref, x_ref, wa_ref, wb_ref, o_ref, wab_ref):
    @pl.when(pl.program_id(1) == 0)
    def _():
        wab_ref[0] = wa_ref[...].astype(BF16)
        wab_ref[1] = wb_ref[...].astype(BF16)

    x = x_ref[...]
    a = _dot(x, wab_ref[0])
    o_ref[...] = (jax.nn.silu(a) * _dot(x, wab_ref[1])).astype(o_ref.dtype)


def _ple_kernel(l_ref, hb_ref, p_ref, wpi_ref, wpg_ref, h_ref, of_ref, ob_ref):
    p_in = _dot(p_ref[...].astype(BF16), wpi_ref[...].astype(BF16))
    out = h_ref[...] + p_in * jax.nn.sigmoid(_dot(hb_ref[...], wpg_ref[...].astype(BF16)))
    of_ref[...] = out
    ob_ref[...] = out.astype(ob_ref.dtype)


def _rope_tables(n, dim):
    inv = 1.0 / (ROPE_THETA ** (jnp.arange(0, dim, 2, dtype=F32) / dim))
    ang = jnp.arange(n, dtype=F32)[:, None] * inv[None, :]
    return jnp.cos(ang), jnp.sin(ang)


def _layer(lidx, h, hb, consts, weights, bsz, seq):
    (cos128, sin128, cos64, sin64a, sin64b, cmp_cos, cmp_sin, overlap, block_of_key) = consts
    (w_proj, fox_bias, pe_k, pe_v, wk1, wk2, wv1, wv2, kv_norm, kv_up,
     w_br_fox, w_br_nsa, w_br_dsa, w_out, ln1_g, ln1_b, w_ffn_in, w_ffn_out, ln2_g, ln2_b, p, w_ple_in, w_ple_gate) = weights
    t = bsz * seq
    nq = seq // TQ
    d = D_MODEL

    z_a = _proj("proj_plain", lidx, hb, w_proj, COL_PLAIN, N_PLAIN, 2048, 512, BF16, seq)
    z_b = _proj("proj_rope128", lidx, hb, w_proj, COL_ROPE128, N_ROPE128, 2048, 512, BF16, seq,
                rope=((HEAD_DIM // 2,), cos128, sin128))
    z_c = _proj("proj_rope64", lidx, hb, w_proj, COL_ROPE64, N_ROPE64, 2048, 256, BF16, seq,
                rope=((IDX_DIM // 2, LANE - IDX_DIM // 2), cos64, sin64a, sin64b))
    z_d = _proj("proj_f32", lidx, hb, w_proj, COL_F32, N_F32, 1024, N_F32, F32, seq)

    cum = _call(_cum_kernel, "fox_cum", lidx, [z_d, fox_bias], grid=(bsz,),
                in_specs=[pl.BlockSpec((seq, LANE), lambda b, l: (b, ZD_FOX_F)),
                          pl.BlockSpec((None, 1, LANE), lambda b, l: (l[0], 0, 0))],
                out_specs=pl.BlockSpec((seq, LANE), lambda b, l: (b, 0)),
                out_shape=jax.ShapeDtypeStruct((t, LANE), F32))
    cum_rows = jnp.transpose(cum.reshape(bsz, seq, LANE)[:, :, :8], (0, 2, 1)).reshape(bsz, 8 * (seq // TK), TK)
    fq = FOX_HEADS * HEAD_DIM
    o_fox = _call(_fox_kernel, "fox_attn", lidx, [z_a, z_a, z_a, cum, cum_rows], grid=(bsz, nq),
                  in_specs=[pl.BlockSpec((TQ, fq), lambda b, i, l: (b * nq + i, 0)),
                            pl.BlockSpec((seq, fq), lambda b, i, l: (b, 1)),
                            pl.BlockSpec((seq, fq), lambda b, i, l: (b, 2)),
                            pl.BlockSpec((TQ, LANE), lambda b, i, l: (b * nq + i, 0)),
                            pl.BlockSpec((None, 8 * (seq // TK), TK), lambda b, i, l: (b, 0, 0))],
                  out_specs=pl.BlockSpec((TQ, fq), lambda b, i, l: (b * nq + i, 0)),
                  out_shape=jax.ShapeDtypeStruct((t, fq), BF16),
                  scratch_shapes=[pltpu.VMEM((FOX_HEADS, seq // TK, TQ, TK), F32),
                                  pltpu.VMEM((FOX_HEADS, TQ, LANE), F32), pltpu.VMEM((FOX_HEADS, TQ, LANE), F32),
                                  pltpu.VMEM((FOX_HEADS, TQ, 2 * HEAD_DIM), F32)])

    nchunk = seq // CMP_STRIDE
    half = CMP_STRIDE * HEAD_DIM
    r_k = z_d[:, ZD_NSA_KC * LANE:(ZD_NSA_KC + 1) * LANE].reshape(bsz * nchunk, half)
    r_v = z_d[:, ZD_NSA_VC * LANE:(ZD_NSA_VC + 1) * LANE].reshape(bsz * nchunk, half)
    k_cmp, v_cmp = _call(
        _cmp_kernel, "nsa_compress", lidx, [r_k, r_v, pe_k, pe_v, wk1, wk2, wv1, wv2, cmp_cos, cmp_sin], grid=(bsz,),
        in_specs=[pl.BlockSpec((nchunk, half), lambda b, l: (b, 0)),
                  pl.BlockSpec((nchunk, half), lambda b, l: (b, 0)),
                  pl.BlockSpec((None, 2, half), lambda b, l: (l[0], 0, 0)),
                  pl.BlockSpec((None, 2, half), lambda b, l: (l[0], 0, 0)),
                  pl.BlockSpec((None, 2 * half, CMP_HIDDEN), lambda b, l: (l[0], 0, 0)),
                  pl.BlockSpec((None, CMP_HIDDEN, HEAD_DIM), lambda b, l: (l[0], 0, 0)),
                  pl.BlockSpec((None, 2 * half, CMP_HIDDEN), lambda b, l: (l[0], 0, 0)),
                  pl.BlockSpec((None, CMP_HIDDEN, HEAD_DIM), lambda b, l: (l[0], 0, 0)),
                  pl.BlockSpec((nchunk, HEAD_DIM), lambda b, l: (0, 0)),
                  pl.BlockSpec((nchunk, HEAD_DIM), lambda b, l: (0, 0))],
        out_specs=[pl.BlockSpec((nchunk, HEAD_DIM), lambda b, l: (b, 0)),
                   pl.BlockSpec((nchunk, HEAD_DIM), lambda b, l: (b, 0))],
        out_shape=[jax.ShapeDtypeStruct((bsz * nchunk, HEAD_DIM), BF16)] * 2)
    nsq = NSA_HEADS * HEAD_DIM
    o_nsa = _call(
        _nsa_kernel, "nsa_attn", lidx, [z_b, z_b, z_b, z_a, z_a, k_cmp, v_cmp, z_d, overlap, block_of_key], grid=(bsz, nq),
        in_specs=[pl.BlockSpec((TQ, nsq), lambda b, i, l: (b * nq + i, 0)),
                  pl.BlockSpec((seq, LANE), lambda b, i, l: (b, 4)),
                  pl.BlockSpec((seq, LANE), lambda b, i, l: (b, 5)),
                  pl.BlockSpec((seq, LANE), lambda b, i, l: (b, 18)),
                  pl.BlockSpec((seq, LANE), lambda b, i, l: (b, 19)),
                  pl.BlockSpec((nchunk, HEAD_DIM), lambda b, i, l: (b, 0)),
                  pl.BlockSpec((nchunk, HEAD_DIM), lambda b, i, l: (b, 0)),
                  pl.BlockSpec((TQ, LANE), lambda b, i, l: (b * nq + i, ZD_NSA_G)),
                  pl.BlockSpec((N_CMP_PAD, LANE), lambda b, i, l: (0, 0)),
                  pl.BlockSpec((seq, LANE), lambda b, i, l: (0, 0))],
        out_specs=pl.BlockSpec((TQ, nsq), lambda b, i, l: (b * nq + i, 0)),
        out_shape=jax.ShapeDtypeStruct((t, nsq), BF16),
        scratch_shapes=[pltpu.VMEM((NSA_HEADS * TQ, HEAD_DIM), BF16),
                        pltpu.VMEM((seq // TK, NSA_HEADS * TQ, TK), F32),
                        pltpu.VMEM((NSA_HEADS * TQ, LANE), F32),
                        pltpu.VMEM((NSA_HEADS * TQ, 2 * HEAD_DIM), F32)])

    tm_kv = 1024
    k_d, v_d = _call(
        _dsa_kv_kernel, "dsa_kv", lidx, [z_d, kv_norm, kv_up, cos128, sin128], grid=(t // tm_kv,),
        in_specs=[pl.BlockSpec((tm_kv, DSA_KV_RANK), lambda i, l: (i, 0)),
                  pl.BlockSpec((None, 1, DSA_KV_RANK), lambda i, l: (l[0], 0, 0)),
                  pl.BlockSpec((None, DSA_KV_RANK, 2 * HEAD_DIM), lambda i, l: (l[0], 0, 0)),
                  pl.BlockSpec((tm_kv, LANE), lambda i, l: (i % (seq // tm_kv), 0)),
                  pl.BlockSpec((tm_kv, LANE), lambda i, l: (i % (seq // tm_kv), 0))],
        out_specs=[pl.BlockSpec((tm_kv, HEAD_DIM), lambda i, l: (i, 0)),
                   pl.BlockSpec((tm_kv, HEAD_DIM), lambda i, l: (i, 0))],
        out_shape=[jax.ShapeDtypeStruct((t, HEAD_DIM), BF16)] * 2)
    dq = DSA_HEADS * HEAD_DIM
    iqw = IDX_HEADS * IDX_DIM
    o_dsa = _call(
        _dsa_kernel, "dsa_attn", lidx, [z_b, k_d, v_d, z_c, z_c, z_c, z_d], grid=(bsz, nq),
        in_specs=[pl.BlockSpec((TQ, dq), lambda b, i, l: (b * nq + i, 1)),
                  pl.BlockSpec((seq, HEAD_DIM), lambda b, i, l: (b, 0)),
                  pl.BlockSpec((seq, HEAD_DIM), lambda b, i, l: (b, 0)),
                  pl.BlockSpec((TQ, iqw), lambda b, i, l: (b * nq + i, 0)),
                  pl.BlockSpec((seq, LANE), lambda b, i, l: (b, iqw // LANE)),
                  pl.BlockSpec((seq, LANE), lambda b, i, l: (b, iqw // LANE + 1)),
                  pl.BlockSpec((TQ, LANE), lambda b, i, l: (b * nq + i, ZD_IDX_W))],
        out_specs=pl.BlockSpec((TQ, dq), lambda b, i, l: (b * nq + i, 0)),
        out_shape=jax.ShapeDtypeStruct((t, dq), BF16),
        scratch_shapes=[pltpu.VMEM((seq // TK, TQ, TK), I32), pltpu.VMEM((seq // TK, TK, TQ), I32),
                        pltpu.VMEM((seq // TK, TK, TQ), I16), pltpu.VMEM((seq // TK, TK, TQ), I16),
                        pltpu.VMEM((IDX_HEADS, TQ, LANE), F32), pltpu.VMEM((2, TQ, LANE), I32),
                        pltpu.VMEM((DSA_HEADS * TQ, HEAD_DIM), BF16),
                        pltpu.VMEM((seq // TK, DSA_HEADS * TQ, TK), F32),
                        pltpu.VMEM((DSA_HEADS * TQ, LANE), F32),
                        pltpu.VMEM((DSA_HEADS * TQ, 2 * HEAD_DIM), F32)])

    tm, tn = 1024, 512
    ncol = d // tn
    gate0 = COL_GATE // tn
    assert gate0 * tn == COL_GATE
    mixed = _call(
        _merge_kernel, "merge", lidx, [hb, o_fox, o_nsa, o_dsa, w_proj, w_proj, w_proj, w_br_fox, w_br_nsa, w_br_dsa],
        grid=(t // tm, ncol),
        in_specs=[pl.BlockSpec((tm, d), lambda i, j, l: (i, 0)),
                  pl.BlockSpec((tm, fq), lambda i, j, l: (i, 0)),
                  pl.BlockSpec((tm, nsq), lambda i, j, l: (i, 0)),
                  pl.BlockSpec((tm, dq), lambda i, j, l: (i, 0)),
                  pl.BlockSpec((None, d, tn), lambda i, j, l: (l[0], 0, gate0 + j)),
                  pl.BlockSpec((None, d, tn), lambda i, j, l: (l[0], 0, gate0 + ncol + j)),
                  pl.BlockSpec((None, d, tn), lambda i, j, l: (l[0], 0, gate0 + 2 * ncol + j)),
                  pl.BlockSpec((None, fq, tn), lambda i, j, l: (l[0], 0, j)),
                  pl.BlockSpec((None, nsq, tn), lambda i, j, l: (l[0], 0, j)),
                  pl.BlockSpec((None, dq, tn), lambda i, j, l: (l[0], 0, j))],
        out_specs=pl.BlockSpec((tm, tn), lambda i, j, l: (i, j)),
        out_shape=jax.ShapeDtypeStruct((t, d), BF16))

    h, hb = _out_ln("out_ln", lidx, mixed, w_out, h, ln1_g, ln1_b)

    nff = D_FF // tn
    act = _call(
        _swiglu_kernel, "swiglu", lidx, [hb, w_ffn_in, w_ffn_in], grid=(nff, t // tm),
        in_specs=[pl.BlockSpec((tm, d), lambda j, i, l: (i, 0)),
                  pl.BlockSpec((None, d, tn), lambda j, i, l: (l[0], 0, j)),
                  pl.BlockSpec((None, d, tn), lambda j, i, l: (l[0], 0, nff + j))],
        out_specs=pl.BlockSpec((tm, tn), lambda j, i, l: (i, j)),
        out_shape=jax.ShapeDtypeStruct((t, D_FF), BF16),
        scratch_shapes=[pltpu.VMEM((2, d, tn), BF16)])
    h, hb = _ffn_ln("ffn_ln", lidx, act, w_ffn_out, h, ln2_g, ln2_b, tk=512)

    h, hb = _call(
        _ple_kernel, "ple", lidx, [hb, p, w_ple_in, w_ple_gate, h], grid=(t // tm, ncol),
        in_specs=[pl.BlockSpec((tm, d), lambda i, j, l: (i, 0)),
                  pl.BlockSpec((None, tm, PLE_DIM), lambda i, j, l: (l[0], i, 0)),
                  pl.BlockSpec((None, PLE_DIM, tn), lambda i, j, l: (l[0], 0, j)),
                  pl.BlockSpec((None, d, tn), lambda i, j, l: (l[0], 0, j)),
                  pl.BlockSpec((tm, tn), lambda i, j, l: (i, j))],
        out_specs=[pl.BlockSpec((tm, tn), lambda i, j, l: (i, j)),
                   pl.BlockSpec((tm, tn), lambda i, j, l: (i, j))],
        out_shape=[jax.ShapeDtypeStruct((t, d), F32), jax.ShapeDtypeStruct((t, d), BF16)])
    return h, hb


def kernel(x, p, w_in, fox_f_bias, nsa_pe_k, nsa_pe_v, nsa_cmp_k1, nsa_cmp_k2, nsa_cmp_v1, nsa_cmp_v2, dsa_kv_norm, dsa_kv_up, w_br_fox, w_br_nsa, w_br_dsa, w_out, ln1_g, ln1_b, w_ffn_in, w_ffn_out, ln2_g, ln2_b, w_ple_in, w_ple_gate):
    bsz, seq, d = x.shape
    depth = w_in.shape[0]
    t = bsz * seq
    assert d == D_MODEL and seq % 1024 == 0 and depth == DEPTH

    cat = functools.partial(jnp.concatenate, axis=-1)
    w_proj = _regroup(w_in)

    fox_bias = jnp.pad(fox_f_bias, ((0, 0), (0, LANE - FOX_HEADS))).reshape(depth, 1, LANE)
    half = CMP_STRIDE * HEAD_DIM
    weights = (
        w_proj, fox_bias,
        nsa_pe_k.reshape(depth, 2, half), nsa_pe_v.reshape(depth, 2, half),
        nsa_cmp_k1.astype(BF16), nsa_cmp_k2.astype(BF16), nsa_cmp_v1.astype(BF16), nsa_cmp_v2.astype(BF16),
        dsa_kv_norm.reshape(depth, 1, DSA_KV_RANK), dsa_kv_up.astype(BF16),
        w_br_fox.astype(BF16), w_br_nsa.astype(BF16), w_br_dsa.astype(BF16), w_out.astype(BF16),
        ln1_g.reshape(depth, 1, d), ln1_b.reshape(depth, 1, d),
        w_ffn_in, w_ffn_out.astype(BF16),
        ln2_g.reshape(depth, 1, d), ln2_b.reshape(depth, 1, d),
        p.reshape(depth, t, PLE_DIM), w_ple_in, w_ple_gate,
    )

    cos, sin = _rope_tables(seq, HEAD_DIM)
    cos128 = cat([cos, cos])
    sin128 = cat([-sin, sin])
    cos_i, sin_i = _rope_tables(seq, IDX_DIM)
    zi = jnp.zeros_like(sin_i)
    cos64 = cat([cos_i, cos_i, cos_i, cos_i])
    sin64a = cat([zi, sin_i, zi, sin_i])
    sin64b = cat([-sin_i, zi, -sin_i, zi])
    n_cmp = (seq - CMP_LEN) // CMP_STRIDE + 1
    c_end = jnp.minimum(jnp.arange(N_CMP_PAD) * CMP_STRIDE + CMP_LEN - 1, seq - 1)
    cmp_cos, cmp_sin = cos128[c_end], sin128[c_end]
    n_slc = seq // SLC_LEN
    c_start = jnp.arange(N_CMP_PAD) * CMP_STRIDE
    s_start = jnp.arange(LANE) * SLC_LEN
    overlap = jnp.maximum(jnp.minimum(c_start[:, None] + CMP_LEN - 1, s_start[None, :] + SLC_LEN - 1)
                          - jnp.maximum(c_start[:, None], s_start[None, :]) + 1, 0).astype(F32) / CMP_LEN
    overlap = jnp.where((jnp.arange(N_CMP_PAD)[:, None] < n_cmp) & (jnp.arange(LANE)[None, :] < n_slc), overlap, 0.0)
    overlap = overlap.T.astype(BF16)
    block_of_key = (jnp.arange(seq)[:, None] // SLC_LEN == jnp.arange(LANE)[None, :]).astype(BF16)
    consts = (cos128, sin128, cos64, sin64a, sin64b, cmp_cos, cmp_sin, overlap, block_of_key)

    h = x.reshape(t, d)
    hb = h.astype(BF16)
    for layer in range(depth):
        lidx = jnp.full((1,), layer, I32)
        h, hb = _layer(lidx, h, hb, consts, weights, bsz, seq)
    return h.reshape(bsz, seq, d)
```

```python
import functools

import jax
import jax.numpy as jnp
from jax import lax
from jax.experimental import pallas as pl
from jax.experimental.pallas import tpu as pltpu

F32, BF16, I32, I16 = jnp.float32, jnp.bfloat16, jnp.int32, jnp.int16

D_MODEL = 2048
DEPTH = 4
HEAD_DIM = 128
ROPE_THETA = 10000.0
NEG_INF = -1e30
FOX_HEADS = 6
NSA_HEADS = 4
CMP_LEN = 32
CMP_STRIDE = 16
CMP_HIDDEN = 256
SLC_LEN = 64
SLC_TOPN = 16
WIN = 512
SLC_FORCE = 1e4
DSA_HEADS = 6
DSA_KV_RANK = 256
IDX_HEADS = 16
IDX_DIM = 64
IDX_TOPK_MAX = 256
D_FF = ((8 * D_MODEL + 3 * 256 - 1) // (3 * 256)) * 256
PLE_DIM = 256
ALPHA = (2 * DEPTH) ** 0.25
SCALE = HEAD_DIM ** -0.5
LOG2E = 1.4426950408889634
SCALE_LOG2E = SCALE * LOG2E

LANE = 128
VMEM_LIMIT_BYTES = 48 * 1024 * 1024
VMEM_LIMIT_LARGE_BYTES = 56 * 1024 * 1024

_IN_SPLITS = (
    ('fox_q', FOX_HEADS * HEAD_DIM), ('fox_k', FOX_HEADS * HEAD_DIM), ('fox_v', FOX_HEADS * HEAD_DIM), ('fox_f', FOX_HEADS),
    ('nsa_q', NSA_HEADS * HEAD_DIM), ('nsa_kc', HEAD_DIM), ('nsa_vc', HEAD_DIM), ('nsa_ks', HEAD_DIM), ('nsa_vs', HEAD_DIM),
    ('nsa_kw', HEAD_DIM), ('nsa_vw', HEAD_DIM), ('nsa_g', 3 * NSA_HEADS),
    ('dsa_q', DSA_HEADS * HEAD_DIM), ('dsa_ckv', DSA_KV_RANK), ('idx_q', IDX_HEADS * IDX_DIM), ('idx_k', IDX_DIM), ('idx_w', IDX_HEADS),
    ('gate', 3 * D_MODEL),
)
_IN_OFFSETS = {}
_off = 0
for _name, _width in _IN_SPLITS:
    _IN_OFFSETS[_name] = (_off, _off + _width)
    _off += _width

MISC_FOX_F = 0
MISC_NSA_G = FOX_HEADS
MISC_IDX_W = FOX_HEADS + 3 * NSA_HEADS

TQ = 256
TK = 256
N_CMP_PAD = 128


def _dot(a, b):
    return jnp.dot(a, b, preferred_element_type=F32)


def _dot_nt(a, b):
    return lax.dot_general(a, b, (((1,), (1,)), ((), ())), preferred_element_type=F32)


def _fori_pairs(n, body, init):
    def four(jj, carry):
        for u in range(4):
            carry = body(4 * jj + u, carry)
        return carry
    carry = lax.fori_loop(0, lax.shift_right_logical(n, jnp.int32(2)), four, init)
    base = n & jnp.int32(-4)
    carry = lax.cond((n & 2) == 2, lambda c: body(base + 1, body(base, c)), lambda c: c, carry)
    return lax.cond((n & 1) == 1, lambda c: body(n - 1, c), lambda c: c, carry)


def _split3(a):
    a1 = a.astype(BF16)
    r1 = a - a1.astype(F32)
    a2 = r1.astype(BF16)
    a3 = (r1 - a2.astype(F32)).astype(BF16)
    return a1, a2, a3


def _call(kernel, name, lidx, args, grid, in_specs, out_specs, out_shape, scratch_shapes=(),
          vmem_limit_bytes=VMEM_LIMIT_BYTES):
    return pl.pallas_call(
        kernel,
        grid_spec=pltpu.PrefetchScalarGridSpec(
            num_scalar_prefetch=1, grid=grid, in_specs=in_specs, out_specs=out_specs,
            scratch_shapes=list(scratch_shapes)),
        out_shape=out_shape,
        compiler_params=pltpu.CompilerParams(
            dimension_semantics=("arbitrary",) * len(grid), vmem_limit_bytes=vmem_limit_bytes),
        name=name,
    )(lidx, *args)


def _regroup_plan():
    plan = []

    def whole(name):
        a, b = _IN_OFFSETS[name]
        for g in range((b - a) // LANE):
            plan.append((a + g * LANE, 0, LANE))

    whole('dsa_ckv')
    lane0 = {}
    start = 0
    for name in ('fox_f', 'nsa_g', 'idx_w'):
        a, b = _IN_OFFSETS[name]
        lane0[name] = (a - start, start, start + b - a)
        start += b - a
    plan.append(lane0['fox_f'])
    whole('nsa_kc')
    whole('nsa_vc')
    plan.append(lane0['nsa_g'])
    plan.append(lane0['idx_w'])
    plan.append((0, 0, 0))
    for name in ('fox_q', 'fox_k', 'fox_v', 'nsa_vs', 'nsa_vw'):
        whole(name)
    for name in ('nsa_q', 'nsa_ks', 'nsa_kw', 'dsa_q'):
        whole(name)
    whole('gate')
    whole('idx_q')
    ik = _IN_OFFSETS['idx_k'][0]
    plan.append((ik, 0, IDX_DIM))
    plan.append((ik - IDX_DIM, IDX_DIM, LANE))
    return tuple(plan)


_REGROUP_PLAN = _regroup_plan()
COL_F32 = 0
N_F32 = DSA_KV_RANK + 6 * LANE
ZD_FOX_F, ZD_NSA_KC, ZD_NSA_VC, ZD_NSA_G, ZD_IDX_W = 2, 3, 4, 5, 6
COL_PLAIN = COL_F32 + N_F32
N_PLAIN = (3 * FOX_HEADS + 2) * HEAD_DIM
COL_ROPE128 = COL_PLAIN + N_PLAIN
N_ROPE128 = (NSA_HEADS + 2 + DSA_HEADS) * HEAD_DIM
COL_GATE = COL_ROPE128 + N_ROPE128
COL_ROPE64 = COL_GATE + 3 * D_MODEL
N_ROPE64 = IDX_HEADS * IDX_DIM + 2 * LANE
N_REGROUPED = COL_ROPE64 + N_ROPE64
assert N_REGROUPED == len(_REGROUP_PLAN) * LANE


def _regroup_kernel(src_ref, lo_ref, hi_ref, x_ref, o_ref):
    g = pl.program_id(0)
    col = lax.broadcasted_iota(I32, (LANE, x_ref.shape[2]), 0)
    keep = (col >= lo_ref[g]) & (col < hi_ref[g])
    for layer in range(x_ref.shape[1]):
        o_ref[layer] = jnp.where(keep, x_ref[:, layer, :], 0.0).T.astype(o_ref.dtype)


def _regroup(w_in):
    depth, d, n_in = w_in.shape
    w_t = jnp.transpose(w_in, (2, 0, 1))
    src = jnp.array([p[0] for p in _REGROUP_PLAN], I32)
    lo = jnp.array([p[1] for p in _REGROUP_PLAN], I32)
    hi = jnp.array([p[2] for p in _REGROUP_PLAN], I32)
    return pl.pallas_call(
        _regroup_kernel,
        grid_spec=pltpu.PrefetchScalarGridSpec(
            num_scalar_prefetch=3, grid=(len(_REGROUP_PLAN),),
            in_specs=[pl.BlockSpec((pl.Element(LANE), pl.Element(depth), pl.Element(d)),
                                   lambda g, src, lo, hi: (src[g], 0, 0))],
            out_specs=pl.BlockSpec((depth, d, LANE), lambda g, src, lo, hi: (0, 0, g))),
        out_shape=jax.ShapeDtypeStruct((depth, d, N_REGROUPED), BF16),
        compiler_params=pltpu.CompilerParams(dimension_semantics=("arbitrary",), vmem_limit_bytes=VMEM_LIMIT_BYTES),
        name="regroup_w_in",
    )(src, lo, hi, w_t)


def _proj_kernel(l_ref, x_ref, w_ref, *refs, shifts):
    o_ref = refs[-1]
    acc = _dot(x_ref[...], w_ref[...])
    if not shifts:
        o_ref[...] = acc.astype(o_ref.dtype)
        return
    c_ref = refs[0]
    s_refs = refs[1:-1]
    for g in range(acc.shape[1] // LANE):
        xg = acc[:, g * LANE:(g + 1) * LANE]
        out = xg * c_ref[...]
        for shift, s_ref in zip(shifts, s_refs):
            out = out + pltpu.roll(xg, shift, 1) * s_ref[...]
        o_ref[:, g * LANE:(g + 1) * LANE] = out.astype(o_ref.dtype)


def _proj(name, lidx, hb, w, col0, n, tm, tn, out_dtype, seq, rope=None):
    t, d = hb.shape
    jb = col0 // tn
    assert jb * tn == col0
    in_specs = [pl.BlockSpec((tm, d), lambda i, j, l: (i, 0)),
                pl.BlockSpec((None, d, tn), lambda i, j, l: (l[0], 0, jb + j))]
    args = [hb, w]
    shifts = ()
    if rope is not None:
        shifts, tables = rope[0], rope[1:]
        nrow = seq // tm
        for tab in tables:
            in_specs.append(pl.BlockSpec((tm, LANE), lambda i, j, l: (i % nrow, 0)))
            args.append(tab)
    return _call(functools.partial(_proj_kernel, shifts=shifts), name, lidx, args,
                 grid=(t // tm, n // tn), in_specs=in_specs,
                 out_specs=pl.BlockSpec((tm, tn), lambda i, j, l: (i, j)),
                 out_shape=jax.ShapeDtypeStruct((t, n), out_dtype))


def _cum_kernel(l_ref, f_ref, bias_ref, o_ref):
    chunk = 256
    row = lax.broadcasted_iota(I32, (chunk, chunk), 0)
    col = lax.broadcasted_iota(I32, (chunk, chunk), 1)
    tri = jnp.where(row >= col, 1.0, 0.0).astype(BF16)
    carry = jnp.zeros((1, LANE), F32)
    for c in range(f_ref.shape[0] // chunk):
        logf = jax.nn.log_sigmoid(f_ref[c * chunk:(c + 1) * chunk, :] + bias_ref[...])
        l1, l2, l3 = _split3(logf)
        cs = _dot(tri, l1) + _dot(tri, l2) + _dot(tri, l3) + carry
        o_ref[c * chunk:(c + 1) * chunk, :] = cs
        carry = cs[chunk - 1:chunk, :]


def _max_lanes(x):
    out = x[:, 0:LANE]
    for g in range(1, x.shape[1] // LANE):
        out = jnp.maximum(out, x[:, g * LANE:(g + 1) * LANE])
    return out


def _tile_lanes(x, n):
    return jnp.concatenate([x] * n, axis=1)


def _row_max_to_lanes(mx):
    return jnp.broadcast_to(jnp.max(mx, axis=-1, keepdims=True), mx.shape)


def _exp_pv(s, m_lanes, v):
    p = jnp.exp2(s - _tile_lanes(m_lanes, s.shape[1] // LANE)).astype(BF16)
    return _dot(p, jnp.concatenate([v, jnp.ones_like(v)], axis=1))


def _fox_kernel(l_ref, q_ref, k_ref, v_ref, cq_ref, ck_ref, o_ref, s_ref, mx_ref, cqb_ref, acc_ref):
    qi = pl.program_id(1)
    nh = FOX_HEADS
    nchunk = ck_ref.shape[0] // 8
    row = lax.broadcasted_iota(I32, (TQ, TK), 0)
    col = lax.broadcasted_iota(I32, (TQ, TK), 1)
    cq_all = cq_ref[...]
    for h in range(nh):
        cqb_ref[h] = jnp.broadcast_to(cq_all[:, h:h + 1] * LOG2E, (TQ, LANE))
    mx_ref[...] = jnp.full(mx_ref.shape, NEG_INF, F32)
    acc_ref[...] = jnp.zeros(acc_ref.shape, F32)

    def score_step(j, diagonal):
        start = pl.multiple_of(j * TK, TK)
        for h in range(nh):
            hs = slice(h * HEAD_DIM, (h + 1) * HEAD_DIM)
            s = (_dot_nt(q_ref[:, hs], k_ref[pl.ds(start, TK), hs]) * SCALE_LOG2E
                 + _tile_lanes(cqb_ref[h], TK // LANE) - ck_ref[pl.ds(h * nchunk + j, 1), :] * LOG2E)
            if diagonal:
                s = jnp.where(col <= row, s, NEG_INF)
            s_ref[h, j] = s
            mx_ref[h] = jnp.maximum(mx_ref[h], _max_lanes(s))

    def score_body(j, carry):
        score_step(j, False)
        return carry

    _fori_pairs(qi, score_body, 0)
    score_step(qi, True)
    for h in range(nh):
        mx_ref[h] = _row_max_to_lanes(mx_ref[h])

    def pv_body(j, carry):
        start = pl.multiple_of(j * TK, TK)
        for h in range(nh):
            hs = slice(h * HEAD_DIM, (h + 1) * HEAD_DIM)
            acc_ref[h] += _exp_pv(s_ref[h, j], mx_ref[h], v_ref[pl.ds(start, TK), hs])
        return carry

    _fori_pairs(qi + 1, pv_body, 0)
    for h in range(nh):
        a = acc_ref[h]
        o_ref[:, h * HEAD_DIM:(h + 1) * HEAD_DIM] = (a[:, :HEAD_DIM] / a[:, HEAD_DIM:]).astype(o_ref.dtype)


def _cmp_kernel(l_ref, rk_ref, rv_ref, pek_ref, pev_ref, wk1_ref, wk2_ref, wv1_ref, wv2_ref,
                c_ref, s_ref, ko_ref, vo_ref):
    half = CMP_STRIDE * HEAD_DIM

    def compress(r_ref, pe_ref, w1_ref, w2_ref):
        r = r_ref[...]
        lo = (r + pe_ref[0:1, :]).astype(BF16)
        hi = (r + pe_ref[1:2, :]).astype(BF16)
        a = _dot(lo, w1_ref[0:half, :])
        b = _dot(hi, w1_ref[half:2 * half, :])
        hid = a + pltpu.roll(b, N_CMP_PAD - 1, 0)
        return _dot(jax.nn.gelu(hid).astype(BF16), w2_ref[...])

    kc = compress(rk_ref, pek_ref, wk1_ref, wk2_ref)
    kc = kc * c_ref[...] + pltpu.roll(kc, HEAD_DIM // 2, 1) * s_ref[...]
    ko_ref[...] = kc.astype(ko_ref.dtype)
    vo_ref[...] = compress(rv_ref, pev_ref, wv1_ref, wv2_ref).astype(vo_ref.dtype)


def _masked_softmax(s, mask):
    s = jnp.where(mask, s, NEG_INF)
    m = jnp.max(s, axis=-1, keepdims=True)
    e = jnp.where(mask, jnp.exp(s - m), 0.0)
    den = jnp.sum(e, axis=-1, keepdims=True)
    return e, jnp.where(den > 0.0, den, 1.0)


def _nsa_kernel(l_ref, q_ref, ks_ref, kw_ref, vs_ref, vw_ref, kc_ref, vc_ref, g_ref, ovt_ref, et_ref,
                o_ref, q_sc, s_ref, mx_ref, acc_ref):
    qi = pl.program_id(1)
    nh = NSA_HEADS
    for h in range(nh):
        q_sc[h * TQ:(h + 1) * TQ, :] = q_ref[:, h * HEAD_DIM:(h + 1) * HEAD_DIM]
    q = q_sc[...]
    pos = qi * TQ + lax.broadcasted_iota(I32, (TQ, 1), 0)
    lane = lax.broadcasted_iota(I32, (TQ, LANE), 1)

    cvis = (lane * CMP_STRIDE + (CMP_LEN - 1)) <= pos
    s_c = (_dot_nt(q, kc_ref[...]) * SCALE).reshape(nh, TQ, N_CMP_PAD)
    e_c, den_c = _masked_softmax(s_c, cvis[None])
    p_c = e_c / den_c
    o_cmp = _dot(p_c.reshape(nh * TQ, N_CMP_PAD).astype(BF16), vc_ref[...])

    p_sum = p_c[0]
    for h in range(1, nh):
        p_sum = p_sum + p_c[h]
    p1, p2, p3 = _split3(p_sum)
    ovt = ovt_ref[...]
    n_slc = et_ref.shape[0] // SLC_LEN
    imp = (_dot_nt(ovt, p1) + _dot_nt(ovt, p2) + _dot_nt(ovt, p3))[0:n_slc, :]
    blk = lax.broadcasted_iota(I32, (n_slc, TQ), 0)
    pos_t = qi * TQ + lax.broadcasted_iota(I32, (n_slc, TQ), 1)
    cur = lax.shift_right_logical(pos_t, SLC_LEN.bit_length() - 1)
    forced = (blk == 0) | (blk == cur) | (blk == cur - 1)
    imp = jnp.where(forced, SLC_FORCE, jnp.where(blk * SLC_LEN <= pos_t, imp, -SLC_FORCE))
    rank = jnp.zeros((n_slc, TQ), F32)
    for k in range(n_slc):
        ik = imp[k:k + 1, :]
        ahead = (ik > imp) | ((ik == imp) & (blk > k))
        rank = rank + jnp.where(ahead, 1.0, 0.0)
    sel_t = jnp.where(rank < float(min(SLC_TOPN, n_slc)), 1.0, 0.0)
    sel_t = jnp.concatenate([sel_t, jnp.zeros((LANE - n_slc, TQ), F32)], axis=0)
    sel = sel_t.T.astype(BF16)

    mx_ref[...] = jnp.full(mx_ref.shape, NEG_INF, F32)
    acc_ref[...] = jnp.zeros(acc_ref.shape, F32)
    row = lax.broadcasted_iota(I32, (TQ, TK), 0)
    col = lax.broadcasted_iota(I32, (TQ, TK), 1)

    def score_step(j, diagonal):
        start = pl.multiple_of(j * TK, TK)
        vis = _dot_nt(sel, et_ref[pl.ds(start, TK), :]) > 0.5
        if diagonal:
            vis = vis & (col <= row)
        s = (_dot_nt(q_sc[...], ks_ref[pl.ds(start, TK), :]) * SCALE_LOG2E).reshape(nh, TQ, TK)
        s = jnp.where(vis[None], s, NEG_INF).reshape(nh * TQ, TK)
        s_ref[j] = s
        mx_ref[...] = jnp.maximum(mx_ref[...], _max_lanes(s))

    def score_body(j, carry):
        score_step(j, False)
        return carry

    _fori_pairs(qi, score_body, 0)
    score_step(qi, True)
    mx_ref[...] = _row_max_to_lanes(mx_ref[...])

    def pv_body(j, carry):
        start = pl.multiple_of(j * TK, TK)
        acc_ref[...] += _exp_pv(s_ref[j], mx_ref[...], vs_ref[pl.ds(start, TK), :])
        return carry

    _fori_pairs(qi + 1, pv_body, 0)
    a_slc = acc_ref[...]
    o_slc = a_slc[:, :HEAD_DIM] / a_slc[:, HEAD_DIM:]

    band = WIN + TQ
    start_w = pl.multiple_of(jnp.maximum(qi * TQ - WIN, 0), TQ)
    kwb = kw_ref[pl.ds(start_w, band), :]
    vwb = vw_ref[pl.ds(start_w, band), :]
    dist = pos - (start_w + lax.broadcasted_iota(I32, (TQ, band), 1))
    wvis = (dist >= 0) & (dist < WIN)
    s_w = (_dot_nt(q, kwb) * SCALE_LOG2E).reshape(nh, TQ, band)
    s_w = jnp.where(wvis[None], s_w, NEG_INF).reshape(nh * TQ, band)
    a_win = _exp_pv(s_w, _row_max_to_lanes(_max_lanes(s_w)), vwb)
    o_win = a_win[:, :HEAD_DIM] / a_win[:, HEAD_DIM:]

    gate = jax.nn.sigmoid(g_ref[...])
    for h in range(nh):
        rows = slice(h * TQ, (h + 1) * TQ)
        g0 = gate[:, MISC_NSA_G + 3 * h:MISC_NSA_G + 3 * h + 1]
        g1 = gate[:, MISC_NSA_G + 3 * h + 1:MISC_NSA_G + 3 * h + 2]
        g2 = gate[:, MISC_NSA_G + 3 * h + 2:MISC_NSA_G + 3 * h + 3]
        out = g0 * o_cmp[rows] + g1 * o_slc[rows] + g2 * o_win[rows]
        o_ref[:, h * HEAD_DIM:(h + 1) * HEAD_DIM] = out.astype(o_ref.dtype)


def _dsa_kv_kernel(l_ref, x_ref, g_ref, up_ref, c_ref, s_ref, ko_ref, vo_ref):
    x = x_ref[...]
    r = x * lax.rsqrt(jnp.mean(jnp.square(x), axis=-1, keepdims=True) + 1e-6) * g_ref[...]
    kv = _dot(r.astype(BF16), up_ref[...])
    k = kv[:, :HEAD_DIM]
    k = k * c_ref[...] + pltpu.roll(k, HEAD_DIM // 2, 1) * s_ref[...]
    ko_ref[...] = k.astype(ko_ref.dtype)
    vo_ref[...] = kv[:, HEAD_DIM:].astype(vo_ref.dtype)


def _dsa_kernel(l_ref, q_ref, kd_ref, vd_ref, iq_ref, ika_ref, ikb_ref, w_ref, o_ref,
                key_ref, keyt_ref, hi_ref, lo_ref, wb_ref, sel_ref, q_sc, s_ref, mx_ref, acc_ref):
    qi = pl.program_id(1)
    nh = DSA_HEADS
    seq = kd_ref.shape[0]
    k_top = float(min(IDX_TOPK_MAX, seq // 4))
    row = lax.broadcasted_iota(I32, (TQ, TK), 0)
    col = lax.broadcasted_iota(I32, (TQ, TK), 1)
    int_min = jnp.int32(-2 ** 31)
    idx_bits = (seq - 1).bit_length()

    w_all = w_ref[...] * (IDX_HEADS ** -0.5 * IDX_DIM ** -0.5)
    for h in range(IDX_HEADS):
        wb_ref[h] = jnp.broadcast_to(w_all[:, MISC_IDX_W + h:MISC_IDX_W + h + 1], (TQ, LANE))

    def score_step(j, diagonal):
        start = pl.multiple_of(j * TK, TK)
        ka = ika_ref[pl.ds(start, TK), :]
        kb = ikb_ref[pl.ds(start, TK), :]
        sc = jnp.zeros((TQ, TK), F32)
        for g in range(IDX_HEADS // 2):
            qpair = iq_ref[:, g * LANE:(g + 1) * LANE]
            wa = _tile_lanes(wb_ref[2 * g], TK // LANE)
            wb = _tile_lanes(wb_ref[2 * g + 1], TK // LANE)
            sc = sc + jnp.maximum(_dot_nt(qpair, ka), 0.0) * wa + jnp.maximum(_dot_nt(qpair, kb), 0.0) * wb
        if diagonal:
            sc = jnp.where(col <= row, sc, NEG_INF)
        bits = lax.bitcast_convert_type(sc, I32)
        key = jnp.where(bits < 0, bits ^ jnp.int32(0x7FFFFFFF), bits)
        key = jnp.where(sc == 0.0, 0, key)
        key_ref[j] = key
        key_t = key.T
        keyt_ref[j] = key_t
        hi_ref[j] = (key_t >> 16).astype(I16)
        lo_ref[j] = ((key_t & 0xFFFF) - 32768).astype(I16)

    def score_body(j, carry):
        score_step(j, False)
        return carry

    _fori_pairs(qi, score_body, 0)
    score_step(qi, True)

    krow = lax.broadcasted_iota(I32, (TK, TQ), 0)

    def count(pred_fn):
        def body(j, acc):
            hit = jnp.where(pred_fn(keyt_ref[j], j), 1.0, 0.0)
            return acc + jnp.sum(hit.reshape(TK // 8, 8, TQ), axis=0)
        part = _fori_pairs(qi + 1, body, jnp.zeros((8, TQ), F32))
        return jnp.sum(part, axis=0, keepdims=True)

    def count16(ref, cand, strict):
        cand16 = cand.astype(I16)

        def body(j, acc):
            x = ref[j]
            hit = jnp.where(x > cand16 if strict else x >= cand16, jnp.int16(1), jnp.int16(0))
            for g in range(TK // 16):
                acc = acc + hit[g * 16:(g + 1) * 16, :]
            return acc
        part = _fori_pairs(qi + 1, body, jnp.zeros((16, TQ), I16))
        return jnp.sum(part.astype(F32), axis=0, keepdims=True)

    def bisect16(ref, need):
        zero = jnp.zeros((1, TQ), I32)
        v = jnp.where(count16(ref, zero, False) >= need, zero, zero - 32768)

        def bit(i, v):
            cand = v + (jnp.int32(1) << (14 - i))
            return jnp.where(count16(ref, cand, False) >= need, cand, v)
        return lax.fori_loop(0, 15, bit, v)

    hi_thr = bisect16(hi_ref, k_top)
    need_lo = k_top - count16(hi_ref, hi_thr, True)
    hi_thr16 = hi_thr.astype(I16)

    def mask_lo(j, carry):
        lo_ref[j] = jnp.where(hi_ref[j] == hi_thr16, lo_ref[j], jnp.int16(-32768))
        return carry

    _fori_pairs(qi + 1, mask_lo, 0)
    lo_thr = bisect16(lo_ref, need_lo)
    thr = lax.shift_left(hi_thr, jnp.int32(16)) + (lo_thr + 32768)
    n_gt = count(lambda key, j: key > thr)
    n_ge = count(lambda key, j: key >= thr)

    def index_bisect():
        def index_bit(i, cut):
            cand = cut + (jnp.int32(1) << (idx_bits - 1 - i))
            n_before = count(lambda key, j: (key == thr) & (j * TK + krow < cand))
            return jnp.where(n_gt + n_before < k_top, cand, cut)
        return lax.fori_loop(0, idx_bits, index_bit, jnp.zeros((1, TQ), I32))

    def take_all_ties():
        return jnp.full((1, TQ), seq - 1, I32)

    cut = lax.cond(jnp.max(n_ge) > k_top, index_bisect, take_all_ties)
    sel_ref[0] = jnp.broadcast_to(thr, (LANE, TQ)).T
    sel_ref[1] = jnp.broadcast_to(cut, (LANE, TQ)).T

    for h in range(nh):
        q_sc[h * TQ:(h + 1) * TQ, :] = q_ref[:, h * HEAD_DIM:(h + 1) * HEAD_DIM]
    mx_ref[...] = jnp.full(mx_ref.shape, NEG_INF, F32)
    acc_ref[...] = jnp.zeros(acc_ref.shape, F32)

    def attn_score_step(j, diagonal):
        start = pl.multiple_of(j * TK, TK)
        key = key_ref[j]
        thr_t = _tile_lanes(sel_ref[0], TK // LANE)
        cut_t = _tile_lanes(sel_ref[1], TK // LANE)
        vis = (key > thr_t) | ((key == thr_t) & (j * TK + col <= cut_t))
        if diagonal:
            vis = vis & (col <= row)
        s = (_dot_nt(q_sc[...], kd_ref[pl.ds(start, TK), :]) * SCALE_LOG2E).reshape(nh, TQ, TK)
        s = jnp.where(vis[None], s, NEG_INF).reshape(nh * TQ, TK)
        s_ref[j] = s
        mx_ref[...] = jnp.maximum(mx_ref[...], _max_lanes(s))

    def attn_score_body(j, carry):
        attn_score_step(j, False)
        return carry

    _fori_pairs(qi, attn_score_body, 0)
    attn_score_step(qi, True)
    mx_ref[...] = _row_max_to_lanes(mx_ref[...])

    def pv_body(j, carry):
        start = pl.multiple_of(j * TK, TK)
        acc_ref[...] += _exp_pv(s_ref[j], mx_ref[...], vd_ref[pl.ds(start, TK), :])
        return carry

    _fori_pairs(qi + 1, pv_body, 0)
    a = acc_ref[...]
    out = a[:, :HEAD_DIM] / a[:, HEAD_DIM:]
    for h in range(nh):
        o_ref[:, h * HEAD_DIM:(h + 1) * HEAD_DIM] = out[h * TQ:(h + 1) * TQ].astype(o_ref.dtype)


def _merge_kernel(l_ref, hb_ref, of_ref, on_ref, od_ref, wg1_ref, wg2_ref, wg3_ref, wf_ref, wn_ref, wd_ref, o_ref):
    hb = hb_ref[...]
    mixed = (jax.nn.sigmoid(_dot(hb, wg1_ref[...])) * _dot(of_ref[...], wf_ref[...])
             + jax.nn.sigmoid(_dot(hb, wg2_ref[...])) * _dot(on_ref[...], wn_ref[...])
             + jax.nn.sigmoid(_dot(hb, wg3_ref[...])) * _dot(od_ref[...], wd_ref[...]))
    o_ref[...] = mixed.astype(o_ref.dtype)


LN_SUB_ROWS = 256


def _layer_norm_store(y, g_ref, b_ref, of_ref, ob_ref, rows):
    mu = jnp.mean(y, axis=-1, keepdims=True)
    yc = y - mu
    var = jnp.mean(jnp.square(yc), axis=-1, keepdims=True)
    out = yc * lax.rsqrt(var + 1e-5) * g_ref[...] + b_ref[...]
    of_ref[rows, :] = out
    ob_ref[rows, :] = out.astype(ob_ref.dtype)


def _out_ln_kernel(l_ref, x_ref, w_ref, h_ref, g_ref, b_ref, of_ref, ob_ref):
    w = w_ref[...]
    for r in range(x_ref.shape[0] // LN_SUB_ROWS):
        rows = slice(r * LN_SUB_ROWS, (r + 1) * LN_SUB_ROWS)
        y = ALPHA * h_ref[rows, :] + _dot(x_ref[rows, :], w)
        _layer_norm_store(y, g_ref, b_ref, of_ref, ob_ref, rows)


def _out_ln(name, lidx, x, w, h, g, b):
    t, kdim = x.shape
    d = h.shape[1]
    tm = 512
    return _call(_out_ln_kernel, name, lidx, [x, w, h, g, b], grid=(t // tm,),
                 in_specs=[pl.BlockSpec((tm, kdim), lambda i, l: (i, 0)),
                           pl.BlockSpec((None, kdim, d), lambda i, l: (l[0], 0, 0)),
                           pl.BlockSpec((tm, d), lambda i, l: (i, 0)),
                           pl.BlockSpec((None, 1, d), lambda i, l: (l[0], 0, 0)),
                           pl.BlockSpec((None, 1, d), lambda i, l: (l[0], 0, 0))],
                 out_specs=[pl.BlockSpec((tm, d), lambda i, l: (i, 0)),
                            pl.BlockSpec((tm, d), lambda i, l: (i, 0))],
                 out_shape=[jax.ShapeDtypeStruct((t, d), F32), jax.ShapeDtypeStruct((t, d), BF16)])


def _ffn_ln_kernel(l_ref, x_ref, w_ref, h_ref, g_ref, b_ref, of_ref, ob_ref, *, nk):
    k = pl.program_id(1)

    @pl.when(k == 0)
    def _():
        of_ref[...] = ALPHA * h_ref[...] + _dot(x_ref[...], w_ref[...])

    @pl.when((k > 0) & (k < nk - 1))
    def _():
        of_ref[...] += _dot(x_ref[...], w_ref[...])

    @pl.when(k == nk - 1)
    def _():
        w = w_ref[...]
        for r in range(x_ref.shape[0] // LN_SUB_ROWS):
            rows = slice(r * LN_SUB_ROWS, (r + 1) * LN_SUB_ROWS)
            y = of_ref[rows, :] + _dot(x_ref[rows, :], w)
            _layer_norm_store(y, g_ref, b_ref, of_ref, ob_ref, rows)


def _ffn_ln(name, lidx, x, w, h, g, b, tk):
    t, kdim = x.shape
    d = h.shape[1]
    tm = 1024
    nk = kdim // tk
    assert nk >= 2
    return _call(functools.partial(_ffn_ln_kernel, nk=nk), name, lidx, [x, w, h, g, b],
                 grid=(t // tm, nk),
                 in_specs=[pl.BlockSpec((tm, tk), lambda i, k, l: (i, k)),
                           pl.BlockSpec((None, tk, d), lambda i, k, l: (l[0], k, 0)),
                           pl.BlockSpec((tm, d), lambda i, k, l: (i, 0)),
                           pl.BlockSpec((None, 1, d), lambda i, k, l: (l[0], 0, 0)),
                           pl.BlockSpec((None, 1, d), lambda i, k, l: (l[0], 0, 0))],
                 out_specs=[pl.BlockSpec((tm, d), lambda i, k, l: (i, 0)),
                            pl.BlockSpec((tm, d), lambda i, k, l: (i, 0))],
                 out_shape=[jax.ShapeDtypeStruct((t, d), F32), jax.ShapeDtypeStruct((t, d), BF16)],
                 vmem_limit_bytes=VMEM_LIMIT_LARGE_BYTES)


def _swiglu_kernel(l_ref, x_ref, wa_ref, wb_ref, o_ref, wab_ref):
    @pl.when(pl.program_id(1) == 0)
    def _():
        wab_ref[0] = wa_ref[...].astype(BF16)
        wab_ref[1] = wb_ref[...].astype(BF16)

    x = x_ref[...]
    a = _dot(x, wab_ref[0])
    o_ref[...] = (jax.nn.silu(a) * _dot(x, wab_ref[1])).astype(o_ref.dtype)


def _ple_kernel(l_ref, hb_ref, p_ref, wpi_ref, wpg_ref, h_ref, of_ref, ob_ref):
    p_in = _dot(p_ref[...].astype(BF16), wpi_ref[...].astype(BF16))
    out = h_ref[...] + p_in * jax.nn.sigmoid(_dot(hb_ref[...], wpg_ref[...].astype(BF16)))
    of_ref[...] = out
    ob_ref[...] = out.astype(ob_ref.dtype)


def _rope_tables(n, dim):
    inv = 1.0 / (ROPE_THETA ** (jnp.arange(0, dim, 2, dtype=F32) / dim))
    ang = jnp.arange(n, dtype=F32)[:, None] * inv[None, :]
    return jnp.cos(ang), jnp.sin(ang)


def _layer(lidx, h, hb, consts, weights, bsz, seq):
    (cos128, sin128, cos64, sin64a, sin64b, cmp_cos, cmp_sin, overlap, block_of_key) = consts
    (w_proj, fox_bias, pe_k, pe_v, wk1, wk2, wv1, wv2, kv_norm, kv_up,
     w_br_fox, w_br_nsa, w_br_dsa, w_out, ln1_g, ln1_b, w_ffn_in, w_ffn_out, ln2_g, ln2_b, p, w_ple_in, w_ple_gate) = weights
    t = bsz * seq
    nq = seq // TQ
    d = D_MODEL

    z_a = _proj("proj_plain", lidx, hb, w_proj, COL_PLAIN, N_PLAIN, 2048, 512, BF16, seq)
    z_b = _proj("proj_rope128", lidx, hb, w_proj, COL_ROPE128, N_ROPE128, 2048, 512, BF16, seq,
                rope=((HEAD_DIM // 2,), cos128, sin128))
    z_c = _proj("proj_rope64", lidx, hb, w_proj, COL_ROPE64, N_ROPE64, 2048, 256, BF16, seq,
                rope=((IDX_DIM // 2, LANE - IDX_DIM // 2), cos64, sin64a, sin64b))
    z_d = _proj("proj_f32", lidx, hb, w_proj, COL_F32, N_F32, 1024, N_F32, F32, seq)

    cum = _call(_cum_kernel, "fox_cum", lidx, [z_d, fox_bias], grid=(bsz,),
                in_specs=[pl.BlockSpec((seq, LANE), lambda b, l: (b, ZD_FOX_F)),
                          pl.BlockSpec((None, 1, LANE), lambda b, l: (l[0], 0, 0))],
                out_specs=pl.BlockSpec((seq, LANE), lambda b, l: (b, 0)),
                out_shape=jax.ShapeDtypeStruct((t, LANE), F32))
    cum_rows = jnp.transpose(cum.reshape(bsz, seq, LANE)[:, :, :8], (0, 2, 1)).reshape(bsz, 8 * (seq // TK), TK)
    fq = FOX_HEADS * HEAD_DIM
    o_fox = _call(_fox_kernel, "fox_attn", lidx, [z_a, z_a, z_a, cum, cum_rows], grid=(bsz, nq),
                  in_specs=[pl.BlockSpec((TQ, fq), lambda b, i, l: (b * nq + i, 0)),
                            pl.BlockSpec((seq, fq), lambda b, i, l: (b, 1)),
                            pl.BlockSpec((seq, fq), lambda b, i, l: (b, 2)),
                            pl.BlockSpec((TQ, LANE), lambda b, i, l: (b * nq + i, 0)),
                            pl.BlockSpec((None, 8 * (seq // TK), TK), lambda b, i, l: (b, 0, 0))],
                  out_specs=pl.BlockSpec((TQ, fq), lambda b, i, l: (b * nq + i, 0)),
                  out_shape=jax.ShapeDtypeStruct((t, fq), BF16),
                  scratch_shapes=[pltpu.VMEM((FOX_HEADS, seq // TK, TQ, TK), F32),
                                  pltpu.VMEM((FOX_HEADS, TQ, LANE), F32), pltpu.VMEM((FOX_HEADS, TQ, LANE), F32),
                                  pltpu.VMEM((FOX_HEADS, TQ, 2 * HEAD_DIM), F32)])

    nchunk = seq // CMP_STRIDE
    half = CMP_STRIDE * HEAD_DIM
    r_k = z_d[:, ZD_NSA_KC * LANE:(ZD_NSA_KC + 1) * LANE].reshape(bsz * nchunk, half)
    r_v = z_d[:, ZD_NSA_VC * LANE:(ZD_NSA_VC + 1) * LANE].reshape(bsz * nchunk, half)
    k_cmp, v_cmp = _call(
        _cmp_kernel, "nsa_compress", lidx, [r_k, r_v, pe_k, pe_v, wk1, wk2, wv1, wv2, cmp_cos, cmp_sin], grid=(bsz,),
        in_specs=[pl.BlockSpec((nchunk, half), lambda b, l: (b, 0)),
                  pl.BlockSpec((nchunk, half), lambda b, l: (b, 0)),
                  pl.BlockSpec((None, 2, half), lambda b, l: (l[0], 0, 0)),
                  pl.BlockSpec((None, 2, half), lambda b, l: (l[0], 0, 0)),
                  pl.BlockSpec((None, 2 * half, CMP_HIDDEN), lambda b, l: (l[0], 0, 0)),
                  pl.BlockSpec((None, CMP_HIDDEN, HEAD_DIM), lambda b, l: (l[0], 0, 0)),
                  pl.BlockSpec((None, 2 * half, CMP_HIDDEN), lambda b, l: (l[0], 0, 0)),
                  pl.BlockSpec((None, CMP_HIDDEN, HEAD_DIM), lambda b, l: (l[0], 0, 0)),
                  pl.BlockSpec((nchunk, HEAD_DIM), lambda b, l: (0, 0)),
                  pl.BlockSpec((nchunk, HEAD_DIM), lambda b, l: (0, 0))],
        out_specs=[pl.BlockSpec((nchunk, HEAD_DIM), lambda b, l: (b, 0)),
                   pl.BlockSpec((nchunk, HEAD_DIM), lambda b, l: (b, 0))],
        out_shape=[jax.ShapeDtypeStruct((bsz * nchunk, HEAD_DIM), BF16)] * 2)
    nsq = NSA_HEADS * HEAD_DIM
    o_nsa = _call(
        _nsa_kernel, "nsa_attn", lidx, [z_b, z_b, z_b, z_a, z_a, k_cmp, v_cmp, z_d, overlap, block_of_key], grid=(bsz, nq),
        in_specs=[pl.BlockSpec((TQ, nsq), lambda b, i, l: (b * nq + i, 0)),
                  pl.BlockSpec((seq, LANE), lambda b, i, l: (b, 4)),
                  pl.BlockSpec((seq, LANE), lambda b, i, l: (b, 5)),
                  pl.BlockSpec((seq, LANE), lambda b, i, l: (b, 18)),
                  pl.BlockSpec((seq, LANE), lambda b, i, l: (b, 19)),
                  pl.BlockSpec((nchunk, HEAD_DIM), lambda b, i, l: (b, 0)),
                  pl.BlockSpec((nchunk, HEAD_DIM), lambda b, i, l: (b, 0)),
                  pl.BlockSpec((TQ, LANE), lambda b, i, l: (b * nq + i, ZD_NSA_G)),
                  pl.BlockSpec((N_CMP_PAD, LANE), lambda b, i, l: (0, 0)),
                  pl.BlockSpec((seq, LANE), lambda b, i, l: (0, 0))],
        out_specs=pl.BlockSpec((TQ, nsq), lambda b, i, l: (b * nq + i, 0)),
        out_shape=jax.ShapeDtypeStruct((t, nsq), BF16),
        scratch_shapes=[pltpu.VMEM((NSA_HEADS * TQ, HEAD_DIM), BF16),
                        pltpu.VMEM((seq // TK, NSA_HEADS * TQ, TK), F32),
                        pltpu.VMEM((NSA_HEADS * TQ, LANE), F32),
                        pltpu.VMEM((NSA_HEADS * TQ, 2 * HEAD_DIM), F32)])

    tm_kv = 1024
    k_d, v_d = _call(
        _dsa_kv_kernel, "dsa_kv", lidx, [z_d, kv_norm, kv_up, cos128, sin128], grid=(t // tm_kv,),
        in_specs=[pl.BlockSpec((tm_kv, DSA_KV_RANK), lambda i, l: (i, 0)),
                  pl.BlockSpec((None, 1, DSA_KV_RANK), lambda i, l: (l[0], 0, 0)),
                  pl.BlockSpec((None, DSA_KV_RANK, 2 * HEAD_DIM), lambda i, l: (l[0], 0, 0)),
                  pl.BlockSpec((tm_kv, LANE), lambda i, l: (i % (seq // tm_kv), 0)),
                  pl.BlockSpec((tm_kv, LANE), lambda i, l: (i % (seq // tm_kv), 0))],
        out_specs=[pl.BlockSpec((tm_kv, HEAD_DIM), lambda i, l: (i, 0)),
                   pl.BlockSpec((tm_kv, HEAD_DIM), lambda i, l: (i, 0))],
        out_shape=[jax.ShapeDtypeStruct((t, HEAD_DIM), BF16)] * 2)
    dq = DSA_HEADS * HEAD_DIM
    iqw = IDX_HEADS * IDX_DIM
    o_dsa = _call(
        _dsa_kernel, "dsa_attn", lidx, [z_b, k_d, v_d, z_c, z_c, z_c, z_d], grid=(bsz, nq),
        in_specs=[pl.BlockSpec((TQ, dq), lambda b, i, l: (b * nq + i, 1)),
                  pl.BlockSpec((seq, HEAD_DIM), lambda b, i, l: (b, 0)),
                  pl.BlockSpec((seq, HEAD_DIM), lambda b, i, l: (b, 0)),
                  pl.BlockSpec((TQ, iqw), lambda b, i, l: (b * nq + i, 0)),
                  pl.BlockSpec((seq, LANE), lambda b, i, l: (b, iqw // LANE)),
                  pl.BlockSpec((seq, LANE), lambda b, i, l: (b, iqw // LANE + 1)),
                  pl.BlockSpec((TQ, LANE), lambda b, i, l: (b * nq + i, ZD_IDX_W))],
        out_specs=pl.BlockSpec((TQ, dq), lambda b, i, l: (b * nq + i, 0)),
        out_shape=jax.ShapeDtypeStruct((t, dq), BF16),
        scratch_shapes=[pltpu.VMEM((seq // TK, TQ, TK), I32), pltpu.VMEM((seq // TK, TK, TQ), I32),
                        pltpu.VMEM((seq // TK, TK, TQ), I16), pltpu.VMEM((seq // TK, TK, TQ), I16),
                        pltpu.VMEM((IDX_HEADS, TQ, LANE), F32), pltpu.VMEM((2, TQ, LANE), I32),
                        pltpu.VMEM((DSA_HEADS * TQ, HEAD_DIM), BF16),
                        pltpu.VMEM((seq // TK, DSA_HEADS * TQ, TK), F32),
                        pltpu.VMEM((DSA_HEADS * TQ, LANE), F32),
                        pltpu.VMEM((DSA_HEADS * TQ, 2 * HEAD_DIM), F32)])

    tm, tn = 1024, 512
    ncol = d // tn
    gate0 = COL_GATE // tn
    assert gate0 * tn == COL_GATE
    mixed = _call(
        _merge_kernel, "merge", lidx, [hb, o_fox, o_nsa, o_dsa, w_proj, w_proj, w_proj, w_br_fox, w_br_nsa, w_br_dsa],
        grid=(t // tm, ncol),
        in_specs=[pl.BlockSpec((tm, d), lambda i, j, l: (i, 0)),
                  pl.BlockSpec((tm, fq), lambda i, j, l: (i, 0)),
                  pl.BlockSpec((tm, nsq), lambda i, j, l: (i, 0)),
                  pl.BlockSpec((tm, dq), lambda i, j, l: (i, 0)),
                  pl.BlockSpec((None, d, tn), lambda i, j, l: (l[0], 0, gate0 + j)),
                  pl.BlockSpec((None, d, tn), lambda i, j, l: (l[0], 0, gate0 + ncol + j)),
                  pl.BlockSpec((None, d, tn), lambda i, j, l: (l[0], 0, gate0 + 2 * ncol + j)),
                  pl.BlockSpec((None, fq, tn), lambda i, j, l: (l[0], 0, j)),
                  pl.BlockSpec((None, nsq, tn), lambda i, j, l: (l[0], 0, j)),
                  pl.BlockSpec((None, dq, tn), lambda i, j, l: (l[0], 0, j))],
        out_specs=pl.BlockSpec((tm, tn), lambda i, j, l: (i, j)),
        out_shape=jax.ShapeDtypeStruct((t, d), BF16))

    h, hb = _out_ln("out_ln", lidx, mixed, w_out, h, ln1_g, ln1_b)

    nff = D_FF // tn
    act = _call(
        _swiglu_kernel, "swiglu", lidx, [hb, w_ffn_in, w_ffn_in], grid=(nff, t // tm),
        in_specs=[pl.BlockSpec((tm, d), lambda j, i, l: (i, 0)),
                  pl.BlockSpec((None, d, tn), lambda j, i, l: (l[0], 0, j)),
                  pl.BlockSpec((None, d, tn), lambda j, i, l: (l[0], 0, nff + j))],
        out_specs=pl.BlockSpec((tm, tn), lambda j, i, l: (i, j)),
        out_shape=jax.ShapeDtypeStruct((t, D_FF), BF16),
        scratch_shapes=[pltpu.VMEM((2, d, tn), BF16)])
    h, hb = _ffn_ln("ffn_ln", lidx, act, w_ffn_out, h, ln2_g, ln2_b, tk=512)

    h, hb = _call(
        _ple_kernel, "ple", lidx, [hb, p, w_ple_in, w_ple_gate, h], grid=(t // tm, ncol),
        in_specs=[pl.BlockSpec((tm, d), lambda i, j, l: (i, 0)),
                  pl.BlockSpec((None, tm, PLE_DIM), lambda i, j, l: (l[0], i, 0)),
                  pl.BlockSpec((None, PLE_DIM, tn), lambda i, j, l: (l[0], 0, j)),
                  pl.BlockSpec((None, d, tn), lambda i, j, l: (l[0], 0, j)),
                  pl.BlockSpec((tm, tn), lambda i, j, l: (i, j))],
        out_specs=[pl.BlockSpec((tm, tn), lambda i, j, l: (i, j)),
                   pl.BlockSpec((tm, tn), lambda i, j, l: (i, j))],
        out_shape=[jax.ShapeDtypeStruct((t, d), F32), jax.ShapeDtypeStruct((t, d), BF16)])
    return h, hb


def kernel(x, p, w_in, fox_f_bias, nsa_pe_k, nsa_pe_v, nsa_cmp_k1, nsa_cmp_k2, nsa_cmp_v1, nsa_cmp_v2, dsa_kv_norm, dsa_kv_up, w_br_fox, w_br_nsa, w_br_dsa, w_out, ln1_g, ln1_b, w_ffn_in, w_ffn_out, ln2_g, ln2_b, w_ple_in, w_ple_gate):
    bsz, seq, d = x.shape
    depth = w_in.shape[0]
    t = bsz * seq
    assert d == D_MODEL and seq % 1024 == 0 and depth == DEPTH

    cat = functools.partial(jnp.concatenate, axis=-1)
    w_proj = _regroup(w_in)

    fox_bias = jnp.pad(fox_f_bias, ((0, 0), (0, LANE - FOX_HEADS))).reshape(depth, 1, LANE)
    half = CMP_STRIDE * HEAD_DIM
    weights = (
        w_proj, fox_bias,
        nsa_pe_k.reshape(depth, 2, half), nsa_pe_v.reshape(depth, 2, half),
        nsa_cmp_k1.astype(BF16), nsa_cmp_k2.astype(BF16), nsa_cmp_v1.astype(BF16), nsa_cmp_v2.astype(BF16),
        dsa_kv_norm.reshape(depth, 1, DSA_KV_RANK), dsa_kv_up.astype(BF16),
        w_br_fox.astype(BF16), w_br_nsa.astype(BF16), w_br_dsa.astype(BF16), w_out.astype(BF16),
        ln1_g.reshape(depth, 1, d), ln1_b.reshape(depth, 1, d),
        w_ffn_in, w_ffn_out.astype(BF16),
        ln2_g.reshape(depth, 1, d), ln2_b.reshape(depth, 1, d),
        p.reshape(depth, t, PLE_DIM), w_ple_in, w_ple_gate,
    )

    cos, sin = _rope_tables(seq, HEAD_DIM)
    cos128 = cat([cos, cos])
    sin128 = cat([-sin, sin])
    cos_i, sin_i = _rope_tables(seq, IDX_DIM)
    zi = jnp.zeros_like(sin_i)
    cos64 = cat([cos_i, cos_i, cos_i, cos_i])
    sin64a = cat([zi, sin_i, zi, sin_i])
    sin64b = cat([-sin_i, zi, -sin_i, zi])
    n_cmp = (seq - CMP_LEN) // CMP_STRIDE + 1
    c_end = jnp.minimum(jnp.arange(N_CMP_PAD) * CMP_STRIDE + CMP_LEN - 1, seq - 1)
    cmp_cos, cmp_sin = cos128[c_end], sin128[c_end]
    n_slc = seq // SLC_LEN
    c_start = jnp.arange(N_CMP_PAD) * CMP_STRIDE
    s_start = jnp.arange(LANE) * SLC_LEN
    overlap = jnp.maximum(jnp.minimum(c_start[:, None] + CMP_LEN - 1, s_start[None, :] + SLC_LEN - 1)
                          - jnp.maximum(c_start[:, None], s_start[None, :]) + 1, 0).astype(F32) / CMP_LEN
    overlap = jnp.where((jnp.arange(N_CMP_PAD)[:, None] < n_cmp) & (jnp.arange(LANE)[None, :] < n_slc), overlap, 0.0)
    overlap = overlap.T.astype(BF16)
    block_of_key = (jnp.arange(seq)[:, None] // SLC_LEN == jnp.arange(LANE)[None, :]).astype(BF16)
    consts = (cos128, sin128, cos64, sin64a, sin64b, cmp_cos, cmp_sin, overlap, block_of_key)

    h = x.reshape(t, d)
    hb = h.astype(BF16)
    for layer in range(depth):
        lidx = jnp.full((1,), layer, I32)
        h, hb = _layer(lidx, h, hb, consts, weights, bsz, seq)
    return h.reshape(bsz, seq, d)
```

```python
import functools

import jax
import jax.numpy as jnp
from jax import lax
from jax.experimental import pallas as pl
from jax.experimental.pallas import tpu as pltpu

F32, BF16, I32, I16 = jnp.float32, jnp.bfloat16, jnp.int32, jnp.int16

D_MODEL = 2048
DEPTH = 4
HEAD_DIM = 128
ROPE_THETA = 10000.0
NEG_INF = -1e30
FOX_HEADS = 6
NSA_HEADS = 4
CMP_LEN = 32
CMP_STRIDE = 16
CMP_HIDDEN = 256
SLC_LEN = 64
SLC_TOPN = 16
WIN = 512
SLC_FORCE = 1e4
DSA_HEADS = 6
DSA_KV_RANK = 256
IDX_HEADS = 16
IDX_DIM = 64
IDX_TOPK_MAX = 256
D_FF = ((8 * D_MODEL + 3 * 256 - 1) // (3 * 256)) * 256
PLE_DIM = 256
ALPHA = (2 * DEPTH) ** 0.25
SCALE = HEAD_DIM ** -0.5
LOG2E = 1.4426950408889634
SCALE_LOG2E = SCALE * LOG2E

LANE = 128
VMEM_LIMIT_BYTES = 48 * 1024 * 1024
VMEM_LIMIT_LARGE_BYTES = 56 * 1024 * 1024

_IN_SPLITS = (
    ('fox_q', FOX_HEADS * HEAD_DIM), ('fox_k', FOX_HEADS * HEAD_DIM), ('fox_v', FOX_HEADS * HEAD_DIM), ('fox_f', FOX_HEADS),
    ('nsa_q', NSA_HEADS * HEAD_DIM), ('nsa_kc', HEAD_DIM), ('nsa_vc', HEAD_DIM), ('nsa_ks', HEAD_DIM), ('nsa_vs', HEAD_DIM),
    ('nsa_kw', HEAD_DIM), ('nsa_vw', HEAD_DIM), ('nsa_g', 3 * NSA_HEADS),
    ('dsa_q', DSA_HEADS * HEAD_DIM), ('dsa_ckv', DSA_KV_RANK), ('idx_q', IDX_HEADS * IDX_DIM), ('idx_k', IDX_DIM), ('idx_w', IDX_HEADS),
    ('gate', 3 * D_MODEL),
)
_IN_OFFSETS = {}
_off = 0
for _name, _width in _IN_SPLITS:
    _IN_OFFSETS[_name] = (_off, _off + _width)
    _off += _width

MISC_FOX_F = 0
MISC_NSA_G = FOX_HEADS
MISC_IDX_W = FOX_HEADS + 3 * NSA_HEADS

TQ = 256
TK = 256
N_CMP_PAD = 128


def _dot(a, b):
    return jnp.dot(a, b, preferred_element_type=F32)


def _dot_nt(a, b):
    return lax.dot_general(a, b, (((1,), (1,)), ((), ())), preferred_element_type=F32)


def _fori_pairs(n, body, init):
    def four(jj, carry):
        for u in range(4):
            carry = body(4 * jj + u, carry)
        return carry
    carry = lax.fori_loop(0, lax.shift_right_logical(n, jnp.int32(2)), four, init)
    base = n & jnp.int32(-4)
    carry = lax.cond((n & 2) == 2, lambda c: body(base + 1, body(base, c)), lambda c: c, carry)
    return lax.cond((n & 1) == 1, lambda c: body(n - 1, c), lambda c: c, carry)


def _split3(a):
    a1 = a.astype(BF16)
    r1 = a - a1.astype(F32)
    a2 = r1.astype(BF16)
    a3 = (r1 - a2.astype(F32)).astype(BF16)
    return a1, a2, a3


def _call(kernel, name, lidx, args, grid, in_specs, out_specs, out_shape, scratch_shapes=(),
          vmem_limit_bytes=VMEM_LIMIT_BYTES):
    return pl.pallas_call(
        kernel,
        grid_spec=pltpu.PrefetchScalarGridSpec(
            num_scalar_prefetch=1, grid=grid, in_specs=in_specs, out_specs=out_specs,
            scratch_shapes=list(scratch_shapes)),
        out_shape=out_shape,
        compiler_params=pltpu.CompilerParams(
            dimension_semantics=("arbitrary",) * len(grid), vmem_limit_bytes=vmem_limit_bytes),
        name=name,
    )(lidx, *args)


def _regroup_plan():
    plan = []

    def whole(name):
        a, b = _IN_OFFSETS[name]
        for g in range((b - a) // LANE):
            plan.append((a + g * LANE, 0, LANE))

    whole('dsa_ckv')
    lane0 = {}
    start = 0
    for name in ('fox_f', 'nsa_g', 'idx_w'):
        a, b = _IN_OFFSETS[name]
        lane0[name] = (a - start, start, start + b - a)
        start += b - a
    plan.append(lane0['fox_f'])
    whole('nsa_kc')
    whole('nsa_vc')
    plan.append(lane0['nsa_g'])
    plan.append(lane0['idx_w'])
    plan.append((0, 0, 0))
    for name in ('fox_q', 'fox_k', 'fox_v', 'nsa_vs', 'nsa_vw'):
        whole(name)
    for name in ('nsa_q', 'nsa_ks', 'nsa_kw', 'dsa_q'):
        whole(name)
    whole('gate')
    whole('idx_q')
    ik = _IN_OFFSETS['idx_k'][0]
    plan.append((ik, 0, IDX_DIM))
    plan.append((ik - IDX_DIM, IDX_DIM, LANE))
    return tuple(plan)


_REGROUP_PLAN = _regroup_plan()
COL_F32 = 0
N_F32 = DSA_KV_RANK + 6 * LANE
ZD_FOX_F, ZD_NSA_KC, ZD_NSA_VC, ZD_NSA_G, ZD_IDX_W = 2, 3, 4, 5, 6
COL_PLAIN = COL_F32 + N_F32
N_PLAIN = (3 * FOX_HEADS + 2) * HEAD_DIM
COL_ROPE128 = COL_PLAIN + N_PLAIN
N_ROPE128 = (NSA_HEADS + 2 + DSA_HEADS) * HEAD_DIM
COL_GATE = COL_ROPE128 + N_ROPE128
COL_ROPE64 = COL_GATE + 3 * D_MODEL
N_ROPE64 = IDX_HEADS * IDX_DIM + 2 * LANE
N_REGROUPED = COL_ROPE64 + N_ROPE64
assert N_REGROUPED == len(_REGROUP_PLAN) * LANE


def _regroup_kernel(src_ref, lo_ref, hi_ref, x_ref, o_ref):
    g = pl.program_id(0)
    col = lax.broadcasted_iota(I32, (LANE, x_ref.shape[2]), 0)
    keep = (col >= lo_ref[g]) & (col < hi_ref[g])
    for layer in range(x_ref.shape[1]):
        o_ref[layer] = jnp.where(keep, x_ref[:, layer, :], 0.0).T.astype(o_ref.dtype)


def _regroup(w_in):
    depth, d, n_in = w_in.shape
    w_t = jnp.transpose(w_in, (2, 0, 1))
    src = jnp.array([p[0] for p in _REGROUP_PLAN], I32)
    lo = jnp.array([p[1] for p in _REGROUP_PLAN], I32)
    hi = jnp.array([p[2] for p in _REGROUP_PLAN], I32)
    return pl.pallas_call(
        _regroup_kernel,
        grid_spec=pltpu.PrefetchScalarGridSpec(
            num_scalar_prefetch=3, grid=(len(_REGROUP_PLAN),),
            in_specs=[pl.BlockSpec((pl.Element(LANE), pl.Element(depth), pl.Element(d)),
                                   lambda g, src, lo, hi: (src[g], 0, 0))],
            out_specs=pl.BlockSpec((depth, d, LANE), lambda g, src, lo, hi: (0, 0, g))),
        out_shape=jax.ShapeDtypeStruct((depth, d, N_REGROUPED), BF16),
        compiler_params=pltpu.CompilerParams(dimension_semantics=("arbitrary",), vmem_limit_bytes=VMEM_LIMIT_BYTES),
        name="regroup_w_in",
    )(src, lo, hi, w_t)


def _proj_kernel(l_ref, x_ref, w_ref, *refs, shifts):
    o_ref = refs[-1]
    acc = _dot(x_ref[...], w_ref[...])
    if not shifts:
        o_ref[...] = acc.astype(o_ref.dtype)
        return
    c_ref = refs[0]
    s_refs = refs[1:-1]
    for g in range(acc.shape[1] // LANE):
        xg = acc[:, g * LANE:(g + 1) * LANE]
        out = xg * c_ref[...]
        for shift, s_ref in zip(shifts, s_refs):
            out = out + pltpu.roll(xg, shift, 1) * s_ref[...]
        o_ref[:, g * LANE:(g + 1) * LANE] = out.astype(o_ref.dtype)


def _proj(name, lidx, hb, w, col0, n, tm, tn, out_dtype, seq, rope=None):
    t, d = hb.shape
    jb = col0 // tn
    assert jb * tn == col0
    in_specs = [pl.BlockSpec((tm, d), lambda i, j, l: (i, 0)),
                pl.BlockSpec((None, d, tn), lambda i, j, l: (l[0], 0, jb + j))]
    args = [hb, w]
    shifts = ()
    if rope is not None:
        shifts, tables = rope[0], rope[1:]
        nrow = seq // tm
        for tab in tables:
            in_specs.append(pl.BlockSpec((tm, LANE), lambda i, j, l: (i % nrow, 0)))
            args.append(tab)
    return _call(functools.partial(_proj_kernel, shifts=shifts), name, lidx, args,
                 grid=(t // tm, n // tn), in_specs=in_specs,
                 out_specs=pl.BlockSpec((tm, tn), lambda i, j, l: (i, j)),
                 out_shape=jax.ShapeDtypeStruct((t, n), out_dtype))


def _cum_kernel(l_ref, f_ref, bias_ref, o_ref):
    chunk = 256
    row = lax.broadcasted_iota(I32, (chunk, chunk), 0)
    col = lax.broadcasted_iota(I32, (chunk, chunk), 1)
    tri = jnp.where(row >= col, 1.0, 0.0).astype(BF16)
    carry = jnp.zeros((1, LANE), F32)
    for c in range(f_ref.shape[0] // chunk):
        logf = jax.nn.log_sigmoid(f_ref[c * chunk:(c + 1) * chunk, :] + bias_ref[...])
        l1, l2, l3 = _split3(logf)
        cs = _dot(tri, l1) + _dot(tri, l2) + _dot(tri, l3) + carry
        o_ref[c * chunk:(c + 1) * chunk, :] = cs
        carry = cs[chunk - 1:chunk, :]


def _max_lanes(x):
    out = x[:, 0:LANE]
    for g in range(1, x.shape[1] // LANE):
        out = jnp.maximum(out, x[:, g * LANE:(g + 1) * LANE])
    return out


def _tile_lanes(x, n):
    return jnp.concatenate([x] * n, axis=1)


def _row_max_to_lanes(mx):
    return jnp.broadcast_to(jnp.max(mx, axis=-1, keepdims=True), mx.shape)


def _exp_pv(s, m_lanes, v):
    p = jnp.exp2(s - _tile_lanes(m_lanes, s.shape[1] // LANE)).astype(BF16)
    return _dot(p, jnp.concatenate([v, jnp.ones_like(v)], axis=1))


def _fox_kernel(l_ref, q_ref, k_ref, v_ref, cq_ref, ck_ref, o_ref, s_ref, mx_ref, cqb_ref, acc_ref):
    qi = pl.program_id(1)
    nh = FOX_HEADS
    nchunk = ck_ref.shape[0] // 8
    row = lax.broadcasted_iota(I32, (TQ, TK), 0)
    col = lax.broadcasted_iota(I32, (TQ, TK), 1)
    cq_all = cq_ref[...]
    for h in range(nh):
        cqb_ref[h] = jnp.broadcast_to(cq_all[:, h:h + 1] * LOG2E, (TQ, LANE))
    mx_ref[...] = jnp.full(mx_ref.shape, NEG_INF, F32)
    acc_ref[...] = jnp.zeros(acc_ref.shape, F32)

    def score_step(j, diagonal):
        start = pl.multiple_of(j * TK, TK)
        for h in range(nh):
            hs = slice(h * HEAD_DIM, (h + 1) * HEAD_DIM)
            s = (_dot_nt(q_ref[:, hs], k_ref[pl.ds(start, TK), hs]) * SCALE_LOG2E
                 + _tile_lanes(cqb_ref[h], TK // LANE) - ck_ref[pl.ds(h * nchunk + j, 1), :] * LOG2E)
            if diagonal:
                s = jnp.where(col <= row, s, NEG_INF)
            s_ref[h, j] = s
            mx_ref[h] = jnp.maximum(mx_ref[h], _max_lanes(s))

    def score_body(j, carry):
        score_step(j, False)
        return carry

    _fori_pairs(qi, score_body, 0)
    score_step(qi, True)
    for h in range(nh):
        mx_ref[h] = _row_max_to_lanes(mx_ref[h])

    def pv_body(j, carry):
        start = pl.multiple_of(j * TK, TK)
        for h in range(nh):
            hs = slice(h * HEAD_DIM, (h + 1) * HEAD_DIM)
            acc_ref[h] += _exp_pv(s_ref[h, j], mx_ref[h], v_ref[pl.ds(start, TK), hs])
        return carry

    _fori_pairs(qi + 1, pv_body, 0)
    for h in range(nh):
        a = acc_ref[h]
        o_ref[:, h * HEAD_DIM:(h + 1) * HEAD_DIM] = (a[:, :HEAD_DIM] / a[:, HEAD_DIM:]).astype(o_ref.dtype)


def _cmp_kernel(l_ref, rk_ref, rv_ref, pek_ref, pev_ref, wk1_ref, wk2_ref, wv1_ref, wv2_ref,
                c_ref, s_ref, ko_ref, vo_ref):
    half = CMP_STRIDE * HEAD_DIM

    def compress(r_ref, pe_ref, w1_ref, w2_ref):
        r = r_ref[...]
        lo = (r + pe_ref[0:1, :]).astype(BF16)
        hi = (r + pe_ref[1:2, :]).astype(BF16)
        a = _dot(lo, w1_ref[0:half, :])
        b = _dot(hi, w1_ref[half:2 * half, :])
        hid = a + pltpu.roll(b, N_CMP_PAD - 1, 0)
        return _dot(jax.nn.gelu(hid).astype(BF16), w2_ref[...])

    kc = compress(rk_ref, pek_ref, wk1_ref, wk2_ref)
    kc = kc * c_ref[...] + pltpu.roll(kc, HEAD_DIM // 2, 1) * s_ref[...]
    ko_ref[...] = kc.astype(ko_ref.dtype)
    vo_ref[...] = compress(rv_ref, pev_ref, wv1_ref, wv2_ref).astype(vo_ref.dtype)


def _masked_softmax(s, mask):
    s = jnp.where(mask, s, NEG_INF)
    m = jnp.max(s, axis=-1, keepdims=True)
    e = jnp.where(mask, jnp.exp(s - m), 0.0)
    den = jnp.sum(e, axis=-1, keepdims=True)
    return e, jnp.where(den > 0.0, den, 1.0)


def _nsa_kernel(l_ref, q_ref, ks_ref, kw_ref, vs_ref, vw_ref, kc_ref, vc_ref, g_ref, ovt_ref, et_ref,
                o_ref, q_sc, s_ref, mx_ref, acc_ref):
    qi = pl.program_id(1)
    nh = NSA_HEADS
    for h in range(nh):
        q_sc[h * TQ:(h + 1) * TQ, :] = q_ref[:, h * HEAD_DIM:(h + 1) * HEAD_DIM]
    q = q_sc[...]
    pos = qi * TQ + lax.broadcasted_iota(I32, (TQ, 1), 0)
    lane = lax.broadcasted_iota(I32, (TQ, LANE), 1)

    cvis = (lane * CMP_STRIDE + (CMP_LEN - 1)) <= pos
    s_c = (_dot_nt(q, kc_ref[...]) * SCALE).reshape(nh, TQ, N_CMP_PAD)
    e_c, den_c = _masked_softmax(s_c, cvis[None])
    p_c = e_c / den_c
    o_cmp = _dot(p_c.reshape(nh * TQ, N_CMP_PAD).astype(BF16), vc_ref[...])

    p_sum = p_c[0]
    for h in range(1, nh):
        p_sum = p_sum + p_c[h]
    p1, p2, p3 = _split3(p_sum)
    ovt = ovt_ref[...]
    n_slc = et_ref.shape[0] // SLC_LEN
    imp = (_dot_nt(ovt, p1) + _dot_nt(ovt, p2) + _dot_nt(ovt, p3))[0:n_slc, :]
    blk = lax.broadcasted_iota(I32, (n_slc, TQ), 0)
    pos_t = qi * TQ + lax.broadcasted_iota(I32, (n_slc, TQ), 1)
    cur = lax.shift_right_logical(pos_t, SLC_LEN.bit_length() - 1)
    forced = (blk == 0) | (blk == cur) | (blk == cur - 1)
    imp = jnp.where(forced, SLC_FORCE, jnp.where(blk * SLC_LEN <= pos_t, imp, -SLC_FORCE))
    rank = jnp.zeros((n_slc, TQ), F32)
    for k in range(n_slc):
        ik = imp[k:k + 1, :]
        ahead = (ik > imp) | ((ik == imp) & (blk > k))
        rank = rank + jnp.where(ahead, 1.0, 0.0)
    sel_t = jnp.where(rank < float(min(SLC_TOPN, n_slc)), 1.0, 0.0)
    sel_t = jnp.concatenate([sel_t, jnp.zeros((LANE - n_slc, TQ), F32)], axis=0)
    sel = sel_t.T.astype(BF16)

    mx_ref[...] = jnp.full(mx_ref.shape, NEG_INF, F32)
    acc_ref[...] = jnp.zeros(acc_ref.shape, F32)
    row = lax.broadcasted_iota(I32, (TQ, TK), 0)
    col = lax.broadcasted_iota(I32, (TQ, TK), 1)

    def score_step(j, diagonal):
        start = pl.multiple_of(j * TK, TK)
        vis = _dot_nt(sel, et_ref[pl.ds(start, TK), :]) > 0.5
        if diagonal:
            vis = vis & (col <= row)
        s = (_dot_nt(q_sc[...], ks_ref[pl.ds(start, TK), :]) * SCALE_LOG2E).reshape(nh, TQ, TK)
        s = jnp.where(vis[None], s, NEG_INF).reshape(nh * TQ, TK)
        s_ref[j] = s
        mx_ref[...] = jnp.maximum(mx_ref[...], _max_lanes(s))

    def score_body(j, carry):
        score_step(j, False)
        return carry

    _fori_pairs(qi, score_body, 0)
    score_step(qi, True)
    mx_ref[...] = _row_max_to_lanes(mx_ref[...])

    def pv_body(j, carry):
        start = pl.multiple_of(j * TK, TK)
        acc_ref[...] += _exp_pv(s_ref[j], mx_ref[...], vs_ref[pl.ds(start, TK), :])
        return carry

    _fori_pairs(qi + 1, pv_body, 0)
    a_slc = acc_ref[...]
    o_slc = a_slc[:, :HEAD_DIM] / a_slc[:, HEAD_DIM:]

    band = WIN + TQ
    start_w = pl.multiple_of(jnp.maximum(qi * TQ - WIN, 0), TQ)
    kwb = kw_ref[pl.ds(start_w, band), :]
    vwb = vw_ref[pl.ds(start_w, band), :]
    dist = pos - (start_w + lax.broadcasted_iota(I32, (TQ, band), 1))
    wvis = (dist >= 0) & (dist < WIN)
    s_w = (_dot_nt(q, kwb) * SCALE_LOG2E).reshape(nh, TQ, band)
    s_w = jnp.where(wvis[None], s_w, NEG_INF).reshape(nh * TQ, band)
    a_win = _exp_pv(s_w, _row_max_to_lanes(_max_lanes(s_w)), vwb)
    o_win = a_win[:, :HEAD_DIM] / a_win[:, HEAD_DIM:]

    gate = jax.nn.sigmoid(g_ref[...])
    for h in range(nh):
        rows = slice(h * TQ, (h + 1) * TQ)
        g0 = gate[:, MISC_NSA_G + 3 * h:MISC_NSA_G + 3 * h + 1]
        g1 = gate[:, MISC_NSA_G + 3 * h + 1:MISC_NSA_G + 3 * h + 2]
        g2 = gate[:, MISC_NSA_G + 3 * h + 2:MISC_NSA_G + 3 * h + 3]
        out = g0 * o_cmp[rows] + g1 * o_slc[rows] + g2 * o_win[rows]
        o_ref[:, h * HEAD_DIM:(h + 1) * HEAD_DIM] = out.astype(o_ref.dtype)


def _dsa_kv_kernel(l_ref, x_ref, g_ref, up_ref, c_ref, s_ref, ko_ref, vo_ref):
    x = x_ref[...]
    r = x * lax.rsqrt(jnp.mean(jnp.square(x), axis=-1, keepdims=True) + 1e-6) * g_ref[...]
    kv = _dot(r.astype(BF16), up_ref[...])
    k = kv[:, :HEAD_DIM]
    k = k * c_ref[...] + pltpu.roll(k, HEAD_DIM // 2, 1) * s_ref[...]
    ko_ref[...] = k.astype(ko_ref.dtype)
    vo_ref[...] = kv[:, HEAD_DIM:].astype(vo_ref.dtype)


def _dsa_kernel(l_ref, q_ref, kd_ref, vd_ref, iq_ref, ika_ref, ikb_ref, w_ref, o_ref,
                key_ref, keyt_ref, hi_ref, lo_ref, wb_ref, sel_ref, q_sc, s_ref, mx_ref, acc_ref):
    qi = pl.program_id(1)
    nh = DSA_HEADS
    seq = kd_ref.shape[0]
    k_top = float(min(IDX_TOPK_MAX, seq // 4))
    row = lax.broadcasted_iota(I32, (TQ, TK), 0)
    col = lax.broadcasted_iota(I32, (TQ, TK), 1)
    int_min = jnp.int32(-2 ** 31)
    idx_bits = (seq - 1).bit_length()

    w_all = w_ref[...] * (IDX_HEADS ** -0.5 * IDX_DIM ** -0.5)
    for h in range(IDX_HEADS):
        wb_ref[h] = jnp.broadcast_to(w_all[:, MISC_IDX_W + h:MISC_IDX_W + h + 1], (TQ, LANE))

    def score_step(j, diagonal):
        start = pl.multiple_of(j * TK, TK)
        ka = ika_ref[pl.ds(start, TK), :]
        kb = ikb_ref[pl.ds(start, TK), :]
        sc = jnp.zeros((TQ, TK), F32)
        for g in range(IDX_HEADS // 2):
            qpair = iq_ref[:, g * LANE:(g + 1) * LANE]
            wa = _tile_lanes(wb_ref[2 * g], TK // LANE)
            wb = _tile_lanes(wb_ref[2 * g + 1], TK // LANE)
            sc = sc + jnp.maximum(_dot_nt(qpair, ka), 0.0) * wa + jnp.maximum(_dot_nt(qpair, kb), 0.0) * wb
        if diagonal:
            sc = jnp.where(col <= row, sc, NEG_INF)
        bits = lax.bitcast_convert_type(sc, I32)
        key = jnp.where(bits < 0, bits ^ jnp.int32(0x7FFFFFFF), bits)
        key = jnp.where(sc == 0.0, 0, key)
        key_ref[j] = key
        key_t = key.T
        keyt_ref[j] = key_t
        hi_ref[j] = (key_t >> 16).astype(I16)
        lo_ref[j] = ((key_t & 0xFFFF) - 32768).astype(I16)

    def score_body(j, carry):
        score_step(j, False)
        return carry

    _fori_pairs(qi, score_body, 0)
    score_step(qi, True)

    krow = lax.broadcasted_iota(I32, (TK, TQ), 0)

    def count(pred_fn):
        def body(j, acc):
            hit = jnp.where(pred_fn(keyt_ref[j], j), 1.0, 0.0)
            return acc + jnp.sum(hit.reshape(TK // 8, 8, TQ), axis=0)
        part = _fori_pairs(qi + 1, body, jnp.zeros((8, TQ), F32))
        return jnp.sum(part, axis=0, keepdims=True)

    def count16(ref, cand, strict):
        cand16 = cand.astype(I16)

        def body(j, acc):
            x = ref[j]
            hit = jnp.where(x > cand16 if strict else x >= cand16, jnp.int16(1), jnp.int16(0))
            for g in range(TK // 16):
                acc = acc + hit[g * 16:(g + 1) * 16, :]
            return acc
        part = _fori_pairs(qi + 1, body, jnp.zeros((16, TQ), I16))
        return jnp.sum(part.astype(F32), axis=0, keepdims=True)

    def bisect16(ref, need):
        zero = jnp.zeros((1, TQ), I32)
        v = jnp.where(count16(ref, zero, False) >= need, zero, zero - 32768)

        def bit(i, v):
            cand = v + (jnp.int32(1) << (14 - i))
            return jnp.where(count16(ref, cand, False) >= need, cand, v)
        return lax.fori_loop(0, 15, bit, v)

    hi_thr = bisect16(hi_ref, k_top)
    need_lo = k_top - count16(hi_ref, hi_thr, True)
    hi_thr16 = hi_thr.astype(I16)

    def mask_lo(j, carry):
        lo_ref[j] = jnp.where(hi_ref[j] == hi_thr16, lo_ref[j], jnp.int16(-32768))
        return carry

    _fori_pairs(qi + 1, mask_lo, 0)
    lo_thr = bisect16(lo_ref, need_lo)
    thr = lax.shift_left(hi_thr, jnp.int32(16)) + (lo_thr + 32768)
    n_gt = count(lambda key, j: key > thr)
    n_ge = count(lambda key, j: key >= thr)

    def index_bisect():
        def index_bit(i, cut):
            cand = cut + (jnp.int32(1) << (idx_bits - 1 - i))
            n_before = count(lambda key, j: (key == thr) & (j * TK + krow < cand))
            return jnp.where(n_gt + n_before < k_top, cand, cut)
        return lax.fori_loop(0, idx_bits, index_bit, jnp.zeros((1, TQ), I32))

    def take_all_ties():
        return jnp.full((1, TQ), seq - 1, I32)

    cut = lax.cond(jnp.max(n_ge) > k_top, index_bisect, take_all_ties)
    sel_ref[0] = jnp.broadcast_to(thr, (LANE, TQ)).T
    sel_ref[1] = jnp.broadcast_to(cut, (LANE, TQ)).T

    for h in range(nh):
        q_sc[h * TQ:(h + 1) * TQ, :] = q_ref[:, h * HEAD_DIM:(h + 1) * HEAD_DIM]
    mx_ref[...] = jnp.full(mx_ref.shape, NEG_INF, F32)
    acc_ref[...] = jnp.zeros(acc_ref.shape, F32)

    def attn_score_step(j, diagonal):
        start = pl.multiple_of(j * TK, TK)
        key = key_ref[j]
        thr_t = _tile_lanes(sel_ref[0], TK // LANE)
        cut_t = _tile_lanes(sel_ref[1], TK // LANE)
        vis = (key > thr_t) | ((key == thr_t) & (j * TK + col <= cut_t))
        if diagonal:
            vis = vis & (col <= row)
        s = (_dot_nt(q_sc[...], kd_ref[pl.ds(start, TK), :]) * SCALE_LOG2E).reshape(nh, TQ, TK)
        s = jnp.where(vis[None], s, NEG_INF).reshape(nh * TQ, TK)
        s_ref[j] = s
        mx_ref[...] = jnp.maximum(mx_ref[...], _max_lanes(s))

    def attn_score_body(j, carry):
        attn_score_step(j, False)
        return carry

    _fori_pairs(qi, attn_score_body, 0)
    attn_score_step(qi, True)
    mx_ref[...] = _row_max_to_lanes(mx_ref[...])

    def pv_body(j, carry):
        start = pl.multiple_of(j * TK, TK)
        acc_ref[...] += _exp_pv(s_ref[j], mx_ref[...], vd_ref[pl.ds(start, TK), :])
        return carry

    _fori_pairs(qi + 1, pv_body, 0)
    a = acc_ref[...]
    out = a[:, :HEAD_DIM] / a[:, HEAD_DIM:]
    for h in range(nh):
        o_ref[:, h * HEAD_DIM:(h + 1) * HEAD_DIM] = out[h * TQ:(h + 1) * TQ].astype(o_ref.dtype)


def _merge_kernel(l_ref, hb_ref, of_ref, on_ref, od_ref, wg1_ref, wg2_ref, wg3_ref, wf_ref, wn_ref, wd_ref, o_ref):
    hb = hb_ref[...]
    mixed = (jax.nn.sigmoid(_dot(hb, wg1_ref[...])) * _dot(of_ref[...], wf_ref[...])
             + jax.nn.sigmoid(_dot(hb, wg2_ref[...])) * _dot(on_ref[...], wn_ref[...])
             + jax.nn.sigmoid(_dot(hb, wg3_ref[...])) * _dot(od_ref[...], wd_ref[...]))
    o_ref[...] = mixed.astype(o_ref.dtype)


LN_SUB_ROWS = 256


def _layer_norm_store(y, g_ref, b_ref, of_ref, ob_ref, rows):
    mu = jnp.mean(y, axis=-1, keepdims=True)
    yc = y - mu
    var = jnp.mean(jnp.square(yc), axis=-1, keepdims=True)
    out = yc * lax.rsqrt(var + 1e-5) * g_ref[...] + b_ref[...]
    of_ref[rows, :] = out
    ob_ref[rows, :] = out.astype(ob_ref.dtype)


def _out_ln_kernel(l_ref, x_ref, w_ref, h_ref, g_ref, b_ref, of_ref, ob_ref):
    w = w_ref[...]
    for r in range(x_ref.shape[0] // LN_SUB_ROWS):
        rows = slice(r * LN_SUB_ROWS, (r + 1) * LN_SUB_ROWS)
        y = ALPHA * h_ref[rows, :] + _dot(x_ref[rows, :], w)
        _layer_norm_store(y, g_ref, b_ref, of_ref, ob_ref, rows)


def _out_ln(name, lidx, x, w, h, g, b):
    t, kdim = x.shape
    d = h.shape[1]
    tm = 512
    return _call(_out_ln_kernel, name, lidx, [x, w, h, g, b], grid=(t // tm,),
                 in_specs=[pl.BlockSpec((tm, kdim), lambda i, l: (i, 0)),
                           pl.BlockSpec((None, kdim, d), lambda i, l: (l[0], 0, 0)),
                           pl.BlockSpec((tm, d), lambda i, l: (i, 0)),
                           pl.BlockSpec((None, 1, d), lambda i, l: (l[0], 0, 0)),
                           pl.BlockSpec((None, 1, d), lambda i, l: (l[0], 0, 0))],
                 out_specs=[pl.BlockSpec((tm, d), lambda i, l: (i, 0)),
                            pl.BlockSpec((tm, d), lambda i, l: (i, 0))],
                 out_shape=[jax.ShapeDtypeStruct((t, d), F32), jax.ShapeDtypeStruct((t, d), BF16)])


def _ffn_ln_kernel(l_ref, x_ref, w_ref, h_ref, g_ref, b_ref, of_ref, ob_ref, *, nk):
    k = pl.program_id(1)

    @pl.when(k == 0)
    def _():
        of_ref[...] = ALPHA * h_ref[...] + _dot(x_ref[...], w_ref[...])

    @pl.when((k > 0) & (k < nk - 1))
    def _():
        of_ref[...] += _dot(x_ref[...], w_ref[...])

    @pl.when(k == nk - 1)
    def _():
        w = w_ref[...]
        for r in range(x_ref.shape[0] // LN_SUB_ROWS):
            rows = slice(r * LN_SUB_ROWS, (r + 1) * LN_SUB_ROWS)
            y = of_ref[rows, :] + _dot(x_ref[rows, :], w)
            _layer_norm_store(y, g_ref, b_ref, of_ref, ob_ref, rows)


def _ffn_ln(name, lidx, x, w, h, g, b, tk):
    t, kdim = x.shape
    d = h.shape[1]
    tm = 1024
    nk = kdim // tk
    assert nk >= 2
    return _call(functools.partial(_ffn_ln_kernel, nk=nk), name, lidx, [x, w, h, g, b],
                 grid=(t // tm, nk),
                 in_specs=[pl.BlockSpec((tm, tk), lambda i, k, l: (i, k)),
                           pl.BlockSpec((None, tk, d), lambda i, k, l: (l[0], k, 0)),
                           pl.BlockSpec((tm, d), lambda i, k, l: (i, 0)),
                           pl.BlockSpec((None, 1, d), lambda i, k, l: (l[0], 0, 0)),
                           pl.BlockSpec((None, 1, d), lambda i, k, l: (l[0], 0, 0))],
                 out_specs=[pl.BlockSpec((tm, d), lambda i, k, l: (i, 0)),
                            pl.BlockSpec((tm, d), lambda i, k, l: (i, 0))],
                 out_shape=[jax.ShapeDtypeStruct((t, d), F32), jax.ShapeDtypeStruct((t, d), BF16)],
                 vmem_limit_bytes=VMEM_LIMIT_LARGE_BYTES)


def _swiglu_kernel(l_ref, x_ref, wa_ref, wb_ref, o_ref, wab_ref):
    @pl.when(pl.program_id(1) == 0)
    def _():
        wab_ref[0] = wa_ref[...].astype(BF16)
        wab_ref[1] = wb_ref[...].astype(BF16)

    x = x_ref[...]
    a = _dot(x, wab_ref[0])
    o_ref[...] = (jax.nn.silu(a) * _dot(x, wab_ref[1])).astype(o_ref.dtype)


def _ple_kernel(l_ref, h_ref, p_ref, wpi_ref, wpg_ref, of_ref, ob_ref):
    wpi = wpi_ref[...]
    wpg = wpg_ref[...]
    for r in range(h_ref.shape[0] // LN_SUB_ROWS):
        rows = slice(r * LN_SUB_ROWS, (r + 1) * LN_SUB_ROWS)
        h = h_ref[rows, :]
        p_in = _dot(p_ref[rows, :].astype(BF16), wpi)
        out = h + p_in * jax.nn.sigmoid(_dot(h.astype(BF16), wpg))
        of_ref[rows, :] = out
        ob_ref[rows, :] = out.astype(ob_ref.dtype)


def _rope_tables(n, dim):
    inv = 1.0 / (ROPE_THETA ** (jnp.arange(0, dim, 2, dtype=F32) / dim))
    ang = jnp.arange(n, dtype=F32)[:, None] * inv[None, :]
    return jnp.cos(ang), jnp.sin(ang)


def _layer(lidx, h, hb, consts, weights, bsz, seq):
    (cos128, sin128, cos64, sin64a, sin64b, cmp_cos, cmp_sin, overlap, block_of_key) = consts
    (w_proj, fox_bias, pe_k, pe_v, wk1, wk2, wv1, wv2, kv_norm, kv_up,
     w_br_fox, w_br_nsa, w_br_dsa, w_out, ln1_g, ln1_b, w_ffn_in, w_ffn_out, ln2_g, ln2_b, p, w_ple_in, w_ple_gate) = weights
    t = bsz * seq
    nq = seq // TQ
    d = D_MODEL

    z_a = _proj("proj_plain", lidx, hb, w_proj, COL_PLAIN, N_PLAIN, 2048, 512, BF16, seq)
    z_b = _proj("proj_rope128", lidx, hb, w_proj, COL_ROPE128, N_ROPE128, 2048, 512, BF16, seq,
                rope=((HEAD_DIM // 2,), cos128, sin128))
    z_c = _proj("proj_rope64", lidx, hb, w_proj, COL_ROPE64, N_ROPE64, 2048, 256, BF16, seq,
                rope=((IDX_DIM // 2, LANE - IDX_DIM // 2), cos64, sin64a, sin64b))
    z_d = _proj("proj_f32", lidx, hb, w_proj, COL_F32, N_F32, 1024, N_F32, F32, seq)

    cum = _call(_cum_kernel, "fox_cum", lidx, [z_d, fox_bias], grid=(bsz,),
                in_specs=[pl.BlockSpec((seq, LANE), lambda b, l: (b, ZD_FOX_F)),
                          pl.BlockSpec((None, 1, LANE), lambda b, l: (l[0], 0, 0))],
                out_specs=pl.BlockSpec((seq, LANE), lambda b, l: (b, 0)),
                out_shape=jax.ShapeDtypeStruct((t, LANE), F32))
    cum_rows = jnp.transpose(cum.reshape(bsz, seq, LANE)[:, :, :8], (0, 2, 1)).reshape(bsz, 8 * (seq // TK), TK)
    fq = FOX_HEADS * HEAD_DIM
    o_fox = _call(_fox_kernel, "fox_attn", lidx, [z_a, z_a, z_a, cum, cum_rows], grid=(bsz, nq),
                  in_specs=[pl.BlockSpec((TQ, fq), lambda b, i, l: (b * nq + i, 0)),
                            pl.BlockSpec((seq, fq), lambda b, i, l: (b, 1)),
                            pl.BlockSpec((seq, fq), lambda b, i, l: (b, 2)),
                            pl.BlockSpec((TQ, LANE), lambda b, i, l: (b * nq + i, 0)),
                            pl.BlockSpec((None, 8 * (seq // TK), TK), lambda b, i, l: (b, 0, 0))],
                  out_specs=pl.BlockSpec((TQ, fq), lambda b, i, l: (b * nq + i, 0)),
                  out_shape=jax.ShapeDtypeStruct((t, fq), BF16),
                  scratch_shapes=[pltpu.VMEM((FOX_HEADS, seq // TK, TQ, TK), F32),
                                  pltpu.VMEM((FOX_HEADS, TQ, LANE), F32), pltpu.VMEM((FOX_HEADS, TQ, LANE), F32),
                                  pltpu.VMEM((FOX_HEADS, TQ, 2 * HEAD_DIM), F32)])

    nchunk = seq // CMP_STRIDE
    half = CMP_STRIDE * HEAD_DIM
    r_k = z_d[:, ZD_NSA_KC * LANE:(ZD_NSA_KC + 1) * LANE].reshape(bsz * nchunk, half)
    r_v = z_d[:, ZD_NSA_VC * LANE:(ZD_NSA_VC + 1) * LANE].reshape(bsz * nchunk, half)
    k_cmp, v_cmp = _call(
        _cmp_kernel, "nsa_compress", lidx, [r_k, r_v, pe_k, pe_v, wk1, wk2, wv1, wv2, cmp_cos, cmp_sin], grid=(bsz,),
        in_specs=[pl.BlockSpec((nchunk, half), lambda b, l: (b, 0)),
                  pl.BlockSpec((nchunk, half), lambda b, l: (b, 0)),
                  pl.BlockSpec((None, 2, half), lambda b, l: (l[0], 0, 0)),
                  pl.BlockSpec((None, 2, half), lambda b, l: (l[0], 0, 0)),
                  pl.BlockSpec((None, 2 * half, CMP_HIDDEN), lambda b, l: (l[0], 0, 0)),
                  pl.BlockSpec((None, CMP_HIDDEN, HEAD_DIM), lambda b, l: (l[0], 0, 0)),
                  pl.BlockSpec((None, 2 * half, CMP_HIDDEN), lambda b, l: (l[0], 0, 0)),
                  pl.BlockSpec((None, CMP_HIDDEN, HEAD_DIM), lambda b, l: (l[0], 0, 0)),
                  pl.BlockSpec((nchunk, HEAD_DIM), lambda b, l: (0, 0)),
                  pl.BlockSpec((nchunk, HEAD_DIM), lambda b, l: (0, 0))],
        out_specs=[pl.BlockSpec((nchunk, HEAD_DIM), lambda b, l: (b, 0)),
                   pl.BlockSpec((nchunk, HEAD_DIM), lambda b, l: (b, 0))],
        out_shape=[jax.ShapeDtypeStruct((bsz * nchunk, HEAD_DIM), BF16)] * 2)
    nsq = NSA_HEADS * HEAD_DIM
    o_nsa = _call(
        _nsa_kernel, "nsa_attn", lidx, [z_b, z_b, z_b, z_a, z_a, k_cmp, v_cmp, z_d, overlap, block_of_key], grid=(bsz, nq),
        in_specs=[pl.BlockSpec((TQ, nsq), lambda b, i, l: (b * nq + i, 0)),
                  pl.BlockSpec((seq, LANE), lambda b, i, l: (b, 4)),
                  pl.BlockSpec((seq, LANE), lambda b, i, l: (b, 5)),
                  pl.BlockSpec((seq, LANE), lambda b, i, l: (b, 18)),
                  pl.BlockSpec((seq, LANE), lambda b, i, l: (b, 19)),
                  pl.BlockSpec((nchunk, HEAD_DIM), lambda b, i, l: (b, 0)),
                  pl.BlockSpec((nchunk, HEAD_DIM), lambda b, i, l: (b, 0)),
                  pl.BlockSpec((TQ, LANE), lambda b, i, l: (b * nq + i, ZD_NSA_G)),
                  pl.BlockSpec((N_CMP_PAD, LANE), lambda b, i, l: (0, 0)),
                  pl.BlockSpec((seq, LANE), lambda b, i, l: (0, 0))],
        out_specs=pl.BlockSpec((TQ, nsq), lambda b, i, l: (b * nq + i, 0)),
        out_shape=jax.ShapeDtypeStruct((t, nsq), BF16),
        scratch_shapes=[pltpu.VMEM((NSA_HEADS * TQ, HEAD_DIM), BF16),
                        pltpu.VMEM((seq // TK, NSA_HEADS * TQ, TK), F32),
                        pltpu.VMEM((NSA_HEADS * TQ, LANE), F32),
                        pltpu.VMEM((NSA_HEADS * TQ, 2 * HEAD_DIM), F32)])

    tm_kv = 1024
    k_d, v_d = _call(
        _dsa_kv_kernel, "dsa_kv", lidx, [z_d, kv_norm, kv_up, cos128, sin128], grid=(t // tm_kv,),
        in_specs=[pl.BlockSpec((tm_kv, DSA_KV_RANK), lambda i, l: (i, 0)),
                  pl.BlockSpec((None, 1, DSA_KV_RANK), lambda i, l: (l[0], 0, 0)),
                  pl.BlockSpec((None, DSA_KV_RANK, 2 * HEAD_DIM), lambda i, l: (l[0], 0, 0)),
                  pl.BlockSpec((tm_kv, LANE), lambda i, l: (i % (seq // tm_kv), 0)),
                  pl.BlockSpec((tm_kv, LANE), lambda i, l: (i % (seq // tm_kv), 0))],
        out_specs=[pl.BlockSpec((tm_kv, HEAD_DIM), lambda i, l: (i, 0)),
                   pl.BlockSpec((tm_kv, HEAD_DIM), lambda i, l: (i, 0))],
        out_shape=[jax.ShapeDtypeStruct((t, HEAD_DIM), BF16)] * 2)
    dq = DSA_HEADS * HEAD_DIM
    iqw = IDX_HEADS * IDX_DIM
    o_dsa = _call(
        _dsa_kernel, "dsa_attn", lidx, [z_b, k_d, v_d, z_c, z_c, z_c, z_d], grid=(bsz, nq),
        in_specs=[pl.BlockSpec((TQ, dq), lambda b, i, l: (b * nq + i, 1)),
                  pl.BlockSpec((seq, HEAD_DIM), lambda b, i, l: (b, 0)),
                  pl.BlockSpec((seq, HEAD_DIM), lambda b, i, l: (b, 0)),
                  pl.BlockSpec((TQ, iqw), lambda b, i, l: (b * nq + i, 0)),
                  pl.BlockSpec((seq, LANE), lambda b, i, l: (b, iqw // LANE)),
                  pl.BlockSpec((seq, LANE), lambda b, i, l: (b, iqw // LANE + 1)),
                  pl.BlockSpec((TQ, LANE), lambda b, i, l: (b * nq + i, ZD_IDX_W))],
        out_specs=pl.BlockSpec((TQ, dq), lambda b, i, l: (b * nq + i, 0)),
        out_shape=jax.ShapeDtypeStruct((t, dq), BF16),
        scratch_shapes=[pltpu.VMEM((seq // TK, TQ, TK), I32), pltpu.VMEM((seq // TK, TK, TQ), I32),
                        pltpu.VMEM((seq // TK, TK, TQ), I16), pltpu.VMEM((seq // TK, TK, TQ), I16),
                        pltpu.VMEM((IDX_HEADS, TQ, LANE), F32), pltpu.VMEM((2, TQ, LANE), I32),
                        pltpu.VMEM((DSA_HEADS * TQ, HEAD_DIM), BF16),
                        pltpu.VMEM((seq // TK, DSA_HEADS * TQ, TK), F32),
                        pltpu.VMEM((DSA_HEADS * TQ, LANE), F32),
                        pltpu.VMEM((DSA_HEADS * TQ, 2 * HEAD_DIM), F32)])

    tm, tn = 1024, 512
    ncol = d // tn
    gate0 = COL_GATE // tn
    assert gate0 * tn == COL_GATE
    mixed = _call(
        _merge_kernel, "merge", lidx, [hb, o_fox, o_nsa, o_dsa, w_proj, w_proj, w_proj, w_br_fox, w_br_nsa, w_br_dsa],
        grid=(t // tm, ncol),
        in_specs=[pl.BlockSpec((tm, d), lambda i, j, l: (i, 0)),
                  pl.BlockSpec((tm, fq), lambda i, j, l: (i, 0)),
                  pl.BlockSpec((tm, nsq), lambda i, j, l: (i, 0)),
                  pl.BlockSpec((tm, dq), lambda i, j, l: (i, 0)),
                  pl.BlockSpec((None, d, tn), lambda i, j, l: (l[0], 0, gate0 + j)),
                  pl.BlockSpec((None, d, tn), lambda i, j, l: (l[0], 0, gate0 + ncol + j)),
                  pl.BlockSpec((None, d, tn), lambda i, j, l: (l[0], 0, gate0 + 2 * ncol + j)),
                  pl.BlockSpec((None, fq, tn), lambda i, j, l: (l[0], 0, j)),
                  pl.BlockSpec((None, nsq, tn), lambda i, j, l: (l[0], 0, j)),
                  pl.BlockSpec((None, dq, tn), lambda i, j, l: (l[0], 0, j))],
        out_specs=pl.BlockSpec((tm, tn), lambda i, j, l: (i, j)),
        out_shape=jax.ShapeDtypeStruct((t, d), BF16))

    h, hb = _out_ln("out_ln", lidx, mixed, w_out, h, ln1_g, ln1_b)

    nff = D_FF // tn
    act = _call(
        _swiglu_kernel, "swiglu", lidx, [hb, w_ffn_in, w_ffn_in], grid=(nff, t // tm),
        in_specs=[pl.BlockSpec((tm, d), lambda j, i, l: (i, 0)),
                  pl.BlockSpec((None, d, tn), lambda j, i, l: (l[0], 0, j)),
                  pl.BlockSpec((None, d, tn), lambda j, i, l: (l[0], 0, nff + j))],
        out_specs=pl.BlockSpec((tm, tn), lambda j, i, l: (i, j)),
        out_shape=jax.ShapeDtypeStruct((t, D_FF), BF16),
        scratch_shapes=[pltpu.VMEM((2, d, tn), BF16)])
    h, hb = _ffn_ln("ffn_ln", lidx, act, w_ffn_out, h, ln2_g, ln2_b, tk=512)

    tmp = 512
    h, hb = _call(
        _ple_kernel, "ple", lidx, [h, p, w_ple_in, w_ple_gate], grid=(t // tmp,),
        in_specs=[pl.BlockSpec((tmp, d), lambda i, l: (i, 0)),
                  pl.BlockSpec((None, tmp, PLE_DIM), lambda i, l: (l[0], i, 0)),
                  pl.BlockSpec((None, PLE_DIM, d), lambda i, l: (l[0], 0, 0)),
                  pl.BlockSpec((None, d, d), lambda i, l: (l[0], 0, 0))],
        out_specs=[pl.BlockSpec((tmp, d), lambda i, l: (i, 0)),
                   pl.BlockSpec((tmp, d), lambda i, l: (i, 0))],
        out_shape=[jax.ShapeDtypeStruct((t, d), F32), jax.ShapeDtypeStruct((t, d), BF16)])
    return h, hb


def kernel(x, p, w_in, fox_f_bias, nsa_pe_k, nsa_pe_v, nsa_cmp_k1, nsa_cmp_k2, nsa_cmp_v1, nsa_cmp_v2, dsa_kv_norm, dsa_kv_up, w_br_fox, w_br_nsa, w_br_dsa, w_out, ln1_g, ln1_b, w_ffn_in, w_ffn_out, ln2_g, ln2_b, w_ple_in, w_ple_gate):
    bsz, seq, d = x.shape
    depth = w_in.shape[0]
    t = bsz * seq
    assert d == D_MODEL and seq % 1024 == 0 and depth == DEPTH

    cat = functools.partial(jnp.concatenate, axis=-1)
    w_proj = _regroup(w_in)

    fox_bias = jnp.pad(fox_f_bias, ((0, 0), (0, LANE - FOX_HEADS))).reshape(depth, 1, LANE)
    half = CMP_STRIDE * HEAD_DIM
    weights = (
        w_proj, fox_bias,
        nsa_pe_k.reshape(depth, 2, half), nsa_pe_v.reshape(depth, 2, half),
        nsa_cmp_k1.astype(BF16), nsa_cmp_k2.astype(BF16), nsa_cmp_v1.astype(BF16), nsa_cmp_v2.astype(BF16),
        dsa_kv_norm.reshape(depth, 1, DSA_KV_RANK), dsa_kv_up.astype(BF16),
        w_br_fox.astype(BF16), w_br_nsa.astype(BF16), w_br_dsa.astype(BF16), w_out.astype(BF16),
        ln1_g.reshape(depth, 1, d), ln1_b.reshape(depth, 1, d),
        w_ffn_in, w_ffn_out.astype(BF16),
        ln2_g.reshape(depth, 1, d), ln2_b.reshape(depth, 1, d),
        p.reshape(depth, t, PLE_DIM), w_ple_in.astype(BF16), w_ple_gate.astype(BF16),
    )

    cos, sin = _rope_tables(seq, HEAD_DIM)
    cos128 = cat([cos, cos])
    sin128 = cat([-sin, sin])
    cos_i, sin_i = _rope_tables(seq, IDX_DIM)
    zi = jnp.zeros_like(sin_i)
    cos64 = cat([cos_i, cos_i, cos_i, cos_i])
    sin64a = cat([zi, sin_i, zi, sin_i])
    sin64b = cat([-sin_i, zi, -sin_i, zi])
    n_cmp = (seq - CMP_LEN) // CMP_STRIDE + 1
    c_end = jnp.minimum(jnp.arange(N_CMP_PAD) * CMP_STRIDE + CMP_LEN - 1, seq - 1)
    cmp_cos, cmp_sin = cos128[c_end], sin128[c_end]
    n_slc = seq // SLC_LEN
    c_start = jnp.arange(N_CMP_PAD) * CMP_STRIDE
    s_start = jnp.arange(LANE) * SLC_LEN
    overlap = jnp.maximum(jnp.minimum(c_start[:, None] + CMP_LEN - 1, s_start[None, :] + SLC_LEN - 1)
                          - jnp.maximum(c_start[:, None], s_start[None, :]) + 1, 0).astype(F32) / CMP_LEN
    overlap = jnp.where((jnp.arange(N_CMP_PAD)[:, None] < n_cmp) & (jnp.arange(LANE)[None, :] < n_slc), overlap, 0.0)
    overlap = overlap.T.astype(BF16)
    block_of_key = (jnp.arange(seq)[:, None] // SLC_LEN == jnp.arange(LANE)[None, :]).astype(BF16)
    consts = (cos128, sin128, cos64, sin64a, sin64b, cmp_cos, cmp_sin, overlap, block_of_key)

    h = x.reshape(t, d)
    hb = h.astype(BF16)
    for layer in range(depth):
        lidx = jnp.full((1,), layer, I32)
        h, hb = _layer(lidx, h, hb, consts, weights, bsz, seq)
    return h.reshape(bsz, seq, d)
```

```python
import functools

import jax
import jax.numpy as jnp
from jax import lax
from jax.experimental import pallas as pl
from jax.experimental.pallas import tpu as pltpu

F32, BF16, I32, I16 = jnp.float32, jnp.bfloat16, jnp.int32, jnp.int16

D_MODEL = 2048
DEPTH = 4
HEAD_DIM = 128
ROPE_THETA = 10000.0
NEG_INF = -1e30
FOX_HEADS = 6
NSA_HEADS = 4
CMP_LEN = 32
CMP_STRIDE = 16
CMP_HIDDEN = 256
SLC_LEN = 64
SLC_TOPN = 16
WIN = 512
SLC_FORCE = 1e4
DSA_HEADS = 6
DSA_KV_RANK = 256
IDX_HEADS = 16
IDX_DIM = 64
IDX_TOPK_MAX = 256
D_FF = ((8 * D_MODEL + 3 * 256 - 1) // (3 * 256)) * 256
PLE_DIM = 256
ALPHA = (2 * DEPTH) ** 0.25
SCALE = HEAD_DIM ** -0.5
LOG2E = 1.4426950408889634
SCALE_LOG2E = SCALE * LOG2E

LANE = 128
VMEM_LIMIT_BYTES = 48 * 1024 * 1024
VMEM_LIMIT_LARGE_BYTES = 58 * 1024 * 1024

_IN_SPLITS = (
    ('fox_q', FOX_HEADS * HEAD_DIM), ('fox_k', FOX_HEADS * HEAD_DIM), ('fox_v', FOX_HEADS * HEAD_DIM), ('fox_f', FOX_HEADS),
    ('nsa_q', NSA_HEADS * HEAD_DIM), ('nsa_kc', HEAD_DIM), ('nsa_vc', HEAD_DIM), ('nsa_ks', HEAD_DIM), ('nsa_vs', HEAD_DIM),
    ('nsa_kw', HEAD_DIM), ('nsa_vw', HEAD_DIM), ('nsa_g', 3 * NSA_HEADS),
    ('dsa_q', DSA_HEADS * HEAD_DIM), ('dsa_ckv', DSA_KV_RANK), ('idx_q', IDX_HEADS * IDX_DIM), ('idx_k', IDX_DIM), ('idx_w', IDX_HEADS),
    ('gate', 3 * D_MODEL),
)
_IN_OFFSETS = {}
_off = 0
for _name, _width in _IN_SPLITS:
    _IN_OFFSETS[_name] = (_off, _off + _width)
    _off += _width

MISC_FOX_F = 0
MISC_NSA_G = FOX_HEADS
MISC_IDX_W = FOX_HEADS + 3 * NSA_HEADS

TQ = 256
TK = 256
N_CMP_PAD = 128


def _dot(a, b):
    return jnp.dot(a, b, preferred_element_type=F32)


def _dot_nt(a, b):
    return lax.dot_general(a, b, (((1,), (1,)), ((), ())), preferred_element_type=F32)


def _fori_pairs(n, body, init):
    def four(jj, carry):
        for u in range(4):
            carry = body(4 * jj + u, carry)
        return carry
    carry = lax.fori_loop(0, lax.shift_right_logical(n, jnp.int32(2)), four, init)
    base = n & jnp.int32(-4)
    carry = lax.cond((n & 2) == 2, lambda c: body(base + 1, body(base, c)), lambda c: c, carry)
    return lax.cond((n & 1) == 1, lambda c: body(n - 1, c), lambda c: c, carry)


def _split3(a):
    a1 = a.astype(BF16)
    r1 = a - a1.astype(F32)
    a2 = r1.astype(BF16)
    a3 = (r1 - a2.astype(F32)).astype(BF16)
    return a1, a2, a3


def _call(kernel, name, lidx, args, grid, in_specs, out_specs, out_shape, scratch_shapes=(),
          vmem_limit_bytes=VMEM_LIMIT_BYTES):
    return pl.pallas_call(
        kernel,
        grid_spec=pltpu.PrefetchScalarGridSpec(
            num_scalar_prefetch=1, grid=grid, in_specs=in_specs, out_specs=out_specs,
            scratch_shapes=list(scratch_shapes)),
        out_shape=out_shape,
        compiler_params=pltpu.CompilerParams(
            dimension_semantics=("arbitrary",) * len(grid), vmem_limit_bytes=vmem_limit_bytes),
        name=name,
    )(lidx, *args)


def _regroup_plan():
    plan = []

    def whole(name):
        a, b = _IN_OFFSETS[name]
        for g in range((b - a) // LANE):
            plan.append((a + g * LANE, 0, LANE))

    whole('dsa_ckv')
    lane0 = {}
    start = 0
    for name in ('fox_f', 'nsa_g', 'idx_w'):
        a, b = _IN_OFFSETS[name]
        lane0[name] = (a - start, start, start + b - a)
        start += b - a
    plan.append(lane0['fox_f'])
    whole('nsa_kc')
    whole('nsa_vc')
    plan.append(lane0['nsa_g'])
    plan.append(lane0['idx_w'])
    plan.append((0, 0, 0))
    for name in ('fox_q', 'fox_k', 'fox_v', 'nsa_vs', 'nsa_vw'):
        whole(name)
    for name in ('nsa_q', 'nsa_ks', 'nsa_kw', 'dsa_q'):
        whole(name)
    whole('gate')
    whole('idx_q')
    ik = _IN_OFFSETS['idx_k'][0]
    plan.append((ik, 0, IDX_DIM))
    plan.append((ik - IDX_DIM, IDX_DIM, LANE))
    return tuple(plan)


_REGROUP_PLAN = _regroup_plan()
COL_F32 = 0
N_F32 = DSA_KV_RANK + 6 * LANE
ZD_FOX_F, ZD_NSA_KC, ZD_NSA_VC, ZD_NSA_G, ZD_IDX_W = 2, 3, 4, 5, 6
COL_PLAIN = COL_F32 + N_F32
N_PLAIN = (3 * FOX_HEADS + 2) * HEAD_DIM
COL_ROPE128 = COL_PLAIN + N_PLAIN
N_ROPE128 = (NSA_HEADS + 2 + DSA_HEADS) * HEAD_DIM
COL_GATE = COL_ROPE128 + N_ROPE128
COL_ROPE64 = COL_GATE + 3 * D_MODEL
N_ROPE64 = IDX_HEADS * IDX_DIM + 2 * LANE
N_REGROUPED = COL_ROPE64 + N_ROPE64
assert N_REGROUPED == len(_REGROUP_PLAN) * LANE


def _regroup_kernel(src_ref, lo_ref, hi_ref, x_ref, o_ref):
    g = pl.program_id(0)
    col = lax.broadcasted_iota(I32, (LANE, x_ref.shape[2]), 0)
    keep = (col >= lo_ref[g]) & (col < hi_ref[g])
    for layer in range(x_ref.shape[1]):
        o_ref[layer] = jnp.where(keep, x_ref[:, layer, :], 0.0).T.astype(o_ref.dtype)


def _regroup(w_in):
    depth, d, n_in = w_in.shape
    w_t = jnp.transpose(w_in, (2, 0, 1))
    src = jnp.array([p[0] for p in _REGROUP_PLAN], I32)
    lo = jnp.array([p[1] for p in _REGROUP_PLAN], I32)
    hi = jnp.array([p[2] for p in _REGROUP_PLAN], I32)
    return pl.pallas_call(
        _regroup_kernel,
        grid_spec=pltpu.PrefetchScalarGridSpec(
            num_scalar_prefetch=3, grid=(len(_REGROUP_PLAN),),
            in_specs=[pl.BlockSpec((pl.Element(LANE), pl.Element(depth), pl.Element(d)),
                                   lambda g, src, lo, hi: (src[g], 0, 0))],
            out_specs=pl.BlockSpec((depth, d, LANE), lambda g, src, lo, hi: (0, 0, g))),
        out_shape=jax.ShapeDtypeStruct((depth, d, N_REGROUPED), BF16),
        compiler_params=pltpu.CompilerParams(dimension_semantics=("arbitrary",), vmem_limit_bytes=VMEM_LIMIT_BYTES),
        name="regroup_w_in",
    )(src, lo, hi, w_t)


def _proj_kernel(l_ref, x_ref, w_ref, *refs, shifts):
    o_ref = refs[-1]
    acc = _dot(x_ref[...], w_ref[...])
    if not shifts:
        o_ref[...] = acc.astype(o_ref.dtype)
        return
    c_ref = refs[0]
    s_refs = refs[1:-1]
    for g in range(acc.shape[1] // LANE):
        xg = acc[:, g * LANE:(g + 1) * LANE]
        out = xg * c_ref[...]
        for shift, s_ref in zip(shifts, s_refs):
            out = out + pltpu.roll(xg, shift, 1) * s_ref[...]
        o_ref[:, g * LANE:(g + 1) * LANE] = out.astype(o_ref.dtype)


def _proj(name, lidx, hb, w, col0, n, tm, tn, out_dtype, seq, rope=None):
    t, d = hb.shape
    jb = col0 // tn
    assert jb * tn == col0
    in_specs = [pl.BlockSpec((tm, d), lambda i, j, l: (i, 0)),
                pl.BlockSpec((None, d, tn), lambda i, j, l: (l[0], 0, jb + j))]
    args = [hb, w]
    shifts = ()
    if rope is not None:
        shifts, tables = rope[0], rope[1:]
        nrow = seq // tm
        for tab in tables:
            in_specs.append(pl.BlockSpec((tm, LANE), lambda i, j, l: (i % nrow, 0)))
            args.append(tab)
    return _call(functools.partial(_proj_kernel, shifts=shifts), name, lidx, args,
                 grid=(t // tm, n // tn), in_specs=in_specs,
                 out_specs=pl.BlockSpec((tm, tn), lambda i, j, l: (i, j)),
                 out_shape=jax.ShapeDtypeStruct((t, n), out_dtype))


def _cum_kernel(l_ref, f_ref, bias_ref, o_ref):
    chunk = 256
    row = lax.broadcasted_iota(I32, (chunk, chunk), 0)
    col = lax.broadcasted_iota(I32, (chunk, chunk), 1)
    tri = jnp.where(row >= col, 1.0, 0.0).astype(BF16)
    carry = jnp.zeros((1, LANE), F32)
    for c in range(f_ref.shape[0] // chunk):
        logf = jax.nn.log_sigmoid(f_ref[c * chunk:(c + 1) * chunk, :] + bias_ref[...])
        l1, l2, l3 = _split3(logf)
        cs = _dot(tri, l1) + _dot(tri, l2) + _dot(tri, l3) + carry
        o_ref[c * chunk:(c + 1) * chunk, :] = cs
        carry = cs[chunk - 1:chunk, :]


def _max_lanes(x):
    out = x[:, 0:LANE]
    for g in range(1, x.shape[1] // LANE):
        out = jnp.maximum(out, x[:, g * LANE:(g + 1) * LANE])
    return out


def _tile_lanes(x, n):
    return jnp.concatenate([x] * n, axis=1)


def _row_max_to_lanes(mx):
    return jnp.broadcast_to(jnp.max(mx, axis=-1, keepdims=True), mx.shape)


def _exp_pv(s, m_lanes, v):
    p = jnp.exp2(s - _tile_lanes(m_lanes, s.shape[1] // LANE)).astype(BF16)
    return _dot(p, jnp.concatenate([v, jnp.ones_like(v)], axis=1))


def _fox_kernel(l_ref, q_ref, k_ref, v_ref, cq_ref, ck_ref, o_ref, s_ref, mx_ref, cqb_ref, acc_ref):
    qi = pl.program_id(1)
    nh = FOX_HEADS
    nchunk = ck_ref.shape[0] // 8
    row = lax.broadcasted_iota(I32, (TQ, TK), 0)
    col = lax.broadcasted_iota(I32, (TQ, TK), 1)
    cq_all = cq_ref[...]
    for h in range(nh):
        cqb_ref[h] = jnp.broadcast_to(cq_all[:, h:h + 1] * LOG2E, (TQ, LANE))
    mx_ref[...] = jnp.full(mx_ref.shape, NEG_INF, F32)
    acc_ref[...] = jnp.zeros(acc_ref.shape, F32)

    def score_step(j, diagonal):
        start = pl.multiple_of(j * TK, TK)
        for h in range(nh):
            hs = slice(h * HEAD_DIM, (h + 1) * HEAD_DIM)
            s = (_dot_nt(q_ref[:, hs], k_ref[pl.ds(start, TK), hs]) * SCALE_LOG2E
                 + _tile_lanes(cqb_ref[h], TK // LANE) - ck_ref[pl.ds(h * nchunk + j, 1), :] * LOG2E)
            if diagonal:
                s = jnp.where(col <= row, s, NEG_INF)
            s_ref[h, j] = s
            mx_ref[h] = jnp.maximum(mx_ref[h], _max_lanes(s))

    def score_body(j, carry):
        score_step(j, False)
        return carry

    _fori_pairs(qi, score_body, 0)
    score_step(qi, True)
    for h in range(nh):
        mx_ref[h] = _row_max_to_lanes(mx_ref[h])

    def pv_body(j, carry):
        start = pl.multiple_of(j * TK, TK)
        for h in range(nh):
            hs = slice(h * HEAD_DIM, (h + 1) * HEAD_DIM)
            acc_ref[h] += _exp_pv(s_ref[h, j], mx_ref[h], v_ref[pl.ds(start, TK), hs])
        return carry

    _fori_pairs(qi + 1, pv_body, 0)
    for h in range(nh):
        a = acc_ref[h]
        o_ref[:, h * HEAD_DIM:(h + 1) * HEAD_DIM] = (a[:, :HEAD_DIM] / a[:, HEAD_DIM:]).astype(o_ref.dtype)


def _cmp_kernel(l_ref, rk_ref, rv_ref, pek_ref, pev_ref, wk1_ref, wk2_ref, wv1_ref, wv2_ref,
                c_ref, s_ref, ko_ref, vo_ref):
    half = CMP_STRIDE * HEAD_DIM

    def compress(r_ref, pe_ref, w1_ref, w2_ref):
        r = r_ref[...]
        lo = (r + pe_ref[0:1, :]).astype(BF16)
        hi = (r + pe_ref[1:2, :]).astype(BF16)
        a = _dot(lo, w1_ref[0:half, :])
        b = _dot(hi, w1_ref[half:2 * half, :])
        hid = a + pltpu.roll(b, N_CMP_PAD - 1, 0)
        return _dot(jax.nn.gelu(hid).astype(BF16), w2_ref[...])

    kc = compress(rk_ref, pek_ref, wk1_ref, wk2_ref)
    kc = kc * c_ref[...] + pltpu.roll(kc, HEAD_DIM // 2, 1) * s_ref[...]
    ko_ref[...] = kc.astype(ko_ref.dtype)
    vo_ref[...] = compress(rv_ref, pev_ref, wv1_ref, wv2_ref).astype(vo_ref.dtype)


def _masked_softmax(s, mask):
    s = jnp.where(mask, s, NEG_INF)
    m = jnp.max(s, axis=-1, keepdims=True)
    e = jnp.where(mask, jnp.exp(s - m), 0.0)
    den = jnp.sum(e, axis=-1, keepdims=True)
    return e, jnp.where(den > 0.0, den, 1.0)


def _nsa_kernel(l_ref, q_ref, ks_ref, kw_ref, vs_ref, vw_ref, kc_ref, vc_ref, g_ref, ovt_ref, et_ref,
                o_ref, q_sc, s_ref, mx_ref, acc_ref):
    qi = pl.program_id(1)
    nh = NSA_HEADS
    for h in range(nh):
        q_sc[h * TQ:(h + 1) * TQ, :] = q_ref[:, h * HEAD_DIM:(h + 1) * HEAD_DIM]
    q = q_sc[...]
    pos = qi * TQ + lax.broadcasted_iota(I32, (TQ, 1), 0)
    lane = lax.broadcasted_iota(I32, (TQ, LANE), 1)

    cvis = (lane * CMP_STRIDE + (CMP_LEN - 1)) <= pos
    s_c = (_dot_nt(q, kc_ref[...]) * SCALE).reshape(nh, TQ, N_CMP_PAD)
    e_c, den_c = _masked_softmax(s_c, cvis[None])
    p_c = e_c / den_c
    o_cmp = _dot(p_c.reshape(nh * TQ, N_CMP_PAD).astype(BF16), vc_ref[...])

    p_sum = p_c[0]
    for h in range(1, nh):
        p_sum = p_sum + p_c[h]
    p1, p2, p3 = _split3(p_sum)
    ovt = ovt_ref[...]
    n_slc = et_ref.shape[0] // SLC_LEN
    imp = (_dot_nt(ovt, p1) + _dot_nt(ovt, p2) + _dot_nt(ovt, p3))[0:n_slc, :]
    blk = lax.broadcasted_iota(I32, (n_slc, TQ), 0)
    pos_t = qi * TQ + lax.broadcasted_iota(I32, (n_slc, TQ), 1)
    cur = lax.shift_right_logical(pos_t, SLC_LEN.bit_length() - 1)
    forced = (blk == 0) | (blk == cur) | (blk == cur - 1)
    imp = jnp.where(forced, SLC_FORCE, jnp.where(blk * SLC_LEN <= pos_t, imp, -SLC_FORCE))
    rank = jnp.zeros((n_slc, TQ), F32)
    for k in range(n_slc):
        ik = imp[k:k + 1, :]
        ahead = (ik > imp) | ((ik == imp) & (blk > k))
        rank = rank + jnp.where(ahead, 1.0, 0.0)
    sel_t = jnp.where(rank < float(min(SLC_TOPN, n_slc)), 1.0, 0.0)
    sel_t = jnp.concatenate([sel_t, jnp.zeros((LANE - n_slc, TQ), F32)], axis=0)
    sel = sel_t.T.astype(BF16)

    mx_ref[...] = jnp.full(mx_ref.shape, NEG_INF, F32)
    acc_ref[...] = jnp.zeros(acc_ref.shape, F32)
    row = lax.broadcasted_iota(I32, (TQ, TK), 0)
    col = lax.broadcasted_iota(I32, (TQ, TK), 1)

    def score_step(j, diagonal):
        start = pl.multiple_of(j * TK, TK)
        vis = _dot_nt(sel, et_ref[pl.ds(start, TK), :]) > 0.5
        if diagonal:
            vis = vis & (col <= row)
        s = (_dot_nt(q_sc[...], ks_ref[pl.ds(start, TK), :]) * SCALE_LOG2E).reshape(nh, TQ, TK)
        s = jnp.where(vis[None], s, NEG_INF).reshape(nh * TQ, TK)
        s_ref[j] = s
        mx_ref[...] = jnp.maximum(mx_ref[...], _max_lanes(s))

    def score_body(j, carry):
        score_step(j, False)
        return carry

    _fori_pairs(qi, score_body, 0)
    score_step(qi, True)
    mx_ref[...] = _row_max_to_lanes(mx_ref[...])

    def pv_body(j, carry):
        start = pl.multiple_of(j * TK, TK)
        acc_ref[...] += _exp_pv(s_ref[j], mx_ref[...], vs_ref[pl.ds(start, TK), :])
        return carry

    _fori_pairs(qi + 1, pv_body, 0)
    a_slc = acc_ref[...]
    o_slc = a_slc[:, :HEAD_DIM] / a_slc[:, HEAD_DIM:]

    band = WIN + TQ
    start_w = pl.multiple_of(jnp.maximum(qi * TQ - WIN, 0), TQ)
    kwb = kw_ref[pl.ds(start_w, band), :]
    vwb = vw_ref[pl.ds(start_w, band), :]
    dist = pos - (start_w + lax.broadcasted_iota(I32, (TQ, band), 1))
    wvis = (dist >= 0) & (dist < WIN)
    s_w = (_dot_nt(q, kwb) * SCALE_LOG2E).reshape(nh, TQ, band)
    s_w = jnp.where(wvis[None], s_w, NEG_INF).reshape(nh * TQ, band)
    a_win = _exp_pv(s_w, _row_max_to_lanes(_max_lanes(s_w)), vwb)
    o_win = a_win[:, :HEAD_DIM] / a_win[:, HEAD_DIM:]

    gate = jax.nn.sigmoid(g_ref[...])
    for h in range(nh):
        rows = slice(h * TQ, (h + 1) * TQ)
        g0 = gate[:, MISC_NSA_G + 3 * h:MISC_NSA_G + 3 * h + 1]
        g1 = gate[:, MISC_NSA_G + 3 * h + 1:MISC_NSA_G + 3 * h + 2]
        g2 = gate[:, MISC_NSA_G + 3 * h + 2:MISC_NSA_G + 3 * h + 3]
        out = g0 * o_cmp[rows] + g1 * o_slc[rows] + g2 * o_win[rows]
        o_ref[:, h * HEAD_DIM:(h + 1) * HEAD_DIM] = out.astype(o_ref.dtype)


def _dsa_kv_kernel(l_ref, x_ref, g_ref, up_ref, c_ref, s_ref, ko_ref, vo_ref):
    x = x_ref[...]
    r = x * lax.rsqrt(jnp.mean(jnp.square(x), axis=-1, keepdims=True) + 1e-6) * g_ref[...]
    kv = _dot(r.astype(BF16), up_ref[...])
    k = kv[:, :HEAD_DIM]
    k = k * c_ref[...] + pltpu.roll(k, HEAD_DIM // 2, 1) * s_ref[...]
    ko_ref[...] = k.astype(ko_ref.dtype)
    vo_ref[...] = kv[:, HEAD_DIM:].astype(vo_ref.dtype)


def _dsa_kernel(l_ref, q_ref, kd_ref, vd_ref, iq_ref, ika_ref, ikb_ref, w_ref, o_ref,
                key_ref, keyt_ref, hi_ref, lo_ref, wb_ref, sel_ref, q_sc, s_ref, mx_ref, acc_ref):
    qi = pl.program_id(1)
    nh = DSA_HEADS
    seq = kd_ref.shape[0]
    k_top = float(min(IDX_TOPK_MAX, seq // 4))
    row = lax.broadcasted_iota(I32, (TQ, TK), 0)
    col = lax.broadcasted_iota(I32, (TQ, TK), 1)
    int_min = jnp.int32(-2 ** 31)
    idx_bits = (seq - 1).bit_length()

    w_all = w_ref[...] * (IDX_HEADS ** -0.5 * IDX_DIM ** -0.5)
    for h in range(IDX_HEADS):
        wb_ref[h] = jnp.broadcast_to(w_all[:, MISC_IDX_W + h:MISC_IDX_W + h + 1], (TQ, LANE))

    def score_step(j, diagonal):
        start = pl.multiple_of(j * TK, TK)
        ka = ika_ref[pl.ds(start, TK), :]
        kb = ikb_ref[pl.ds(start, TK), :]
        sc = jnp.zeros((TQ, TK), F32)
        for g in range(IDX_HEADS // 2):
            qpair = iq_ref[:, g * LANE:(g + 1) * LANE]
            wa = _tile_lanes(wb_ref[2 * g], TK // LANE)
            wb = _tile_lanes(wb_ref[2 * g + 1], TK // LANE)
            sc = sc + jnp.maximum(_dot_nt(qpair, ka), 0.0) * wa + jnp.maximum(_dot_nt(qpair, kb), 0.0) * wb
        if diagonal:
            sc = jnp.where(col <= row, sc, NEG_INF)
        bits = lax.bitcast_convert_type(sc, I32)
        key = jnp.where(bits < 0, bits ^ jnp.int32(0x7FFFFFFF), bits)
        key = jnp.where(sc == 0.0, 0, key)
        key_ref[j] = key
        key_t = key.T
        keyt_ref[j] = key_t
        hi_ref[j] = (key_t >> 16).astype(I16)
        lo_ref[j] = ((key_t & 0xFFFF) - 32768).astype(I16)

    def score_body(j, carry):
        score_step(j, False)
        return carry

    _fori_pairs(qi, score_body, 0)
    score_step(qi, True)

    krow = lax.broadcasted_iota(I32, (TK, TQ), 0)

    def count(pred_fn):
        def body(j, acc):
            hit = jnp.where(pred_fn(keyt_ref[j], j), 1.0, 0.0)
            return acc + jnp.sum(hit.reshape(TK // 8, 8, TQ), axis=0)
        part = _fori_pairs(qi + 1, body, jnp.zeros((8, TQ), F32))
        return jnp.sum(part, axis=0, keepdims=True)

    def count16(ref, cand, strict):
        cand16 = cand.astype(I16)

        def body(j, acc):
            x = ref[j]
            hit = jnp.where(x > cand16 if strict else x >= cand16, jnp.int16(1), jnp.int16(0))
            for g in range(TK // 16):
                acc = acc + hit[g * 16:(g + 1) * 16, :]
            return acc
        part = _fori_pairs(qi + 1, body, jnp.zeros((16, TQ), I16))
        return jnp.sum(part.astype(F32), axis=0, keepdims=True)

    def bisect16(ref, need):
        zero = jnp.zeros((1, TQ), I32)
        v = jnp.where(count16(ref, zero, False) >= need, zero, zero - 32768)

        def bit(i, v):
            cand = v + (jnp.int32(1) << (14 - i))
            return jnp.where(count16(ref, cand, False) >= need, cand, v)
        return lax.fori_loop(0, 15, bit, v)

    hi_thr = bisect16(hi_ref, k_top)
    need_lo = k_top - count16(hi_ref, hi_thr, True)
    hi_thr16 = hi_thr.astype(I16)

    def mask_lo(j, carry):
        lo_ref[j] = jnp.where(hi_ref[j] == hi_thr16, lo_ref[j], jnp.int16(-32768))
        return carry

    _fori_pairs(qi + 1, mask_lo, 0)
    lo_thr = bisect16(lo_ref, need_lo)
    thr = lax.shift_left(hi_thr, jnp.int32(16)) + (lo_thr + 32768)
    n_gt = count(lambda key, j: key > thr)
    n_ge = count(lambda key, j: key >= thr)

    def index_bisect():
        def index_bit(i, cut):
            cand = cut + (jnp.int32(1) << (idx_bits - 1 - i))
            n_before = count(lambda key, j: (key == thr) & (j * TK + krow < cand))
            return jnp.where(n_gt + n_before < k_top, cand, cut)
        return lax.fori_loop(0, idx_bits, index_bit, jnp.zeros((1, TQ), I32))

    def take_all_ties():
        return jnp.full((1, TQ), seq - 1, I32)

    cut = lax.cond(jnp.max(n_ge) > k_top, index_bisect, take_all_ties)
    sel_ref[0] = jnp.broadcast_to(thr, (LANE, TQ)).T
    sel_ref[1] = jnp.broadcast_to(cut, (LANE, TQ)).T

    for h in range(nh):
        q_sc[h * TQ:(h + 1) * TQ, :] = q_ref[:, h * HEAD_DIM:(h + 1) * HEAD_DIM]
    mx_ref[...] = jnp.full(mx_ref.shape, NEG_INF, F32)
    acc_ref[...] = jnp.zeros(acc_ref.shape, F32)

    def attn_score_step(j, diagonal):
        start = pl.multiple_of(j * TK, TK)
        key = key_ref[j]
        thr_t = _tile_lanes(sel_ref[0], TK // LANE)
        cut_t = _tile_lanes(sel_ref[1], TK // LANE)
        vis = (key > thr_t) | ((key == thr_t) & (j * TK + col <= cut_t))
        if diagonal:
            vis = vis & (col <= row)
        s = (_dot_nt(q_sc[...], kd_ref[pl.ds(start, TK), :]) * SCALE_LOG2E).reshape(nh, TQ, TK)
        s = jnp.where(vis[None], s, NEG_INF).reshape(nh * TQ, TK)
        s_ref[j] = s
        mx_ref[...] = jnp.maximum(mx_ref[...], _max_lanes(s))

    def attn_score_body(j, carry):
        attn_score_step(j, False)
        return carry

    _fori_pairs(qi, attn_score_body, 0)
    attn_score_step(qi, True)
    mx_ref[...] = _row_max_to_lanes(mx_ref[...])

    def pv_body(j, carry):
        start = pl.multiple_of(j * TK, TK)
        acc_ref[...] += _exp_pv(s_ref[j], mx_ref[...], vd_ref[pl.ds(start, TK), :])
        return carry

    _fori_pairs(qi + 1, pv_body, 0)
    a = acc_ref[...]
    out = a[:, :HEAD_DIM] / a[:, HEAD_DIM:]
    for h in range(nh):
        o_ref[:, h * HEAD_DIM:(h + 1) * HEAD_DIM] = out[h * TQ:(h + 1) * TQ].astype(o_ref.dtype)


def _merge_kernel(l_ref, hb_ref, of_ref, on_ref, od_ref, wg1_ref, wg2_ref, wg3_ref, wf_ref, wn_ref, wd_ref, o_ref):
    hb = hb_ref[...]
    mixed = (jax.nn.sigmoid(_dot(hb, wg1_ref[...])) * _dot(of_ref[...], wf_ref[...])
             + jax.nn.sigmoid(_dot(hb, wg2_ref[...])) * _dot(on_ref[...], wn_ref[...])
             + jax.nn.sigmoid(_dot(hb, wg3_ref[...])) * _dot(od_ref[...], wd_ref[...]))
    o_ref[...] = mixed.astype(o_ref.dtype)


LN_SUB_ROWS = 256


def _layer_norm_store(y, g_ref, b_ref, of_ref, ob_ref, rows):
    mu = jnp.mean(y, axis=-1, keepdims=True)
    yc = y - mu
    var = jnp.mean(jnp.square(yc), axis=-1, keepdims=True)
    out = yc * lax.rsqrt(var + 1e-5) * g_ref[...] + b_ref[...]
    of_ref[rows, :] = out
    if ob_ref is not None:
        ob_ref[rows, :] = out.astype(ob_ref.dtype)


def _out_ln_kernel(l_ref, x_ref, w_ref, h_ref, g_ref, b_ref, of_ref, ob_ref):
    w = w_ref[...]
    for r in range(x_ref.shape[0] // LN_SUB_ROWS):
        rows = slice(r * LN_SUB_ROWS, (r + 1) * LN_SUB_ROWS)
        y = ALPHA * h_ref[rows, :] + _dot(x_ref[rows, :], w)
        _layer_norm_store(y, g_ref, b_ref, of_ref, ob_ref, rows)


def _out_ln(name, lidx, x, w, h, g, b):
    t, kdim = x.shape
    d = h.shape[1]
    tm = 512
    return _call(_out_ln_kernel, name, lidx, [x, w, h, g, b], grid=(t // tm,),
                 in_specs=[pl.BlockSpec((tm, kdim), lambda i, l: (i, 0)),
                           pl.BlockSpec((None, kdim, d), lambda i, l: (l[0], 0, 0)),
                           pl.BlockSpec((tm, d), lambda i, l: (i, 0)),
                           pl.BlockSpec((None, 1, d), lambda i, l: (l[0], 0, 0)),
                           pl.BlockSpec((None, 1, d), lambda i, l: (l[0], 0, 0))],
                 out_specs=[pl.BlockSpec((tm, d), lambda i, l: (i, 0)),
                            pl.BlockSpec((tm, d), lambda i, l: (i, 0))],
                 out_shape=[jax.ShapeDtypeStruct((t, d), F32), jax.ShapeDtypeStruct((t, d), BF16)])


def _ffn_ln_kernel(l_ref, x_ref, w_ref, h_ref, g_ref, b_ref, of_ref, *, nk):
    k = pl.program_id(1)

    @pl.when(k == 0)
    def _():
        of_ref[...] = ALPHA * h_ref[...] + _dot(x_ref[...], w_ref[...])

    @pl.when((k > 0) & (k < nk - 1))
    def _():
        of_ref[...] += _dot(x_ref[...], w_ref[...])

    @pl.when(k == nk - 1)
    def _():
        w = w_ref[...]
        for r in range(x_ref.shape[0] // LN_SUB_ROWS):
            rows = slice(r * LN_SUB_ROWS, (r + 1) * LN_SUB_ROWS)
            y = of_ref[rows, :] + _dot(x_ref[rows, :], w)
            _layer_norm_store(y, g_ref, b_ref, of_ref, None, rows)


def _ffn_ln(name, lidx, x, w, h, g, b, tk):
    t, kdim = x.shape
    d = h.shape[1]
    tm = 1024
    nk = kdim // tk
    assert nk >= 2
    return _call(functools.partial(_ffn_ln_kernel, nk=nk), name, lidx, [x, w, h, g, b],
                 grid=(t // tm, nk),
                 in_specs=[pl.BlockSpec((tm, tk), lambda i, k, l: (i, k)),
                           pl.BlockSpec((None, tk, d), lambda i, k, l: (l[0], k, 0)),
                           pl.BlockSpec((tm, d), lambda i, k, l: (i, 0)),
                           pl.BlockSpec((None, 1, d), lambda i, k, l: (l[0], 0, 0)),
                           pl.BlockSpec((None, 1, d), lambda i, k, l: (l[0], 0, 0))],
                 out_specs=pl.BlockSpec((tm, d), lambda i, k, l: (i, 0)),
                 out_shape=jax.ShapeDtypeStruct((t, d), F32),
                 vmem_limit_bytes=VMEM_LIMIT_LARGE_BYTES)


def _swiglu_kernel(l_ref, x_ref, wa_ref, wb_ref, o_ref, wab_ref):
    @pl.when(pl.program_id(1) == 0)
    def _():
        wab_ref[0] = wa_ref[...].astype(BF16)
        wab_ref[1] = wb_ref[...].astype(BF16)

    x = x_ref[...]
    a = _dot(x, wab_ref[0])
    o_ref[...] = (jax.nn.silu(a) * _dot(x, wab_ref[1])).astype(o_ref.dtype)


def _ple_kernel(l_ref, h_ref, p_ref, wpi_ref, wpg_ref, of_ref, ob_ref):
    wpi = wpi_ref[...]
    wpg = wpg_ref[...]
    for r in range(h_ref.shape[0] // LN_SUB_ROWS):
        rows = slice(r * LN_SUB_ROWS, (r + 1) * LN_SUB_ROWS)
        h = h_ref[rows, :]
        p_in = _dot(p_ref[rows, :].astype(BF16), wpi)
        out = h + p_in * jax.nn.sigmoid(_dot(h.astype(BF16), wpg))
        of_ref[rows, :] = out
        ob_ref[rows, :] = out.astype(ob_ref.dtype)


def _rope_tables(n, dim):
    inv = 1.0 / (ROPE_THETA ** (jnp.arange(0, dim, 2, dtype=F32) / dim))
    ang = jnp.arange(n, dtype=F32)[:, None] * inv[None, :]
    return jnp.cos(ang), jnp.sin(ang)


def _layer(lidx, h, hb, consts, weights, bsz, seq):
    (cos128, sin128, cos64, sin64a, sin64b, cmp_cos, cmp_sin, overlap, block_of_key) = consts
    (w_proj, fox_bias, pe_k, pe_v, wk1, wk2, wv1, wv2, kv_norm, kv_up,
     w_br_fox, w_br_nsa, w_br_dsa, w_out, ln1_g, ln1_b, w_ffn_in, w_ffn_out, ln2_g, ln2_b, p, w_ple_in, w_ple_gate) = weights
    t = bsz * seq
    nq = seq // TQ
    d = D_MODEL

    z_a = _proj("proj_plain", lidx, hb, w_proj, COL_PLAIN, N_PLAIN, 2048, 512, BF16, seq)
    z_b = _proj("proj_rope128", lidx, hb, w_proj, COL_ROPE128, N_ROPE128, 2048, 512, BF16, seq,
                rope=((HEAD_DIM // 2,), cos128, sin128))
    z_c = _proj("proj_rope64", lidx, hb, w_proj, COL_ROPE64, N_ROPE64, 2048, 256, BF16, seq,
                rope=((IDX_DIM // 2, LANE - IDX_DIM // 2), cos64, sin64a, sin64b))
    z_d = _proj("proj_f32", lidx, hb, w_proj, COL_F32, N_F32, 1024, N_F32, F32, seq)

    cum = _call(_cum_kernel, "fox_cum", lidx, [z_d, fox_bias], grid=(bsz,),
                in_specs=[pl.BlockSpec((seq, LANE), lambda b, l: (b, ZD_FOX_F)),
                          pl.BlockSpec((None, 1, LANE), lambda b, l: (l[0], 0, 0))],
                out_specs=pl.BlockSpec((seq, LANE), lambda b, l: (b, 0)),
                out_shape=jax.ShapeDtypeStruct((t, LANE), F32))
    cum_rows = jnp.transpose(cum.reshape(bsz, seq, LANE)[:, :, :8], (0, 2, 1)).reshape(bsz, 8 * (seq // TK), TK)
    fq = FOX_HEADS * HEAD_DIM
    o_fox = _call(_fox_kernel, "fox_attn", lidx, [z_a, z_a, z_a, cum, cum_rows], grid=(bsz, nq),
                  in_specs=[pl.BlockSpec((TQ, fq), lambda b, i, l: (b * nq + i, 0)),
                            pl.BlockSpec((seq, fq), lambda b, i, l: (b, 1)),
                            pl.BlockSpec((seq, fq), lambda b, i, l: (b, 2)),
                            pl.BlockSpec((TQ, LANE), lambda b, i, l: (b * nq + i, 0)),
                            pl.BlockSpec((None, 8 * (seq // TK), TK), lambda b, i, l: (b, 0, 0))],
                  out_specs=pl.BlockSpec((TQ, fq), lambda b, i, l: (b * nq + i, 0)),
                  out_shape=jax.ShapeDtypeStruct((t, fq), BF16),
                  scratch_shapes=[pltpu.VMEM((FOX_HEADS, seq // TK, TQ, TK), F32),
                                  pltpu.VMEM((FOX_HEADS, TQ, LANE), F32), pltpu.VMEM((FOX_HEADS, TQ, LANE), F32),
                                  pltpu.VMEM((FOX_HEADS, TQ, 2 * HEAD_DIM), F32)])

    nchunk = seq // CMP_STRIDE
    half = CMP_STRIDE * HEAD_DIM
    r_k = z_d[:, ZD_NSA_KC * LANE:(ZD_NSA_KC + 1) * LANE].reshape(bsz * nchunk, half)
    r_v = z_d[:, ZD_NSA_VC * LANE:(ZD_NSA_VC + 1) * LANE].reshape(bsz * nchunk, half)
    k_cmp, v_cmp = _call(
        _cmp_kernel, "nsa_compress", lidx, [r_k, r_v, pe_k, pe_v, wk1, wk2, wv1, wv2, cmp_cos, cmp_sin], grid=(bsz,),
        in_specs=[pl.BlockSpec((nchunk, half), lambda b, l: (b, 0)),
                  pl.BlockSpec((nchunk, half), lambda b, l: (b, 0)),
                  pl.BlockSpec((None, 2, half), lambda b, l: (l[0], 0, 0)),
                  pl.BlockSpec((None, 2, half), lambda b, l: (l[0], 0, 0)),
                  pl.BlockSpec((None, 2 * half, CMP_HIDDEN), lambda b, l: (l[0], 0, 0)),
                  pl.BlockSpec((None, CMP_HIDDEN, HEAD_DIM), lambda b, l: (l[0], 0, 0)),
                  pl.BlockSpec((None, 2 * half, CMP_HIDDEN), lambda b, l: (l[0], 0, 0)),
                  pl.BlockSpec((None, CMP_HIDDEN, HEAD_DIM), lambda b, l: (l[0], 0, 0)),
                  pl.BlockSpec((nchunk, HEAD_DIM), lambda b, l: (0, 0)),
                  pl.BlockSpec((nchunk, HEAD_DIM), lambda b, l: (0, 0))],
        out_specs=[pl.BlockSpec((nchunk, HEAD_DIM), lambda b, l: (b, 0)),
                   pl.BlockSpec((nchunk, HEAD_DIM), lambda b, l: (b, 0))],
        out_shape=[jax.ShapeDtypeStruct((bsz * nchunk, HEAD_DIM), BF16)] * 2)
    nsq = NSA_HEADS * HEAD_DIM
    o_nsa = _call(
        _nsa_kernel, "nsa_attn", lidx, [z_b, z_b, z_b, z_a, z_a, k_cmp, v_cmp, z_d, overlap, block_of_key], grid=(bsz, nq),
        in_specs=[pl.BlockSpec((TQ, nsq), lambda b, i, l: (b * nq + i, 0)),
                  pl.BlockSpec((seq, LANE), lambda b, i, l: (b, 4)),
                  pl.BlockSpec((seq, LANE), lambda b, i, l: (b, 5)),
                  pl.BlockSpec((seq, LANE), lambda b, i, l: (b, 18)),
                  pl.BlockSpec((seq, LANE), lambda b, i, l: (b, 19)),
                  pl.BlockSpec((nchunk, HEAD_DIM), lambda b, i, l: (b, 0)),
                  pl.BlockSpec((nchunk, HEAD_DIM), lambda b, i, l: (b, 0)),
                  pl.BlockSpec((TQ, LANE), lambda b, i, l: (b * nq + i, ZD_NSA_G)),
                  pl.BlockSpec((N_CMP_PAD, LANE), lambda b, i, l: (0, 0)),
                  pl.BlockSpec((seq, LANE), lambda b, i, l: (0, 0))],
        out_specs=pl.BlockSpec((TQ, nsq), lambda b, i, l: (b * nq + i, 0)),
        out_shape=jax.ShapeDtypeStruct((t, nsq), BF16),
        scratch_shapes=[pltpu.VMEM((NSA_HEADS * TQ, HEAD_DIM), BF16),
                        pltpu.VMEM((seq // TK, NSA_HEADS * TQ, TK), F32),
                        pltpu.VMEM((NSA_HEADS * TQ, LANE), F32),
                        pltpu.VMEM((NSA_HEADS * TQ, 2 * HEAD_DIM), F32)])

    tm_kv = 1024
    k_d, v_d = _call(
        _dsa_kv_kernel, "dsa_kv", lidx, [z_d, kv_norm, kv_up, cos128, sin128], grid=(t // tm_kv,),
        in_specs=[pl.BlockSpec((tm_kv, DSA_KV_RANK), lambda i, l: (i, 0)),
                  pl.BlockSpec((None, 1, DSA_KV_RANK), lambda i, l: (l[0], 0, 0)),
                  pl.BlockSpec((None, DSA_KV_RANK, 2 * HEAD_DIM), lambda i, l: (l[0], 0, 0)),
                  pl.BlockSpec((tm_kv, LANE), lambda i, l: (i % (seq // tm_kv), 0)),
                  pl.BlockSpec((tm_kv, LANE), lambda i, l: (i % (seq // tm_kv), 0))],
        out_specs=[pl.BlockSpec((tm_kv, HEAD_DIM), lambda i, l: (i, 0)),
                   pl.BlockSpec((tm_kv, HEAD_DIM), lambda i, l: (i, 0))],
        out_shape=[jax.ShapeDtypeStruct((t, HEAD_DIM), BF16)] * 2)
    dq = DSA_HEADS * HEAD_DIM
    iqw = IDX_HEADS * IDX_DIM
    o_dsa = _call(
        _dsa_kernel, "dsa_attn", lidx, [z_b, k_d, v_d, z_c, z_c, z_c, z_d], grid=(bsz, nq),
        in_specs=[pl.BlockSpec((TQ, dq), lambda b, i, l: (b * nq + i, 1)),
                  pl.BlockSpec((seq, HEAD_DIM), lambda b, i, l: (b, 0)),
                  pl.BlockSpec((seq, HEAD_DIM), lambda b, i, l: (b, 0)),
                  pl.BlockSpec((TQ, iqw), lambda b, i, l: (b * nq + i, 0)),
                  pl.BlockSpec((seq, LANE), lambda b, i, l: (b, iqw // LANE)),
                  pl.BlockSpec((seq, LANE), lambda b, i, l: (b, iqw // LANE + 1)),
                  pl.BlockSpec((TQ, LANE), lambda b, i, l: (b * nq + i, ZD_IDX_W))],
        out_specs=pl.BlockSpec((TQ, dq), lambda b, i, l: (b * nq + i, 0)),
        out_shape=jax.ShapeDtypeStruct((t, dq), BF16),
        scratch_shapes=[pltpu.VMEM((seq // TK, TQ, TK), I32), pltpu.VMEM((seq // TK, TK, TQ), I32),
                        pltpu.VMEM((seq // TK, TK, TQ), I16), pltpu.VMEM((seq // TK, TK, TQ), I16),
                        pltpu.VMEM((IDX_HEADS, TQ, LANE), F32), pltpu.VMEM((2, TQ, LANE), I32),
                        pltpu.VMEM((DSA_HEADS * TQ, HEAD_DIM), BF16),
                        pltpu.VMEM((seq // TK, DSA_HEADS * TQ, TK), F32),
                        pltpu.VMEM((DSA_HEADS * TQ, LANE), F32),
                        pltpu.VMEM((DSA_HEADS * TQ, 2 * HEAD_DIM), F32)])

    tm, tn = 1024, 512
    ncol = d // tn
    gate0 = COL_GATE // tn
    assert gate0 * tn == COL_GATE
    mixed = _call(
        _merge_kernel, "merge", lidx, [hb, o_fox, o_nsa, o_dsa, w_proj, w_proj, w_proj, w_br_fox, w_br_nsa, w_br_dsa],
        grid=(t // tm, ncol),
        in_specs=[pl.BlockSpec((tm, d), lambda i, j, l: (i, 0)),
                  pl.BlockSpec((tm, fq), lambda i, j, l: (i, 0)),
                  pl.BlockSpec((tm, nsq), lambda i, j, l: (i, 0)),
                  pl.BlockSpec((tm, dq), lambda i, j, l: (i, 0)),
                  pl.BlockSpec((None, d, tn), lambda i, j, l: (l[0], 0, gate0 + j)),
                  pl.BlockSpec((None, d, tn), lambda i, j, l: (l[0], 0, gate0 + ncol + j)),
                  pl.BlockSpec((None, d, tn), lambda i, j, l: (l[0], 0, gate0 + 2 * ncol + j)),
                  pl.BlockSpec((None, fq, tn), lambda i, j, l: (l[0], 0, j)),
                  pl.BlockSpec((None, nsq, tn), lambda i, j, l: (l[0], 0, j)),
                  pl.BlockSpec((None, dq, tn), lambda i, j, l: (l[0], 0, j))],
        out_specs=pl.BlockSpec((tm, tn), lambda i, j, l: (i, j)),
        out_shape=jax.ShapeDtypeStruct((t, d), BF16))

    h, hb = _out_ln("out_ln", lidx, mixed, w_out, h, ln1_g, ln1_b)

    nff = D_FF // tn
    act = _call(
        _swiglu_kernel, "swiglu", lidx, [hb, w_ffn_in, w_ffn_in], grid=(nff, t // tm),
        in_specs=[pl.BlockSpec((tm, d), lambda j, i, l: (i, 0)),
                  pl.BlockSpec((None, d, tn), lambda j, i, l: (l[0], 0, j)),
                  pl.BlockSpec((None, d, tn), lambda j, i, l: (l[0], 0, nff + j))],
        out_specs=pl.BlockSpec((tm, tn), lambda j, i, l: (i, j)),
        out_shape=jax.ShapeDtypeStruct((t, D_FF), BF16),
        scratch_shapes=[pltpu.VMEM((2, d, tn), BF16)])
    h = _ffn_ln("ffn_ln", lidx, act, w_ffn_out, h, ln2_g, ln2_b, tk=D_FF // 4)

    tmp = 512
    h, hb = _call(
        _ple_kernel, "ple", lidx, [h, p, w_ple_in, w_ple_gate], grid=(t // tmp,),
        in_specs=[pl.BlockSpec((tmp, d), lambda i, l: (i, 0)),
                  pl.BlockSpec((None, tmp, PLE_DIM), lambda i, l: (l[0], i, 0)),
                  pl.BlockSpec((None, PLE_DIM, d), lambda i, l: (l[0], 0, 0)),
                  pl.BlockSpec((None, d, d), lambda i, l: (l[0], 0, 0))],
        out_specs=[pl.BlockSpec((tmp, d), lambda i, l: (i, 0)),
                   pl.BlockSpec((tmp, d), lambda i, l: (i, 0))],
        out_shape=[jax.ShapeDtypeStruct((t, d), F32), jax.ShapeDtypeStruct((t, d), BF16)])
    return h, hb


def kernel(x, p, w_in, fox_f_bias, nsa_pe_k, nsa_pe_v, nsa_cmp_k1, nsa_cmp_k2, nsa_cmp_v1, nsa_cmp_v2, dsa_kv_norm, dsa_kv_up, w_br_fox, w_br_nsa, w_br_dsa, w_out, ln1_g, ln1_b, w_ffn_in, w_ffn_out, ln2_g, ln2_b, w_ple_in, w_ple_gate):
    bsz, seq, d = x.shape
    depth = w_in.shape[0]
    t = bsz * seq
    assert d == D_MODEL and seq % 1024 == 0 and depth == DEPTH

    cat = functools.partial(jnp.concatenate, axis=-1)
    w_proj = _regroup(w_in)

    fox_bias = jnp.pad(fox_f_bias, ((0, 0), (0, LANE - FOX_HEADS))).reshape(depth, 1, LANE)
    half = CMP_STRIDE * HEAD_DIM
    weights = (
        w_proj, fox_bias,
        nsa_pe_k.reshape(depth, 2, half), nsa_pe_v.reshape(depth, 2, half),
        nsa_cmp_k1.astype(BF16), nsa_cmp_k2.astype(BF16), nsa_cmp_v1.astype(BF16), nsa_cmp_v2.astype(BF16),
        dsa_kv_norm.reshape(depth, 1, DSA_KV_RANK), dsa_kv_up.astype(BF16),
        w_br_fox.astype(BF16), w_br_nsa.astype(BF16), w_br_dsa.astype(BF16), w_out.astype(BF16),
        ln1_g.reshape(depth, 1, d), ln1_b.reshape(depth, 1, d),
        w_ffn_in, w_ffn_out.astype(BF16),
        ln2_g.reshape(depth, 1, d), ln2_b.reshape(depth, 1, d),
        p.reshape(depth, t, PLE_DIM), w_ple_in.astype(BF16), w_ple_gate.astype(BF16),
    )

    cos, sin = _rope_tables(seq, HEAD_DIM)
    cos128 = cat([cos, cos])
    sin128 = cat([-sin, sin])
    cos_i, sin_i = _rope_tables(seq, IDX_DIM)
    zi = jnp.zeros_like(sin_i)
    cos64 = cat([cos_i, cos_i, cos_i, cos_i])
    sin64a = cat([zi, sin_i, zi, sin_i])
    sin64b = cat([-sin_i, zi, -sin_i, zi])
    n_cmp = (seq - CMP_LEN) // CMP_STRIDE + 1
    c_end = jnp.minimum(jnp.arange(N_CMP_PAD) * CMP_STRIDE + CMP_LEN - 1, seq - 1)
    cmp_cos, cmp_sin = cos128[c_end], sin128[c_end]
    n_slc = seq // SLC_LEN
    c_start = jnp.arange(N_CMP_PAD) * CMP_STRIDE
    s_start = jnp.arange(LANE) * SLC_LEN
    overlap = jnp.maximum(jnp.minimum(c_start[:, None] + CMP_LEN - 1, s_start[None, :] + SLC_LEN - 1)
                          - jnp.maximum(c_start[:, None], s_start[None, :]) + 1, 0).astype(F32) / CMP_LEN
    overlap = jnp.where((jnp.arange(N_CMP_PAD)[:, None] < n_cmp) & (jnp.arange(LANE)[None, :] < n_slc), overlap, 0.0)
    overlap = overlap.T.astype(BF16)
    block_of_key = (jnp.arange(seq)[:, None] // SLC_LEN == jnp.arange(LANE)[None, :]).astype(BF16)
    consts = (cos128, sin128, cos64, sin64a, sin64b, cmp_cos, cmp_sin, overlap, block_of_key)

    h = x.reshape(t, d)
    hb = h.astype(BF16)
    for layer in range(depth):
        lidx = jnp.full((1,), layer, I32)
        h, hb = _layer(lidx, h, hb, consts, weights, bsz, seq)
    return h.reshape(bsz, seq, d)
```

```python
import functools

import jax
import jax.numpy as jnp
from jax import lax
from jax.experimental import pallas as pl
from jax.experimental.pallas import tpu as pltpu

F32, BF16, I32, I16 = jnp.float32, jnp.bfloat16, jnp.int32, jnp.int16

D_MODEL = 2048
DEPTH = 4
HEAD_DIM = 128
ROPE_THETA = 10000.0
NEG_INF = -1e30
FOX_HEADS = 6
NSA_HEADS = 4
CMP_LEN = 32
CMP_STRIDE = 16
CMP_HIDDEN = 256
SLC_LEN = 64
SLC_TOPN = 16
WIN = 512
SLC_FORCE = 1e4
DSA_HEADS = 6
DSA_KV_RANK = 256
IDX_HEADS = 16
IDX_DIM = 64
IDX_TOPK_MAX = 256
D_FF = ((8 * D_MODEL + 3 * 256 - 1) // (3 * 256)) * 256
PLE_DIM = 256
ALPHA = (2 * DEPTH) ** 0.25
SCALE = HEAD_DIM ** -0.5
LOG2E = 1.4426950408889634
SCALE_LOG2E = SCALE * LOG2E

LANE = 128
VMEM_LIMIT_BYTES = 48 * 1024 * 1024
VMEM_LIMIT_LARGE_BYTES = 58 * 1024 * 1024

_IN_SPLITS = (
    ('fox_q', FOX_HEADS * HEAD_DIM), ('fox_k', FOX_HEADS * HEAD_DIM), ('fox_v', FOX_HEADS * HEAD_DIM), ('fox_f', FOX_HEADS),
    ('nsa_q', NSA_HEADS * HEAD_DIM), ('nsa_kc', HEAD_DIM), ('nsa_vc', HEAD_DIM), ('nsa_ks', HEAD_DIM), ('nsa_vs', HEAD_DIM),
    ('nsa_kw', HEAD_DIM), ('nsa_vw', HEAD_DIM), ('nsa_g', 3 * NSA_HEADS),
    ('dsa_q', DSA_HEADS * HEAD_DIM), ('dsa_ckv', DSA_KV_RANK), ('idx_q', IDX_HEADS * IDX_DIM), ('idx_k', IDX_DIM), ('idx_w', IDX_HEADS),
    ('gate', 3 * D_MODEL),
)
_IN_OFFSETS = {}
_off = 0
for _name, _width in _IN_SPLITS:
    _IN_OFFSETS[_name] = (_off, _off + _width)
    _off += _width

MISC_FOX_F = 0
MISC_NSA_G = FOX_HEADS
MISC_IDX_W = FOX_HEADS + 3 * NSA_HEADS

TQ = 256
TK = 256
N_CMP_PAD = 128


def _dot(a, b):
    return jnp.dot(a, b, preferred_element_type=F32)


def _dot_nt(a, b):
    return lax.dot_general(a, b, (((1,), (1,)), ((), ())), preferred_element_type=F32)


def _fori_pairs(n, body, init):
    def four(jj, carry):
        for u in range(4):
            carry = body(4 * jj + u, carry)
        return carry
    carry = lax.fori_loop(0, lax.shift_right_logical(n, jnp.int32(2)), four, init)
    base = n & jnp.int32(-4)
    carry = lax.cond((n & 2) == 2, lambda c: body(base + 1, body(base, c)), lambda c: c, carry)
    return lax.cond((n & 1) == 1, lambda c: body(n - 1, c), lambda c: c, carry)


def _split3(a):
    a1 = a.astype(BF16)
    r1 = a - a1.astype(F32)
    a2 = r1.astype(BF16)
    a3 = (r1 - a2.astype(F32)).astype(BF16)
    return a1, a2, a3


def _call(kernel, name, lidx, args, grid, in_specs, out_specs, out_shape, scratch_shapes=(),
          vmem_limit_bytes=VMEM_LIMIT_BYTES):
    return pl.pallas_call(
        kernel,
        grid_spec=pltpu.PrefetchScalarGridSpec(
            num_scalar_prefetch=1, grid=grid, in_specs=in_specs, out_specs=out_specs,
            scratch_shapes=list(scratch_shapes)),
        out_shape=out_shape,
        compiler_params=pltpu.CompilerParams(
            dimension_semantics=("arbitrary",) * len(grid), vmem_limit_bytes=vmem_limit_bytes),
        name=name,
    )(lidx, *args)


def _regroup_plan():
    plan = []

    def whole(name):
        a, b = _IN_OFFSETS[name]
        for g in range((b - a) // LANE):
            plan.append((a + g * LANE, 0, LANE))

    whole('dsa_ckv')
    lane0 = {}
    start = 0
    for name in ('fox_f', 'nsa_g', 'idx_w'):
        a, b = _IN_OFFSETS[name]
        lane0[name] = (a - start, start, start + b - a)
        start += b - a
    plan.append(lane0['fox_f'])
    whole('nsa_kc')
    whole('nsa_vc')
    plan.append(lane0['nsa_g'])
    plan.append(lane0['idx_w'])
    plan.append((0, 0, 0))
    for name in ('fox_q', 'fox_k', 'fox_v', 'nsa_vs', 'nsa_vw'):
        whole(name)
    for name in ('nsa_q', 'nsa_ks', 'nsa_kw', 'dsa_q'):
        whole(name)
    whole('gate')
    whole('idx_q')
    ik = _IN_OFFSETS['idx_k'][0]
    plan.append((ik, 0, IDX_DIM))
    plan.append((ik - IDX_DIM, IDX_DIM, LANE))
    return tuple(plan)


_REGROUP_PLAN = _regroup_plan()
COL_F32 = 0
N_F32 = DSA_KV_RANK + 6 * LANE
ZD_FOX_F, ZD_NSA_KC, ZD_NSA_VC, ZD_NSA_G, ZD_IDX_W = 2, 3, 4, 5, 6
COL_PLAIN = COL_F32 + N_F32
N_PLAIN = (3 * FOX_HEADS + 2) * HEAD_DIM
COL_ROPE128 = COL_PLAIN + N_PLAIN
N_ROPE128 = (NSA_HEADS + 2 + DSA_HEADS) * HEAD_DIM
COL_GATE = COL_ROPE128 + N_ROPE128
COL_ROPE64 = COL_GATE + 3 * D_MODEL
N_ROPE64 = IDX_HEADS * IDX_DIM + 2 * LANE
N_REGROUPED = COL_ROPE64 + N_ROPE64
assert N_REGROUPED == len(_REGROUP_PLAN) * LANE


def _regroup_kernel(src_ref, lo_ref, hi_ref, x_ref, o_ref):
    g = pl.program_id(0)
    col = lax.broadcasted_iota(I32, (LANE, x_ref.shape[2]), 0)
    keep = (col >= lo_ref[g]) & (col < hi_ref[g])
    for layer in range(x_ref.shape[1]):
        o_ref[layer] = jnp.where(keep, x_ref[:, layer, :], 0.0).T.astype(o_ref.dtype)


def _regroup(w_in):
    depth, d, n_in = w_in.shape
    w_t = jnp.transpose(w_in, (2, 0, 1))
    src = jnp.array([p[0] for p in _REGROUP_PLAN], I32)
    lo = jnp.array([p[1] for p in _REGROUP_PLAN], I32)
    hi = jnp.array([p[2] for p in _REGROUP_PLAN], I32)
    return pl.pallas_call(
        _regroup_kernel,
        grid_spec=pltpu.PrefetchScalarGridSpec(
            num_scalar_prefetch=3, grid=(len(_REGROUP_PLAN),),
            in_specs=[pl.BlockSpec((pl.Element(LANE), pl.Element(depth), pl.Element(d)),
                                   lambda g, src, lo, hi: (src[g], 0, 0))],
            out_specs=pl.BlockSpec((depth, d, LANE), lambda g, src, lo, hi: (0, 0, g))),
        out_shape=jax.ShapeDtypeStruct((depth, d, N_REGROUPED), BF16),
        compiler_params=pltpu.CompilerParams(dimension_semantics=("arbitrary",), vmem_limit_bytes=VMEM_LIMIT_BYTES),
        name="regroup_w_in",
    )(src, lo, hi, w_t)


def _proj_kernel(l_ref, x_ref, w_ref, *refs, shifts):
    o_ref = refs[-1]
    acc = _dot(x_ref[...], w_ref[...])
    if not shifts:
        o_ref[...] = acc.astype(o_ref.dtype)
        return
    c_ref = refs[0]
    s_refs = refs[1:-1]
    for g in range(acc.shape[1] // LANE):
        xg = acc[:, g * LANE:(g + 1) * LANE]
        out = xg * c_ref[...]
        for shift, s_ref in zip(shifts, s_refs):
            out = out + pltpu.roll(xg, shift, 1) * s_ref[...]
        o_ref[:, g * LANE:(g + 1) * LANE] = out.astype(o_ref.dtype)


def _proj(name, lidx, hb, w, col0, n, tm, tn, out_dtype, seq, rope=None):
    t, d = hb.shape
    jb = col0 // tn
    assert jb * tn == col0
    in_specs = [pl.BlockSpec((tm, d), lambda i, j, l: (i, 0)),
                pl.BlockSpec((None, d, tn), lambda i, j, l: (l[0], 0, jb + j))]
    args = [hb, w]
    shifts = ()
    if rope is not None:
        shifts, tables = rope[0], rope[1:]
        nrow = seq // tm
        for tab in tables:
            in_specs.append(pl.BlockSpec((tm, LANE), lambda i, j, l: (i % nrow, 0)))
            args.append(tab)
    return _call(functools.partial(_proj_kernel, shifts=shifts), name, lidx, args,
                 grid=(t // tm, n // tn), in_specs=in_specs,
                 out_specs=pl.BlockSpec((tm, tn), lambda i, j, l: (i, j)),
                 out_shape=jax.ShapeDtypeStruct((t, n), out_dtype))


def _cum_kernel(l_ref, f_ref, bias_ref, o_ref):
    chunk = 256
    row = lax.broadcasted_iota(I32, (chunk, chunk), 0)
    col = lax.broadcasted_iota(I32, (chunk, chunk), 1)
    tri = jnp.where(row >= col, 1.0, 0.0).astype(BF16)
    carry = jnp.zeros((1, LANE), F32)
    for c in range(f_ref.shape[0] // chunk):
        logf = jax.nn.log_sigmoid(f_ref[c * chunk:(c + 1) * chunk, :] + bias_ref[...])
        l1, l2, l3 = _split3(logf)
        cs = _dot(tri, l1) + _dot(tri, l2) + _dot(tri, l3) + carry
        o_ref[c * chunk:(c + 1) * chunk, :] = cs
        carry = cs[chunk - 1:chunk, :]


def _max_lanes(x):
    out = x[:, 0:LANE]
    for g in range(1, x.shape[1] // LANE):
        out = jnp.maximum(out, x[:, g * LANE:(g + 1) * LANE])
    return out


def _tile_lanes(x, n):
    return jnp.concatenate([x] * n, axis=1)


def _row_max_to_lanes(mx):
    return jnp.broadcast_to(jnp.max(mx, axis=-1, keepdims=True), mx.shape)


def _exp_pv(s, m_lanes, v):
    p = jnp.exp2(s - _tile_lanes(m_lanes, s.shape[1] // LANE)).astype(BF16)
    return _dot(p, jnp.concatenate([v, jnp.ones_like(v)], axis=1))


def _fox_kernel(l_ref, q_ref, k_ref, v_ref, cq_ref, ck_ref, o_ref, s_ref, mx_ref, cqb_ref, acc_ref):
    qi = pl.program_id(1)
    nh = FOX_HEADS
    nchunk = ck_ref.shape[0] // 8
    row = lax.broadcasted_iota(I32, (TQ, TK), 0)
    col = lax.broadcasted_iota(I32, (TQ, TK), 1)
    cq_all = cq_ref[...]
    for h in range(nh):
        cqb_ref[h] = jnp.broadcast_to(cq_all[:, h:h + 1] * LOG2E, (TQ, LANE))
    mx_ref[...] = jnp.full(mx_ref.shape, NEG_INF, F32)
    acc_ref[...] = jnp.zeros(acc_ref.shape, F32)

    def score_step(j, diagonal):
        start = pl.multiple_of(j * TK, TK)
        for h in range(nh):
            hs = slice(h * HEAD_DIM, (h + 1) * HEAD_DIM)
            s = (_dot_nt(q_ref[:, hs], k_ref[pl.ds(start, TK), hs]) * SCALE_LOG2E
                 + _tile_lanes(cqb_ref[h], TK // LANE) - ck_ref[pl.ds(h * nchunk + j, 1), :] * LOG2E)
            if diagonal:
                s = jnp.where(col <= row, s, NEG_INF)
            s_ref[h, j] = s
            mx_ref[h] = jnp.maximum(mx_ref[h], _max_lanes(s))

    def score_body(j, carry):
        score_step(j, False)
        return carry

    _fori_pairs(qi, score_body, 0)
    score_step(qi, True)
    for h in range(nh):
        mx_ref[h] = _row_max_to_lanes(mx_ref[h])

    def pv_body(j, carry):
        start = pl.multiple_of(j * TK, TK)
        for h in range(nh):
            hs = slice(h * HEAD_DIM, (h + 1) * HEAD_DIM)
            acc_ref[h] += _exp_pv(s_ref[h, j], mx_ref[h], v_ref[pl.ds(start, TK), hs])
        return carry

    _fori_pairs(qi + 1, pv_body, 0)
    for h in range(nh):
        a = acc_ref[h]
        o_ref[:, h * HEAD_DIM:(h + 1) * HEAD_DIM] = (a[:, :HEAD_DIM] / a[:, HEAD_DIM:]).astype(o_ref.dtype)


def _cmp_kernel(l_ref, rk_ref, rv_ref, pek_ref, pev_ref, wk1_ref, wk2_ref, wv1_ref, wv2_ref,
                c_ref, s_ref, ko_ref, vo_ref):
    half = CMP_STRIDE * HEAD_DIM

    def compress(r_ref, pe_ref, w1_ref, w2_ref):
        r = r_ref[...]
        lo = (r + pe_ref[0:1, :]).astype(BF16)
        hi = (r + pe_ref[1:2, :]).astype(BF16)
        a = _dot(lo, w1_ref[0:half, :])
        b = _dot(hi, w1_ref[half:2 * half, :])
        hid = a + pltpu.roll(b, N_CMP_PAD - 1, 0)
        return _dot(jax.nn.gelu(hid).astype(BF16), w2_ref[...])

    kc = compress(rk_ref, pek_ref, wk1_ref, wk2_ref)
    kc = kc * c_ref[...] + pltpu.roll(kc, HEAD_DIM // 2, 1) * s_ref[...]
    ko_ref[...] = kc.astype(ko_ref.dtype)
    vo_ref[...] = compress(rv_ref, pev_ref, wv1_ref, wv2_ref).astype(vo_ref.dtype)


def _masked_softmax(s, mask):
    s = jnp.where(mask, s, NEG_INF)
    m = jnp.max(s, axis=-1, keepdims=True)
    e = jnp.where(mask, jnp.exp(s - m), 0.0)
    den = jnp.sum(e, axis=-1, keepdims=True)
    return e, jnp.where(den > 0.0, den, 1.0)


def _nsa_kernel(l_ref, q_ref, ks_ref, kw_ref, vs_ref, vw_ref, kc_ref, vc_ref, g_ref, ovt_ref, et_ref,
                o_ref, q_sc, ow_ref, s_ref, mx_ref, acc_ref):
    qi = pl.program_id(1)
    nh = NSA_HEADS
    for h in range(nh):
        q_sc[h * TQ:(h + 1) * TQ, :] = q_ref[:, h * HEAD_DIM:(h + 1) * HEAD_DIM]
    q = q_sc[...]
    pos = qi * TQ + lax.broadcasted_iota(I32, (TQ, 1), 0)
    lane = lax.broadcasted_iota(I32, (TQ, LANE), 1)

    band = WIN + TQ
    start_w = pl.multiple_of(jnp.maximum(qi * TQ - WIN, 0), TQ)
    kwb = kw_ref[pl.ds(start_w, band), :]
    vwb = vw_ref[pl.ds(start_w, band), :]
    dist = pos - (start_w + lax.broadcasted_iota(I32, (TQ, band), 1))
    wvis = (dist >= 0) & (dist < WIN)
    for h in range(nh):
        s_w = _dot_nt(q_ref[:, h * HEAD_DIM:(h + 1) * HEAD_DIM], kwb) * SCALE_LOG2E
        s_w = jnp.where(wvis, s_w, NEG_INF)
        a_win = _exp_pv(s_w, _row_max_to_lanes(_max_lanes(s_w)), vwb)
        ow_ref[h * TQ:(h + 1) * TQ, :] = a_win[:, :HEAD_DIM] / a_win[:, HEAD_DIM:]

    cvis = (lane * CMP_STRIDE + (CMP_LEN - 1)) <= pos
    s_c = (_dot_nt(q, kc_ref[...]) * SCALE).reshape(nh, TQ, N_CMP_PAD)
    e_c, den_c = _masked_softmax(s_c, cvis[None])
    p_c = e_c / den_c
    o_cmp = _dot(p_c.reshape(nh * TQ, N_CMP_PAD).astype(BF16), vc_ref[...])

    p_sum = p_c[0]
    for h in range(1, nh):
        p_sum = p_sum + p_c[h]
    p1, p2, p3 = _split3(p_sum)
    ovt = ovt_ref[...]
    n_slc = et_ref.shape[0] // SLC_LEN
    imp = (_dot_nt(ovt, p1) + _dot_nt(ovt, p2) + _dot_nt(ovt, p3))[0:n_slc, :]
    blk = lax.broadcasted_iota(I32, (n_slc, TQ), 0)
    pos_t = qi * TQ + lax.broadcasted_iota(I32, (n_slc, TQ), 1)
    cur = lax.shift_right_logical(pos_t, SLC_LEN.bit_length() - 1)
    forced = (blk == 0) | (blk == cur) | (blk == cur - 1)
    imp = jnp.where(forced, SLC_FORCE, jnp.where(blk * SLC_LEN <= pos_t, imp, -SLC_FORCE))
    rank = jnp.zeros((n_slc, TQ), F32)
    for k in range(n_slc):
        ik = imp[k:k + 1, :]
        ahead = (ik > imp) | ((ik == imp) & (blk > k))
        rank = rank + jnp.where(ahead, 1.0, 0.0)
    sel_t = jnp.where(rank < float(min(SLC_TOPN, n_slc)), 1.0, 0.0)
    sel_t = jnp.concatenate([sel_t, jnp.zeros((LANE - n_slc, TQ), F32)], axis=0)
    sel = sel_t.T.astype(BF16)

    mx_ref[...] = jnp.full(mx_ref.shape, NEG_INF, F32)
    acc_ref[...] = jnp.zeros(acc_ref.shape, F32)
    row = lax.broadcasted_iota(I32, (TQ, TK), 0)
    col = lax.broadcasted_iota(I32, (TQ, TK), 1)

    def score_step(j, diagonal):
        start = pl.multiple_of(j * TK, TK)
        vis = _dot_nt(sel, et_ref[pl.ds(start, TK), :]) > 0.5
        if diagonal:
            vis = vis & (col <= row)
        s = (_dot_nt(q_sc[...], ks_ref[pl.ds(start, TK), :]) * SCALE_LOG2E).reshape(nh, TQ, TK)
        s = jnp.where(vis[None], s, NEG_INF).reshape(nh * TQ, TK)
        s_ref[j] = s
        mx_ref[...] = jnp.maximum(mx_ref[...], _max_lanes(s))

    def score_body(j, carry):
        score_step(j, False)
        return carry

    _fori_pairs(qi, score_body, 0)
    score_step(qi, True)
    mx_ref[...] = _row_max_to_lanes(mx_ref[...])

    def pv_body(j, carry):
        start = pl.multiple_of(j * TK, TK)
        acc_ref[...] += _exp_pv(s_ref[j], mx_ref[...], vs_ref[pl.ds(start, TK), :])
        return carry

    _fori_pairs(qi + 1, pv_body, 0)
    a_slc = acc_ref[...]
    o_slc = a_slc[:, :HEAD_DIM] / a_slc[:, HEAD_DIM:]

    o_win = ow_ref[...]
    gate = jax.nn.sigmoid(g_ref[...])
    for h in range(nh):
        rows = slice(h * TQ, (h + 1) * TQ)
        g0 = gate[:, MISC_NSA_G + 3 * h:MISC_NSA_G + 3 * h + 1]
        g1 = gate[:, MISC_NSA_G + 3 * h + 1:MISC_NSA_G + 3 * h + 2]
        g2 = gate[:, MISC_NSA_G + 3 * h + 2:MISC_NSA_G + 3 * h + 3]
        out = g0 * o_cmp[rows] + g1 * o_slc[rows] + g2 * o_win[rows]
        o_ref[:, h * HEAD_DIM:(h + 1) * HEAD_DIM] = out.astype(o_ref.dtype)


def _dsa_kv_kernel(l_ref, x_ref, g_ref, up_ref, c_ref, s_ref, ko_ref, vo_ref):
    x = x_ref[...]
    r = x * lax.rsqrt(jnp.mean(jnp.square(x), axis=-1, keepdims=True) + 1e-6) * g_ref[...]
    kv = _dot(r.astype(BF16), up_ref[...])
    k = kv[:, :HEAD_DIM]
    k = k * c_ref[...] + pltpu.roll(k, HEAD_DIM // 2, 1) * s_ref[...]
    ko_ref[...] = k.astype(ko_ref.dtype)
    vo_ref[...] = kv[:, HEAD_DIM:].astype(vo_ref.dtype)


def _dsa_kernel(l_ref, q_ref, kd_ref, vd_ref, iq_ref, ika_ref, ikb_ref, w_ref, o_ref,
                key_ref, keyt_ref, hi_ref, lo_ref, wb_ref, sel_ref, q_sc, s_ref, mx_ref, acc_ref):
    qi = pl.program_id(1)
    nh = DSA_HEADS
    seq = kd_ref.shape[0]
    k_top = float(min(IDX_TOPK_MAX, seq // 4))
    row = lax.broadcasted_iota(I32, (TQ, TK), 0)
    col = lax.broadcasted_iota(I32, (TQ, TK), 1)
    int_min = jnp.int32(-2 ** 31)
    idx_bits = (seq - 1).bit_length()

    w_all = w_ref[...] * (IDX_HEADS ** -0.5 * IDX_DIM ** -0.5)
    for h in range(IDX_HEADS):
        wb_ref[h] = jnp.broadcast_to(w_all[:, MISC_IDX_W + h:MISC_IDX_W + h + 1], (TQ, LANE))

    def score_step(j, diagonal):
        start = pl.multiple_of(j * TK, TK)
        ka = ika_ref[pl.ds(start, TK), :]
        kb = ikb_ref[pl.ds(start, TK), :]
        sc = jnp.zeros((TQ, TK), F32)
        for g in range(IDX_HEADS // 2):
            qpair = iq_ref[:, g * LANE:(g + 1) * LANE]
            wa = _tile_lanes(wb_ref[2 * g], TK // LANE)
            wb = _tile_lanes(wb_ref[2 * g + 1], TK // LANE)
            sc = sc + jnp.maximum(_dot_nt(qpair, ka), 0.0) * wa + jnp.maximum(_dot_nt(qpair, kb), 0.0) * wb
        if diagonal:
            sc = jnp.where(col <= row, sc, NEG_INF)
        bits = lax.bitcast_convert_type(sc, I32)
        key = jnp.where(bits < 0, bits ^ jnp.int32(0x7FFFFFFF), bits)
        key = jnp.where(sc == 0.0, 0, key)
        key_ref[j] = key
        key_t = key.T
        keyt_ref[j] = key_t
        hi_ref[j] = (key_t >> 16).astype(I16)
        lo_ref[j] = ((key_t & 0xFFFF) - 32768).astype(I16)

    def score_body(j, carry):
        score_step(j, False)
        return carry

    _fori_pairs(qi, score_body, 0)
    score_step(qi, True)

    krow = lax.broadcasted_iota(I32, (TK, TQ), 0)

    def count(pred_fn):
        def body(j, acc):
            hit = jnp.where(pred_fn(keyt_ref[j], j), 1.0, 0.0)
            return acc + jnp.sum(hit.reshape(TK // 8, 8, TQ), axis=0)
        part = _fori_pairs(qi + 1, body, jnp.zeros((8, TQ), F32))
        return jnp.sum(part, axis=0, keepdims=True)

    def count16(ref, cand, strict):
        cand16 = cand.astype(I16)

        def body(j, acc):
            x = ref[j]
            hit = jnp.where(x > cand16 if strict else x >= cand16, jnp.int16(1), jnp.int16(0))
            for g in range(TK // 16):
                acc = acc + hit[g * 16:(g + 1) * 16, :]
            return acc
        part = _fori_pairs(qi + 1, body, jnp.zeros((16, TQ), I16))
        return jnp.sum(part.astype(F32), axis=0, keepdims=True)

    def bisect16(ref, need):
        zero = jnp.zeros((1, TQ), I32)
        v = jnp.where(count16(ref, zero, False) >= need, zero, zero - 32768)

        def bit(i, v):
            cand = v + (jnp.int32(1) << (14 - i))
            return jnp.where(count16(ref, cand, False) >= need, cand, v)
        return lax.fori_loop(0, 15, bit, v)

    hi_thr = bisect16(hi_ref, k_top)
    need_lo = k_top - count16(hi_ref, hi_thr, True)
    hi_thr16 = hi_thr.astype(I16)

    def mask_lo(j, carry):
        lo_ref[j] = jnp.where(hi_ref[j] == hi_thr16, lo_ref[j], jnp.int16(-32768))
        return carry

    _fori_pairs(qi + 1, mask_lo, 0)
    lo_thr = bisect16(lo_ref, need_lo)
    thr = lax.shift_left(hi_thr, jnp.int32(16)) + (lo_thr + 32768)
    n_gt = count(lambda key, j: key > thr)
    n_ge = count(lambda key, j: key >= thr)

    def index_bisect():
        def index_bit(i, cut):
            cand = cut + (jnp.int32(1) << (idx_bits - 1 - i))
            n_before = count(lambda key, j: (key == thr) & (j * TK + krow < cand))
            return jnp.where(n_gt + n_before < k_top, cand, cut)
        return lax.fori_loop(0, idx_bits, index_bit, jnp.zeros((1, TQ), I32))

    def take_all_ties():
        return jnp.full((1, TQ), seq - 1, I32)

    cut = lax.cond(jnp.max(n_ge) > k_top, index_bisect, take_all_ties)
    sel_ref[0] = jnp.broadcast_to(thr, (LANE, TQ)).T
    sel_ref[1] = jnp.broadcast_to(cut, (LANE, TQ)).T

    for h in range(nh):
        q_sc[h * TQ:(h + 1) * TQ, :] = q_ref[:, h * HEAD_DIM:(h + 1) * HEAD_DIM]
    mx_ref[...] = jnp.full(mx_ref.shape, NEG_INF, F32)
    acc_ref[...] = jnp.zeros(acc_ref.shape, F32)

    def attn_score_step(j, diagonal):
        start = pl.multiple_of(j * TK, TK)
        key = key_ref[j]
        thr_t = _tile_lanes(sel_ref[0], TK // LANE)
        cut_t = _tile_lanes(sel_ref[1], TK // LANE)
        vis = (key > thr_t) | ((key == thr_t) & (j * TK + col <= cut_t))
        if diagonal:
            vis = vis & (col <= row)
        s = (_dot_nt(q_sc[...], kd_ref[pl.ds(start, TK), :]) * SCALE_LOG2E).reshape(nh, TQ, TK)
        s = jnp.where(vis[None], s, NEG_INF).reshape(nh * TQ, TK)
        s_ref[j] = s
        mx_ref[...] = jnp.maximum(mx_ref[...], _max_lanes(s))

    def attn_score_body(j, carry):
        attn_score_step(j, False)
        return carry

    _fori_pairs(qi, attn_score_body, 0)
    attn_score_step(qi, True)
    mx_ref[...] = _row_max_to_lanes(mx_ref[...])

    def pv_body(j, carry):
        start = pl.multiple_of(j * TK, TK)
        acc_ref[...] += _exp_pv(s_ref[j], mx_ref[...], vd_ref[pl.ds(start, TK), :])
        return carry

    _fori_pairs(qi + 1, pv_body, 0)
    a = acc_ref[...]
    out = a[:, :HEAD_DIM] / a[:, HEAD_DIM:]
    for h in range(nh):
        o_ref[:, h * HEAD_DIM:(h + 1) * HEAD_DIM] = out[h * TQ:(h + 1) * TQ].astype(o_ref.dtype)


def _merge_kernel(l_ref, hb_ref, of_ref, on_ref, od_ref, wg1_ref, wg2_ref, wg3_ref, wf_ref, wn_ref, wd_ref, o_ref):
    hb = hb_ref[...]
    mixed = (jax.nn.sigmoid(_dot(hb, wg1_ref[...])) * _dot(of_ref[...], wf_ref[...])
             + jax.nn.sigmoid(_dot(hb, wg2_ref[...])) * _dot(on_ref[...], wn_ref[...])
             + jax.nn.sigmoid(_dot(hb, wg3_ref[...])) * _dot(od_ref[...], wd_ref[...]))
    o_ref[...] = mixed.astype(o_ref.dtype)


LN_SUB_ROWS = 256


def _layer_norm_store(y, g_ref, b_ref, of_ref, ob_ref, rows):
    mu = jnp.mean(y, axis=-1, keepdims=True)
    yc = y - mu
    var = jnp.mean(jnp.square(yc), axis=-1, keepdims=True)
    out = yc * lax.rsqrt(var + 1e-5) * g_ref[...] + b_ref[...]
    of_ref[rows, :] = out
    if ob_ref is not None:
        ob_ref[rows, :] = out.astype(ob_ref.dtype)


def _out_ln_kernel(l_ref, x_ref, w_ref, h_ref, g_ref, b_ref, of_ref, ob_ref):
    w = w_ref[...]
    for r in range(x_ref.shape[0] // LN_SUB_ROWS):
        rows = slice(r * LN_SUB_ROWS, (r + 1) * LN_SUB_ROWS)
        y = ALPHA * h_ref[rows, :] + _dot(x_ref[rows, :], w)
        _layer_norm_store(y, g_ref, b_ref, of_ref, ob_ref, rows)


def _out_ln(name, lidx, x, w, h, g, b):
    t, kdim = x.shape
    d = h.shape[1]
    tm = 512
    return _call(_out_ln_kernel, name, lidx, [x, w, h, g, b], grid=(t // tm,),
                 in_specs=[pl.BlockSpec((tm, kdim), lambda i, l: (i, 0)),
                           pl.BlockSpec((None, kdim, d), lambda i, l: (l[0], 0, 0)),
                           pl.BlockSpec((tm, d), lambda i, l: (i, 0)),
                           pl.BlockSpec((None, 1, d), lambda i, l: (l[0], 0, 0)),
                           pl.BlockSpec((None, 1, d), lambda i, l: (l[0], 0, 0))],
                 out_specs=[pl.BlockSpec((tm, d), lambda i, l: (i, 0)),
                            pl.BlockSpec((tm, d), lambda i, l: (i, 0))],
                 out_shape=[jax.ShapeDtypeStruct((t, d), F32), jax.ShapeDtypeStruct((t, d), BF16)])


def _ffn_ln_kernel(l_ref, x_ref, w_ref, h_ref, g_ref, b_ref, of_ref, *, nk):
    k = pl.program_id(1)

    @pl.when(k == 0)
    def _():
        of_ref[...] = ALPHA * h_ref[...] + _dot(x_ref[...], w_ref[...])

    @pl.when((k > 0) & (k < nk - 1))
    def _():
        of_ref[...] += _dot(x_ref[...], w_ref[...])

    @pl.when(k == nk - 1)
    def _():
        w = w_ref[...]
        for r in range(x_ref.shape[0] // LN_SUB_ROWS):
            rows = slice(r * LN_SUB_ROWS, (r + 1) * LN_SUB_ROWS)
            y = of_ref[rows, :] + _dot(x_ref[rows, :], w)
            _layer_norm_store(y, g_ref, b_ref, of_ref, None, rows)


def _ffn_ln(name, lidx, x, w, h, g, b, tk):
    t, kdim = x.shape
    d = h.shape[1]
    tm = 1024
    nk = kdim // tk
    assert nk >= 2
    return _call(functools.partial(_ffn_ln_kernel, nk=nk), name, lidx, [x, w, h, g, b],
                 grid=(t // tm, nk),
                 in_specs=[pl.BlockSpec((tm, tk), lambda i, k, l: (i, k)),
                           pl.BlockSpec((None, tk, d), lambda i, k, l: (l[0], k, 0)),
                           pl.BlockSpec((tm, d), lambda i, k, l: (i, 0)),
                           pl.BlockSpec((None, 1, d), lambda i, k, l: (l[0], 0, 0)),
                           pl.BlockSpec((None, 1, d), lambda i, k, l: (l[0], 0, 0))],
                 out_specs=pl.BlockSpec((tm, d), lambda i, k, l: (i, 0)),
                 out_shape=jax.ShapeDtypeStruct((t, d), F32),
                 vmem_limit_bytes=VMEM_LIMIT_LARGE_BYTES)


def _swiglu_kernel(l_ref, x_ref, wa_ref, wb_ref, o_ref, wab_ref):
    @pl.when(pl.program_id(1) == 0)
    def _():
        wab_ref[0] = wa_ref[...].astype(BF16)
        wab_ref[1] = wb_ref[...].astype(BF16)

    x = x_ref[...]
    a = _dot(x, wab_ref[0])
    o_ref[...] = (jax.nn.silu(a) * _dot(x, wab_ref[1])).astype(o_ref.dtype)


def _ple_kernel(l_ref, h_ref, p_ref, wpi_ref, wpg_ref, of_ref, ob_ref):
    wpi = wpi_ref[...]
    wpg = wpg_ref[...]
    for r in range(h_ref.shape[0] // LN_SUB_ROWS):
        rows = slice(r * LN_SUB_ROWS, (r + 1) * LN_SUB_ROWS)
        h = h_ref[rows, :]
        p_in = _dot(p_ref[rows, :].astype(BF16), wpi)
        out = h + p_in * jax.nn.sigmoid(_dot(h.astype(BF16), wpg))
        of_ref[rows, :] = out
        ob_ref[rows, :] = out.astype(ob_ref.dtype)


def _rope_tables(n, dim):
    inv = 1.0 / (ROPE_THETA ** (jnp.arange(0, dim, 2, dtype=F32) / dim))
    ang = jnp.arange(n, dtype=F32)[:, None] * inv[None, :]
    return jnp.cos(ang), jnp.sin(ang)


def _layer(lidx, h, hb, consts, weights, bsz, seq):
    (cos128, sin128, cos64, sin64a, sin64b, cmp_cos, cmp_sin, overlap, block_of_key) = consts
    (w_proj, fox_bias, pe_k, pe_v, wk1, wk2, wv1, wv2, kv_norm, kv_up,
     w_br_fox, w_br_nsa, w_br_dsa, w_out, ln1_g, ln1_b, w_ffn_in, w_ffn_out, ln2_g, ln2_b, p, w_ple_in, w_ple_gate) = weights
    t = bsz * seq
    nq = seq // TQ
    d = D_MODEL

    z_a = _proj("proj_plain", lidx, hb, w_proj, COL_PLAIN, N_PLAIN, 2048, 512, BF16, seq)
    z_b = _proj("proj_rope128", lidx, hb, w_proj, COL_ROPE128, N_ROPE128, 2048, 512, BF16, seq,
                rope=((HEAD_DIM // 2,), cos128, sin128))
    z_c = _proj("proj_rope64", lidx, hb, w_proj, COL_ROPE64, N_ROPE64, 2048, 256, BF16, seq,
                rope=((IDX_DIM // 2, LANE - IDX_DIM // 2), cos64, sin64a, sin64b))
    z_d = _proj("proj_f32", lidx, hb, w_proj, COL_F32, N_F32, 1024, N_F32, F32, seq)

    cum = _call(_cum_kernel, "fox_cum", lidx, [z_d, fox_bias], grid=(bsz,),
                in_specs=[pl.BlockSpec((seq, LANE), lambda b, l: (b, ZD_FOX_F)),
                          pl.BlockSpec((None, 1, LANE), lambda b, l: (l[0], 0, 0))],
                out_specs=pl.BlockSpec((seq, LANE), lambda b, l: (b, 0)),
                out_shape=jax.ShapeDtypeStruct((t, LANE), F32))
    cum_rows = jnp.transpose(cum.reshape(bsz, seq, LANE)[:, :, :8], (0, 2, 1)).reshape(bsz, 8 * (seq // TK), TK)
    fq = FOX_HEADS * HEAD_DIM
    o_fox = _call(_fox_kernel, "fox_attn", lidx, [z_a, z_a, z_a, cum, cum_rows], grid=(bsz, nq),
                  in_specs=[pl.BlockSpec((TQ, fq), lambda b, i, l: (b * nq + i, 0)),
                            pl.BlockSpec((seq, fq), lambda b, i, l: (b, 1)),
                            pl.BlockSpec((seq, fq), lambda b, i, l: (b, 2)),
                            pl.BlockSpec((TQ, LANE), lambda b, i, l: (b * nq + i, 0)),
                            pl.BlockSpec((None, 8 * (seq // TK), TK), lambda b, i, l: (b, 0, 0))],
                  out_specs=pl.BlockSpec((TQ, fq), lambda b, i, l: (b * nq + i, 0)),
                  out_shape=jax.ShapeDtypeStruct((t, fq), BF16),
                  scratch_shapes=[pltpu.VMEM((FOX_HEADS, seq // TK, TQ, TK), F32),
                                  pltpu.VMEM((FOX_HEADS, TQ, LANE), F32), pltpu.VMEM((FOX_HEADS, TQ, LANE), F32),
                                  pltpu.VMEM((FOX_HEADS, TQ, 2 * HEAD_DIM), F32)])

    nchunk = seq // CMP_STRIDE
    half = CMP_STRIDE * HEAD_DIM
    r_k = z_d[:, ZD_NSA_KC * LANE:(ZD_NSA_KC + 1) * LANE].reshape(bsz * nchunk, half)
    r_v = z_d[:, ZD_NSA_VC * LANE:(ZD_NSA_VC + 1) * LANE].reshape(bsz * nchunk, half)
    k_cmp, v_cmp = _call(
        _cmp_kernel, "nsa_compress", lidx, [r_k, r_v, pe_k, pe_v, wk1, wk2, wv1, wv2, cmp_cos, cmp_sin], grid=(bsz,),
        in_specs=[pl.BlockSpec((nchunk, half), lambda b, l: (b, 0)),
                  pl.BlockSpec((nchunk, half), lambda b, l: (b, 0)),
                  pl.BlockSpec((None, 2, half), lambda b, l: (l[0], 0, 0)),
                  pl.BlockSpec((None, 2, half), lambda b, l: (l[0], 0, 0)),
                  pl.BlockSpec((None, 2 * half, CMP_HIDDEN), lambda b, l: (l[0], 0, 0)),
                  pl.BlockSpec((None, CMP_HIDDEN, HEAD_DIM), lambda b, l: (l[0], 0, 0)),
                  pl.BlockSpec((None, 2 * half, CMP_HIDDEN), lambda b, l: (l[0], 0, 0)),
                  pl.BlockSpec((None, CMP_HIDDEN, HEAD_DIM), lambda b, l: (l[0], 0, 0)),
                  pl.BlockSpec((nchunk, HEAD_DIM), lambda b, l: (0, 0)),
                  pl.BlockSpec((nchunk, HEAD_DIM), lambda b, l: (0, 0))],
        out_specs=[pl.BlockSpec((nchunk, HEAD_DIM), lambda b, l: (b, 0)),
                   pl.BlockSpec((nchunk, HEAD_DIM), lambda b, l: (b, 0))],
        out_shape=[jax.ShapeDtypeStruct((bsz * nchunk, HEAD_DIM), BF16)] * 2)
    nsq = NSA_HEADS * HEAD_DIM
    o_nsa = _call(
        _nsa_kernel, "nsa_attn", lidx, [z_b, z_b, z_b, z_a, z_a, k_cmp, v_cmp, z_d, overlap, block_of_key], grid=(bsz, nq),
        in_specs=[pl.BlockSpec((TQ, nsq), lambda b, i, l: (b * nq + i, 0)),
                  pl.BlockSpec((seq, LANE), lambda b, i, l: (b, 4)),
                  pl.BlockSpec((seq, LANE), lambda b, i, l: (b, 5)),
                  pl.BlockSpec((seq, LANE), lambda b, i, l: (b, 18)),
                  pl.BlockSpec((seq, LANE), lambda b, i, l: (b, 19)),
                  pl.BlockSpec((nchunk, HEAD_DIM), lambda b, i, l: (b, 0)),
                  pl.BlockSpec((nchunk, HEAD_DIM), lambda b, i, l: (b, 0)),
                  pl.BlockSpec((TQ, LANE), lambda b, i, l: (b * nq + i, ZD_NSA_G)),
                  pl.BlockSpec((N_CMP_PAD, LANE), lambda b, i, l: (0, 0)),
                  pl.BlockSpec((seq, LANE), lambda b, i, l: (0, 0))],
        out_specs=pl.BlockSpec((TQ, nsq), lambda b, i, l: (b * nq + i, 0)),
        out_shape=jax.ShapeDtypeStruct((t, nsq), BF16),
        scratch_shapes=[pltpu.VMEM((NSA_HEADS * TQ, HEAD_DIM), BF16),
                        pltpu.VMEM((NSA_HEADS * TQ, HEAD_DIM), F32),
                        pltpu.VMEM((seq // TK, NSA_HEADS * TQ, TK), F32),
                        pltpu.VMEM((NSA_HEADS * TQ, LANE), F32),
                        pltpu.VMEM((NSA_HEADS * TQ, 2 * HEAD_DIM), F32)])

    tm_kv = 1024
    k_d, v_d = _call(
        _dsa_kv_kernel, "dsa_kv", lidx, [z_d, kv_norm, kv_up, cos128, sin128], grid=(t // tm_kv,),
        in_specs=[pl.BlockSpec((tm_kv, DSA_KV_RANK), lambda i, l: (i, 0)),
                  pl.BlockSpec((None, 1, DSA_KV_RANK), lambda i, l: (l[0], 0, 0)),
                  pl.BlockSpec((None, DSA_KV_RANK, 2 * HEAD_DIM), lambda i, l: (l[0], 0, 0)),
                  pl.BlockSpec((tm_kv, LANE), lambda i, l: (i % (seq // tm_kv), 0)),
                  pl.BlockSpec((tm_kv, LANE), lambda i, l: (i % (seq // tm_kv), 0))],
        out_specs=[pl.BlockSpec((tm_kv, HEAD_DIM), lambda i, l: (i, 0)),
                   pl.BlockSpec((tm_kv, HEAD_DIM), lambda i, l: (i, 0))],
        out_shape=[jax.ShapeDtypeStruct((t, HEAD_DIM), BF16)] * 2)
    dq = DSA_HEADS * HEAD_DIM
    iqw = IDX_HEADS * IDX_DIM
    o_dsa = _call(
        _dsa_kernel, "dsa_attn", lidx, [z_b, k_d, v_d, z_c, z_c, z_c, z_d], grid=(bsz, nq),
        in_specs=[pl.BlockSpec((TQ, dq), lambda b, i, l: (b * nq + i, 1)),
                  pl.BlockSpec((seq, HEAD_DIM), lambda b, i, l: (b, 0)),
                  pl.BlockSpec((seq, HEAD_DIM), lambda b, i, l: (b, 0)),
                  pl.BlockSpec((TQ, iqw), lambda b, i, l: (b * nq + i, 0)),
                  pl.BlockSpec((seq, LANE), lambda b, i, l: (b, iqw // LANE)),
                  pl.BlockSpec((seq, LANE), lambda b, i, l: (b, iqw // LANE + 1)),
                  pl.BlockSpec((TQ, LANE), lambda b, i, l: (b * nq + i, ZD_IDX_W))],
        out_specs=pl.BlockSpec((TQ, dq), lambda b, i, l: (b * nq + i, 0)),
        out_shape=jax.ShapeDtypeStruct((t, dq), BF16),
        scratch_shapes=[pltpu.VMEM((seq // TK, TQ, TK), I32), pltpu.VMEM((seq // TK, TK, TQ), I32),
                        pltpu.VMEM((seq // TK, TK, TQ), I16), pltpu.VMEM((seq // TK, TK, TQ), I16),
                        pltpu.VMEM((IDX_HEADS, TQ, LANE), F32), pltpu.VMEM((2, TQ, LANE), I32),
                        pltpu.VMEM((DSA_HEADS * TQ, HEAD_DIM), BF16),
                        pltpu.VMEM((seq // TK, DSA_HEADS * TQ, TK), F32),
                        pltpu.VMEM((DSA_HEADS * TQ, LANE), F32),
                        pltpu.VMEM((DSA_HEADS * TQ, 2 * HEAD_DIM), F32)])

    tm, tn = 1024, 512
    ncol = d // tn
    gate0 = COL_GATE // tn
    assert gate0 * tn == COL_GATE
    mixed = _call(
        _merge_kernel, "merge", lidx, [hb, o_fox, o_nsa, o_dsa, w_proj, w_proj, w_proj, w_br_fox, w_br_nsa, w_br_dsa],
        grid=(t // tm, ncol),
        in_specs=[pl.BlockSpec((tm, d), lambda i, j, l: (i, 0)),
                  pl.BlockSpec((tm, fq), lambda i, j, l: (i, 0)),
                  pl.BlockSpec((tm, nsq), lambda i, j, l: (i, 0)),
                  pl.BlockSpec((tm, dq), lambda i, j, l: (i, 0)),
                  pl.BlockSpec((None, d, tn), lambda i, j, l: (l[0], 0, gate0 + j)),
                  pl.BlockSpec((None, d, tn), lambda i, j, l: (l[0], 0, gate0 + ncol + j)),
                  pl.BlockSpec((None, d, tn), lambda i, j, l: (l[0], 0, gate0 + 2 * ncol + j)),
                  pl.BlockSpec((None, fq, tn), lambda i, j, l: (l[0], 0, j)),
                  pl.BlockSpec((None, nsq, tn), lambda i, j, l: (l[0], 0, j)),
                  pl.BlockSpec((None, dq, tn), lambda i, j, l: (l[0], 0, j))],
        out_specs=pl.BlockSpec((tm, tn), lambda i, j, l: (i, j)),
        out_shape=jax.ShapeDtypeStruct((t, d), BF16))

    h, hb = _out_ln("out_ln", lidx, mixed, w_out, h, ln1_g, ln1_b)

    nff = D_FF // tn
    act = _call(
        _swiglu_kernel, "swiglu", lidx, [hb, w_ffn_in, w_ffn_in], grid=(nff, t // tm),
        in_specs=[pl.BlockSpec((tm, d), lambda j, i, l: (i, 0)),
                  pl.BlockSpec((None, d, tn), lambda j, i, l: (l[0], 0, j)),
                  pl.BlockSpec((None, d, tn), lambda j, i, l: (l[0], 0, nff + j))],
        out_specs=pl.BlockSpec((tm, tn), lambda j, i, l: (i, j)),
        out_shape=jax.ShapeDtypeStruct((t, D_FF), BF16),
        scratch_shapes=[pltpu.VMEM((2, d, tn), BF16)])
    h = _ffn_ln("ffn_ln", lidx, act, w_ffn_out, h, ln2_g, ln2_b, tk=D_FF // 4)

    tmp = 512
    h, hb = _call(
        _ple_kernel, "ple", lidx, [h, p, w_ple_in, w_ple_gate], grid=(t // tmp,),
        in_specs=[pl.BlockSpec((tmp, d), lambda i, l: (i, 0)),
                  pl.BlockSpec((None, tmp, PLE_DIM), lambda i, l: (l[0], i, 0)),
                  pl.BlockSpec((None, PLE_DIM, d), lambda i, l: (l[0], 0, 0)),
                  pl.BlockSpec((None, d, d), lambda i, l: (l[0], 0, 0))],
        out_specs=[pl.BlockSpec((tmp, d), lambda i, l: (i, 0)),
                   pl.BlockSpec((tmp, d), lambda i, l: (i, 0))],
        out_shape=[jax.ShapeDtypeStruct((t, d), F32), jax.ShapeDtypeStruct((t, d), BF16)])
    return h, hb


def kernel(x, p, w_in, fox_f_bias, nsa_pe_k, nsa_pe_v, nsa_cmp_k1, nsa_cmp_k2, nsa_cmp_v1, nsa_cmp_v2, dsa_kv_norm, dsa_kv_up, w_br_fox, w_br_nsa, w_br_dsa, w_out, ln1_g, ln1_b, w_ffn_in, w_ffn_out, ln2_g, ln2_b, w_ple_in, w_ple_gate):
    bsz, seq, d = x.shape
    depth = w_in.shape[0]
    t = bsz * seq
    assert d == D_MODEL and seq % 1024 == 0 and depth == DEPTH

    cat = functools.partial(jnp.concatenate, axis=-1)
    w_proj = _regroup(w_in)

    fox_bias = jnp.pad(fox_f_bias, ((0, 0), (0, LANE - FOX_HEADS))).reshape(depth, 1, LANE)
    half = CMP_STRIDE * HEAD_DIM
    weights = (
        w_proj, fox_bias,
        nsa_pe_k.reshape(depth, 2, half), nsa_pe_v.reshape(depth, 2, half),
        nsa_cmp_k1.astype(BF16), nsa_cmp_k2.astype(BF16), nsa_cmp_v1.astype(BF16), nsa_cmp_v2.astype(BF16),
        dsa_kv_norm.reshape(depth, 1, DSA_KV_RANK), dsa_kv_up.astype(BF16),
        w_br_fox.astype(BF16), w_br_nsa.astype(BF16), w_br_dsa.astype(BF16), w_out.astype(BF16),
        ln1_g.reshape(depth, 1, d), ln1_b.reshape(depth, 1, d),
        w_ffn_in, w_ffn_out.astype(BF16),
        ln2_g.reshape(depth, 1, d), ln2_b.reshape(depth, 1, d),
        p.reshape(depth, t, PLE_DIM), w_ple_in.astype(BF16), w_ple_gate.astype(BF16),
    )

    cos, sin = _rope_tables(seq, HEAD_DIM)
    cos128 = cat([cos, cos])
    sin128 = cat([-sin, sin])
    cos_i, sin_i = _rope_tables(seq, IDX_DIM)
    zi = jnp.zeros_like(sin_i)
    cos64 = cat([cos_i, cos_i, cos_i, cos_i])
    sin64a = cat([zi, sin_i, zi, sin_i])
    sin64b = cat([-sin_i, zi, -sin_i, zi])
    n_cmp = (seq - CMP_LEN) // CMP_STRIDE + 1
    c_end = jnp.minimum(jnp.arange(N_CMP_PAD) * CMP_STRIDE + CMP_LEN - 1, seq - 1)
    cmp_cos, cmp_sin = cos128[c_end], sin128[c_end]
    n_slc = seq // SLC_LEN
    c_start = jnp.arange(N_CMP_PAD) * CMP_STRIDE
    s_start = jnp.arange(LANE) * SLC_LEN
    overlap = jnp.maximum(jnp.minimum(c_start[:, None] + CMP_LEN - 1, s_start[None, :] + SLC_LEN - 1)
                          - jnp.maximum(c_start[:, None], s_start[None, :]) + 1, 0).astype(F32) / CMP_LEN
    overlap = jnp.where((jnp.arange(N_CMP_PAD)[:, None] < n_cmp) & (jnp.arange(LANE)[None, :] < n_slc), overlap, 0.0)
    overlap = overlap.T.astype(BF16)
    block_of_key = (jnp.arange(seq)[:, None] // SLC_LEN == jnp.arange(LANE)[None, :]).astype(BF16)
    consts = (cos128, sin128, cos64, sin64a, sin64b, cmp_cos, cmp_sin, overlap, block_of_key)

    h = x.reshape(t, d)
    hb = h.astype(BF16)
    for layer in range(depth):
        lidx = jnp.full((1,), layer, I32)
        h, hb = _layer(lidx, h, hb, consts, weights, bsz, seq)
    return h.reshape(bsz, seq, d)
```

```python
import functools

import jax
import jax.numpy as jnp
from jax import lax
from jax.experimental import pallas as pl
from jax.experimental.pallas import tpu as pltpu

F32, BF16, I32, I16 = jnp.float32, jnp.bfloat16, jnp.int32, jnp.int16

D_MODEL = 2048
DEPTH = 4
HEAD_DIM = 128
ROPE_THETA = 10000.0
NEG_INF = -1e30
FOX_HEADS = 6
NSA_HEADS = 4
CMP_LEN = 32
CMP_STRIDE = 16
CMP_HIDDEN = 256
SLC_LEN = 64
SLC_TOPN = 16
WIN = 512
SLC_FORCE = 1e4
DSA_HEADS = 6
DSA_KV_RANK = 256
IDX_HEADS = 16
IDX_DIM = 64
IDX_TOPK_MAX = 256
D_FF = ((8 * D_MODEL + 3 * 256 - 1) // (3 * 256)) * 256
PLE_DIM = 256
ALPHA = (2 * DEPTH) ** 0.25
SCALE = HEAD_DIM ** -0.5
LOG2E = 1.4426950408889634
SCALE_LOG2E = SCALE * LOG2E

LANE = 128
VMEM_LIMIT_BYTES = 48 * 1024 * 1024
VMEM_LIMIT_LARGE_BYTES = 58 * 1024 * 1024

_IN_SPLITS = (
    ('fox_q', FOX_HEADS * HEAD_DIM), ('fox_k', FOX_HEADS * HEAD_DIM), ('fox_v', FOX_HEADS * HEAD_DIM), ('fox_f', FOX_HEADS),
    ('nsa_q', NSA_HEADS * HEAD_DIM), ('nsa_kc', HEAD_DIM), ('nsa_vc', HEAD_DIM), ('nsa_ks', HEAD_DIM), ('nsa_vs', HEAD_DIM),
    ('nsa_kw', HEAD_DIM), ('nsa_vw', HEAD_DIM), ('nsa_g', 3 * NSA_HEADS),
    ('dsa_q', DSA_HEADS * HEAD_DIM), ('dsa_ckv', DSA_KV_RANK), ('idx_q', IDX_HEADS * IDX_DIM), ('idx_k', IDX_DIM), ('idx_w', IDX_HEADS),
    ('gate', 3 * D_MODEL),
)
_IN_OFFSETS = {}
_off = 0
for _name, _width in _IN_SPLITS:
    _IN_OFFSETS[_name] = (_off, _off + _width)
    _off += _width

MISC_FOX_F = 0
MISC_NSA_G = FOX_HEADS
MISC_IDX_W = FOX_HEADS + 3 * NSA_HEADS

TQ = 256
TK = 256
N_CMP_PAD = 128


def _dot(a, b):
    return jnp.dot(a, b, preferred_element_type=F32)


def _dot_nt(a, b):
    return lax.dot_general(a, b, (((1,), (1,)), ((), ())), preferred_element_type=F32)


def _fori_pairs(n, body, init):
    def four(jj, carry):
        for u in range(4):
            carry = body(4 * jj + u, carry)
        return carry
    carry = lax.fori_loop(0, lax.shift_right_logical(n, jnp.int32(2)), four, init)
    base = n & jnp.int32(-4)
    carry = lax.cond((n & 2) == 2, lambda c: body(base + 1, body(base, c)), lambda c: c, carry)
    return lax.cond((n & 1) == 1, lambda c: body(n - 1, c), lambda c: c, carry)


def _split3(a):
    a1 = a.astype(BF16)
    r1 = a - a1.astype(F32)
    a2 = r1.astype(BF16)
    a3 = (r1 - a2.astype(F32)).astype(BF16)
    return a1, a2, a3


def _call(kernel, name, lidx, args, grid, in_specs, out_specs, out_shape, scratch_shapes=(),
          vmem_limit_bytes=VMEM_LIMIT_BYTES):
    return pl.pallas_call(
        kernel,
        grid_spec=pltpu.PrefetchScalarGridSpec(
            num_scalar_prefetch=1, grid=grid, in_specs=in_specs, out_specs=out_specs,
            scratch_shapes=list(scratch_shapes)),
        out_shape=out_shape,
        compiler_params=pltpu.CompilerParams(
            dimension_semantics=("arbitrary",) * len(grid), vmem_limit_bytes=vmem_limit_bytes),
        name=name,
    )(lidx, *args)


def _regroup_plan():
    plan = []

    def whole(name):
        a, b = _IN_OFFSETS[name]
        for g in range((b - a) // LANE):
            plan.append((a + g * LANE, 0, LANE))

    whole('dsa_ckv')
    lane0 = {}
    start = 0
    for name in ('fox_f', 'nsa_g', 'idx_w'):
        a, b = _IN_OFFSETS[name]
        lane0[name] = (a - start, start, start + b - a)
        start += b - a
    plan.append(lane0['fox_f'])
    whole('nsa_kc')
    whole('nsa_vc')
    plan.append(lane0['nsa_g'])
    plan.append(lane0['idx_w'])
    plan.append((0, 0, 0))
    for name in ('fox_q', 'fox_k', 'fox_v', 'nsa_vs', 'nsa_vw'):
        whole(name)
    for name in ('nsa_q', 'nsa_ks', 'nsa_kw', 'dsa_q'):
        whole(name)
    whole('gate')
    whole('idx_q')
    ik = _IN_OFFSETS['idx_k'][0]
    plan.append((ik, 0, IDX_DIM))
    plan.append((ik - IDX_DIM, IDX_DIM, LANE))
    return tuple(plan)


_REGROUP_PLAN = _regroup_plan()
COL_F32 = 0
N_F32 = DSA_KV_RANK + 6 * LANE
ZD_FOX_F, ZD_NSA_KC, ZD_NSA_VC, ZD_NSA_G, ZD_IDX_W = 2, 3, 4, 5, 6
COL_PLAIN = COL_F32 + N_F32
N_PLAIN = (3 * FOX_HEADS + 2) * HEAD_DIM
COL_ROPE128 = COL_PLAIN + N_PLAIN
N_ROPE128 = (NSA_HEADS + 2 + DSA_HEADS) * HEAD_DIM
COL_GATE = COL_ROPE128 + N_ROPE128
COL_ROPE64 = COL_GATE + 3 * D_MODEL
N_ROPE64 = IDX_HEADS * IDX_DIM + 2 * LANE
N_REGROUPED = COL_ROPE64 + N_ROPE64
assert N_REGROUPED == len(_REGROUP_PLAN) * LANE


def _regroup_kernel(src_ref, lo_ref, hi_ref, x_ref, o_ref):
    g = pl.program_id(0)
    col = lax.broadcasted_iota(I32, (LANE, x_ref.shape[2]), 0)
    keep = (col >= lo_ref[g]) & (col < hi_ref[g])
    for layer in range(x_ref.shape[1]):
        o_ref[layer] = jnp.where(keep, x_ref[:, layer, :], 0.0).T.astype(o_ref.dtype)


def _regroup(w_in):
    depth, d, n_in = w_in.shape
    w_t = jnp.transpose(w_in, (2, 0, 1))
    src = jnp.array([p[0] for p in _REGROUP_PLAN], I32)
    lo = jnp.array([p[1] for p in _REGROUP_PLAN], I32)
    hi = jnp.array([p[2] for p in _REGROUP_PLAN], I32)
    return pl.pallas_call(
        _regroup_kernel,
        grid_spec=pltpu.PrefetchScalarGridSpec(
            num_scalar_prefetch=3, grid=(len(_REGROUP_PLAN),),
            in_specs=[pl.BlockSpec((pl.Element(LANE), pl.Element(depth), pl.Element(d)),
                                   lambda g, src, lo, hi: (src[g], 0, 0))],
            out_specs=pl.BlockSpec((depth, d, LANE), lambda g, src, lo, hi: (0, 0, g))),
        out_shape=jax.ShapeDtypeStruct((depth, d, N_REGROUPED), BF16),
        compiler_params=pltpu.CompilerParams(dimension_semantics=("arbitrary",), vmem_limit_bytes=VMEM_LIMIT_BYTES),
        name="regroup_w_in",
    )(src, lo, hi, w_t)


def _proj_kernel(l_ref, x_ref, w_ref, *refs, shifts):
    o_ref = refs[-1]
    acc = _dot(x_ref[...], w_ref[...])
    if not shifts:
        o_ref[...] = acc.astype(o_ref.dtype)
        return
    c_ref = refs[0]
    s_refs = refs[1:-1]
    for g in range(acc.shape[1] // LANE):
        xg = acc[:, g * LANE:(g + 1) * LANE]
        out = xg * c_ref[...]
        for shift, s_ref in zip(shifts, s_refs):
            out = out + pltpu.roll(xg, shift, 1) * s_ref[...]
        o_ref[:, g * LANE:(g + 1) * LANE] = out.astype(o_ref.dtype)


def _proj(name, lidx, hb, w, col0, n, tm, tn, out_dtype, seq, rope=None):
    t, d = hb.shape
    jb = col0 // tn
    assert jb * tn == col0
    in_specs = [pl.BlockSpec((tm, d), lambda i, j, l: (i, 0)),
                pl.BlockSpec((None, d, tn), lambda i, j, l: (l[0], 0, jb + j))]
    args = [hb, w]
    shifts = ()
    if rope is not None:
        shifts, tables = rope[0], rope[1:]
        nrow = seq // tm
        for tab in tables:
            in_specs.append(pl.BlockSpec((tm, LANE), lambda i, j, l: (i % nrow, 0)))
            args.append(tab)
    return _call(functools.partial(_proj_kernel, shifts=shifts), name, lidx, args,
                 grid=(t // tm, n // tn), in_specs=in_specs,
                 out_specs=pl.BlockSpec((tm, tn), lambda i, j, l: (i, j)),
                 out_shape=jax.ShapeDtypeStruct((t, n), out_dtype))


def _cum_kernel(l_ref, f_ref, bias_ref, o_ref):
    chunk = 256
    row = lax.broadcasted_iota(I32, (chunk, chunk), 0)
    col = lax.broadcasted_iota(I32, (chunk, chunk), 1)
    tri = jnp.where(row >= col, 1.0, 0.0).astype(BF16)
    carry = jnp.zeros((1, LANE), F32)
    for c in range(f_ref.shape[0] // chunk):
        logf = jax.nn.log_sigmoid(f_ref[c * chunk:(c + 1) * chunk, :] + bias_ref[...])
        l1, l2, l3 = _split3(logf)
        cs = _dot(tri, l1) + _dot(tri, l2) + _dot(tri, l3) + carry
        o_ref[c * chunk:(c + 1) * chunk, :] = cs
        carry = cs[chunk - 1:chunk, :]


def _max_lanes(x):
    out = x[:, 0:LANE]
    for g in range(1, x.shape[1] // LANE):
        out = jnp.maximum(out, x[:, g * LANE:(g + 1) * LANE])
    return out


def _tile_lanes(x, n):
    return jnp.concatenate([x] * n, axis=1)


def _row_max_to_lanes(mx):
    return jnp.broadcast_to(jnp.max(mx, axis=-1, keepdims=True), mx.shape)


def _exp_pv(s, m_lanes, v):
    p = jnp.exp2(s - _tile_lanes(m_lanes, s.shape[1] // LANE)).astype(BF16)
    return _dot(p, jnp.concatenate([v, jnp.ones_like(v)], axis=1))


def _fox_kernel(l_ref, q_ref, k_ref, v_ref, cq_ref, ck_ref, o_ref, s_ref, mx_ref, cqb_ref, acc_ref):
    qi = pl.program_id(1)
    nh = FOX_HEADS
    nchunk = ck_ref.shape[0] // 8
    row = lax.broadcasted_iota(I32, (TQ, TK), 0)
    col = lax.broadcasted_iota(I32, (TQ, TK), 1)
    cq_all = cq_ref[...]
    for h in range(nh):
        cqb_ref[h] = jnp.broadcast_to(cq_all[:, h:h + 1] * LOG2E, (TQ, LANE))
    mx_ref[...] = jnp.full(mx_ref.shape, NEG_INF, F32)
    acc_ref[...] = jnp.zeros(acc_ref.shape, F32)

    def score_step(j, diagonal):
        start = pl.multiple_of(j * TK, TK)
        for h in range(nh):
            hs = slice(h * HEAD_DIM, (h + 1) * HEAD_DIM)
            s = (_dot_nt(q_ref[:, hs], k_ref[pl.ds(start, TK), hs]) * SCALE_LOG2E
                 + _tile_lanes(cqb_ref[h], TK // LANE) - ck_ref[pl.ds(h * nchunk + j, 1), :] * LOG2E)
            if diagonal:
                s = jnp.where(col <= row, s, NEG_INF)
            s_ref[h, j] = s
            mx_ref[h] = jnp.maximum(mx_ref[h], _max_lanes(s))

    def score_body(j, carry):
        score_step(j, False)
        return carry

    _fori_pairs(qi, score_body, 0)
    score_step(qi, True)
    for h in range(nh):
        mx_ref[h] = _row_max_to_lanes(mx_ref[h])

    def pv_body(j, carry):
        start = pl.multiple_of(j * TK, TK)
        for h in range(nh):
            hs = slice(h * HEAD_DIM, (h + 1) * HEAD_DIM)
            acc_ref[h] += _exp_pv(s_ref[h, j], mx_ref[h], v_ref[pl.ds(start, TK), hs])
        return carry

    _fori_pairs(qi + 1, pv_body, 0)
    for h in range(nh):
        a = acc_ref[h]
        o_ref[:, h * HEAD_DIM:(h + 1) * HEAD_DIM] = (a[:, :HEAD_DIM] / a[:, HEAD_DIM:]).astype(o_ref.dtype)


def _cmp_kernel(l_ref, rk_ref, rv_ref, pek_ref, pev_ref, wk1_ref, wk2_ref, wv1_ref, wv2_ref,
                c_ref, s_ref, ko_ref, vo_ref):
    half = CMP_STRIDE * HEAD_DIM

    def compress(r_ref, pe_ref, w1_ref, w2_ref):
        r = r_ref[...]
        lo = (r + pe_ref[0:1, :]).astype(BF16)
        hi = (r + pe_ref[1:2, :]).astype(BF16)
        a = _dot(lo, w1_ref[0:half, :])
        b = _dot(hi, w1_ref[half:2 * half, :])
        hid = a + pltpu.roll(b, N_CMP_PAD - 1, 0)
        return _dot(jax.nn.gelu(hid).astype(BF16), w2_ref[...])

    kc = compress(rk_ref, pek_ref, wk1_ref, wk2_ref)
    kc = kc * c_ref[...] + pltpu.roll(kc, HEAD_DIM // 2, 1) * s_ref[...]
    ko_ref[...] = kc.astype(ko_ref.dtype)
    vo_ref[...] = compress(rv_ref, pev_ref, wv1_ref, wv2_ref).astype(vo_ref.dtype)


def _masked_softmax(s, mask):
    s = jnp.where(mask, s, NEG_INF)
    m = jnp.max(s, axis=-1, keepdims=True)
    e = jnp.where(mask, jnp.exp(s - m), 0.0)
    den = jnp.sum(e, axis=-1, keepdims=True)
    return e, jnp.where(den > 0.0, den, 1.0)


def _nsa_kernel(l_ref, q_ref, ks_ref, kw_ref, vs_ref, vw_ref, kc_ref, vc_ref, g_ref, ovt_ref, et_ref,
                o_ref, q_sc, ow_ref, s_ref, mx_ref, acc_ref):
    qi = pl.program_id(1)
    nh = NSA_HEADS
    for h in range(nh):
        q_sc[h * TQ:(h + 1) * TQ, :] = q_ref[:, h * HEAD_DIM:(h + 1) * HEAD_DIM]
    q = q_sc[...]
    pos = qi * TQ + lax.broadcasted_iota(I32, (TQ, 1), 0)
    lane = lax.broadcasted_iota(I32, (TQ, LANE), 1)

    band = WIN + TQ
    start_w = pl.multiple_of(jnp.maximum(qi * TQ - WIN, 0), TQ)
    kwb = kw_ref[pl.ds(start_w, band), :]
    vwb = vw_ref[pl.ds(start_w, band), :]
    dist = pos - (start_w + lax.broadcasted_iota(I32, (TQ, band), 1))
    wvis = (dist >= 0) & (dist < WIN)
    for h in range(nh):
        s_w = _dot_nt(q_ref[:, h * HEAD_DIM:(h + 1) * HEAD_DIM], kwb) * SCALE_LOG2E
        s_w = jnp.where(wvis, s_w, NEG_INF)
        a_win = _exp_pv(s_w, _row_max_to_lanes(_max_lanes(s_w)), vwb)
        ow_ref[h * TQ:(h + 1) * TQ, :] = a_win[:, :HEAD_DIM] / a_win[:, HEAD_DIM:]

    cvis = (lane * CMP_STRIDE + (CMP_LEN - 1)) <= pos
    s_c = (_dot_nt(q, kc_ref[...]) * SCALE).reshape(nh, TQ, N_CMP_PAD)
    e_c, den_c = _masked_softmax(s_c, cvis[None])
    p_c = e_c / den_c
    o_cmp = _dot(p_c.reshape(nh * TQ, N_CMP_PAD).astype(BF16), vc_ref[...])

    p_sum = p_c[0]
    for h in range(1, nh):
        p_sum = p_sum + p_c[h]
    p1, p2, p3 = _split3(p_sum)
    ovt = ovt_ref[...]
    n_slc = et_ref.shape[0] // SLC_LEN
    imp = (_dot_nt(ovt, p1) + _dot_nt(ovt, p2) + _dot_nt(ovt, p3))[0:n_slc, :]
    blk = lax.broadcasted_iota(I32, (n_slc, TQ), 0)
    pos_t = qi * TQ + lax.broadcasted_iota(I32, (n_slc, TQ), 1)
    cur = lax.shift_right_logical(pos_t, SLC_LEN.bit_length() - 1)
    forced = (blk == 0) | (blk == cur) | (blk == cur - 1)
    imp = jnp.where(forced, SLC_FORCE, jnp.where(blk * SLC_LEN <= pos_t, imp, -SLC_FORCE))
    rank = jnp.zeros((n_slc, TQ), F32)
    for k in range(n_slc):
        ik = imp[k:k + 1, :]
        ahead = (ik > imp) | ((ik == imp) & (blk > k))
        rank = rank + jnp.where(ahead, 1.0, 0.0)
    sel_t = jnp.where(rank < float(min(SLC_TOPN, n_slc)), 1.0, 0.0)
    sel_t = jnp.concatenate([sel_t, jnp.zeros((LANE - n_slc, TQ), F32)], axis=0)
    sel = sel_t.T.astype(BF16)

    mx_ref[...] = jnp.full(mx_ref.shape, NEG_INF, F32)
    acc_ref[...] = jnp.zeros(acc_ref.shape, F32)
    row = lax.broadcasted_iota(I32, (TQ, TK), 0)
    col = lax.broadcasted_iota(I32, (TQ, TK), 1)

    def score_body(j, carry):
        start = pl.multiple_of(j * TK, TK)
        vis = _dot_nt(sel, et_ref[pl.ds(start, TK), :]) > 0.5
        vis = vis & (col - row <= (qi - j) * TK)
        s = (_dot_nt(q_sc[...], ks_ref[pl.ds(start, TK), :]) * SCALE_LOG2E).reshape(nh, TQ, TK)
        s = jnp.where(vis[None], s, NEG_INF).reshape(nh * TQ, TK)
        s_ref[j] = s
        mx_ref[...] = jnp.maximum(mx_ref[...], _max_lanes(s))
        return carry

    _fori_pairs(qi + 1, score_body, 0)
    mx_ref[...] = _row_max_to_lanes(mx_ref[...])

    def pv_body(j, carry):
        start = pl.multiple_of(j * TK, TK)
        acc_ref[...] += _exp_pv(s_ref[j], mx_ref[...], vs_ref[pl.ds(start, TK), :])
        return carry

    _fori_pairs(qi + 1, pv_body, 0)
    a_slc = acc_ref[...]
    o_slc = a_slc[:, :HEAD_DIM] / a_slc[:, HEAD_DIM:]

    o_win = ow_ref[...]
    gate = jax.nn.sigmoid(g_ref[...])
    for h in range(nh):
        rows = slice(h * TQ, (h + 1) * TQ)
        g0 = gate[:, MISC_NSA_G + 3 * h:MISC_NSA_G + 3 * h + 1]
        g1 = gate[:, MISC_NSA_G + 3 * h + 1:MISC_NSA_G + 3 * h + 2]
        g2 = gate[:, MISC_NSA_G + 3 * h + 2:MISC_NSA_G + 3 * h + 3]
        out = g0 * o_cmp[rows] + g1 * o_slc[rows] + g2 * o_win[rows]
        o_ref[:, h * HEAD_DIM:(h + 1) * HEAD_DIM] = out.astype(o_ref.dtype)


def _dsa_kv_kernel(l_ref, x_ref, g_ref, up_ref, c_ref, s_ref, ko_ref, vo_ref):
    x = x_ref[...]
    r = x * lax.rsqrt(jnp.mean(jnp.square(x), axis=-1, keepdims=True) + 1e-6) * g_ref[...]
    kv = _dot(r.astype(BF16), up_ref[...])
    k = kv[:, :HEAD_DIM]
    k = k * c_ref[...] + pltpu.roll(k, HEAD_DIM // 2, 1) * s_ref[...]
    ko_ref[...] = k.astype(ko_ref.dtype)
    vo_ref[...] = kv[:, HEAD_DIM:].astype(vo_ref.dtype)


def _dsa_kernel(l_ref, q_ref, kd_ref, vd_ref, iq_ref, ika_ref, ikb_ref, w_ref, o_ref,
                key_ref, keyt_ref, hi_ref, lo_ref, wt_ref, sel_ref, q_sc, s_ref, mx_ref, acc_ref):
    qi = pl.program_id(1)
    nh = DSA_HEADS
    seq = kd_ref.shape[0]
    k_top = float(min(IDX_TOPK_MAX, seq // 4))
    row = lax.broadcasted_iota(I32, (TQ, TK), 0)
    col = lax.broadcasted_iota(I32, (TQ, TK), 1)
    int_min = jnp.int32(-2 ** 31)
    idx_bits = (seq - 1).bit_length()

    wt_ref[...] = (w_ref[...] * (IDX_HEADS ** -0.5 * IDX_DIM ** -0.5)).T

    def score_body(j, carry):
        start = pl.multiple_of(j * TK, TK)
        ka = ika_ref[pl.ds(start, TK), :]
        kb = ikb_ref[pl.ds(start, TK), :]
        sc = jnp.zeros((TK, TQ), F32)
        for g in range(IDX_HEADS // 2):
            qpair = iq_ref[:, g * LANE:(g + 1) * LANE]
            wa = wt_ref[MISC_IDX_W + 2 * g:MISC_IDX_W + 2 * g + 1, :]
            wb = wt_ref[MISC_IDX_W + 2 * g + 1:MISC_IDX_W + 2 * g + 2, :]
            sc = sc + jnp.maximum(_dot_nt(ka, qpair), 0.0) * wa + jnp.maximum(_dot_nt(kb, qpair), 0.0) * wb
        sc = jnp.where(row - col <= (qi - j) * TK, sc, NEG_INF)
        bits = lax.bitcast_convert_type(sc, I32)
        key_t = jnp.where(bits < 0, bits ^ jnp.int32(0x7FFFFFFF), bits)
        key_t = jnp.where(sc == 0.0, 0, key_t)
        key_ref[j] = key_t.T
        keyt_ref[j] = key_t
        hi_ref[j] = (key_t >> 16).astype(I16)
        lo_ref[j] = ((key_t & 0xFFFF) - 32768).astype(I16)
        return carry

    _fori_pairs(qi + 1, score_body, 0)

    krow = lax.broadcasted_iota(I32, (TK, TQ), 0)

    def count(pred_fn):
        def body(j, acc):
            hit = jnp.where(pred_fn(keyt_ref[j], j), 1.0, 0.0)
            return acc + jnp.sum(hit.reshape(TK // 8, 8, TQ), axis=0)
        part = _fori_pairs(qi + 1, body, jnp.zeros((8, TQ), F32))
        return jnp.sum(part, axis=0, keepdims=True)

    def count16(ref, cand, strict):
        cand16 = cand.astype(I16)

        def body(j, acc):
            x = ref[j]
            hit = jnp.where(x > cand16 if strict else x >= cand16, jnp.int16(1), jnp.int16(0))
            for g in range(TK // 16):
                acc = acc + hit[g * 16:(g + 1) * 16, :]
            return acc
        part = _fori_pairs(qi + 1, body, jnp.zeros((16, TQ), I16))
        return jnp.sum(part.astype(F32), axis=0, keepdims=True)

    def bisect16(ref, need):
        zero = jnp.zeros((1, TQ), I32)
        v = jnp.where(count16(ref, zero, False) >= need, zero, zero - 32768)

        def bit(i, v):
            cand = v + (jnp.int32(1) << (14 - i))
            return jnp.where(count16(ref, cand, False) >= need, cand, v)
        return lax.fori_loop(0, 15, bit, v)

    hi_thr = bisect16(hi_ref, k_top)
    need_lo = k_top - count16(hi_ref, hi_thr, True)
    hi_thr16 = hi_thr.astype(I16)

    def mask_lo(j, carry):
        lo_ref[j] = jnp.where(hi_ref[j] == hi_thr16, lo_ref[j], jnp.int16(-32768))
        return carry

    _fori_pairs(qi + 1, mask_lo, 0)
    lo_thr = bisect16(lo_ref, need_lo)
    thr = lax.shift_left(hi_thr, jnp.int32(16)) + (lo_thr + 32768)
    n_gt = count(lambda key, j: key > thr)
    n_ge = count(lambda key, j: key >= thr)

    def index_bisect():
        def index_bit(i, cut):
            cand = cut + (jnp.int32(1) << (idx_bits - 1 - i))
            n_before = count(lambda key, j: (key == thr) & (j * TK + krow < cand))
            return jnp.where(n_gt + n_before < k_top, cand, cut)
        return lax.fori_loop(0, idx_bits, index_bit, jnp.zeros((1, TQ), I32))

    def take_all_ties():
        return jnp.full((1, TQ), seq - 1, I32)

    cut = lax.cond(jnp.max(n_ge) > k_top, index_bisect, take_all_ties)
    sel_ref[0] = jnp.broadcast_to(thr, (LANE, TQ)).T
    sel_ref[1] = jnp.broadcast_to(cut, (LANE, TQ)).T

    for h in range(nh):
        q_sc[h * TQ:(h + 1) * TQ, :] = q_ref[:, h * HEAD_DIM:(h + 1) * HEAD_DIM]
    mx_ref[...] = jnp.full(mx_ref.shape, NEG_INF, F32)
    acc_ref[...] = jnp.zeros(acc_ref.shape, F32)

    def attn_score_body(j, carry):
        start = pl.multiple_of(j * TK, TK)
        key = key_ref[j]
        thr_t = _tile_lanes(sel_ref[0], TK // LANE)
        cut_t = _tile_lanes(sel_ref[1], TK // LANE)
        vis = (key > thr_t) | ((key == thr_t) & (j * TK + col <= cut_t))
        vis = vis & (col - row <= (qi - j) * TK)
        s = (_dot_nt(q_sc[...], kd_ref[pl.ds(start, TK), :]) * SCALE_LOG2E).reshape(nh, TQ, TK)
        s = jnp.where(vis[None], s, NEG_INF).reshape(nh * TQ, TK)
        s_ref[j] = s
        mx_ref[...] = jnp.maximum(mx_ref[...], _max_lanes(s))
        return carry

    _fori_pairs(qi + 1, attn_score_body, 0)
    mx_ref[...] = _row_max_to_lanes(mx_ref[...])

    def pv_body(j, carry):
        start = pl.multiple_of(j * TK, TK)
        acc_ref[...] += _exp_pv(s_ref[j], mx_ref[...], vd_ref[pl.ds(start, TK), :])
        return carry

    _fori_pairs(qi + 1, pv_body, 0)
    a = acc_ref[...]
    out = a[:, :HEAD_DIM] / a[:, HEAD_DIM:]
    for h in range(nh):
        o_ref[:, h * HEAD_DIM:(h + 1) * HEAD_DIM] = out[h * TQ:(h + 1) * TQ].astype(o_ref.dtype)


def _merge_kernel(l_ref, hb_ref, of_ref, on_ref, od_ref, wg1_ref, wg2_ref, wg3_ref, wf_ref, wn_ref, wd_ref, o_ref):
    hb = hb_ref[...]
    mixed = (jax.nn.sigmoid(_dot(hb, wg1_ref[...])) * _dot(of_ref[...], wf_ref[...])
             + jax.nn.sigmoid(_dot(hb, wg2_ref[...])) * _dot(on_ref[...], wn_ref[...])
             + jax.nn.sigmoid(_dot(hb, wg3_ref[...])) * _dot(od_ref[...], wd_ref[...]))
    o_ref[...] = mixed.astype(o_ref.dtype)


LN_SUB_ROWS = 256


def _layer_norm_store(y, g_ref, b_ref, of_ref, ob_ref, rows):
    mu = jnp.mean(y, axis=-1, keepdims=True)
    yc = y - mu
    var = jnp.mean(jnp.square(yc), axis=-1, keepdims=True)
    out = yc * lax.rsqrt(var + 1e-5) * g_ref[...] + b_ref[...]
    of_ref[rows, :] = out
    if ob_ref is not None:
        ob_ref[rows, :] = out.astype(ob_ref.dtype)


def _out_ln_kernel(l_ref, x_ref, w_ref, h_ref, g_ref, b_ref, of_ref, ob_ref):
    w = w_ref[...]
    for r in range(x_ref.shape[0] // LN_SUB_ROWS):
        rows = slice(r * LN_SUB_ROWS, (r + 1) * LN_SUB_ROWS)
        y = ALPHA * h_ref[rows, :] + _dot(x_ref[rows, :], w)
        _layer_norm_store(y, g_ref, b_ref, of_ref, ob_ref, rows)


def _out_ln(name, lidx, x, w, h, g, b):
    t, kdim = x.shape
    d = h.shape[1]
    tm = 512
    return _call(_out_ln_kernel, name, lidx, [x, w, h, g, b], grid=(t // tm,),
                 in_specs=[pl.BlockSpec((tm, kdim), lambda i, l: (i, 0)),
                           pl.BlockSpec((None, kdim, d), lambda i, l: (l[0], 0, 0)),
                           pl.BlockSpec((tm, d), lambda i, l: (i, 0)),
                           pl.BlockSpec((None, 1, d), lambda i, l: (l[0], 0, 0)),
                           pl.BlockSpec((None, 1, d), lambda i, l: (l[0], 0, 0))],
                 out_specs=[pl.BlockSpec((tm, d), lambda i, l: (i, 0)),
                            pl.BlockSpec((tm, d), lambda i, l: (i, 0))],
                 out_shape=[jax.ShapeDtypeStruct((t, d), F32), jax.ShapeDtypeStruct((t, d), BF16)])


def _ffn_ln_kernel(l_ref, x_ref, w_ref, h_ref, g_ref, b_ref, of_ref, *, nk):
    k = pl.program_id(1)

    @pl.when(k == 0)
    def _():
        of_ref[...] = ALPHA * h_ref[...] + _dot(x_ref[...], w_ref[...])

    @pl.when((k > 0) & (k < nk - 1))
    def _():
        of_ref[...] += _dot(x_ref[...], w_ref[...])

    @pl.when(k == nk - 1)
    def _():
        w = w_ref[...]
        for r in range(x_ref.shape[0] // LN_SUB_ROWS):
            rows = slice(r * LN_SUB_ROWS, (r + 1) * LN_SUB_ROWS)
            y = of_ref[rows, :] + _dot(x_ref[rows, :], w)
            _layer_norm_store(y, g_ref, b_ref, of_ref, None, rows)


def _ffn_ln(name, lidx, x, w, h, g, b, tk):
    t, kdim = x.shape
    d = h.shape[1]
    tm = 1024
    nk = kdim // tk
    assert nk >= 2
    return _call(functools.partial(_ffn_ln_kernel, nk=nk), name, lidx, [x, w, h, g, b],
                 grid=(t // tm, nk),
                 in_specs=[pl.BlockSpec((tm, tk), lambda i, k, l: (i, k)),
                           pl.BlockSpec((None, tk, d), lambda i, k, l: (l[0], k, 0)),
                           pl.BlockSpec((tm, d), lambda i, k, l: (i, 0)),
                           pl.BlockSpec((None, 1, d), lambda i, k, l: (l[0], 0, 0)),
                           pl.BlockSpec((None, 1, d), lambda i, k, l: (l[0], 0, 0))],
                 out_specs=pl.BlockSpec((tm, d), lambda i, k, l: (i, 0)),
                 out_shape=jax.ShapeDtypeStruct((t, d), F32),
                 vmem_limit_bytes=VMEM_LIMIT_LARGE_BYTES)


def _swiglu_kernel(l_ref, x_ref, wa_ref, wb_ref, o_ref, wab_ref):
    @pl.when(pl.program_id(1) == 0)
    def _():
        wab_ref[0] = wa_ref[...].astype(BF16)
        wab_ref[1] = wb_ref[...].astype(BF16)

    x = x_ref[...]
    a = _dot(x, wab_ref[0])
    o_ref[...] = (jax.nn.silu(a) * _dot(x, wab_ref[1])).astype(o_ref.dtype)


def _ple_kernel(l_ref, h_ref, p_ref, wpi_ref, wpg_ref, of_ref, ob_ref):
    wpi = wpi_ref[...]
    wpg = wpg_ref[...]
    for r in range(h_ref.shape[0] // LN_SUB_ROWS):
        rows = slice(r * LN_SUB_ROWS, (r + 1) * LN_SUB_ROWS)
        h = h_ref[rows, :]
        p_in = _dot(p_ref[rows, :].astype(BF16), wpi)
        out = h + p_in * jax.nn.sigmoid(_dot(h.astype(BF16), wpg))
        of_ref[rows, :] = out
        ob_ref[rows, :] = out.astype(ob_ref.dtype)


def _rope_tables(n, dim):
    inv = 1.0 / (ROPE_THETA ** (jnp.arange(0, dim, 2, dtype=F32) / dim))
    ang = jnp.arange(n, dtype=F32)[:, None] * inv[None, :]
    return jnp.cos(ang), jnp.sin(ang)


def _layer(lidx, h, hb, consts, weights, bsz, seq):
    (cos128, sin128, cos64, sin64a, sin64b, cmp_cos, cmp_sin, overlap, block_of_key) = consts
    (w_proj, fox_bias, pe_k, pe_v, wk1, wk2, wv1, wv2, kv_norm, kv_up,
     w_br_fox, w_br_nsa, w_br_dsa, w_out, ln1_g, ln1_b, w_ffn_in, w_ffn_out, ln2_g, ln2_b, p, w_ple_in, w_ple_gate) = weights
    t = bsz * seq
    nq = seq // TQ
    d = D_MODEL

    z_a = _proj("proj_plain", lidx, hb, w_proj, COL_PLAIN, N_PLAIN, 2048, 512, BF16, seq)
    z_b = _proj("proj_rope128", lidx, hb, w_proj, COL_ROPE128, N_ROPE128, 2048, 512, BF16, seq,
                rope=((HEAD_DIM // 2,), cos128, sin128))
    z_c = _proj("proj_rope64", lidx, hb, w_proj, COL_ROPE64, N_ROPE64, 2048, 256, BF16, seq,
                rope=((IDX_DIM // 2, LANE - IDX_DIM // 2), cos64, sin64a, sin64b))
    z_d = _proj("proj_f32", lidx, hb, w_proj, COL_F32, N_F32, 1024, N_F32, F32, seq)

    cum = _call(_cum_kernel, "fox_cum", lidx, [z_d, fox_bias], grid=(bsz,),
                in_specs=[pl.BlockSpec((seq, LANE), lambda b, l: (b, ZD_FOX_F)),
                          pl.BlockSpec((None, 1, LANE), lambda b, l: (l[0], 0, 0))],
                out_specs=pl.BlockSpec((seq, LANE), lambda b, l: (b, 0)),
                out_shape=jax.ShapeDtypeStruct((t, LANE), F32))
    cum_rows = jnp.transpose(cum.reshape(bsz, seq, LANE)[:, :, :8], (0, 2, 1)).reshape(bsz, 8 * (seq // TK), TK)
    fq = FOX_HEADS * HEAD_DIM
    o_fox = _call(_fox_kernel, "fox_attn", lidx, [z_a, z_a, z_a, cum, cum_rows], grid=(bsz, nq),
                  in_specs=[pl.BlockSpec((TQ, fq), lambda b, i, l: (b * nq + i, 0)),
                            pl.BlockSpec((seq, fq), lambda b, i, l: (b, 1)),
                            pl.BlockSpec((seq, fq), lambda b, i, l: (b, 2)),
                            pl.BlockSpec((TQ, LANE), lambda b, i, l: (b * nq + i, 0)),
                            pl.BlockSpec((None, 8 * (seq // TK), TK), lambda b, i, l: (b, 0, 0))],
                  out_specs=pl.BlockSpec((TQ, fq), lambda b, i, l: (b * nq + i, 0)),
                  out_shape=jax.ShapeDtypeStruct((t, fq), BF16),
                  scratch_shapes=[pltpu.VMEM((FOX_HEADS, seq // TK, TQ, TK), F32),
                                  pltpu.VMEM((FOX_HEADS, TQ, LANE), F32), pltpu.VMEM((FOX_HEADS, TQ, LANE), F32),
                                  pltpu.VMEM((FOX_HEADS, TQ, 2 * HEAD_DIM), F32)])

    nchunk = seq // CMP_STRIDE
    half = CMP_STRIDE * HEAD_DIM
    r_k = z_d[:, ZD_NSA_KC * LANE:(ZD_NSA_KC + 1) * LANE].reshape(bsz * nchunk, half)
    r_v = z_d[:, ZD_NSA_VC * LANE:(ZD_NSA_VC + 1) * LANE].reshape(bsz * nchunk, half)
    k_cmp, v_cmp = _call(
        _cmp_kernel, "nsa_compress", lidx, [r_k, r_v, pe_k, pe_v, wk1, wk2, wv1, wv2, cmp_cos, cmp_sin], grid=(bsz,),
        in_specs=[pl.BlockSpec((nchunk, half), lambda b, l: (b, 0)),
                  pl.BlockSpec((nchunk, half), lambda b, l: (b, 0)),
                  pl.BlockSpec((None, 2, half), lambda b, l: (l[0], 0, 0)),
                  pl.BlockSpec((None, 2, half), lambda b, l: (l[0], 0, 0)),
                  pl.BlockSpec((None, 2 * half, CMP_HIDDEN), lambda b, l: (l[0], 0, 0)),
                  pl.BlockSpec((None, CMP_HIDDEN, HEAD_DIM), lambda b, l: (l[0], 0, 0)),
                  pl.BlockSpec((None, 2 * half, CMP_HIDDEN), lambda b, l: (l[0], 0, 0)),
                  pl.BlockSpec((None, CMP_HIDDEN, HEAD_DIM), lambda b, l: (l[0], 0, 0)),
                  pl.BlockSpec((nchunk, HEAD_DIM), lambda b, l: (0, 0)),
                  pl.BlockSpec((nchunk, HEAD_DIM), lambda b, l: (0, 0))],
        out_specs=[pl.BlockSpec((nchunk, HEAD_DIM), lambda b, l: (b, 0)),
                   pl.BlockSpec((nchunk, HEAD_DIM), lambda b, l: (b, 0))],
        out_shape=[jax.ShapeDtypeStruct((bsz * nchunk, HEAD_DIM), BF16)] * 2)
    nsq = NSA_HEADS * HEAD_DIM
    o_nsa = _call(
        _nsa_kernel, "nsa_attn", lidx, [z_b, z_b, z_b, z_a, z_a, k_cmp, v_cmp, z_d, overlap, block_of_key], grid=(bsz, nq),
        in_specs=[pl.BlockSpec((TQ, nsq), lambda b, i, l: (b * nq + i, 0)),
                  pl.BlockSpec((seq, LANE), lambda b, i, l: (b, 4)),
                  pl.BlockSpec((seq, LANE), lambda b, i, l: (b, 5)),
                  pl.BlockSpec((seq, LANE), lambda b, i, l: (b, 18)),
                  pl.BlockSpec((seq, LANE), lambda b, i, l: (b, 19)),
                  pl.BlockSpec((nchunk, HEAD_DIM), lambda b, i, l: (b, 0)),
                  pl.BlockSpec((nchunk, HEAD_DIM), lambda b, i, l: (b, 0)),
                  pl.BlockSpec((TQ, LANE), lambda b, i, l: (b * nq + i, ZD_NSA_G)),
                  pl.BlockSpec((N_CMP_PAD, LANE), lambda b, i, l: (0, 0)),
                  pl.BlockSpec((seq, LANE), lambda b, i, l: (0, 0))],
        out_specs=pl.BlockSpec((TQ, nsq), lambda b, i, l: (b * nq + i, 0)),
        out_shape=jax.ShapeDtypeStruct((t, nsq), BF16),
        scratch_shapes=[pltpu.VMEM((NSA_HEADS * TQ, HEAD_DIM), BF16),
                        pltpu.VMEM((NSA_HEADS * TQ, HEAD_DIM), F32),
                        pltpu.VMEM((seq // TK, NSA_HEADS * TQ, TK), F32),
                        pltpu.VMEM((NSA_HEADS * TQ, LANE), F32),
                        pltpu.VMEM((NSA_HEADS * TQ, 2 * HEAD_DIM), F32)])

    tm_kv = 1024
    k_d, v_d = _call(
        _dsa_kv_kernel, "dsa_kv", lidx, [z_d, kv_norm, kv_up, cos128, sin128], grid=(t // tm_kv,),
        in_specs=[pl.BlockSpec((tm_kv, DSA_KV_RANK), lambda i, l: (i, 0)),
                  pl.BlockSpec((None, 1, DSA_KV_RANK), lambda i, l: (l[0], 0, 0)),
                  pl.BlockSpec((None, DSA_KV_RANK, 2 * HEAD_DIM), lambda i, l: (l[0], 0, 0)),
                  pl.BlockSpec((tm_kv, LANE), lambda i, l: (i % (seq // tm_kv), 0)),
                  pl.BlockSpec((tm_kv, LANE), lambda i, l: (i % (seq // tm_kv), 0))],
        out_specs=[pl.BlockSpec((tm_kv, HEAD_DIM), lambda i, l: (i, 0)),
                   pl.BlockSpec((tm_kv, HEAD_DIM), lambda i, l: (i, 0))],
        out_shape=[jax.ShapeDtypeStruct((t, HEAD_DIM), BF16)] * 2)
    dq = DSA_HEADS * HEAD_DIM
    iqw = IDX_HEADS * IDX_DIM
    o_dsa = _call(
        _dsa_kernel, "dsa_attn", lidx, [z_b, k_d, v_d, z_c, z_c, z_c, z_d], grid=(bsz, nq),
        in_specs=[pl.BlockSpec((TQ, dq), lambda b, i, l: (b * nq + i, 1)),
                  pl.BlockSpec((seq, HEAD_DIM), lambda b, i, l: (b, 0)),
                  pl.BlockSpec((seq, HEAD_DIM), lambda b, i, l: (b, 0)),
                  pl.BlockSpec((TQ, iqw), lambda b, i, l: (b * nq + i, 0)),
                  pl.BlockSpec((seq, LANE), lambda b, i, l: (b, iqw // LANE)),
                  pl.BlockSpec((seq, LANE), lambda b, i, l: (b, iqw // LANE + 1)),
                  pl.BlockSpec((TQ, LANE), lambda b, i, l: (b * nq + i, ZD_IDX_W))],
        out_specs=pl.BlockSpec((TQ, dq), lambda b, i, l: (b * nq + i, 0)),
        out_shape=jax.ShapeDtypeStruct((t, dq), BF16),
        scratch_shapes=[pltpu.VMEM((seq // TK, TQ, TK), I32), pltpu.VMEM((seq // TK, TK, TQ), I32),
                        pltpu.VMEM((seq // TK, TK, TQ), I16), pltpu.VMEM((seq // TK, TK, TQ), I16),
                        pltpu.VMEM((LANE, TQ), F32), pltpu.VMEM((2, TQ, LANE), I32),
                        pltpu.VMEM((DSA_HEADS * TQ, HEAD_DIM), BF16),
                        pltpu.VMEM((seq // TK, DSA_HEADS * TQ, TK), F32),
                        pltpu.VMEM((DSA_HEADS * TQ, LANE), F32),
                        pltpu.VMEM((DSA_HEADS * TQ, 2 * HEAD_DIM), F32)])

    tm, tn = 1024, 512
    ncol = d // tn
    gate0 = COL_GATE // tn
    assert gate0 * tn == COL_GATE
    mixed = _call(
        _merge_kernel, "merge", lidx, [hb, o_fox, o_nsa, o_dsa, w_proj, w_proj, w_proj, w_br_fox, w_br_nsa, w_br_dsa],
        grid=(t // tm, ncol),
        in_specs=[pl.BlockSpec((tm, d), lambda i, j, l: (i, 0)),
                  pl.BlockSpec((tm, fq), lambda i, j, l: (i, 0)),
                  pl.BlockSpec((tm, nsq), lambda i, j, l: (i, 0)),
                  pl.BlockSpec((tm, dq), lambda i, j, l: (i, 0)),
                  pl.BlockSpec((None, d, tn), lambda i, j, l: (l[0], 0, gate0 + j)),
                  pl.BlockSpec((None, d, tn), lambda i, j, l: (l[0], 0, gate0 + ncol + j)),
                  pl.BlockSpec((None, d, tn), lambda i, j, l: (l[0], 0, gate0 + 2 * ncol + j)),
                  pl.BlockSpec((None, fq, tn), lambda i, j, l: (l[0], 0, j)),
                  pl.BlockSpec((None, nsq, tn), lambda i, j, l: (l[0], 0, j)),
                  pl.BlockSpec((None, dq, tn), lambda i, j, l: (l[0], 0, j))],
        out_specs=pl.BlockSpec((tm, tn), lambda i, j, l: (i, j)),
        out_shape=jax.ShapeDtypeStruct((t, d), BF16))

    h, hb = _out_ln("out_ln", lidx, mixed, w_out, h, ln1_g, ln1_b)

    nff = D_FF // tn
    act = _call(
        _swiglu_kernel, "swiglu", lidx, [hb, w_ffn_in, w_ffn_in], grid=(nff, t // tm),
        in_specs=[pl.BlockSpec((tm, d), lambda j, i, l: (i, 0)),
                  pl.BlockSpec((None, d, tn), lambda j, i, l: (l[0], 0, j)),
                  pl.BlockSpec((None, d, tn), lambda j, i, l: (l[0], 0, nff + j))],
        out_specs=pl.BlockSpec((tm, tn), lambda j, i, l: (i, j)),
        out_shape=jax.ShapeDtypeStruct((t, D_FF), BF16),
        scratch_shapes=[pltpu.VMEM((2, d, tn), BF16)])
    h = _ffn_ln("ffn_ln", lidx, act, w_ffn_out, h, ln2_g, ln2_b, tk=D_FF // 4)

    tmp = 512
    h, hb = _call(
        _ple_kernel, "ple", lidx, [h, p, w_ple_in, w_ple_gate], grid=(t // tmp,),
        in_specs=[pl.BlockSpec((tmp, d), lambda i, l: (i, 0)),
                  pl.BlockSpec((None, tmp, PLE_DIM), lambda i, l: (l[0], i, 0)),
                  pl.BlockSpec((None, PLE_DIM, d), lambda i, l: (l[0], 0, 0)),
                  pl.BlockSpec((None, d, d), lambda i, l: (l[0], 0, 0))],
        out_specs=[pl.BlockSpec((tmp, d), lambda i, l: (i, 0)),
                   pl.BlockSpec((tmp, d), lambda i, l: (i, 0))],
        out_shape=[jax.ShapeDtypeStruct((t, d), F32), jax.ShapeDtypeStruct((t, d), BF16)])
    return h, hb


def kernel(x, p, w_in, fox_f_bias, nsa_pe_k, nsa_pe_v, nsa_cmp_k1, nsa_cmp_k2, nsa_cmp_v1, nsa_cmp_v2, dsa_kv_norm, dsa_kv_up, w_br_fox, w_br_nsa, w_br_dsa, w_out, ln1_g, ln1_b, w_ffn_in, w_ffn_out, ln2_g, ln2_b, w_ple_in, w_ple_gate):
    bsz, seq, d = x.shape
    depth = w_in.shape[0]
    t = bsz * seq
    assert d == D_MODEL and seq % 1024 == 0 and depth == DEPTH

    cat = functools.partial(jnp.concatenate, axis=-1)
    w_proj = _regroup(w_in)

    fox_bias = jnp.pad(fox_f_bias, ((0, 0), (0, LANE - FOX_HEADS))).reshape(depth, 1, LANE)
    half = CMP_STRIDE * HEAD_DIM
    weights = (
        w_proj, fox_bias,
        nsa_pe_k.reshape(depth, 2, half), nsa_pe_v.reshape(depth, 2, half),
        nsa_cmp_k1.astype(BF16), nsa_cmp_k2.astype(BF16), nsa_cmp_v1.astype(BF16), nsa_cmp_v2.astype(BF16),
        dsa_kv_norm.reshape(depth, 1, DSA_KV_RANK), dsa_kv_up.astype(BF16),
        w_br_fox.astype(BF16), w_br_nsa.astype(BF16), w_br_dsa.astype(BF16), w_out.astype(BF16),
        ln1_g.reshape(depth, 1, d), ln1_b.reshape(depth, 1, d),
        w_ffn_in, w_ffn_out.astype(BF16),
        ln2_g.reshape(depth, 1, d), ln2_b.reshape(depth, 1, d),
        p.reshape(depth, t, PLE_DIM), w_ple_in.astype(BF16), w_ple_gate.astype(BF16),
    )

    cos, sin = _rope_tables(seq, HEAD_DIM)
    cos128 = cat([cos, cos])
    sin128 = cat([-sin, sin])
    cos_i, sin_i = _rope_tables(seq, IDX_DIM)
    zi = jnp.zeros_like(sin_i)
    cos64 = cat([cos_i, cos_i, cos_i, cos_i])
    sin64a = cat([zi, sin_i, zi, sin_i])
    sin64b = cat([-sin_i, zi, -sin_i, zi])
    n_cmp = (seq - CMP_LEN) // CMP_STRIDE + 1
    c_end = jnp.minimum(jnp.arange(N_CMP_PAD) * CMP_STRIDE + CMP_LEN - 1, seq - 1)
    cmp_cos, cmp_sin = cos128[c_end], sin128[c_end]
    n_slc = seq // SLC_LEN
    c_start = jnp.arange(N_CMP_PAD) * CMP_STRIDE
    s_start = jnp.arange(LANE) * SLC_LEN
    overlap = jnp.maximum(jnp.minimum(c_start[:, None] + CMP_LEN - 1, s_start[None, :] + SLC_LEN - 1)
                          - jnp.maximum(c_start[:, None], s_start[None, :]) + 1, 0).astype(F32) / CMP_LEN
    overlap = jnp.where((jnp.arange(N_CMP_PAD)[:, None] < n_cmp) & (jnp.arange(LANE)[None, :] < n_slc), overlap, 0.0)
    overlap = overlap.T.astype(BF16)
    block_of_key = (jnp.arange(seq)[:, None] // SLC_LEN == jnp.arange(LANE)[None, :]).astype(BF16)
    consts = (cos128, sin128, cos64, sin64a, sin64b, cmp_cos, cmp_sin, overlap, block_of_key)

    h = x.reshape(t, d)
    hb = h.astype(BF16)
    for layer in range(depth):
        lidx = jnp.full((1,), layer, I32)
        h, hb = _layer(lidx, h, hb, consts, weights, bsz, seq)
    return h.reshape(bsz, seq, d)
```

```python
import functools

import jax
import jax.numpy as jnp
from jax import lax
from jax.experimental import pallas as pl
from jax.experimental.pallas import tpu as pltpu

F32, BF16, I32, I16 = jnp.float32, jnp.bfloat16, jnp.int32, jnp.int16

D_MODEL = 2048
DEPTH = 4
HEAD_DIM = 128
ROPE_THETA = 10000.0
NEG_INF = -1e30
FOX_HEADS = 6
NSA_HEADS = 4
CMP_LEN = 32
CMP_STRIDE = 16
CMP_HIDDEN = 256
SLC_LEN = 64
SLC_TOPN = 16
WIN = 512
SLC_FORCE = 1e4
DSA_HEADS = 6
DSA_KV_RANK = 256
IDX_HEADS = 16
IDX_DIM = 64
IDX_TOPK_MAX = 256
D_FF = ((8 * D_MODEL + 3 * 256 - 1) // (3 * 256)) * 256
PLE_DIM = 256
ALPHA = (2 * DEPTH) ** 0.25
SCALE = HEAD_DIM ** -0.5
LOG2E = 1.4426950408889634
SCALE_LOG2E = SCALE * LOG2E

LANE = 128
VMEM_LIMIT_BYTES = 48 * 1024 * 1024
VMEM_LIMIT_LARGE_BYTES = 58 * 1024 * 1024

_IN_SPLITS = (
    ('fox_q', FOX_HEADS * HEAD_DIM), ('fox_k', FOX_HEADS * HEAD_DIM), ('fox_v', FOX_HEADS * HEAD_DIM), ('fox_f', FOX_HEADS),
    ('nsa_q', NSA_HEADS * HEAD_DIM), ('nsa_kc', HEAD_DIM), ('nsa_vc', HEAD_DIM), ('nsa_ks', HEAD_DIM), ('nsa_vs', HEAD_DIM),
    ('nsa_kw', HEAD_DIM), ('nsa_vw', HEAD_DIM), ('nsa_g', 3 * NSA_HEADS),
    ('dsa_q', DSA_HEADS * HEAD_DIM), ('dsa_ckv', DSA_KV_RANK), ('idx_q', IDX_HEADS * IDX_DIM), ('idx_k', IDX_DIM), ('idx_w', IDX_HEADS),
    ('gate', 3 * D_MODEL),
)
_IN_OFFSETS = {}
_off = 0
for _name, _width in _IN_SPLITS:
    _IN_OFFSETS[_name] = (_off, _off + _width)
    _off += _width

MISC_FOX_F = 0
MISC_NSA_G = FOX_HEADS
MISC_IDX_W = FOX_HEADS + 3 * NSA_HEADS

TQ = 256
TK = 256
N_CMP_PAD = 128


def _dot(a, b):
    return jnp.dot(a, b, preferred_element_type=F32)


def _dot_nt(a, b):
    return lax.dot_general(a, b, (((1,), (1,)), ((), ())), preferred_element_type=F32)


def _fori_pairs(n, body, init):
    def four(jj, carry):
        for u in range(4):
            carry = body(4 * jj + u, carry)
        return carry
    carry = lax.fori_loop(0, lax.shift_right_logical(n, jnp.int32(2)), four, init)
    base = n & jnp.int32(-4)
    carry = lax.cond((n & 2) == 2, lambda c: body(base + 1, body(base, c)), lambda c: c, carry)
    return lax.cond((n & 1) == 1, lambda c: body(n - 1, c), lambda c: c, carry)


def _split3(a):
    a1 = a.astype(BF16)
    r1 = a - a1.astype(F32)
    a2 = r1.astype(BF16)
    a3 = (r1 - a2.astype(F32)).astype(BF16)
    return a1, a2, a3


def _call(kernel, name, lidx, args, grid, in_specs, out_specs, out_shape, scratch_shapes=(),
          vmem_limit_bytes=VMEM_LIMIT_BYTES):
    return pl.pallas_call(
        kernel,
        grid_spec=pltpu.PrefetchScalarGridSpec(
            num_scalar_prefetch=1, grid=grid, in_specs=in_specs, out_specs=out_specs,
            scratch_shapes=list(scratch_shapes)),
        out_shape=out_shape,
        compiler_params=pltpu.CompilerParams(
            dimension_semantics=("arbitrary",) * len(grid), vmem_limit_bytes=vmem_limit_bytes),
        name=name,
    )(lidx, *args)


def _regroup_plan():
    plan = []

    def whole(name):
        a, b = _IN_OFFSETS[name]
        for g in range((b - a) // LANE):
            plan.append((a + g * LANE, 0, LANE))

    whole('dsa_ckv')
    lane0 = {}
    start = 0
    for name in ('fox_f', 'nsa_g', 'idx_w'):
        a, b = _IN_OFFSETS[name]
        lane0[name] = (a - start, start, start + b - a)
        start += b - a
    plan.append(lane0['fox_f'])
    whole('nsa_kc')
    whole('nsa_vc')
    plan.append(lane0['nsa_g'])
    plan.append(lane0['idx_w'])
    plan.append((0, 0, 0))
    for name in ('fox_q', 'fox_k', 'fox_v', 'nsa_vs', 'nsa_vw'):
        whole(name)
    for name in ('nsa_q', 'nsa_ks', 'nsa_kw', 'dsa_q'):
        whole(name)
    whole('gate')
    whole('idx_q')
    ik = _IN_OFFSETS['idx_k'][0]
    plan.append((ik, 0, IDX_DIM))
    plan.append((ik - IDX_DIM, IDX_DIM, LANE))
    return tuple(plan)


_REGROUP_PLAN = _regroup_plan()
COL_F32 = 0
N_F32 = DSA_KV_RANK + 6 * LANE
ZD_FOX_F, ZD_NSA_KC, ZD_NSA_VC, ZD_NSA_G, ZD_IDX_W = 2, 3, 4, 5, 6
COL_PLAIN = COL_F32 + N_F32
N_PLAIN = (3 * FOX_HEADS + 2) * HEAD_DIM
COL_ROPE128 = COL_PLAIN + N_PLAIN
N_ROPE128 = (NSA_HEADS + 2 + DSA_HEADS) * HEAD_DIM
COL_GATE = COL_ROPE128 + N_ROPE128
COL_ROPE64 = COL_GATE + 3 * D_MODEL
N_ROPE64 = IDX_HEADS * IDX_DIM + 2 * LANE
N_REGROUPED = COL_ROPE64 + N_ROPE64
assert N_REGROUPED == len(_REGROUP_PLAN) * LANE


def _regroup_kernel(src_ref, lo_ref, hi_ref, x_ref, o_ref):
    g = pl.program_id(0)
    col = lax.broadcasted_iota(I32, (LANE, x_ref.shape[2]), 0)
    keep = (col >= lo_ref[g]) & (col < hi_ref[g])
    for layer in range(x_ref.shape[1]):
        o_ref[layer] = jnp.where(keep, x_ref[:, layer, :], 0.0).T.astype(o_ref.dtype)


def _regroup(w_in):
    depth, d, n_in = w_in.shape
    w_t = jnp.transpose(w_in, (2, 0, 1))
    src = jnp.array([p[0] for p in _REGROUP_PLAN], I32)
    lo = jnp.array([p[1] for p in _REGROUP_PLAN], I32)
    hi = jnp.array([p[2] for p in _REGROUP_PLAN], I32)
    return pl.pallas_call(
        _regroup_kernel,
        grid_spec=pltpu.PrefetchScalarGridSpec(
            num_scalar_prefetch=3, grid=(len(_REGROUP_PLAN),),
            in_specs=[pl.BlockSpec((pl.Element(LANE), pl.Element(depth), pl.Element(d)),
                                   lambda g, src, lo, hi: (src[g], 0, 0))],
            out_specs=pl.BlockSpec((depth, d, LANE), lambda g, src, lo, hi: (0, 0, g))),
        out_shape=jax.ShapeDtypeStruct((depth, d, N_REGROUPED), BF16),
        compiler_params=pltpu.CompilerParams(dimension_semantics=("arbitrary",), vmem_limit_bytes=VMEM_LIMIT_BYTES),
        name="regroup_w_in",
    )(src, lo, hi, w_t)


def _proj_kernel(l_ref, x_ref, w_ref, *refs, shifts):
    o_ref = refs[-1]
    acc = _dot(x_ref[...], w_ref[...])
    if not shifts:
        o_ref[...] = acc.astype(o_ref.dtype)
        return
    c_ref = refs[0]
    s_refs = refs[1:-1]
    for g in range(acc.shape[1] // LANE):
        xg = acc[:, g * LANE:(g + 1) * LANE]
        out = xg * c_ref[...]
        for shift, s_ref in zip(shifts, s_refs):
            out = out + pltpu.roll(xg, shift, 1) * s_ref[...]
        o_ref[:, g * LANE:(g + 1) * LANE] = out.astype(o_ref.dtype)


def _proj(name, lidx, hb, w, col0, n, tm, tn, out_dtype, seq, rope=None):
    t, d = hb.shape
    jb = col0 // tn
    assert jb * tn == col0
    in_specs = [pl.BlockSpec((tm, d), lambda i, j, l: (i, 0)),
                pl.BlockSpec((None, d, tn), lambda i, j, l: (l[0], 0, jb + j))]
    args = [hb, w]
    shifts = ()
    if rope is not None:
        shifts, tables = rope[0], rope[1:]
        nrow = seq // tm
        for tab in tables:
            in_specs.append(pl.BlockSpec((tm, LANE), lambda i, j, l: (i % nrow, 0)))
            args.append(tab)
    return _call(functools.partial(_proj_kernel, shifts=shifts), name, lidx, args,
                 grid=(t // tm, n // tn), in_specs=in_specs,
                 out_specs=pl.BlockSpec((tm, tn), lambda i, j, l: (i, j)),
                 out_shape=jax.ShapeDtypeStruct((t, n), out_dtype))


def _cum_kernel(l_ref, f_ref, bias_ref, o_ref):
    chunk = 256
    row = lax.broadcasted_iota(I32, (chunk, chunk), 0)
    col = lax.broadcasted_iota(I32, (chunk, chunk), 1)
    tri = jnp.where(row >= col, 1.0, 0.0).astype(BF16)
    carry = jnp.zeros((1, LANE), F32)
    for c in range(f_ref.shape[0] // chunk):
        logf = jax.nn.log_sigmoid(f_ref[c * chunk:(c + 1) * chunk, :] + bias_ref[...])
        l1, l2, l3 = _split3(logf)
        cs = _dot(tri, l1) + _dot(tri, l2) + _dot(tri, l3) + carry
        o_ref[c * chunk:(c + 1) * chunk, :] = cs
        carry = cs[chunk - 1:chunk, :]


def _max_lanes(x):
    out = x[:, 0:LANE]
    for g in range(1, x.shape[1] // LANE):
        out = jnp.maximum(out, x[:, g * LANE:(g + 1) * LANE])
    return out


def _tile_lanes(x, n):
    return jnp.concatenate([x] * n, axis=1)


def _row_max_to_lanes(mx):
    return jnp.broadcast_to(jnp.max(mx, axis=-1, keepdims=True), mx.shape)


def _exp_pv(s, m_lanes, v):
    p = jnp.exp2(s - _tile_lanes(m_lanes, s.shape[1] // LANE)).astype(BF16)
    return _dot(p, jnp.concatenate([v, jnp.ones_like(v)], axis=1))


def _fox_kernel(l_ref, q_ref, k_ref, v_ref, cq_ref, ck_ref, o_ref, s_ref, mx_ref, cqb_ref, acc_ref):
    qi = pl.program_id(1)
    nh = FOX_HEADS
    nchunk = ck_ref.shape[0] // 8
    row = lax.broadcasted_iota(I32, (TQ, TK), 0)
    col = lax.broadcasted_iota(I32, (TQ, TK), 1)
    cq_all = cq_ref[...]
    for h in range(nh):
        cqb_ref[h] = jnp.broadcast_to(cq_all[:, h:h + 1] * LOG2E, (TQ, LANE))
    mx_ref[...] = jnp.full(mx_ref.shape, NEG_INF, F32)
    acc_ref[...] = jnp.zeros(acc_ref.shape, F32)

    def score_step(j, diagonal):
        start = pl.multiple_of(j * TK, TK)
        for h in range(nh):
            hs = slice(h * HEAD_DIM, (h + 1) * HEAD_DIM)
            s = (_dot_nt(q_ref[:, hs], k_ref[pl.ds(start, TK), hs]) * SCALE_LOG2E
                 + _tile_lanes(cqb_ref[h], TK // LANE) - ck_ref[pl.ds(h * nchunk + j, 1), :] * LOG2E)
            if diagonal:
                s = jnp.where(col <= row, s, NEG_INF)
            s_ref[h, j] = s
            mx_ref[h] = jnp.maximum(mx_ref[h], _max_lanes(s))

    def score_body(j, carry):
        score_step(j, False)
        return carry

    _fori_pairs(qi, score_body, 0)
    score_step(qi, True)
    for h in range(nh):
        mx_ref[h] = _row_max_to_lanes(mx_ref[h])

    def pv_body(j, carry):
        start = pl.multiple_of(j * TK, TK)
        for h in range(nh):
            hs = slice(h * HEAD_DIM, (h + 1) * HEAD_DIM)
            acc_ref[h] += _exp_pv(s_ref[h, j], mx_ref[h], v_ref[pl.ds(start, TK), hs])
        return carry

    _fori_pairs(qi + 1, pv_body, 0)
    for h in range(nh):
        a = acc_ref[h]
        o_ref[:, h * HEAD_DIM:(h + 1) * HEAD_DIM] = (a[:, :HEAD_DIM] / a[:, HEAD_DIM:]).astype(o_ref.dtype)


def _cmp_kernel(l_ref, rk_ref, rv_ref, pek_ref, pev_ref, wk1_ref, wk2_ref, wv1_ref, wv2_ref,
                c_ref, s_ref, ko_ref, vo_ref):
    half = CMP_STRIDE * HEAD_DIM

    def compress(r_ref, pe_ref, w1_ref, w2_ref):
        r = r_ref[...]
        lo = (r + pe_ref[0:1, :]).astype(BF16)
        hi = (r + pe_ref[1:2, :]).astype(BF16)
        a = _dot(lo, w1_ref[0:half, :])
        b = _dot(hi, w1_ref[half:2 * half, :])
        hid = a + pltpu.roll(b, N_CMP_PAD - 1, 0)
        return _dot(jax.nn.gelu(hid).astype(BF16), w2_ref[...])

    kc = compress(rk_ref, pek_ref, wk1_ref, wk2_ref)
    kc = kc * c_ref[...] + pltpu.roll(kc, HEAD_DIM // 2, 1) * s_ref[...]
    ko_ref[...] = kc.astype(ko_ref.dtype)
    vo_ref[...] = compress(rv_ref, pev_ref, wv1_ref, wv2_ref).astype(vo_ref.dtype)


def _masked_softmax(s, mask):
    s = jnp.where(mask, s, NEG_INF)
    m = jnp.max(s, axis=-1, keepdims=True)
    e = jnp.where(mask, jnp.exp(s - m), 0.0)
    den = jnp.sum(e, axis=-1, keepdims=True)
    return e, jnp.where(den > 0.0, den, 1.0)


def _nsa_kernel(l_ref, q_ref, ks_ref, kw_ref, vs_ref, vw_ref, kc_ref, vc_ref, g_ref, ovt_ref, et_ref,
                o_ref, q_sc, ow_ref, s_ref, mx_ref, acc_ref):
    qi = pl.program_id(1)
    nh = NSA_HEADS
    for h in range(nh):
        q_sc[h * TQ:(h + 1) * TQ, :] = q_ref[:, h * HEAD_DIM:(h + 1) * HEAD_DIM]
    q = q_sc[...]
    pos = qi * TQ + lax.broadcasted_iota(I32, (TQ, 1), 0)
    lane = lax.broadcasted_iota(I32, (TQ, LANE), 1)

    band = WIN + TQ
    start_w = pl.multiple_of(jnp.maximum(qi * TQ - WIN, 0), TQ)
    kwb = kw_ref[pl.ds(start_w, band), :]
    vwb = vw_ref[pl.ds(start_w, band), :]
    dist = pos - (start_w + lax.broadcasted_iota(I32, (TQ, band), 1))
    wvis = (dist >= 0) & (dist < WIN)
    for h in range(nh):
        s_w = _dot_nt(q_ref[:, h * HEAD_DIM:(h + 1) * HEAD_DIM], kwb) * SCALE_LOG2E
        s_w = jnp.where(wvis, s_w, NEG_INF)
        a_win = _exp_pv(s_w, _row_max_to_lanes(_max_lanes(s_w)), vwb)
        ow_ref[h * TQ:(h + 1) * TQ, :] = a_win[:, :HEAD_DIM] / a_win[:, HEAD_DIM:]

    cvis = (lane * CMP_STRIDE + (CMP_LEN - 1)) <= pos
    s_c = (_dot_nt(q, kc_ref[...]) * SCALE).reshape(nh, TQ, N_CMP_PAD)
    e_c, den_c = _masked_softmax(s_c, cvis[None])
    p_c = e_c / den_c
    o_cmp = _dot(p_c.reshape(nh * TQ, N_CMP_PAD).astype(BF16), vc_ref[...])

    p_sum = p_c[0]
    for h in range(1, nh):
        p_sum = p_sum + p_c[h]
    p1, p2, p3 = _split3(p_sum)
    ovt = ovt_ref[...]
    n_slc = et_ref.shape[0] // SLC_LEN
    imp = (_dot_nt(ovt, p1) + _dot_nt(ovt, p2) + _dot_nt(ovt, p3))[0:n_slc, :]
    blk = lax.broadcasted_iota(I32, (n_slc, TQ), 0)
    pos_t = qi * TQ + lax.broadcasted_iota(I32, (n_slc, TQ), 1)
    cur = lax.shift_right_logical(pos_t, SLC_LEN.bit_length() - 1)
    forced = (blk == 0) | (blk == cur) | (blk == cur - 1)
    imp = jnp.where(forced, SLC_FORCE, jnp.where(blk * SLC_LEN <= pos_t, imp, -SLC_FORCE))
    rank = jnp.zeros((n_slc, TQ), F32)
    for k in range(n_slc):
        ik = imp[k:k + 1, :]
        ahead = (ik > imp) | ((ik == imp) & (blk > k))
        rank = rank + jnp.where(ahead, 1.0, 0.0)
    sel_t = jnp.where(rank < float(min(SLC_TOPN, n_slc)), 1.0, 0.0)
    sel_t = jnp.concatenate([sel_t, jnp.zeros((LANE - n_slc, TQ), F32)], axis=0)
    sel = sel_t.T.astype(BF16)

    mx_ref[...] = jnp.full(mx_ref.shape, NEG_INF, F32)
    acc_ref[...] = jnp.zeros(acc_ref.shape, F32)
    row = lax.broadcasted_iota(I32, (TQ, TK), 0)
    col = lax.broadcasted_iota(I32, (TQ, TK), 1)

    def score_step(j, diagonal):
        start = pl.multiple_of(j * TK, TK)
        vis = _dot_nt(sel, et_ref[pl.ds(start, TK), :]) > 0.5
        if diagonal:
            vis = vis & (col <= row)
        s = (_dot_nt(q_sc[...], ks_ref[pl.ds(start, TK), :]) * SCALE_LOG2E).reshape(nh, TQ, TK)
        s = jnp.where(vis[None], s, NEG_INF).reshape(nh * TQ, TK)
        s_ref[j] = s
        mx_ref[...] = jnp.maximum(mx_ref[...], _max_lanes(s))

    def score_body(j, carry):
        score_step(j, False)
        return carry

    _fori_pairs(qi, score_body, 0)
    score_step(qi, True)
    mx_ref[...] = _row_max_to_lanes(mx_ref[...])

    def pv_body(j, carry):
        start = pl.multiple_of(j * TK, TK)
        acc_ref[...] += _exp_pv(s_ref[j], mx_ref[...], vs_ref[pl.ds(start, TK), :])
        return carry

    _fori_pairs(qi + 1, pv_body, 0)
    a_slc = acc_ref[...]
    o_slc = a_slc[:, :HEAD_DIM] / a_slc[:, HEAD_DIM:]

    o_win = ow_ref[...]
    gate = jax.nn.sigmoid(g_ref[...])
    for h in range(nh):
        rows = slice(h * TQ, (h + 1) * TQ)
        g0 = gate[:, MISC_NSA_G + 3 * h:MISC_NSA_G + 3 * h + 1]
        g1 = gate[:, MISC_NSA_G + 3 * h + 1:MISC_NSA_G + 3 * h + 2]
        g2 = gate[:, MISC_NSA_G + 3 * h + 2:MISC_NSA_G + 3 * h + 3]
        out = g0 * o_cmp[rows] + g1 * o_slc[rows] + g2 * o_win[rows]
        o_ref[:, h * HEAD_DIM:(h + 1) * HEAD_DIM] = out.astype(o_ref.dtype)


def _dsa_kv_kernel(l_ref, x_ref, g_ref, up_ref, c_ref, s_ref, ko_ref, vo_ref):
    x = x_ref[...]
    r = x * lax.rsqrt(jnp.mean(jnp.square(x), axis=-1, keepdims=True) + 1e-6) * g_ref[...]
    kv = _dot(r.astype(BF16), up_ref[...])
    k = kv[:, :HEAD_DIM]
    k = k * c_ref[...] + pltpu.roll(k, HEAD_DIM // 2, 1) * s_ref[...]
    ko_ref[...] = k.astype(ko_ref.dtype)
    vo_ref[...] = kv[:, HEAD_DIM:].astype(vo_ref.dtype)


def _dsa_kernel(l_ref, q_ref, kd_ref, vd_ref, iq_ref, ika_ref, ikb_ref, w_ref, o_ref,
                sc_ref, sct_ref, wt_ref, thr_sel, cut_sel, q_sc, s_ref, mx_ref, acc_ref):
    qi = pl.program_id(1)
    nh = DSA_HEADS
    seq = kd_ref.shape[0]
    k_top = float(min(IDX_TOPK_MAX, seq // 4))
    row = lax.broadcasted_iota(I32, (TQ, TK), 0)
    col = lax.broadcasted_iota(I32, (TQ, TK), 1)
    int_min = jnp.int32(-2 ** 31)
    idx_bits = (seq - 1).bit_length()

    wt_ref[...] = (w_ref[...] * (IDX_HEADS ** -0.5 * IDX_DIM ** -0.5)).T

    def score_step(j, diagonal):
        start = pl.multiple_of(j * TK, TK)
        ka = ika_ref[pl.ds(start, TK), :]
        kb = ikb_ref[pl.ds(start, TK), :]
        sc = jnp.zeros((TK, TQ), F32)
        for g in range(IDX_HEADS // 2):
            qpair = iq_ref[:, g * LANE:(g + 1) * LANE]
            wa = wt_ref[MISC_IDX_W + 2 * g:MISC_IDX_W + 2 * g + 1, :]
            wb = wt_ref[MISC_IDX_W + 2 * g + 1:MISC_IDX_W + 2 * g + 2, :]
            sc = sc + jnp.maximum(_dot_nt(ka, qpair), 0.0) * wa + jnp.maximum(_dot_nt(kb, qpair), 0.0) * wb
        if diagonal:
            sc = jnp.where(row <= col, sc, NEG_INF)
        sct_ref[j] = sc
        sc_ref[j] = sc.T

    def score_body(j, carry):
        score_step(j, False)
        return carry

    _fori_pairs(qi, score_body, 0)
    score_step(qi, True)

    krow = lax.broadcasted_iota(I32, (TK, TQ), 0)

    def count(pred_fn):
        def body(j, acc):
            hit = jnp.where(pred_fn(sct_ref[j], j), 1.0, 0.0)
            return acc + jnp.sum(hit.reshape(TK // 8, 8, TQ), axis=0)
        part = _fori_pairs(qi + 1, body, jnp.zeros((8, TQ), F32))
        return jnp.sum(part, axis=0, keepdims=True)

    def key_to_score(key):
        return lax.bitcast_convert_type(jnp.where(key < 0, key ^ jnp.int32(0x7FFFFFFF), key), F32)

    thr_key = jnp.where(count(lambda s, j: s >= 0.0) >= k_top, jnp.int32(0), int_min)

    def value_bit(i, thr_key):
        cand = thr_key + (jnp.int32(1) << (30 - i))
        cand_score = key_to_score(cand)
        return jnp.where(count(lambda s, j: s >= cand_score) >= k_top, cand, thr_key)

    thr = key_to_score(lax.fori_loop(0, 31, value_bit, thr_key))
    n_gt = count(lambda s, j: s > thr)
    n_ge = count(lambda s, j: s >= thr)

    def index_bisect():
        def index_bit(i, cut):
            cand = cut + (jnp.int32(1) << (idx_bits - 1 - i))
            n_before = count(lambda s, j: (s == thr) & (j * TK + krow < cand))
            return jnp.where(n_gt + n_before < k_top, cand, cut)
        return lax.fori_loop(0, idx_bits, index_bit, jnp.zeros((1, TQ), I32))

    def take_all_ties():
        return jnp.full((1, TQ), seq - 1, I32)

    cut = lax.cond(jnp.max(n_ge) > k_top, index_bisect, take_all_ties)
    thr_sel[...] = jnp.broadcast_to(thr, (LANE, TQ)).T
    cut_sel[...] = jnp.broadcast_to(cut, (LANE, TQ)).T

    for h in range(nh):
        q_sc[h * TQ:(h + 1) * TQ, :] = q_ref[:, h * HEAD_DIM:(h + 1) * HEAD_DIM]
    mx_ref[...] = jnp.full(mx_ref.shape, NEG_INF, F32)
    acc_ref[...] = jnp.zeros(acc_ref.shape, F32)

    def attn_score_step(j, diagonal):
        start = pl.multiple_of(j * TK, TK)
        sc = sc_ref[j]
        thr_t = _tile_lanes(thr_sel[...], TK // LANE)
        cut_t = _tile_lanes(cut_sel[...], TK // LANE)
        vis = (sc > thr_t) | ((sc == thr_t) & (j * TK + col <= cut_t))
        if diagonal:
            vis = vis & (col <= row)
        s = (_dot_nt(q_sc[...], kd_ref[pl.ds(start, TK), :]) * SCALE_LOG2E).reshape(nh, TQ, TK)
        s = jnp.where(vis[None], s, NEG_INF).reshape(nh * TQ, TK)
        s_ref[j] = s
        mx_ref[...] = jnp.maximum(mx_ref[...], _max_lanes(s))

    def attn_score_body(j, carry):
        attn_score_step(j, False)
        return carry

    _fori_pairs(qi, attn_score_body, 0)
    attn_score_step(qi, True)
    mx_ref[...] = _row_max_to_lanes(mx_ref[...])

    def pv_body(j, carry):
        start = pl.multiple_of(j * TK, TK)
        acc_ref[...] += _exp_pv(s_ref[j], mx_ref[...], vd_ref[pl.ds(start, TK), :])
        return carry

    _fori_pairs(qi + 1, pv_body, 0)
    a = acc_ref[...]
    out = a[:, :HEAD_DIM] / a[:, HEAD_DIM:]
    for h in range(nh):
        o_ref[:, h * HEAD_DIM:(h + 1) * HEAD_DIM] = out[h * TQ:(h + 1) * TQ].astype(o_ref.dtype)


def _merge_kernel(l_ref, hb_ref, of_ref, on_ref, od_ref, wg1_ref, wg2_ref, wg3_ref, wf_ref, wn_ref, wd_ref, o_ref):
    hb = hb_ref[...]
    mixed = (jax.nn.sigmoid(_dot(hb, wg1_ref[...])) * _dot(of_ref[...], wf_ref[...])
             + jax.nn.sigmoid(_dot(hb, wg2_ref[...])) * _dot(on_ref[...], wn_ref[...])
             + jax.nn.sigmoid(_dot(hb, wg3_ref[...])) * _dot(od_ref[...], wd_ref[...]))
    o_ref[...] = mixed.astype(o_ref.dtype)


LN_SUB_ROWS = 256


def _layer_norm_store(y, g_ref, b_ref, of_ref, ob_ref, rows):
    mu = jnp.mean(y, axis=-1, keepdims=True)
    yc = y - mu
    var = jnp.mean(jnp.square(yc), axis=-1, keepdims=True)
    out = yc * lax.rsqrt(var + 1e-5) * g_ref[...] + b_ref[...]
    of_ref[rows, :] = out
    if ob_ref is not None:
        ob_ref[rows, :] = out.astype(ob_ref.dtype)


def _out_ln_kernel(l_ref, x_ref, w_ref, h_ref, g_ref, b_ref, of_ref, ob_ref):
    w = w_ref[...]
    for r in range(x_ref.shape[0] // LN_SUB_ROWS):
        rows = slice(r * LN_SUB_ROWS, (r + 1) * LN_SUB_ROWS)
        y = ALPHA * h_ref[rows, :] + _dot(x_ref[rows, :], w)
        _layer_norm_store(y, g_ref, b_ref, of_ref, ob_ref, rows)


def _out_ln(name, lidx, x, w, h, g, b):
    t, kdim = x.shape
    d = h.shape[1]
    tm = 512
    return _call(_out_ln_kernel, name, lidx, [x, w, h, g, b], grid=(t // tm,),
                 in_specs=[pl.BlockSpec((tm, kdim), lambda i, l: (i, 0)),
                           pl.BlockSpec((None, kdim, d), lambda i, l: (l[0], 0, 0)),
                           pl.BlockSpec((tm, d), lambda i, l: (i, 0)),
                           pl.BlockSpec((None, 1, d), lambda i, l: (l[0], 0, 0)),
                           pl.BlockSpec((None, 1, d), lambda i, l: (l[0], 0, 0))],
                 out_specs=[pl.BlockSpec((tm, d), lambda i, l: (i, 0)),
                            pl.BlockSpec((tm, d), lambda i, l: (i, 0))],
                 out_shape=[jax.ShapeDtypeStruct((t, d), F32), jax.ShapeDtypeStruct((t, d), BF16)])


def _ffn_ln_kernel(l_ref, x_ref, w_ref, h_ref, g_ref, b_ref, of_ref, *, nk):
    k = pl.program_id(1)

    @pl.when(k == 0)
    def _():
        of_ref[...] = ALPHA * h_ref[...] + _dot(x_ref[...], w_ref[...])

    @pl.when((k > 0) & (k < nk - 1))
    def _():
        of_ref[...] += _dot(x_ref[...], w_ref[...])

    @pl.when(k == nk - 1)
    def _():
        w = w_ref[...]
        for r in range(x_ref.shape[0] // LN_SUB_ROWS):
            rows = slice(r * LN_SUB_ROWS, (r + 1) * LN_SUB_ROWS)
            y = of_ref[rows, :] + _dot(x_ref[rows, :], w)
            _layer_norm_store(y, g_ref, b_ref, of_ref, None, rows)


def _ffn_ln(name, lidx, x, w, h, g, b, tk):
    t, kdim = x.shape
    d = h.shape[1]
    tm = 1024
    nk = kdim // tk
    assert nk >= 2
    return _call(functools.partial(_ffn_ln_kernel, nk=nk), name, lidx, [x, w, h, g, b],
                 grid=(t // tm, nk),
                 in_specs=[pl.BlockSpec((tm, tk), lambda i, k, l: (i, k)),
                           pl.BlockSpec((None, tk, d), lambda i, k, l: (l[0], k, 0)),
                           pl.BlockSpec((tm, d), lambda i, k, l: (i, 0)),
                           pl.BlockSpec((None, 1, d), lambda i, k, l: (l[0], 0, 0)),
                           pl.BlockSpec((None, 1, d), lambda i, k, l: (l[0], 0, 0))],
                 out_specs=pl.BlockSpec((tm, d), lambda i, k, l: (i, 0)),
                 out_shape=jax.ShapeDtypeStruct((t, d), F32),
                 vmem_limit_bytes=VMEM_LIMIT_LARGE_BYTES)


def _swiglu_kernel(l_ref, x_ref, wa_ref, wb_ref, o_ref, wab_ref):
    @pl.when(pl.program_id(1) == 0)
    def _():
        wab_ref[0] = wa_ref[...].astype(BF16)
        wab_ref[1] = wb_ref[...].astype(BF16)

    x = x_ref[...]
    a = _dot(x, wab_ref[0])
    o_ref[...] = (jax.nn.silu(a) * _dot(x, wab_ref[1])).astype(o_ref.dtype)


def _ple_kernel(l_ref, h_ref, p_ref, wpi_ref, wpg_ref, of_ref, ob_ref):
    wpi = wpi_ref[...]
    wpg = wpg_ref[...]
    for r in range(h_ref.shape[0] // LN_SUB_ROWS):
        rows = slice(r * LN_SUB_ROWS, (r + 1) * LN_SUB_ROWS)
        h = h_ref[rows, :]
        p_in = _dot(p_ref[rows, :].astype(BF16), wpi)
        out = h + p_in * jax.nn.sigmoid(_dot(h.astype(BF16), wpg))
        of_ref[rows, :] = out
        ob_ref[rows, :] = out.astype(ob_ref.dtype)


def _rope_tables(n, dim):
    inv = 1.0 / (ROPE_THETA ** (jnp.arange(0, dim, 2, dtype=F32) / dim))
    ang = jnp.arange(n, dtype=F32)[:, None] * inv[None, :]
    return jnp.cos(ang), jnp.sin(ang)


def _layer(lidx, h, hb, consts, weights, bsz, seq):
    (cos128, sin128, cos64, sin64a, sin64b, cmp_cos, cmp_sin, overlap, block_of_key) = consts
    (w_proj, fox_bias, pe_k, pe_v, wk1, wk2, wv1, wv2, kv_norm, kv_up,
     w_br_fox, w_br_nsa, w_br_dsa, w_out, ln1_g, ln1_b, w_ffn_in, w_ffn_out, ln2_g, ln2_b, p, w_ple_in, w_ple_gate) = weights
    t = bsz * seq
    nq = seq // TQ
    d = D_MODEL

    z_a = _proj("proj_plain", lidx, hb, w_proj, COL_PLAIN, N_PLAIN, 2048, 512, BF16, seq)
    z_b = _proj("proj_rope128", lidx, hb, w_proj, COL_ROPE128, N_ROPE128, 2048, 512, BF16, seq,
                rope=((HEAD_DIM // 2,), cos128, sin128))
    z_c = _proj("proj_rope64", lidx, hb, w_proj, COL_ROPE64, N_ROPE64, 2048, 256, BF16, seq,
                rope=((IDX_DIM // 2, LANE - IDX_DIM // 2), cos64, sin64a, sin64b))
    z_d = _proj("proj_f32", lidx, hb, w_proj, COL_F32, N_F32, 1024, N_F32, F32, seq)

    cum = _call(_cum_kernel, "fox_cum", lidx, [z_d, fox_bias], grid=(bsz,),
                in_specs=[pl.BlockSpec((seq, LANE), lambda b, l: (b, ZD_FOX_F)),
                          pl.BlockSpec((None, 1, LANE), lambda b, l: (l[0], 0, 0))],
                out_specs=pl.BlockSpec((seq, LANE), lambda b, l: (b, 0)),
                out_shape=jax.ShapeDtypeStruct((t, LANE), F32))
    cum_rows = jnp.transpose(cum.reshape(bsz, seq, LANE)[:, :, :8], (0, 2, 1)).reshape(bsz, 8 * (seq // TK), TK)
    fq = FOX_HEADS * HEAD_DIM
    o_fox = _call(_fox_kernel, "fox_attn", lidx, [z_a, z_a, z_a, cum, cum_rows], grid=(bsz, nq),
                  in_specs=[pl.BlockSpec((TQ, fq), lambda b, i, l: (b * nq + i, 0)),
                            pl.BlockSpec((seq, fq), lambda b, i, l: (b, 1)),
                            pl.BlockSpec((seq, fq), lambda b, i, l: (b, 2)),
                            pl.BlockSpec((TQ, LANE), lambda b, i, l: (b * nq + i, 0)),
                            pl.BlockSpec((None, 8 * (seq // TK), TK), lambda b, i, l: (b, 0, 0))],
                  out_specs=pl.BlockSpec((TQ, fq), lambda b, i, l: (b * nq + i, 0)),
                  out_shape=jax.ShapeDtypeStruct((t, fq), BF16),
                  scratch_shapes=[pltpu.VMEM((FOX_HEADS, seq // TK, TQ, TK), F32),
                                  pltpu.VMEM((FOX_HEADS, TQ, LANE), F32), pltpu.VMEM((FOX_HEADS, TQ, LANE), F32),
                                  pltpu.VMEM((FOX_HEADS, TQ, 2 * HEAD_DIM), F32)])

    nchunk = seq // CMP_STRIDE
    half = CMP_STRIDE * HEAD_DIM
    r_k = z_d[:, ZD_NSA_KC * LANE:(ZD_NSA_KC + 1) * LANE].reshape(bsz * nchunk, half)
    r_v = z_d[:, ZD_NSA_VC * LANE:(ZD_NSA_VC + 1) * LANE].reshape(bsz * nchunk, half)
    k_cmp, v_cmp = _call(
        _cmp_kernel, "nsa_compress", lidx, [r_k, r_v, pe_k, pe_v, wk1, wk2, wv1, wv2, cmp_cos, cmp_sin], grid=(bsz,),
        in_specs=[pl.BlockSpec((nchunk, half), lambda b, l: (b, 0)),
                  pl.BlockSpec((nchunk, half), lambda b, l: (b, 0)),
                  pl.BlockSpec((None, 2, half), lambda b, l: (l[0], 0, 0)),
                  pl.BlockSpec((None, 2, half), lambda b, l: (l[0], 0, 0)),
                  pl.BlockSpec((None, 2 * half, CMP_HIDDEN), lambda b, l: (l[0], 0, 0)),
                  pl.BlockSpec((None, CMP_HIDDEN, HEAD_DIM), lambda b, l: (l[0], 0, 0)),
                  pl.BlockSpec((None, 2 * half, CMP_HIDDEN), lambda b, l: (l[0], 0, 0)),
                  pl.BlockSpec((None, CMP_HIDDEN, HEAD_DIM), lambda b, l: (l[0], 0, 0)),
                  pl.BlockSpec((nchunk, HEAD_DIM), lambda b, l: (0, 0)),
                  pl.BlockSpec((nchunk, HEAD_DIM), lambda b, l: (0, 0))],
        out_specs=[pl.BlockSpec((nchunk, HEAD_DIM), lambda b, l: (b, 0)),
                   pl.BlockSpec((nchunk, HEAD_DIM), lambda b, l: (b, 0))],
        out_shape=[jax.ShapeDtypeStruct((bsz * nchunk, HEAD_DIM), BF16)] * 2)
    nsq = NSA_HEADS * HEAD_DIM
    o_nsa = _call(
        _nsa_kernel, "nsa_attn", lidx, [z_b, z_b, z_b, z_a, z_a, k_cmp, v_cmp, z_d, overlap, block_of_key], grid=(bsz, nq),
        in_specs=[pl.BlockSpec((TQ, nsq), lambda b, i, l: (b * nq + i, 0)),
                  pl.BlockSpec((seq, LANE), lambda b, i, l: (b, 4)),
                  pl.BlockSpec((seq, LANE), lambda b, i, l: (b, 5)),
                  pl.BlockSpec((seq, LANE), lambda b, i, l: (b, 18)),
                  pl.BlockSpec((seq, LANE), lambda b, i, l: (b, 19)),
                  pl.BlockSpec((nchunk, HEAD_DIM), lambda b, i, l: (b, 0)),
                  pl.BlockSpec((nchunk, HEAD_DIM), lambda b, i, l: (b, 0)),
                  pl.BlockSpec((TQ, LANE), lambda b, i, l: (b * nq + i, ZD_NSA_G)),
                  pl.BlockSpec((N_CMP_PAD, LANE), lambda b, i, l: (0, 0)),
                  pl.BlockSpec((seq, LANE), lambda b, i, l: (0, 0))],
        out_specs=pl.BlockSpec((TQ, nsq), lambda b, i, l: (b * nq + i, 0)),
        out_shape=jax.ShapeDtypeStruct((t, nsq), BF16),
        scratch_shapes=[pltpu.VMEM((NSA_HEADS * TQ, HEAD_DIM), BF16),
                        pltpu.VMEM((NSA_HEADS * TQ, HEAD_DIM), F32),
                        pltpu.VMEM((seq // TK, NSA_HEADS * TQ, TK), F32),
                        pltpu.VMEM((NSA_HEADS * TQ, LANE), F32),
                        pltpu.VMEM((NSA_HEADS * TQ, 2 * HEAD_DIM), F32)])

    tm_kv = 1024
    k_d, v_d = _call(
        _dsa_kv_kernel, "dsa_kv", lidx, [z_d, kv_norm, kv_up, cos128, sin128], grid=(t // tm_kv,),
        in_specs=[pl.BlockSpec((tm_kv, DSA_KV_RANK), lambda i, l: (i, 0)),
                  pl.BlockSpec((None, 1, DSA_KV_RANK), lambda i, l: (l[0], 0, 0)),
                  pl.BlockSpec((None, DSA_KV_RANK, 2 * HEAD_DIM), lambda i, l: (l[0], 0, 0)),
                  pl.BlockSpec((tm_kv, LANE), lambda i, l: (i % (seq // tm_kv), 0)),
                  pl.BlockSpec((tm_kv, LANE), lambda i, l: (i % (seq // tm_kv), 0))],
        out_specs=[pl.BlockSpec((tm_kv, HEAD_DIM), lambda i, l: (i, 0)),
                   pl.BlockSpec((tm_kv, HEAD_DIM), lambda i, l: (i, 0))],
        out_shape=[jax.ShapeDtypeStruct((t, HEAD_DIM), BF16)] * 2)
    dq = DSA_HEADS * HEAD_DIM
    iqw = IDX_HEADS * IDX_DIM
    o_dsa = _call(
        _dsa_kernel, "dsa_attn", lidx, [z_b, k_d, v_d, z_c, z_c, z_c, z_d], grid=(bsz, nq),
        in_specs=[pl.BlockSpec((TQ, dq), lambda b, i, l: (b * nq + i, 1)),
                  pl.BlockSpec((seq, HEAD_DIM), lambda b, i, l: (b, 0)),
                  pl.BlockSpec((seq, HEAD_DIM), lambda b, i, l: (b, 0)),
                  pl.BlockSpec((TQ, iqw), lambda b, i, l: (b * nq + i, 0)),
                  pl.BlockSpec((seq, LANE), lambda b, i, l: (b, iqw // LANE)),
                  pl.BlockSpec((seq, LANE), lambda b, i, l: (b, iqw // LANE + 1)),
                  pl.BlockSpec((TQ, LANE), lambda b, i, l: (b * nq + i, ZD_IDX_W))],
        out_specs=pl.BlockSpec((TQ, dq), lambda b, i, l: (b * nq + i, 0)),
        out_shape=jax.ShapeDtypeStruct((t, dq), BF16),
        scratch_shapes=[pltpu.VMEM((seq // TK, TQ, TK), F32), pltpu.VMEM((seq // TK, TK, TQ), F32),
                        pltpu.VMEM((LANE, TQ), F32), pltpu.VMEM((TQ, LANE), F32), pltpu.VMEM((TQ, LANE), I32),
                        pltpu.VMEM((DSA_HEADS * TQ, HEAD_DIM), BF16),
                        pltpu.VMEM((seq // TK, DSA_HEADS * TQ, TK), F32),
                        pltpu.VMEM((DSA_HEADS * TQ, LANE), F32),
                        pltpu.VMEM((DSA_HEADS * TQ, 2 * HEAD_DIM), F32)])

    tm, tn = 1024, 512
    ncol = d // tn
    gate0 = COL_GATE // tn
    assert gate0 * tn == COL_GATE
    mixed = _call(
        _merge_kernel, "merge", lidx, [hb, o_fox, o_nsa, o_dsa, w_proj, w_proj, w_proj, w_br_fox, w_br_nsa, w_br_dsa],
        grid=(t // tm, ncol),
        in_specs=[pl.BlockSpec((tm, d), lambda i, j, l: (i, 0)),
                  pl.BlockSpec((tm, fq), lambda i, j, l: (i, 0)),
                  pl.BlockSpec((tm, nsq), lambda i, j, l: (i, 0)),
                  pl.BlockSpec((tm, dq), lambda i, j, l: (i, 0)),
                  pl.BlockSpec((None, d, tn), lambda i, j, l: (l[0], 0, gate0 + j)),
                  pl.BlockSpec((None, d, tn), lambda i, j, l: (l[0], 0, gate0 + ncol + j)),
                  pl.BlockSpec((None, d, tn), lambda i, j, l: (l[0], 0, gate0 + 2 * ncol + j)),
                  pl.BlockSpec((None, fq, tn), lambda i, j, l: (l[0], 0, j)),
                  pl.BlockSpec((None, nsq, tn), lambda i, j, l: (l[0], 0, j)),
                  pl.BlockSpec((None, dq, tn), lambda i, j, l: (l[0], 0, j))],
        out_specs=pl.BlockSpec((tm, tn), lambda i, j, l: (i, j)),
        out_shape=jax.ShapeDtypeStruct((t, d), BF16))

    h, hb = _out_ln("out_ln", lidx, mixed, w_out, h, ln1_g, ln1_b)

    nff = D_FF // tn
    act = _call(
        _swiglu_kernel, "swiglu", lidx, [hb, w_ffn_in, w_ffn_in], grid=(nff, t // tm),
        in_specs=[pl.BlockSpec((tm, d), lambda j, i, l: (i, 0)),
                  pl.BlockSpec((None, d, tn), lambda j, i, l: (l[0], 0, j)),
                  pl.BlockSpec((None, d, tn), lambda j, i, l: (l[0], 0, nff + j))],
        out_specs=pl.BlockSpec((tm, tn), lambda j, i, l: (i, j)),
        out_shape=jax.ShapeDtypeStruct((t, D_FF), BF16),
        scratch_shapes=[pltpu.VMEM((2, d, tn), BF16)])
    h = _ffn_ln("ffn_ln", lidx, act, w_ffn_out, h, ln2_g, ln2_b, tk=D_FF // 4)

    tmp = 512
    h, hb = _call(
        _ple_kernel, "ple", lidx, [h, p, w_ple_in, w_ple_gate], grid=(t // tmp,),
        in_specs=[pl.BlockSpec((tmp, d), lambda i, l: (i, 0)),
                  pl.BlockSpec((None, tmp, PLE_DIM), lambda i, l: (l[0], i, 0)),
                  pl.BlockSpec((None, PLE_DIM, d), lambda i, l: (l[0], 0, 0)),
                  pl.BlockSpec((None, d, d), lambda i, l: (l[0], 0, 0))],
        out_specs=[pl.BlockSpec((tmp, d), lambda i, l: (i, 0)),
                   pl.BlockSpec((tmp, d), lambda i, l: (i, 0))],
        out_shape=[jax.ShapeDtypeStruct((t, d), F32), jax.ShapeDtypeStruct((t, d), BF16)])
    return h, hb


def kernel(x, p, w_in, fox_f_bias, nsa_pe_k, nsa_pe_v, nsa_cmp_k1, nsa_cmp_k2, nsa_cmp_v1, nsa_cmp_v2, dsa_kv_norm, dsa_kv_up, w_br_fox, w_br_nsa, w_br_dsa, w_out, ln1_g, ln1_b, w_ffn_in, w_ffn_out, ln2_g, ln2_b, w_ple_in, w_ple_gate):
    bsz, seq, d = x.shape
    depth = w_in.shape[0]
    t = bsz * seq
    assert d == D_MODEL and seq % 1024 == 0 and depth == DEPTH

    cat = functools.partial(jnp.concatenate, axis=-1)
    w_proj = _regroup(w_in)

    fox_bias = jnp.pad(fox_f_bias, ((0, 0), (0, LANE - FOX_HEADS))).reshape(depth, 1, LANE)
    half = CMP_STRIDE * HEAD_DIM
    weights = (
        w_proj, fox_bias,
        nsa_pe_k.reshape(depth, 2, half), nsa_pe_v.reshape(depth, 2, half),
        nsa_cmp_k1.astype(BF16), nsa_cmp_k2.astype(BF16), nsa_cmp_v1.astype(BF16), nsa_cmp_v2.astype(BF16),
        dsa_kv_norm.reshape(depth, 1, DSA_KV_RANK), dsa_kv_up.astype(BF16),
        w_br_fox.astype(BF16), w_br_nsa.astype(BF16), w_br_dsa.astype(BF16), w_out.astype(BF16),
        ln1_g.reshape(depth, 1, d), ln1_b.reshape(depth, 1, d),
        w_ffn_in, w_ffn_out.astype(BF16),
        ln2_g.reshape(depth, 1, d), ln2_b.reshape(depth, 1, d),
        p.reshape(depth, t, PLE_DIM), w_ple_in.astype(BF16), w_ple_gate.astype(BF16),
    )

    cos, sin = _rope_tables(seq, HEAD_DIM)
    cos128 = cat([cos, cos])
    sin128 = cat([-sin, sin])
    cos_i, sin_i = _rope_tables(seq, IDX_DIM)
    zi = jnp.zeros_like(sin_i)
    cos64 = cat([cos_i, cos_i, cos_i, cos_i])
    sin64a = cat([zi, sin_i, zi, sin_i])
    sin64b = cat([-sin_i, zi, -sin_i, zi])
    n_cmp = (seq - CMP_LEN) // CMP_STRIDE + 1
    c_end = jnp.minimum(jnp.arange(N_CMP_PAD) * CMP_STRIDE + CMP_LEN - 1, seq - 1)
    cmp_cos, cmp_sin = cos128[c_end], sin128[c_end]
    n_slc = seq // SLC_LEN
    c_start = jnp.arange(N_CMP_PAD) * CMP_STRIDE
    s_start = jnp.arange(LANE) * SLC_LEN
    overlap = jnp.maximum(jnp.minimum(c_start[:, None] + CMP_LEN - 1, s_start[None, :] + SLC_LEN - 1)
                          - jnp.maximum(c_start[:, None], s_start[None, :]) + 1, 0).astype(F32) / CMP_LEN
    overlap = jnp.where((jnp.arange(N_CMP_PAD)[:, None] < n_cmp) & (jnp.arange(LANE)[None, :] < n_slc), overlap, 0.0)
    overlap = overlap.T.astype(BF16)
    block_of_key = (jnp.arange(seq)[:, None] // SLC_LEN == jnp.arange(LANE)[None, :]).astype(BF16)
    consts = (cos128, sin128, cos64, sin64a, sin64b, cmp_cos, cmp_sin, overlap, block_of_key)

    h = x.reshape(t, d)
    hb = h.astype(BF16)
    for layer in range(depth):
        lidx = jnp.full((1,), layer, I32)
        h, hb = _layer(lidx, h, hb, consts, weights, bsz, seq)
    return h.reshape(bsz, seq, d)
```

```python
import functools

import jax
import jax.numpy as jnp
from jax import lax
from jax.experimental import pallas as pl
from jax.experimental.pallas import tpu as pltpu

F32, BF16, I32, I16 = jnp.float32, jnp.bfloat16, jnp.int32, jnp.int16

D_MODEL = 2048
DEPTH = 4
HEAD_DIM = 128
ROPE_THETA = 10000.0
NEG_INF = -1e30
FOX_HEADS = 6
NSA_HEADS = 4
CMP_LEN = 32
CMP_STRIDE = 16
CMP_HIDDEN = 256
SLC_LEN = 64
SLC_TOPN = 16
WIN = 512
SLC_FORCE = 1e4
DSA_HEADS = 6
DSA_KV_RANK = 256
IDX_HEADS = 16
IDX_DIM = 64
IDX_TOPK_MAX = 256
D_FF = ((8 * D_MODEL + 3 * 256 - 1) // (3 * 256)) * 256
PLE_DIM = 256
ALPHA = (2 * DEPTH) ** 0.25
SCALE = HEAD_DIM ** -0.5
LOG2E = 1.4426950408889634
SCALE_LOG2E = SCALE * LOG2E

LANE = 128
VMEM_LIMIT_BYTES = 48 * 1024 * 1024
VMEM_LIMIT_LARGE_BYTES = 58 * 1024 * 1024

_IN_SPLITS = (
    ('fox_q', FOX_HEADS * HEAD_DIM), ('fox_k', FOX_HEADS * HEAD_DIM), ('fox_v', FOX_HEADS * HEAD_DIM), ('fox_f', FOX_HEADS),
    ('nsa_q', NSA_HEADS * HEAD_DIM), ('nsa_kc', HEAD_DIM), ('nsa_vc', HEAD_DIM), ('nsa_ks', HEAD_DIM), ('nsa_vs', HEAD_DIM),
    ('nsa_kw', HEAD_DIM), ('nsa_vw', HEAD_DIM), ('nsa_g', 3 * NSA_HEADS),
    ('dsa_q', DSA_HEADS * HEAD_DIM), ('dsa_ckv', DSA_KV_RANK), ('idx_q', IDX_HEADS * IDX_DIM), ('idx_k', IDX_DIM), ('idx_w', IDX_HEADS),
    ('gate', 3 * D_MODEL),
)
_IN_OFFSETS = {}
_off = 0
for _name, _width in _IN_SPLITS:
    _IN_OFFSETS[_name] = (_off, _off + _width)
    _off += _width

MISC_FOX_F = 0
MISC_NSA_G = FOX_HEADS
MISC_IDX_W = FOX_HEADS + 3 * NSA_HEADS

TQ = 256
TK = 256
N_CMP_PAD = 128


def _dot(a, b):
    return jnp.dot(a, b, preferred_element_type=F32)


def _dot_nt(a, b):
    return lax.dot_general(a, b, (((1,), (1,)), ((), ())), preferred_element_type=F32)


def _fori_pairs(n, body, init):
    def four(jj, carry):
        for u in range(4):
            carry = body(4 * jj + u, carry)
        return carry
    carry = lax.fori_loop(0, lax.shift_right_logical(n, jnp.int32(2)), four, init)
    base = n & jnp.int32(-4)
    carry = lax.cond((n & 2) == 2, lambda c: body(base + 1, body(base, c)), lambda c: c, carry)
    return lax.cond((n & 1) == 1, lambda c: body(n - 1, c), lambda c: c, carry)


def _split3(a):
    a1 = a.astype(BF16)
    r1 = a - a1.astype(F32)
    a2 = r1.astype(BF16)
    a3 = (r1 - a2.astype(F32)).astype(BF16)
    return a1, a2, a3


def _call(kernel, name, lidx, args, grid, in_specs, out_specs, out_shape, scratch_shapes=(),
          vmem_limit_bytes=VMEM_LIMIT_BYTES):
    return pl.pallas_call(
        kernel,
        grid_spec=pltpu.PrefetchScalarGridSpec(
            num_scalar_prefetch=1, grid=grid, in_specs=in_specs, out_specs=out_specs,
            scratch_shapes=list(scratch_shapes)),
        out_shape=out_shape,
        compiler_params=pltpu.CompilerParams(
            dimension_semantics=("arbitrary",) * len(grid), vmem_limit_bytes=vmem_limit_bytes),
        name=name,
    )(lidx, *args)


def _regroup_plan():
    plan = []

    def whole(name):
        a, b = _IN_OFFSETS[name]
        for g in range((b - a) // LANE):
            plan.append((a + g * LANE, 0, LANE))

    whole('dsa_ckv')
    lane0 = {}
    start = 0
    for name in ('fox_f', 'nsa_g', 'idx_w'):
        a, b = _IN_OFFSETS[name]
        lane0[name] = (a - start, start, start + b - a)
        start += b - a
    plan.append(lane0['fox_f'])
    whole('nsa_kc')
    whole('nsa_vc')
    plan.append(lane0['nsa_g'])
    plan.append(lane0['idx_w'])
    plan.append((0, 0, 0))
    for name in ('fox_q', 'fox_k', 'fox_v', 'nsa_vs', 'nsa_vw'):
        whole(name)
    for name in ('nsa_q', 'nsa_ks', 'nsa_kw', 'dsa_q'):
        whole(name)
    whole('gate')
    whole('idx_q')
    ik = _IN_OFFSETS['idx_k'][0]
    plan.append((ik, 0, IDX_DIM))
    plan.append((ik - IDX_DIM, IDX_DIM, LANE))
    return tuple(plan)


_REGROUP_PLAN = _regroup_plan()
COL_F32 = 0
N_F32 = DSA_KV_RANK + 6 * LANE
ZD_FOX_F, ZD_NSA_KC, ZD_NSA_VC, ZD_NSA_G, ZD_IDX_W = 2, 3, 4, 5, 6
COL_PLAIN = COL_F32 + N_F32
N_PLAIN = (3 * FOX_HEADS + 2) * HEAD_DIM
COL_ROPE128 = COL_PLAIN + N_PLAIN
N_ROPE128 = (NSA_HEADS + 2 + DSA_HEADS) * HEAD_DIM
COL_GATE = COL_ROPE128 + N_ROPE128
COL_ROPE64 = COL_GATE + 3 * D_MODEL
N_ROPE64 = IDX_HEADS * IDX_DIM + 2 * LANE
N_REGROUPED = COL_ROPE64 + N_ROPE64
assert N_REGROUPED == len(_REGROUP_PLAN) * LANE


def _regroup_kernel(src_ref, lo_ref, hi_ref, x_ref, o_ref):
    g = pl.program_id(0)
    col = lax.broadcasted_iota(I32, (LANE, x_ref.shape[2]), 0)
    keep = (col >= lo_ref[g]) & (col < hi_ref[g])
    for layer in range(x_ref.shape[1]):
        o_ref[layer] = jnp.where(keep, x_ref[:, layer, :], 0.0).T.astype(o_ref.dtype)


def _regroup(w_in):
    depth, d, n_in = w_in.shape
    w_t = jnp.transpose(w_in, (2, 0, 1))
    src = jnp.array([p[0] for p in _REGROUP_PLAN], I32)
    lo = jnp.array([p[1] for p in _REGROUP_PLAN], I32)
    hi = jnp.array([p[2] for p in _REGROUP_PLAN], I32)
    return pl.pallas_call(
        _regroup_kernel,
        grid_spec=pltpu.PrefetchScalarGridSpec(
            num_scalar_prefetch=3, grid=(len(_REGROUP_PLAN),),
            in_specs=[pl.BlockSpec((pl.Element(LANE), pl.Element(depth), pl.Element(d)),
                                   lambda g, src, lo, hi: (src[g], 0, 0))],
            out_specs=pl.BlockSpec((depth, d, LANE), lambda g, src, lo, hi: (0, 0, g))),
        out_shape=jax.ShapeDtypeStruct((depth, d, N_REGROUPED), BF16),
        compiler_params=pltpu.CompilerParams(dimension_semantics=("arbitrary",), vmem_limit_bytes=VMEM_LIMIT_BYTES),
        name="regroup_w_in",
    )(src, lo, hi, w_t)


def _proj_kernel(l_ref, x_ref, w_ref, *refs, shifts):
    o_ref = refs[-1]
    acc = _dot(x_ref[...], w_ref[...])
    if not shifts:
        o_ref[...] = acc.astype(o_ref.dtype)
        return
    c_ref = refs[0]
    s_refs = refs[1:-1]
    for g in range(acc.shape[1] // LANE):
        xg = acc[:, g * LANE:(g + 1) * LANE]
        out = xg * c_ref[...]
        for shift, s_ref in zip(shifts, s_refs):
            out = out + pltpu.roll(xg, shift, 1) * s_ref[...]
        o_ref[:, g * LANE:(g + 1) * LANE] = out.astype(o_ref.dtype)


def _proj(name, lidx, hb, w, col0, n, tm, tn, out_dtype, seq, rope=None):
    t, d = hb.shape
    jb = col0 // tn
    assert jb * tn == col0
    in_specs = [pl.BlockSpec((tm, d), lambda i, j, l: (i, 0)),
                pl.BlockSpec((None, d, tn), lambda i, j, l: (l[0], 0, jb + j))]
    args = [hb, w]
    shifts = ()
    if rope is not None:
        shifts, tables = rope[0], rope[1:]
        nrow = seq // tm
        for tab in tables:
            in_specs.append(pl.BlockSpec((tm, LANE), lambda i, j, l: (i % nrow, 0)))
            args.append(tab)
    return _call(functools.partial(_proj_kernel, shifts=shifts), name, lidx, args,
                 grid=(t // tm, n // tn), in_specs=in_specs,
                 out_specs=pl.BlockSpec((tm, tn), lambda i, j, l: (i, j)),
                 out_shape=jax.ShapeDtypeStruct((t, n), out_dtype))


def _cum_kernel(l_ref, f_ref, bias_ref, o_ref):
    chunk = 256
    row = lax.broadcasted_iota(I32, (chunk, chunk), 0)
    col = lax.broadcasted_iota(I32, (chunk, chunk), 1)
    tri = jnp.where(row >= col, 1.0, 0.0).astype(BF16)
    carry = jnp.zeros((1, LANE), F32)
    for c in range(f_ref.shape[0] // chunk):
        logf = jax.nn.log_sigmoid(f_ref[c * chunk:(c + 1) * chunk, :] + bias_ref[...])
        l1, l2, l3 = _split3(logf)
        cs = _dot(tri, l1) + _dot(tri, l2) + _dot(tri, l3) + carry
        o_ref[c * chunk:(c + 1) * chunk, :] = cs
        carry = cs[chunk - 1:chunk, :]


def _max_lanes(x):
    out = x[:, 0:LANE]
    for g in range(1, x.shape[1] // LANE):
        out = jnp.maximum(out, x[:, g * LANE:(g + 1) * LANE])
    return out


def _tile_lanes(x, n):
    return jnp.concatenate([x] * n, axis=1)


def _row_max_to_lanes(mx):
    return jnp.broadcast_to(jnp.max(mx, axis=-1, keepdims=True), mx.shape)


def _exp_pv(s, m_lanes, v):
    p = jnp.exp2(s - _tile_lanes(m_lanes, s.shape[1] // LANE)).astype(BF16)
    return _dot(p, jnp.concatenate([v, jnp.ones_like(v)], axis=1))


def _fox_kernel(l_ref, q_ref, k_ref, v_ref, cq_ref, ck_ref, o_ref, s_ref, mx_ref, cqb_ref, acc_ref):
    qi = pl.program_id(1)
    nh = FOX_HEADS
    nchunk = ck_ref.shape[0] // 8
    row = lax.broadcasted_iota(I32, (TQ, TK), 0)
    col = lax.broadcasted_iota(I32, (TQ, TK), 1)
    cq_all = cq_ref[...]
    for h in range(nh):
        cqb_ref[h] = jnp.broadcast_to(cq_all[:, h:h + 1] * LOG2E, (TQ, LANE))
    mx_ref[...] = jnp.full(mx_ref.shape, NEG_INF, F32)
    acc_ref[...] = jnp.zeros(acc_ref.shape, F32)

    def score_step(j, diagonal):
        start = pl.multiple_of(j * TK, TK)
        for h in range(nh):
            hs = slice(h * HEAD_DIM, (h + 1) * HEAD_DIM)
            s = (_dot_nt(q_ref[:, hs], k_ref[pl.ds(start, TK), hs]) * SCALE_LOG2E
                 + _tile_lanes(cqb_ref[h], TK // LANE) - ck_ref[pl.ds(h * nchunk + j, 1), :] * LOG2E)
            if diagonal:
                s = jnp.where(col <= row, s, NEG_INF)
            s_ref[h, j] = s
            mx_ref[h] = jnp.maximum(mx_ref[h], _max_lanes(s))

    def score_body(j, carry):
        score_step(j, False)
        return carry

    _fori_pairs(qi, score_body, 0)
    score_step(qi, True)
    for h in range(nh):
        mx_ref[h] = _row_max_to_lanes(mx_ref[h])

    def pv_body(j, carry):
        start = pl.multiple_of(j * TK, TK)
        for h in range(nh):
            hs = slice(h * HEAD_DIM, (h + 1) * HEAD_DIM)
            acc_ref[h] += _exp_pv(s_ref[h, j], mx_ref[h], v_ref[pl.ds(start, TK), hs])
        return carry

    _fori_pairs(qi + 1, pv_body, 0)
    for h in range(nh):
        a = acc_ref[h]
        o_ref[:, h * HEAD_DIM:(h + 1) * HEAD_DIM] = (a[:, :HEAD_DIM] / a[:, HEAD_DIM:]).astype(o_ref.dtype)


def _cmp_kernel(l_ref, rk_ref, rv_ref, pek_ref, pev_ref, wk1_ref, wk2_ref, wv1_ref, wv2_ref,
                c_ref, s_ref, ko_ref, vo_ref):
    half = CMP_STRIDE * HEAD_DIM

    def compress(r_ref, pe_ref, w1_ref, w2_ref):
        r = r_ref[...]
        lo = (r + pe_ref[0:1, :]).astype(BF16)
        hi = (r + pe_ref[1:2, :]).astype(BF16)
        a = _dot(lo, w1_ref[0:half, :])
        b = _dot(hi, w1_ref[half:2 * half, :])
        hid = a + pltpu.roll(b, N_CMP_PAD - 1, 0)
        return _dot(jax.nn.gelu(hid).astype(BF16), w2_ref[...])

    kc = compress(rk_ref, pek_ref, wk1_ref, wk2_ref)
    kc = kc * c_ref[...] + pltpu.roll(kc, HEAD_DIM // 2, 1) * s_ref[...]
    ko_ref[...] = kc.astype(ko_ref.dtype)
    vo_ref[...] = compress(rv_ref, pev_ref, wv1_ref, wv2_ref).astype(vo_ref.dtype)


def _masked_softmax(s, mask):
    s = jnp.where(mask, s, NEG_INF)
    m = jnp.max(s, axis=-1, keepdims=True)
    e = jnp.where(mask, jnp.exp(s - m), 0.0)
    den = jnp.sum(e, axis=-1, keepdims=True)
    return e, jnp.where(den > 0.0, den, 1.0)


def _nsa_kernel(l_ref, q_ref, ks_ref, kw_ref, vs_ref, vw_ref, kc_ref, vc_ref, g_ref, ovt_ref, et_ref,
                o_ref, q_sc, ow_ref, s_ref, mx_ref, acc_ref):
    qi = pl.program_id(1)
    nh = NSA_HEADS
    for h in range(nh):
        q_sc[h * TQ:(h + 1) * TQ, :] = q_ref[:, h * HEAD_DIM:(h + 1) * HEAD_DIM]
    q = q_sc[...]
    pos = qi * TQ + lax.broadcasted_iota(I32, (TQ, 1), 0)
    lane = lax.broadcasted_iota(I32, (TQ, LANE), 1)

    band = WIN + TQ
    start_w = pl.multiple_of(jnp.maximum(qi * TQ - WIN, 0), TQ)
    kwb = kw_ref[pl.ds(start_w, band), :]
    vwb = vw_ref[pl.ds(start_w, band), :]
    dist = pos - (start_w + lax.broadcasted_iota(I32, (TQ, band), 1))
    wvis = (dist >= 0) & (dist < WIN)
    for h in range(nh):
        s_w = _dot_nt(q_ref[:, h * HEAD_DIM:(h + 1) * HEAD_DIM], kwb) * SCALE_LOG2E
        s_w = jnp.where(wvis, s_w, NEG_INF)
        a_win = _exp_pv(s_w, _row_max_to_lanes(_max_lanes(s_w)), vwb)
        ow_ref[h * TQ:(h + 1) * TQ, :] = a_win[:, :HEAD_DIM] / a_win[:, HEAD_DIM:]

    cvis = (lane * CMP_STRIDE + (CMP_LEN - 1)) <= pos
    s_c = (_dot_nt(q, kc_ref[...]) * SCALE).reshape(nh, TQ, N_CMP_PAD)
    e_c, den_c = _masked_softmax(s_c, cvis[None])
    p_c = e_c / den_c
    o_cmp = _dot(p_c.reshape(nh * TQ, N_CMP_PAD).astype(BF16), vc_ref[...])

    p_sum = p_c[0]
    for h in range(1, nh):
        p_sum = p_sum + p_c[h]
    p1, p2, p3 = _split3(p_sum)
    ovt = ovt_ref[...]
    n_slc = et_ref.shape[0] // SLC_LEN
    imp = (_dot_nt(ovt, p1) + _dot_nt(ovt, p2) + _dot_nt(ovt, p3))[0:n_slc, :]
    blk = lax.broadcasted_iota(I32, (n_slc, TQ), 0)
    pos_t = qi * TQ + lax.broadcasted_iota(I32, (n_slc, TQ), 1)
    cur = lax.shift_right_logical(pos_t, SLC_LEN.bit_length() - 1)
    forced = (blk == 0) | (blk == cur) | (blk == cur - 1)
    imp = jnp.where(forced, SLC_FORCE, jnp.where(blk * SLC_LEN <= pos_t, imp, -SLC_FORCE))
    rank = jnp.zeros((n_slc, TQ), F32)
    for k in range(n_slc):
        ik = imp[k:k + 1, :]
        ahead = (ik > imp) | ((ik == imp) & (blk > k))
        rank = rank + jnp.where(ahead, 1.0, 0.0)
    sel_t = jnp.where(rank < float(min(SLC_TOPN, n_slc)), 1.0, 0.0)
    sel_t = jnp.concatenate([sel_t, jnp.zeros((LANE - n_slc, TQ), F32)], axis=0)
    sel = sel_t.T.astype(BF16)

    mx_ref[...] = jnp.full(mx_ref.shape, NEG_INF, F32)
    acc_ref[...] = jnp.zeros(acc_ref.shape, F32)
    row = lax.broadcasted_iota(I32, (TQ, TK), 0)
    col = lax.broadcasted_iota(I32, (TQ, TK), 1)

    def score_step(j, diagonal):
        start = pl.multiple_of(j * TK, TK)
        vis = _dot_nt(sel, et_ref[pl.ds(start, TK), :]) > 0.5
        if diagonal:
            vis = vis & (col <= row)
        s = (_dot_nt(q_sc[...], ks_ref[pl.ds(start, TK), :]) * SCALE_LOG2E).reshape(nh, TQ, TK)
        s = jnp.where(vis[None], s, NEG_INF).reshape(nh * TQ, TK)
        s_ref[j] = s
        mx_ref[...] = jnp.maximum(mx_ref[...], _max_lanes(s))

    def score_body(j, carry):
        score_step(j, False)
        return carry

    _fori_pairs(qi, score_body, 0)
    score_step(qi, True)
    mx_ref[...] = _row_max_to_lanes(mx_ref[...])

    def pv_body(j, carry):
        start = pl.multiple_of(j * TK, TK)
        acc_ref[...] += _exp_pv(s_ref[j], mx_ref[...], vs_ref[pl.ds(start, TK), :])
        return carry

    _fori_pairs(qi + 1, pv_body, 0)
    a_slc = acc_ref[...]
    o_slc = a_slc[:, :HEAD_DIM] / a_slc[:, HEAD_DIM:]

    o_win = ow_ref[...]
    gate = jax.nn.sigmoid(g_ref[...])
    for h in range(nh):
        rows = slice(h * TQ, (h + 1) * TQ)
        g0 = gate[:, MISC_NSA_G + 3 * h:MISC_NSA_G + 3 * h + 1]
        g1 = gate[:, MISC_NSA_G + 3 * h + 1:MISC_NSA_G + 3 * h + 2]
        g2 = gate[:, MISC_NSA_G + 3 * h + 2:MISC_NSA_G + 3 * h + 3]
        out = g0 * o_cmp[rows] + g1 * o_slc[rows] + g2 * o_win[rows]
        o_ref[:, h * HEAD_DIM:(h + 1) * HEAD_DIM] = out.astype(o_ref.dtype)


def _dsa_kv_kernel(l_ref, x_ref, g_ref, up_ref, c_ref, s_ref, ko_ref, vo_ref):
    x = x_ref[...]
    r = x * lax.rsqrt(jnp.mean(jnp.square(x), axis=-1, keepdims=True) + 1e-6) * g_ref[...]
    kv = _dot(r.astype(BF16), up_ref[...])
    k = kv[:, :HEAD_DIM]
    k = k * c_ref[...] + pltpu.roll(k, HEAD_DIM // 2, 1) * s_ref[...]
    ko_ref[...] = k.astype(ko_ref.dtype)
    vo_ref[...] = kv[:, HEAD_DIM:].astype(vo_ref.dtype)


def _dsa_kernel(l_ref, q_ref, kd_ref, vd_ref, iq_ref, ika_ref, ikb_ref, w_ref, o_ref,
                sc_ref, sct_ref, sb_ref, wt_ref, thr_sel, cut_sel, q_sc, s_ref, mx_ref, acc_ref):
    qi = pl.program_id(1)
    nh = DSA_HEADS
    seq = kd_ref.shape[0]
    k_top = float(min(IDX_TOPK_MAX, seq // 4))
    row = lax.broadcasted_iota(I32, (TQ, TK), 0)
    col = lax.broadcasted_iota(I32, (TQ, TK), 1)
    int_min = jnp.int32(-2 ** 31)
    idx_bits = (seq - 1).bit_length()

    wt_ref[...] = (w_ref[...] * (IDX_HEADS ** -0.5 * IDX_DIM ** -0.5)).T

    def score_step(j, diagonal):
        start = pl.multiple_of(j * TK, TK)
        ka = ika_ref[pl.ds(start, TK), :]
        kb = ikb_ref[pl.ds(start, TK), :]
        sc = jnp.zeros((TK, TQ), F32)
        for g in range(IDX_HEADS // 2):
            qpair = iq_ref[:, g * LANE:(g + 1) * LANE]
            wa = wt_ref[MISC_IDX_W + 2 * g:MISC_IDX_W + 2 * g + 1, :]
            wb = wt_ref[MISC_IDX_W + 2 * g + 1:MISC_IDX_W + 2 * g + 2, :]
            sc = sc + jnp.maximum(_dot_nt(ka, qpair), 0.0) * wa + jnp.maximum(_dot_nt(kb, qpair), 0.0) * wb
        if diagonal:
            sc = jnp.where(row <= col, sc, NEG_INF)
        sct_ref[j] = sc
        sb_ref[j] = sc.astype(BF16)
        sc_ref[j] = sc.T

    def score_body(j, carry):
        score_step(j, False)
        return carry

    _fori_pairs(qi, score_body, 0)
    score_step(qi, True)

    krow = lax.broadcasted_iota(I32, (TK, TQ), 0)

    def count(pred_fn):
        def body(j, acc):
            hit = jnp.where(pred_fn(sct_ref[j], j), 1.0, 0.0)
            return acc + jnp.sum(hit.reshape(TK // 8, 8, TQ), axis=0)
        part = _fori_pairs(qi + 1, body, jnp.zeros((8, TQ), F32))
        return jnp.sum(part, axis=0, keepdims=True)

    def key_to_score(key):
        return lax.bitcast_convert_type(jnp.where(key < 0, key ^ jnp.int32(0x7FFFFFFF), key), F32)

    def count_coarse(cand):
        def body(j, acc):
            hit = jnp.where(sb_ref[j] >= cand, jnp.int16(1), jnp.int16(0))
            for g in range(TK // 16):
                acc = acc + hit[g * 16:(g + 1) * 16, :]
            return acc
        part = _fori_pairs(qi + 1, body, jnp.zeros((16, TQ), I16))
        return jnp.sum(part.astype(F32), axis=0, keepdims=True)

    def coarse_score(hkey):
        hbits = jnp.where(hkey < 0, hkey ^ jnp.int32(0x7FFF), hkey)
        return lax.bitcast_convert_type(lax.shift_left(hbits, jnp.int32(16)), F32)

    zero_row = jnp.zeros((1, TQ), I32)
    hkey = jnp.where(count_coarse(jnp.zeros((1, TQ), BF16)) >= k_top, zero_row, zero_row - 32768)

    def coarse_bit(i, hkey):
        cand = hkey + (jnp.int32(1) << (14 - i))
        return jnp.where(count_coarse(coarse_score(cand).astype(BF16)) >= k_top, cand, hkey)

    coarse_bits = lax.bitcast_convert_type(coarse_score(lax.fori_loop(0, 15, coarse_bit, hkey)), I32)
    thr_key = jnp.where(coarse_bits < 0, coarse_bits ^ jnp.int32(0x7FFFFFFF), coarse_bits) - 65536

    def value_bit(i, thr_key):
        cand = thr_key + (jnp.int32(1) << (16 - i))
        cand_score = key_to_score(cand)
        return jnp.where(count(lambda s, j: s >= cand_score) >= k_top, cand, thr_key)

    thr = key_to_score(lax.fori_loop(0, 17, value_bit, thr_key))
    n_gt = count(lambda s, j: s > thr)
    n_ge = count(lambda s, j: s >= thr)

    def index_bisect():
        def index_bit(i, cut):
            cand = cut + (jnp.int32(1) << (idx_bits - 1 - i))
            n_before = count(lambda s, j: (s == thr) & (j * TK + krow < cand))
            return jnp.where(n_gt + n_before < k_top, cand, cut)
        return lax.fori_loop(0, idx_bits, index_bit, jnp.zeros((1, TQ), I32))

    def take_all_ties():
        return jnp.full((1, TQ), seq - 1, I32)

    cut = lax.cond(jnp.max(n_ge) > k_top, index_bisect, take_all_ties)
    thr_sel[...] = jnp.broadcast_to(thr, (LANE, TQ)).T
    cut_sel[...] = jnp.broadcast_to(cut, (LANE, TQ)).T

    for h in range(nh):
        q_sc[h * TQ:(h + 1) * TQ, :] = q_ref[:, h * HEAD_DIM:(h + 1) * HEAD_DIM]
    mx_ref[...] = jnp.full(mx_ref.shape, NEG_INF, F32)
    acc_ref[...] = jnp.zeros(acc_ref.shape, F32)

    def attn_score_step(j, diagonal):
        start = pl.multiple_of(j * TK, TK)
        sc = sc_ref[j]
        thr_t = _tile_lanes(thr_sel[...], TK // LANE)
        cut_t = _tile_lanes(cut_sel[...], TK // LANE)
        vis = (sc > thr_t) | ((sc == thr_t) & (j * TK + col <= cut_t))
        if diagonal:
            vis = vis & (col <= row)
        s = (_dot_nt(q_sc[...], kd_ref[pl.ds(start, TK), :]) * SCALE_LOG2E).reshape(nh, TQ, TK)
        s = jnp.where(vis[None], s, NEG_INF).reshape(nh * TQ, TK)
        s_ref[j] = s
        mx_ref[...] = jnp.maximum(mx_ref[...], _max_lanes(s))

    def attn_score_body(j, carry):
        attn_score_step(j, False)
        return carry

    _fori_pairs(qi, attn_score_body, 0)
    attn_score_step(qi, True)
    mx_ref[...] = _row_max_to_lanes(mx_ref[...])

    def pv_body(j, carry):
        start = pl.multiple_of(j * TK, TK)
        acc_ref[...] += _exp_pv(s_ref[j], mx_ref[...], vd_ref[pl.ds(start, TK), :])
        return carry

    _fori_pairs(qi + 1, pv_body, 0)
    a = acc_ref[...]
    out = a[:, :HEAD_DIM] / a[:, HEAD_DIM:]
    for h in range(nh):
        o_ref[:, h * HEAD_DIM:(h + 1) * HEAD_DIM] = out[h * TQ:(h + 1) * TQ].astype(o_ref.dtype)


def _merge_kernel(l_ref, hb_ref, of_ref, on_ref, od_ref, wg1_ref, wg2_ref, wg3_ref, wf_ref, wn_ref, wd_ref, o_ref):
    hb = hb_ref[...]
    mixed = (jax.nn.sigmoid(_dot(hb, wg1_ref[...])) * _dot(of_ref[...], wf_ref[...])
             + jax.nn.sigmoid(_dot(hb, wg2_ref[...])) * _dot(on_ref[...], wn_ref[...])
             + jax.nn.sigmoid(_dot(hb, wg3_ref[...])) * _dot(od_ref[...], wd_ref[...]))
    o_ref[...] = mixed.astype(o_ref.dtype)


LN_SUB_ROWS = 256


def _layer_norm_store(y, g_ref, b_ref, of_ref, ob_ref, rows):
    mu = jnp.mean(y, axis=-1, keepdims=True)
    yc = y - mu
    var = jnp.mean(jnp.square(yc), axis=-1, keepdims=True)
    out = yc * lax.rsqrt(var + 1e-5) * g_ref[...] + b_ref[...]
    of_ref[rows, :] = out
    if ob_ref is not None:
        ob_ref[rows, :] = out.astype(ob_ref.dtype)


def _out_ln_kernel(l_ref, x_ref, w_ref, h_ref, g_ref, b_ref, of_ref, ob_ref):
    w = w_ref[...]
    for r in range(x_ref.shape[0] // LN_SUB_ROWS):
        rows = slice(r * LN_SUB_ROWS, (r + 1) * LN_SUB_ROWS)
        y = ALPHA * h_ref[rows, :] + _dot(x_ref[rows, :], w)
        _layer_norm_store(y, g_ref, b_ref, of_ref, ob_ref, rows)


def _out_ln(name, lidx, x, w, h, g, b):
    t, kdim = x.shape
    d = h.shape[1]
    tm = 512
    return _call(_out_ln_kernel, name, lidx, [x, w, h, g, b], grid=(t // tm,),
                 in_specs=[pl.BlockSpec((tm, kdim), lambda i, l: (i, 0)),
                           pl.BlockSpec((None, kdim, d), lambda i, l: (l[0], 0, 0)),
                           pl.BlockSpec((tm, d), lambda i, l: (i, 0)),
                           pl.BlockSpec((None, 1, d), lambda i, l: (l[0], 0, 0)),
                           pl.BlockSpec((None, 1, d), lambda i, l: (l[0], 0, 0))],
                 out_specs=[pl.BlockSpec((tm, d), lambda i, l: (i, 0)),
                            pl.BlockSpec((tm, d), lambda i, l: (i, 0))],
                 out_shape=[jax.ShapeDtypeStruct((t, d), F32), jax.ShapeDtypeStruct((t, d), BF16)])


def _ffn_ln_kernel(l_ref, x_ref, w_ref, h_ref, g_ref, b_ref, of_ref, *, nk):
    k = pl.program_id(1)

    @pl.when(k == 0)
    def _():
        of_ref[...] = ALPHA * h_ref[...] + _dot(x_ref[...], w_ref[...])

    @pl.when((k > 0) & (k < nk - 1))
    def _():
        of_ref[...] += _dot(x_ref[...], w_ref[...])

    @pl.when(k == nk - 1)
    def _():
        w = w_ref[...]
        for r in range(x_ref.shape[0] // LN_SUB_ROWS):
            rows = slice(r * LN_SUB_ROWS, (r + 1) * LN_SUB_ROWS)
            y = of_ref[rows, :] + _dot(x_ref[rows, :], w)
            _layer_norm_store(y, g_ref, b_ref, of_ref, None, rows)


def _ffn_ln(name, lidx, x, w, h, g, b, tk):
    t, kdim = x.shape
    d = h.shape[1]
    tm = 1024
    nk = kdim // tk
    assert nk >= 2
    return _call(functools.partial(_ffn_ln_kernel, nk=nk), name, lidx, [x, w, h, g, b],
                 grid=(t // tm, nk),
                 in_specs=[pl.BlockSpec((tm, tk), lambda i, k, l: (i, k)),
                           pl.BlockSpec((None, tk, d), lambda i, k, l: (l[0], k, 0)),
                           pl.BlockSpec((tm, d), lambda i, k, l: (i, 0)),
                           pl.BlockSpec((None, 1, d), lambda i, k, l: (l[0], 0, 0)),
                           pl.BlockSpec((None, 1, d), lambda i, k, l: (l[0], 0, 0))],
                 out_specs=pl.BlockSpec((tm, d), lambda i, k, l: (i, 0)),
                 out_shape=jax.ShapeDtypeStruct((t, d), F32),
                 vmem_limit_bytes=VMEM_LIMIT_LARGE_BYTES)


def _swiglu_kernel(l_ref, x_ref, wa_ref, wb_ref, o_ref, wab_ref):
    @pl.when(pl.program_id(1) == 0)
    def _():
        wab_ref[0] = wa_ref[...].astype(BF16)
        wab_ref[1] = wb_ref[...].astype(BF16)

    x = x_ref[...]
    a = _dot(x, wab_ref[0])
    o_ref[...] = (jax.nn.silu(a) * _dot(x, wab_ref[1])).astype(o_ref.dtype)


def _ple_kernel(l_ref, h_ref, p_ref, wpi_ref, wpg_ref, of_ref, ob_ref):
    wpi = wpi_ref[...]
    wpg = wpg_ref[...]
    for r in range(h_ref.shape[0] // LN_SUB_ROWS):
        rows = slice(r * LN_SUB_ROWS, (r + 1) * LN_SUB_ROWS)
        h = h_ref[rows, :]
        p_in = _dot(p_ref[rows, :].astype(BF16), wpi)
        out = h + p_in * jax.nn.sigmoid(_dot(h.astype(BF16), wpg))
        of_ref[rows, :] = out
        ob_ref[rows, :] = out.astype(ob_ref.dtype)


def _rope_tables(n, dim):
    inv = 1.0 / (ROPE_THETA ** (jnp.arange(0, dim, 2, dtype=F32) / dim))
    ang = jnp.arange(n, dtype=F32)[:, None] * inv[None, :]
    return jnp.cos(ang), jnp.sin(ang)


def _layer(lidx, h, hb, consts, weights, bsz, seq):
    (cos128, sin128, cos64, sin64a, sin64b, cmp_cos, cmp_sin, overlap, block_of_key) = consts
    (w_proj, fox_bias, pe_k, pe_v, wk1, wk2, wv1, wv2, kv_norm, kv_up,
     w_br_fox, w_br_nsa, w_br_dsa, w_out, ln1_g, ln1_b, w_ffn_in, w_ffn_out, ln2_g, ln2_b, p, w_ple_in, w_ple_gate) = weights
    t = bsz * seq
    nq = seq // TQ
    d = D_MODEL

    z_a = _proj("proj_plain", lidx, hb, w_proj, COL_PLAIN, N_PLAIN, 2048, 512, BF16, seq)
    z_b = _proj("proj_rope128", lidx, hb, w_proj, COL_ROPE128, N_ROPE128, 2048, 512, BF16, seq,
                rope=((HEAD_DIM // 2,), cos128, sin128))
    z_c = _proj("proj_rope64", lidx, hb, w_proj, COL_ROPE64, N_ROPE64, 2048, 256, BF16, seq,
                rope=((IDX_DIM // 2, LANE - IDX_DIM // 2), cos64, sin64a, sin64b))
    z_d = _proj("proj_f32", lidx, hb, w_proj, COL_F32, N_F32, 1024, N_F32, F32, seq)

    cum = _call(_cum_kernel, "fox_cum", lidx, [z_d, fox_bias], grid=(bsz,),
                in_specs=[pl.BlockSpec((seq, LANE), lambda b, l: (b, ZD_FOX_F)),
                          pl.BlockSpec((None, 1, LANE), lambda b, l: (l[0], 0, 0))],
                out_specs=pl.BlockSpec((seq, LANE), lambda b, l: (b, 0)),
                out_shape=jax.ShapeDtypeStruct((t, LANE), F32))
    cum_rows = jnp.transpose(cum.reshape(bsz, seq, LANE)[:, :, :8], (0, 2, 1)).reshape(bsz, 8 * (seq // TK), TK)
    fq = FOX_HEADS * HEAD_DIM
    o_fox = _call(_fox_kernel, "fox_attn", lidx, [z_a, z_a, z_a, cum, cum_rows], grid=(bsz, nq),
                  in_specs=[pl.BlockSpec((TQ, fq), lambda b, i, l: (b * nq + i, 0)),
                            pl.BlockSpec((seq, fq), lambda b, i, l: (b, 1)),
                            pl.BlockSpec((seq, fq), lambda b, i, l: (b, 2)),
                            pl.BlockSpec((TQ, LANE), lambda b, i, l: (b * nq + i, 0)),
                            pl.BlockSpec((None, 8 * (seq // TK), TK), lambda b, i, l: (b, 0, 0))],
                  out_specs=pl.BlockSpec((TQ, fq), lambda b, i, l: (b * nq + i, 0)),
                  out_shape=jax.ShapeDtypeStruct((t, fq), BF16),
                  scratch_shapes=[pltpu.VMEM((FOX_HEADS, seq // TK, TQ, TK), F32),
                                  pltpu.VMEM((FOX_HEADS, TQ, LANE), F32), pltpu.VMEM((FOX_HEADS, TQ, LANE), F32),
                                  pltpu.VMEM((FOX_HEADS, TQ, 2 * HEAD_DIM), F32)])

    nchunk = seq // CMP_STRIDE
    half = CMP_STRIDE * HEAD_DIM
    r_k = z_d[:, ZD_NSA_KC * LANE:(ZD_NSA_KC + 1) * LANE].reshape(bsz * nchunk, half)
    r_v = z_d[:, ZD_NSA_VC * LANE:(ZD_NSA_VC + 1) * LANE].reshape(bsz * nchunk, half)
    k_cmp, v_cmp = _call(
        _cmp_kernel, "nsa_compress", lidx, [r_k, r_v, pe_k, pe_v, wk1, wk2, wv1, wv2, cmp_cos, cmp_sin], grid=(bsz,),
        in_specs=[pl.BlockSpec((nchunk, half), lambda b, l: (b, 0)),
                  pl.BlockSpec((nchunk, half), lambda b, l: (b, 0)),
                  pl.BlockSpec((None, 2, half), lambda b, l: (l[0], 0, 0)),
                  pl.BlockSpec((None, 2, half), lambda b, l: (l[0], 0, 0)),
                  pl.BlockSpec((None, 2 * half, CMP_HIDDEN), lambda b, l: (l[0], 0, 0)),
                  pl.BlockSpec((None, CMP_HIDDEN, HEAD_DIM), lambda b, l: (l[0], 0, 0)),
                  pl.BlockSpec((None, 2 * half, CMP_HIDDEN), lambda b, l: (l[0], 0, 0)),
                  pl.BlockSpec((None, CMP_HIDDEN, HEAD_DIM), lambda b, l: (l[0], 0, 0)),
                  pl.BlockSpec((nchunk, HEAD_DIM), lambda b, l: (0, 0)),
                  pl.BlockSpec((nchunk, HEAD_DIM), lambda b, l: (0, 0))],
        out_specs=[pl.BlockSpec((nchunk, HEAD_DIM), lambda b, l: (b, 0)),
                   pl.BlockSpec((nchunk, HEAD_DIM), lambda b, l: (b, 0))],
        out_shape=[jax.ShapeDtypeStruct((bsz * nchunk, HEAD_DIM), BF16)] * 2)
    nsq = NSA_HEADS * HEAD_DIM
    o_nsa = _call(
        _nsa_kernel, "nsa_attn", lidx, [z_b, z_b, z_b, z_a, z_a, k_cmp, v_cmp, z_d, overlap, block_of_key], grid=(bsz, nq),
        in_specs=[pl.BlockSpec((TQ, nsq), lambda b, i, l: (b * nq + i, 0)),
                  pl.BlockSpec((seq, LANE), lambda b, i, l: (b, 4)),
                  pl.BlockSpec((seq, LANE), lambda b, i, l: (b, 5)),
                  pl.BlockSpec((seq, LANE), lambda b, i, l: (b, 18)),
                  pl.BlockSpec((seq, LANE), lambda b, i, l: (b, 19)),
                  pl.BlockSpec((nchunk, HEAD_DIM), lambda b, i, l: (b, 0)),
                  pl.BlockSpec((nchunk, HEAD_DIM), lambda b, i, l: (b, 0)),
                  pl.BlockSpec((TQ, LANE), lambda b, i, l: (b * nq + i, ZD_NSA_G)),
                  pl.BlockSpec((N_CMP_PAD, LANE), lambda b, i, l: (0, 0)),
                  pl.BlockSpec((seq, LANE), lambda b, i, l: (0, 0))],
        out_specs=pl.BlockSpec((TQ, nsq), lambda b, i, l: (b * nq + i, 0)),
        out_shape=jax.ShapeDtypeStruct((t, nsq), BF16),
        scratch_shapes=[pltpu.VMEM((NSA_HEADS * TQ, HEAD_DIM), BF16),
                        pltpu.VMEM((NSA_HEADS * TQ, HEAD_DIM), F32),
                        pltpu.VMEM((seq // TK, NSA_HEADS * TQ, TK), F32),
                        pltpu.VMEM((NSA_HEADS * TQ, LANE), F32),
                        pltpu.VMEM((NSA_HEADS * TQ, 2 * HEAD_DIM), F32)])

    tm_kv = 1024
    k_d, v_d = _call(
        _dsa_kv_kernel, "dsa_kv", lidx, [z_d, kv_norm, kv_up, cos128, sin128], grid=(t // tm_kv,),
        in_specs=[pl.BlockSpec((tm_kv, DSA_KV_RANK), lambda i, l: (i, 0)),
                  pl.BlockSpec((None, 1, DSA_KV_RANK), lambda i, l: (l[0], 0, 0)),
                  pl.BlockSpec((None, DSA_KV_RANK, 2 * HEAD_DIM), lambda i, l: (l[0], 0, 0)),
                  pl.BlockSpec((tm_kv, LANE), lambda i, l: (i % (seq // tm_kv), 0)),
                  pl.BlockSpec((tm_kv, LANE), lambda i, l: (i % (seq // tm_kv), 0))],
        out_specs=[pl.BlockSpec((tm_kv, HEAD_DIM), lambda i, l: (i, 0)),
                   pl.BlockSpec((tm_kv, HEAD_DIM), lambda i, l: (i, 0))],
        out_shape=[jax.ShapeDtypeStruct((t, HEAD_DIM), BF16)] * 2)
    dq = DSA_HEADS * HEAD_DIM
    iqw = IDX_HEADS * IDX_DIM
    o_dsa = _call(
        _dsa_kernel, "dsa_attn", lidx, [z_b, k_d, v_d, z_c, z_c, z_c, z_d], grid=(bsz, nq),
        in_specs=[pl.BlockSpec((TQ, dq), lambda b, i, l: (b * nq + i, 1)),
                  pl.BlockSpec((seq, HEAD_DIM), lambda b, i, l: (b, 0)),
                  pl.BlockSpec((seq, HEAD_DIM), lambda b, i, l: (b, 0)),
                  pl.BlockSpec((TQ, iqw), lambda b, i, l: (b * nq + i, 0)),
                  pl.BlockSpec((seq, LANE), lambda b, i, l: (b, iqw // LANE)),
                  pl.BlockSpec((seq, LANE), lambda b, i, l: (b, iqw // LANE + 1)),
                  pl.BlockSpec((TQ, LANE), lambda b, i, l: (b * nq + i, ZD_IDX_W))],
        out_specs=pl.BlockSpec((TQ, dq), lambda b, i, l: (b * nq + i, 0)),
        out_shape=jax.ShapeDtypeStruct((t, dq), BF16),
        scratch_shapes=[pltpu.VMEM((seq // TK, TQ, TK), F32), pltpu.VMEM((seq // TK, TK, TQ), F32),
                        pltpu.VMEM((seq // TK, TK, TQ), BF16), pltpu.VMEM((LANE, TQ), F32), pltpu.VMEM((TQ, LANE), F32), pltpu.VMEM((TQ, LANE), I32),
                        pltpu.VMEM((DSA_HEADS * TQ, HEAD_DIM), BF16),
                        pltpu.VMEM((seq // TK, DSA_HEADS * TQ, TK), F32),
                        pltpu.VMEM((DSA_HEADS * TQ, LANE), F32),
                        pltpu.VMEM((DSA_HEADS * TQ, 2 * HEAD_DIM), F32)])

    tm, tn = 1024, 512
    ncol = d // tn
    gate0 = COL_GATE // tn
    assert gate0 * tn == COL_GATE
    mixed = _call(
        _merge_kernel, "merge", lidx, [hb, o_fox, o_nsa, o_dsa, w_proj, w_proj, w_proj, w_br_fox, w_br_nsa, w_br_dsa],
        grid=(t // tm, ncol),
        in_specs=[pl.BlockSpec((tm, d), lambda i, j, l: (i, 0)),
                  pl.BlockSpec((tm, fq), lambda i, j, l: (i, 0)),
                  pl.BlockSpec((tm, nsq), lambda i, j, l: (i, 0)),
                  pl.BlockSpec((tm, dq), lambda i, j, l: (i, 0)),
                  pl.BlockSpec((None, d, tn), lambda i, j, l: (l[0], 0, gate0 + j)),
                  pl.BlockSpec((None, d, tn), lambda i, j, l: (l[0], 0, gate0 + ncol + j)),
                  pl.BlockSpec((None, d, tn), lambda i, j, l: (l[0], 0, gate0 + 2 * ncol + j)),
                  pl.BlockSpec((None, fq, tn), lambda i, j, l: (l[0], 0, j)),
                  pl.BlockSpec((None, nsq, tn), lambda i, j, l: (l[0], 0, j)),
                  pl.BlockSpec((None, dq, tn), lambda i, j, l: (l[0], 0, j))],
        out_specs=pl.BlockSpec((tm, tn), lambda i, j, l: (i, j)),
        out_shape=jax.ShapeDtypeStruct((t, d), BF16))

    h, hb = _out_ln("out_ln", lidx, mixed, w_out, h, ln1_g, ln1_b)

    nff = D_FF // tn
    act = _call(
        _swiglu_kernel, "swiglu", lidx, [hb, w_ffn_in, w_ffn_in], grid=(nff, t // tm),
        in_specs=[pl.BlockSpec((tm, d), lambda j, i, l: (i, 0)),
                  pl.BlockSpec((None, d, tn), lambda j, i, l: (l[0], 0, j)),
                  pl.BlockSpec((None, d, tn), lambda j, i, l: (l[0], 0, nff + j))],
        out_specs=pl.BlockSpec((tm, tn), lambda j, i, l: (i, j)),
        out_shape=jax.ShapeDtypeStruct((t, D_FF), BF16),
        scratch_shapes=[pltpu.VMEM((2, d, tn), BF16)])
    h = _ffn_ln("ffn_ln", lidx, act, w_ffn_out, h, ln2_g, ln2_b, tk=D_FF // 4)

    tmp = 512
    h, hb = _call(
        _ple_kernel, "ple", lidx, [h, p, w_ple_in, w_ple_gate], grid=(t // tmp,),
        in_specs=[pl.BlockSpec((tmp, d), lambda i, l: (i, 0)),
                  pl.BlockSpec((None, tmp, PLE_DIM), lambda i, l: (l[0], i, 0)),
                  pl.BlockSpec((None, PLE_DIM, d), lambda i, l: (l[0], 0, 0)),
                  pl.BlockSpec((None, d, d), lambda i, l: (l[0], 0, 0))],
        out_specs=[pl.BlockSpec((tmp, d), lambda i, l: (i, 0)),
                   pl.BlockSpec((tmp, d), lambda i, l: (i, 0))],
        out_shape=[jax.ShapeDtypeStruct((t, d), F32), jax.ShapeDtypeStruct((t, d), BF16)])
    return h, hb


def kernel(x, p, w_in, fox_f_bias, nsa_pe_k, nsa_pe_v, nsa_cmp_k1, nsa_cmp_k2, nsa_cmp_v1, nsa_cmp_v2, dsa_kv_norm, dsa_kv_up, w_br_fox, w_br_nsa, w_br_dsa, w_out, ln1_g, ln1_b, w_ffn_in, w_ffn_out, ln2_g, ln2_b, w_ple_in, w_ple_gate):
    bsz, seq, d = x.shape
    depth = w_in.shape[0]
    t = bsz * seq
    assert d == D_MODEL and seq % 1024 == 0 and depth == DEPTH

    cat = functools.partial(jnp.concatenate, axis=-1)
    w_proj = _regroup(w_in)

    fox_bias = jnp.pad(fox_f_bias, ((0, 0), (0, LANE - FOX_HEADS))).reshape(depth, 1, LANE)
    half = CMP_STRIDE * HEAD_DIM
    weights = (
        w_proj, fox_bias,
        nsa_pe_k.reshape(depth, 2, half), nsa_pe_v.reshape(depth, 2, half),
        nsa_cmp_k1.astype(BF16), nsa_cmp_k2.astype(BF16), nsa_cmp_v1.astype(BF16), nsa_cmp_v2.astype(BF16),
        dsa_kv_norm.reshape(depth, 1, DSA_KV_RANK), dsa_kv_up.astype(BF16),
        w_br_fox.astype(BF16), w_br_nsa.astype(BF16), w_br_dsa.astype(BF16), w_out.astype(BF16),
        ln1_g.reshape(depth, 1, d), ln1_b.reshape(depth, 1, d),
        w_ffn_in, w_ffn_out.astype(BF16),
        ln2_g.reshape(depth, 1, d), ln2_b.reshape(depth, 1, d),
        p.reshape(depth, t, PLE_DIM), w_ple_in.astype(BF16), w_ple_gate.astype(BF16),
    )

    cos, sin = _rope_tables(seq, HEAD_DIM)
    cos128 = cat([cos, cos])
    sin128 = cat([-sin, sin])
    cos_i, sin_i = _rope_tables(seq, IDX_DIM)
    zi = jnp.zeros_like(sin_i)
    cos64 = cat([cos_i, cos_i, cos_i, cos_i])
    sin64a = cat([zi, sin_i, zi, sin_i])
    sin64b = cat([-sin_i, zi, -sin_i, zi])
    n_cmp = (seq - CMP_LEN) // CMP_STRIDE + 1
    c_end = jnp.minimum(jnp.arange(N_CMP_PAD) * CMP_STRIDE + CMP_LEN - 1, seq - 1)
    cmp_cos, cmp_sin = cos128[c_end], sin128[c_end]
    n_slc = seq // SLC_LEN
    c_start = jnp.arange(N_CMP_PAD) * CMP_STRIDE
    s_start = jnp.arange(LANE) * SLC_LEN
    overlap = jnp.maximum(jnp.minimum(c_start[:, None] + CMP_LEN - 1, s_start[None, :] + SLC_LEN - 1)
                          - jnp.maximum(c_start[:, None], s_start[None, :]) + 1, 0).astype(F32) / CMP_LEN
    overlap = jnp.where((jnp.arange(N_CMP_PAD)[:, None] < n_cmp) & (jnp.arange(LANE)[None, :] < n_slc), overlap, 0.0)
    overlap = overlap.T.astype(BF16)
    block_of_key = (jnp.arange(seq)[:, None] // SLC_LEN == jnp.arange(LANE)[None, :]).astype(BF16)
    consts = (cos128, sin128, cos64, sin64a, sin64b, cmp_cos, cmp_sin, overlap, block_of_key)

    h = x.reshape(t, d)
    hb = h.astype(BF16)
    for layer in range(depth):
        lidx = jnp.full((1,), layer, I32)
        h, hb = _layer(lidx, h, hb, consts, weights, bsz, seq)
    return h.reshape(bsz, seq, d)
```

```python
import functools

import jax
import jax.numpy as jnp
from jax import lax
from jax.experimental import pallas as pl
from jax.experimental.pallas import tpu as pltpu

F32, BF16, I32, I16 = jnp.float32, jnp.bfloat16, jnp.int32, jnp.int16

D_MODEL = 2048
DEPTH = 4
HEAD_DIM = 128
ROPE_THETA = 10000.0
NEG_INF = -1e30
FOX_HEADS = 6
NSA_HEADS = 4
CMP_LEN = 32
CMP_STRIDE = 16
CMP_HIDDEN = 256
SLC_LEN = 64
SLC_TOPN = 16
WIN = 512
SLC_FORCE = 1e4
DSA_HEADS = 6
DSA_KV_RANK = 256
IDX_HEADS = 16
IDX_DIM = 64
IDX_TOPK_MAX = 256
D_FF = ((8 * D_MODEL + 3 * 256 - 1) // (3 * 256)) * 256
PLE_DIM = 256
ALPHA = (2 * DEPTH) ** 0.25
SCALE = HEAD_DIM ** -0.5
LOG2E = 1.4426950408889634
SCALE_LOG2E = SCALE * LOG2E

LANE = 128
VMEM_LIMIT_BYTES = 48 * 1024 * 1024
VMEM_LIMIT_LARGE_BYTES = 58 * 1024 * 1024

_IN_SPLITS = (
    ('fox_q', FOX_HEADS * HEAD_DIM), ('fox_k', FOX_HEADS * HEAD_DIM), ('fox_v', FOX_HEADS * HEAD_DIM), ('fox_f', FOX_HEADS),
    ('nsa_q', NSA_HEADS * HEAD_DIM), ('nsa_kc', HEAD_DIM), ('nsa_vc', HEAD_DIM), ('nsa_ks', HEAD_DIM), ('nsa_vs', HEAD_DIM),
    ('nsa_kw', HEAD_DIM), ('nsa_vw', HEAD_DIM), ('nsa_g', 3 * NSA_HEADS),
    ('dsa_q', DSA_HEADS * HEAD_DIM), ('dsa_ckv', DSA_KV_RANK), ('idx_q', IDX_HEADS * IDX_DIM), ('idx_k', IDX_DIM), ('idx_w', IDX_HEADS),
    ('gate', 3 * D_MODEL),
)
_IN_OFFSETS = {}
_off = 0
for _name, _width in _IN_SPLITS:
    _IN_OFFSETS[_name] = (_off, _off + _width)
    _off += _width

MISC_FOX_F = 0
MISC_NSA_G = FOX_HEADS
MISC_IDX_W = FOX_HEADS + 3 * NSA_HEADS

TQ = 256
TK = 256
N_CMP_PAD = 128


def _dot(a, b):
    return jnp.dot(a, b, preferred_element_type=F32)


def _dot_nt(a, b):
    return lax.dot_general(a, b, (((1,), (1,)), ((), ())), preferred_element_type=F32)


def _fori_pairs(n, body, init):
    def four(jj, carry):
        for u in range(4):
            carry = body(4 * jj + u, carry)
        return carry
    carry = lax.fori_loop(0, lax.shift_right_logical(n, jnp.int32(2)), four, init)
    base = n & jnp.int32(-4)
    carry = lax.cond((n & 2) == 2, lambda c: body(base + 1, body(base, c)), lambda c: c, carry)
    return lax.cond((n & 1) == 1, lambda c: body(n - 1, c), lambda c: c, carry)


def _split3(a):
    a1 = a.astype(BF16)
    r1 = a - a1.astype(F32)
    a2 = r1.astype(BF16)
    a3 = (r1 - a2.astype(F32)).astype(BF16)
    return a1, a2, a3


def _call(kernel, name, lidx, args, grid, in_specs, out_specs, out_shape, scratch_shapes=(),
          vmem_limit_bytes=VMEM_LIMIT_BYTES):
    return pl.pallas_call(
        kernel,
        grid_spec=pltpu.PrefetchScalarGridSpec(
            num_scalar_prefetch=1, grid=grid, in_specs=in_specs, out_specs=out_specs,
            scratch_shapes=list(scratch_shapes)),
        out_shape=out_shape,
        compiler_params=pltpu.CompilerParams(
            dimension_semantics=("arbitrary",) * len(grid), vmem_limit_bytes=vmem_limit_bytes),
        name=name,
    )(lidx, *args)


def _regroup_plan():
    plan = []

    def piece(src0, lo, hi, new_group=True):
        group = (plan[-1][3] + 1 if new_group else plan[-1][3]) if plan else 0
        plan.append((src0, lo, hi, group, int(new_group)))

    def whole(name):
        a, b = _IN_OFFSETS[name]
        for g in range((b - a) // LANE):
            piece(a + g * LANE, 0, LANE)

    for name in ('fox_q', 'fox_k', 'fox_v', 'nsa_vs', 'nsa_vw'):
        whole(name)
    for name in ('nsa_q', 'nsa_ks', 'nsa_kw', 'dsa_q'):
        whole(name)
    whole('gate')
    whole('idx_q')
    ik = _IN_OFFSETS['idx_k'][0]
    piece(ik, 0, IDX_DIM)
    piece(ik - IDX_DIM, IDX_DIM, LANE)
    whole('dsa_ckv')
    start = 0
    for name in ('fox_f', 'nsa_g', 'idx_w'):
        a, b = _IN_OFFSETS[name]
        piece(a - start, start, start + b - a, new_group=(start == 0))
        start += b - a
    whole('nsa_kc')
    whole('nsa_vc')
    return tuple(plan)


_REGROUP_PLAN = _regroup_plan()
COL_PLAIN = 0
N_PLAIN = (3 * FOX_HEADS + 2) * HEAD_DIM
COL_ROPE128 = COL_PLAIN + N_PLAIN
N_ROPE128 = (NSA_HEADS + 2 + DSA_HEADS) * HEAD_DIM
COL_GATE = COL_ROPE128 + N_ROPE128
COL_ROPE64 = COL_GATE + 3 * D_MODEL
N_ROPE64 = IDX_HEADS * IDX_DIM + 2 * LANE
COL_F32 = COL_ROPE64 + N_ROPE64
N_F32 = DSA_KV_RANK + 3 * LANE
ZD_FOX_F = ZD_NSA_G = ZD_IDX_W = 2
ZD_NSA_KC, ZD_NSA_VC = 3, 4
N_REGROUPED = COL_F32 + N_F32
assert N_REGROUPED == (_REGROUP_PLAN[-1][3] + 1) * LANE


def _regroup_kernel(src_ref, lo_ref, hi_ref, dst_ref, first_ref, x_ref, o_ref):
    p = pl.program_id(0)
    col = lax.broadcasted_iota(I32, (LANE, x_ref.shape[2]), 0)
    keep = (col >= lo_ref[p]) & (col < hi_ref[p])

    def piece(layer):
        return jnp.where(keep, x_ref[:, layer, :], 0.0).T.astype(o_ref.dtype)

    @pl.when(first_ref[p] == 1)
    def _():
        for layer in range(x_ref.shape[1]):
            o_ref[layer] = piece(layer)

    @pl.when(first_ref[p] == 0)
    def _():
        for layer in range(x_ref.shape[1]):
            o_ref[layer] += piece(layer)


def _regroup(w_in):
    depth, d, n_in = w_in.shape
    w_t = jnp.transpose(w_in, (2, 0, 1))
    src, lo, hi, dst, first = (jnp.array([p[c] for p in _REGROUP_PLAN], I32) for c in range(5))
    return pl.pallas_call(
        _regroup_kernel,
        grid_spec=pltpu.PrefetchScalarGridSpec(
            num_scalar_prefetch=5, grid=(len(_REGROUP_PLAN),),
            in_specs=[pl.BlockSpec((pl.Element(LANE), pl.Element(depth), pl.Element(d)),
                                   lambda g, src, lo, hi, dst, first: (src[g], 0, 0))],
            out_specs=pl.BlockSpec((depth, d, LANE), lambda g, src, lo, hi, dst, first: (0, 0, dst[g]))),
        out_shape=jax.ShapeDtypeStruct((depth, d, N_REGROUPED), BF16),
        compiler_params=pltpu.CompilerParams(dimension_semantics=("arbitrary",), vmem_limit_bytes=VMEM_LIMIT_BYTES),
        name="regroup_w_in",
    )(src, lo, hi, dst, first, w_t)


def _proj_kernel(l_ref, x_ref, w_ref, *refs, shifts):
    o_ref = refs[-1]
    acc = _dot(x_ref[...], w_ref[...])
    if not shifts:
        o_ref[...] = acc.astype(o_ref.dtype)
        return
    c_ref = refs[0]
    s_refs = refs[1:-1]
    for g in range(acc.shape[1] // LANE):
        xg = acc[:, g * LANE:(g + 1) * LANE]
        out = xg * c_ref[...]
        for shift, s_ref in zip(shifts, s_refs):
            out = out + pltpu.roll(xg, shift, 1) * s_ref[...]
        o_ref[:, g * LANE:(g + 1) * LANE] = out.astype(o_ref.dtype)


def _proj(name, lidx, hb, w, col0, n, tm, tn, out_dtype, seq, rope=None):
    t, d = hb.shape
    jb = col0 // tn
    assert jb * tn == col0
    in_specs = [pl.BlockSpec((tm, d), lambda i, j, l: (i, 0)),
                pl.BlockSpec((None, d, tn), lambda i, j, l: (l[0], 0, jb + j))]
    args = [hb, w]
    shifts = ()
    if rope is not None:
        shifts, tables = rope[0], rope[1:]
        nrow = seq // tm
        for tab in tables:
            in_specs.append(pl.BlockSpec((tm, LANE), lambda i, j, l: (i % nrow, 0)))
            args.append(tab)
    return _call(functools.partial(_proj_kernel, shifts=shifts), name, lidx, args,
                 grid=(t // tm, n // tn), in_specs=in_specs,
                 out_specs=pl.BlockSpec((tm, tn), lambda i, j, l: (i, j)),
                 out_shape=jax.ShapeDtypeStruct((t, n), out_dtype))


def _cum_kernel(l_ref, f_ref, bias_ref, o_ref):
    chunk = 256
    row = lax.broadcasted_iota(I32, (chunk, chunk), 0)
    col = lax.broadcasted_iota(I32, (chunk, chunk), 1)
    tri = jnp.where(row >= col, 1.0, 0.0).astype(BF16)
    carry = jnp.zeros((1, LANE), F32)
    for c in range(f_ref.shape[0] // chunk):
        logf = jax.nn.log_sigmoid(f_ref[c * chunk:(c + 1) * chunk, :] + bias_ref[...])
        l1, l2, l3 = _split3(logf)
        cs = _dot(tri, l1) + _dot(tri, l2) + _dot(tri, l3) + carry
        o_ref[c * chunk:(c + 1) * chunk, :] = cs
        carry = cs[chunk - 1:chunk, :]


def _max_lanes(x):
    out = x[:, 0:LANE]
    for g in range(1, x.shape[1] // LANE):
        out = jnp.maximum(out, x[:, g * LANE:(g + 1) * LANE])
    return out


def _tile_lanes(x, n):
    return jnp.concatenate([x] * n, axis=1)


def _row_max_to_lanes(mx):
    return jnp.broadcast_to(jnp.max(mx, axis=-1, keepdims=True), mx.shape)


def _exp_pv(s, m_lanes, v):
    p = jnp.exp2(s - _tile_lanes(m_lanes, s.shape[1] // LANE)).astype(BF16)
    return _dot(p, jnp.concatenate([v, jnp.ones_like(v)], axis=1))


def _fox_kernel(l_ref, q_ref, k_ref, v_ref, cq_ref, ck_ref, o_ref, s_ref, mx_ref, cqb_ref, acc_ref):
    qi = pl.program_id(1)
    nh = FOX_HEADS
    nchunk = ck_ref.shape[0] // 8
    row = lax.broadcasted_iota(I32, (TQ, TK), 0)
    col = lax.broadcasted_iota(I32, (TQ, TK), 1)
    cq_all = cq_ref[...]
    for h in range(nh):
        cqb_ref[h] = jnp.broadcast_to(cq_all[:, h:h + 1] * LOG2E, (TQ, LANE))
    mx_ref[...] = jnp.full(mx_ref.shape, NEG_INF, F32)
    acc_ref[...] = jnp.zeros(acc_ref.shape, F32)

    def score_step(j, diagonal):
        start = pl.multiple_of(j * TK, TK)
        for h in range(nh):
            hs = slice(h * HEAD_DIM, (h + 1) * HEAD_DIM)
            s = (_dot_nt(q_ref[:, hs], k_ref[pl.ds(start, TK), hs]) * SCALE_LOG2E
                 + _tile_lanes(cqb_ref[h], TK // LANE) - ck_ref[pl.ds(h * nchunk + j, 1), :] * LOG2E)
            if diagonal:
                s = jnp.where(col <= row, s, NEG_INF)
            s_ref[h, j] = s
            mx_ref[h] = jnp.maximum(mx_ref[h], _max_lanes(s))

    def score_body(j, carry):
        score_step(j, False)
        return carry

    _fori_pairs(qi, score_body, 0)
    score_step(qi, True)
    for h in range(nh):
        mx_ref[h] = _row_max_to_lanes(mx_ref[h])

    def pv_body(j, carry):
        start = pl.multiple_of(j * TK, TK)
        for h in range(nh):
            hs = slice(h * HEAD_DIM, (h + 1) * HEAD_DIM)
            acc_ref[h] += _exp_pv(s_ref[h, j], mx_ref[h], v_ref[pl.ds(start, TK), hs])
        return carry

    _fori_pairs(qi + 1, pv_body, 0)
    for h in range(nh):
        a = acc_ref[h]
        o_ref[:, h * HEAD_DIM:(h + 1) * HEAD_DIM] = (a[:, :HEAD_DIM] / a[:, HEAD_DIM:]).astype(o_ref.dtype)


def _cmp_kernel(l_ref, rk_ref, rv_ref, pek_ref, pev_ref, wk1_ref, wk2_ref, wv1_ref, wv2_ref,
                c_ref, s_ref, ko_ref, vo_ref):
    half = CMP_STRIDE * HEAD_DIM

    def compress(r_ref, pe_ref, w1_ref, w2_ref):
        r = r_ref[...]
        lo = (r + pe_ref[0:1, :]).astype(BF16)
        hi = (r + pe_ref[1:2, :]).astype(BF16)
        a = _dot(lo, w1_ref[0:half, :])
        b = _dot(hi, w1_ref[half:2 * half, :])
        hid = a + pltpu.roll(b, N_CMP_PAD - 1, 0)
        return _dot(jax.nn.gelu(hid).astype(BF16), w2_ref[...])

    kc = compress(rk_ref, pek_ref, wk1_ref, wk2_ref)
    kc = kc * c_ref[...] + pltpu.roll(kc, HEAD_DIM // 2, 1) * s_ref[...]
    ko_ref[...] = kc.astype(ko_ref.dtype)
    vo_ref[...] = compress(rv_ref, pev_ref, wv1_ref, wv2_ref).astype(vo_ref.dtype)


def _masked_softmax(s, mask):
    s = jnp.where(mask, s, NEG_INF)
    m = jnp.max(s, axis=-1, keepdims=True)
    e = jnp.where(mask, jnp.exp(s - m), 0.0)
    den = jnp.sum(e, axis=-1, keepdims=True)
    return e, jnp.where(den > 0.0, den, 1.0)


def _nsa_kernel(l_ref, q_ref, ks_ref, kw_ref, vs_ref, vw_ref, kc_ref, vc_ref, g_ref, ovt_ref, et_ref,
                o_ref, q_sc, ow_ref, s_ref, mx_ref, acc_ref):
    qi = pl.program_id(1)
    nh = NSA_HEADS
    for h in range(nh):
        q_sc[h * TQ:(h + 1) * TQ, :] = q_ref[:, h * HEAD_DIM:(h + 1) * HEAD_DIM]
    q = q_sc[...]
    pos = qi * TQ + lax.broadcasted_iota(I32, (TQ, 1), 0)
    lane = lax.broadcasted_iota(I32, (TQ, LANE), 1)

    band = WIN + TQ
    start_w = pl.multiple_of(jnp.maximum(qi * TQ - WIN, 0), TQ)
    kwb = kw_ref[pl.ds(start_w, band), :]
    vwb = vw_ref[pl.ds(start_w, band), :]
    dist = pos - (start_w + lax.broadcasted_iota(I32, (TQ, band), 1))
    wvis = (dist >= 0) & (dist < WIN)
    for h in range(nh):
        s_w = _dot_nt(q_ref[:, h * HEAD_DIM:(h + 1) * HEAD_DIM], kwb) * SCALE_LOG2E
        s_w = jnp.where(wvis, s_w, NEG_INF)
        a_win = _exp_pv(s_w, _row_max_to_lanes(_max_lanes(s_w)), vwb)
        ow_ref[h * TQ:(h + 1) * TQ, :] = a_win[:, :HEAD_DIM] / a_win[:, HEAD_DIM:]

    cvis = (lane * CMP_STRIDE + (CMP_LEN - 1)) <= pos
    s_c = (_dot_nt(q, kc_ref[...]) * SCALE).reshape(nh, TQ, N_CMP_PAD)
    e_c, den_c = _masked_softmax(s_c, cvis[None])
    p_c = e_c / den_c
    o_cmp = _dot(p_c.reshape(nh * TQ, N_CMP_PAD).astype(BF16), vc_ref[...])

    p_sum = p_c[0]
    for h in range(1, nh):
        p_sum = p_sum + p_c[h]
    p1, p2, p3 = _split3(p_sum)
    ovt = ovt_ref[...]
    n_slc = et_ref.shape[0] // SLC_LEN
    imp = (_dot_nt(ovt, p1) + _dot_nt(ovt, p2) + _dot_nt(ovt, p3))[0:n_slc, :]
    blk = lax.broadcasted_iota(I32, (n_slc, TQ), 0)
    pos_t = qi * TQ + lax.broadcasted_iota(I32, (n_slc, TQ), 1)
    cur = lax.shift_right_logical(pos_t, SLC_LEN.bit_length() - 1)
    forced = (blk == 0) | (blk == cur) | (blk == cur - 1)
    imp = jnp.where(forced, SLC_FORCE, jnp.where(blk * SLC_LEN <= pos_t, imp, -SLC_FORCE))
    rank = jnp.zeros((n_slc, TQ), F32)
    for k in range(n_slc):
        ik = imp[k:k + 1, :]
        ahead = (ik > imp) | ((ik == imp) & (blk > k))
        rank = rank + jnp.where(ahead, 1.0, 0.0)
    sel_t = jnp.where(rank < float(min(SLC_TOPN, n_slc)), 1.0, 0.0)
    sel_t = jnp.concatenate([sel_t, jnp.zeros((LANE - n_slc, TQ), F32)], axis=0)
    sel = sel_t.T.astype(BF16)

    mx_ref[...] = jnp.full(mx_ref.shape, NEG_INF, F32)
    acc_ref[...] = jnp.zeros(acc_ref.shape, F32)
    row = lax.broadcasted_iota(I32, (TQ, TK), 0)
    col = lax.broadcasted_iota(I32, (TQ, TK), 1)

    def score_step(j, diagonal):
        start = pl.multiple_of(j * TK, TK)
        vis = _dot_nt(sel, et_ref[pl.ds(start, TK), :]) > 0.5
        if diagonal:
            vis = vis & (col <= row)
        s = (_dot_nt(q_sc[...], ks_ref[pl.ds(start, TK), :]) * SCALE_LOG2E).reshape(nh, TQ, TK)
        s = jnp.where(vis[None], s, NEG_INF).reshape(nh * TQ, TK)
        s_ref[j] = s
        mx_ref[...] = jnp.maximum(mx_ref[...], _max_lanes(s))

    def score_body(j, carry):
        score_step(j, False)
        return carry

    _fori_pairs(qi, score_body, 0)
    score_step(qi, True)
    mx_ref[...] = _row_max_to_lanes(mx_ref[...])

    def pv_body(j, carry):
        start = pl.multiple_of(j * TK, TK)
        acc_ref[...] += _exp_pv(s_ref[j], mx_ref[...], vs_ref[pl.ds(start, TK), :])
        return carry

    _fori_pairs(qi + 1, pv_body, 0)
    a_slc = acc_ref[...]
    o_slc = a_slc[:, :HEAD_DIM] / a_slc[:, HEAD_DIM:]

    o_win = ow_ref[...]
    gate = jax.nn.sigmoid(g_ref[...])
    for h in range(nh):
        rows = slice(h * TQ, (h + 1) * TQ)
        g0 = gate[:, MISC_NSA_G + 3 * h:MISC_NSA_G + 3 * h + 1]
        g1 = gate[:, MISC_NSA_G + 3 * h + 1:MISC_NSA_G + 3 * h + 2]
        g2 = gate[:, MISC_NSA_G + 3 * h + 2:MISC_NSA_G + 3 * h + 3]
        out = g0 * o_cmp[rows] + g1 * o_slc[rows] + g2 * o_win[rows]
        o_ref[:, h * HEAD_DIM:(h + 1) * HEAD_DIM] = out.astype(o_ref.dtype)


def _dsa_kv_kernel(l_ref, x_ref, g_ref, up_ref, c_ref, s_ref, ko_ref, vo_ref):
    x = x_ref[...]
    r = x * lax.rsqrt(jnp.mean(jnp.square(x), axis=-1, keepdims=True) + 1e-6) * g_ref[...]
    kv = _dot(r.astype(BF16), up_ref[...])
    k = kv[:, :HEAD_DIM]
    k = k * c_ref[...] + pltpu.roll(k, HEAD_DIM // 2, 1) * s_ref[...]
    ko_ref[...] = k.astype(ko_ref.dtype)
    vo_ref[...] = kv[:, HEAD_DIM:].astype(vo_ref.dtype)


def _dsa_kernel(l_ref, q_ref, kd_ref, vd_ref, iq_ref, ika_ref, ikb_ref, w_ref, o_ref,
                sc_ref, sct_ref, sb_ref, wt_ref, thr_sel, cut_sel, q_sc, s_ref, mx_ref, acc_ref):
    qi = pl.program_id(1)
    nh = DSA_HEADS
    seq = kd_ref.shape[0]
    k_top = float(min(IDX_TOPK_MAX, seq // 4))
    row = lax.broadcasted_iota(I32, (TQ, TK), 0)
    col = lax.broadcasted_iota(I32, (TQ, TK), 1)
    int_min = jnp.int32(-2 ** 31)
    idx_bits = (seq - 1).bit_length()

    wt_ref[...] = (w_ref[...] * (IDX_HEADS ** -0.5 * IDX_DIM ** -0.5)).T

    def score_step(j, diagonal):
        start = pl.multiple_of(j * TK, TK)
        ka = ika_ref[pl.ds(start, TK), :]
        kb = ikb_ref[pl.ds(start, TK), :]
        sc = jnp.zeros((TK, TQ), F32)
        for g in range(IDX_HEADS // 2):
            qpair = iq_ref[:, g * LANE:(g + 1) * LANE]
            wa = wt_ref[MISC_IDX_W + 2 * g:MISC_IDX_W + 2 * g + 1, :]
            wb = wt_ref[MISC_IDX_W + 2 * g + 1:MISC_IDX_W + 2 * g + 2, :]
            sc = sc + jnp.maximum(_dot_nt(ka, qpair), 0.0) * wa + jnp.maximum(_dot_nt(kb, qpair), 0.0) * wb
        if diagonal:
            sc = jnp.where(row <= col, sc, NEG_INF)
        sct_ref[j] = sc
        sb_ref[j] = sc.astype(BF16)
        sc_ref[j] = sc.T

    def score_body(j, carry):
        score_step(j, False)
        return carry

    _fori_pairs(qi, score_body, 0)
    score_step(qi, True)

    krow = lax.broadcasted_iota(I32, (TK, TQ), 0)

    def count(pred_fn):
        def body(j, acc):
            hit = jnp.where(pred_fn(sct_ref[j], j), 1.0, 0.0)
            return acc + jnp.sum(hit.reshape(TK // 8, 8, TQ), axis=0)
        part = _fori_pairs(qi + 1, body, jnp.zeros((8, TQ), F32))
        return jnp.sum(part, axis=0, keepdims=True)

    def key_to_score(key):
        return lax.bitcast_convert_type(jnp.where(key < 0, key ^ jnp.int32(0x7FFFFFFF), key), F32)

    def count_coarse(cand):
        def body(j, acc):
            hit = jnp.where(sb_ref[j] >= cand, jnp.int16(1), jnp.int16(0))
            for g in range(TK // 16):
                acc = acc + hit[g * 16:(g + 1) * 16, :]
            return acc
        part = _fori_pairs(qi + 1, body, jnp.zeros((16, TQ), I16))
        return jnp.sum(part.astype(F32), axis=0, keepdims=True)

    def coarse_score(hkey):
        hbits = jnp.where(hkey < 0, hkey ^ jnp.int32(0x7FFF), hkey)
        return lax.bitcast_convert_type(lax.shift_left(hbits, jnp.int32(16)), F32)

    zero_row = jnp.zeros((1, TQ), I32)
    hkey = jnp.where(count_coarse(jnp.zeros((1, TQ), BF16)) >= k_top, zero_row, zero_row - 32768)

    def coarse_bit(i, hkey):
        cand = hkey + (jnp.int32(1) << (14 - i))
        return jnp.where(count_coarse(coarse_score(cand).astype(BF16)) >= k_top, cand, hkey)

    coarse_bits = lax.bitcast_convert_type(coarse_score(lax.fori_loop(0, 15, coarse_bit, hkey)), I32)
    thr_key = jnp.where(coarse_bits < 0, coarse_bits ^ jnp.int32(0x7FFFFFFF), coarse_bits) - 65536

    def value_bit(i, thr_key):
        cand = thr_key + (jnp.int32(1) << (16 - i))
        cand_score = key_to_score(cand)
        return jnp.where(count(lambda s, j: s >= cand_score) >= k_top, cand, thr_key)

    thr = key_to_score(lax.fori_loop(0, 17, value_bit, thr_key))
    n_gt = count(lambda s, j: s > thr)
    n_ge = count(lambda s, j: s >= thr)

    def index_bisect():
        def index_bit(i, cut):
            cand = cut + (jnp.int32(1) << (idx_bits - 1 - i))
            n_before = count(lambda s, j: (s == thr) & (j * TK + krow < cand))
            return jnp.where(n_gt + n_before < k_top, cand, cut)
        return lax.fori_loop(0, idx_bits, index_bit, jnp.zeros((1, TQ), I32))

    def take_all_ties():
        return jnp.full((1, TQ), seq - 1, I32)

    cut = lax.cond(jnp.max(n_ge) > k_top, index_bisect, take_all_ties)
    thr_sel[...] = jnp.broadcast_to(thr, (LANE, TQ)).T
    cut_sel[...] = jnp.broadcast_to(cut, (LANE, TQ)).T

    for h in range(nh):
        q_sc[h * TQ:(h + 1) * TQ, :] = q_ref[:, h * HEAD_DIM:(h + 1) * HEAD_DIM]
    mx_ref[...] = jnp.full(mx_ref.shape, NEG_INF, F32)
    acc_ref[...] = jnp.zeros(acc_ref.shape, F32)

    def attn_score_step(j, diagonal):
        start = pl.multiple_of(j * TK, TK)
        sc = sc_ref[j]
        thr_t = _tile_lanes(thr_sel[...], TK // LANE)
        cut_t = _tile_lanes(cut_sel[...], TK // LANE)
        vis = (sc > thr_t) | ((sc == thr_t) & (j * TK + col <= cut_t))
        if diagonal:
            vis = vis & (col <= row)
        s = (_dot_nt(q_sc[...], kd_ref[pl.ds(start, TK), :]) * SCALE_LOG2E).reshape(nh, TQ, TK)
        s = jnp.where(vis[None], s, NEG_INF).reshape(nh * TQ, TK)
        s_ref[j] = s
        mx_ref[...] = jnp.maximum(mx_ref[...], _max_lanes(s))

    def attn_score_body(j, carry):
        attn_score_step(j, False)
        return carry

    _fori_pairs(qi, attn_score_body, 0)
    attn_score_step(qi, True)
    mx_ref[...] = _row_max_to_lanes(mx_ref[...])

    def pv_body(j, carry):
        start = pl.multiple_of(j * TK, TK)
        acc_ref[...] += _exp_pv(s_ref[j], mx_ref[...], vd_ref[pl.ds(start, TK), :])
        return carry

    _fori_pairs(qi + 1, pv_body, 0)
    a = acc_ref[...]
    out = a[:, :HEAD_DIM] / a[:, HEAD_DIM:]
    for h in range(nh):
        o_ref[:, h * HEAD_DIM:(h + 1) * HEAD_DIM] = out[h * TQ:(h + 1) * TQ].astype(o_ref.dtype)


def _merge_kernel(l_ref, hb_ref, of_ref, on_ref, od_ref, wg1_ref, wg2_ref, wg3_ref, wf_ref, wn_ref, wd_ref, o_ref):
    hb = hb_ref[...]
    mixed = (jax.nn.sigmoid(_dot(hb, wg1_ref[...])) * _dot(of_ref[...], wf_ref[...])
             + jax.nn.sigmoid(_dot(hb, wg2_ref[...])) * _dot(on_ref[...], wn_ref[...])
             + jax.nn.sigmoid(_dot(hb, wg3_ref[...])) * _dot(od_ref[...], wd_ref[...]))
    o_ref[...] = mixed.astype(o_ref.dtype)


LN_SUB_ROWS = 256


def _layer_norm_store(y, g_ref, b_ref, of_ref, ob_ref, rows):
    mu = jnp.mean(y, axis=-1, keepdims=True)
    yc = y - mu
    var = jnp.mean(jnp.square(yc), axis=-1, keepdims=True)
    out = yc * lax.rsqrt(var + 1e-5) * g_ref[...] + b_ref[...]
    of_ref[rows, :] = out
    if ob_ref is not None:
        ob_ref[rows, :] = out.astype(ob_ref.dtype)


def _out_ln_kernel(l_ref, x_ref, w_ref, h_ref, g_ref, b_ref, of_ref, ob_ref):
    w = w_ref[...]
    for r in range(x_ref.shape[0] // LN_SUB_ROWS):
        rows = slice(r * LN_SUB_ROWS, (r + 1) * LN_SUB_ROWS)
        y = ALPHA * h_ref[rows, :] + _dot(x_ref[rows, :], w)
        _layer_norm_store(y, g_ref, b_ref, of_ref, ob_ref, rows)


def _out_ln(name, lidx, x, w, h, g, b):
    t, kdim = x.shape
    d = h.shape[1]
    tm = 512
    return _call(_out_ln_kernel, name, lidx, [x, w, h, g, b], grid=(t // tm,),
                 in_specs=[pl.BlockSpec((tm, kdim), lambda i, l: (i, 0)),
                           pl.BlockSpec((None, kdim, d), lambda i, l: (l[0], 0, 0)),
                           pl.BlockSpec((tm, d), lambda i, l: (i, 0)),
                           pl.BlockSpec((None, 1, d), lambda i, l: (l[0], 0, 0)),
                           pl.BlockSpec((None, 1, d), lambda i, l: (l[0], 0, 0))],
                 out_specs=[pl.BlockSpec((tm, d), lambda i, l: (i, 0)),
                            pl.BlockSpec((tm, d), lambda i, l: (i, 0))],
                 out_shape=[jax.ShapeDtypeStruct((t, d), F32), jax.ShapeDtypeStruct((t, d), BF16)])


def _ffn_ln_kernel(l_ref, x_ref, w_ref, h_ref, g_ref, b_ref, of_ref, *, nk):
    k = pl.program_id(1)

    @pl.when(k == 0)
    def _():
        of_ref[...] = ALPHA * h_ref[...] + _dot(x_ref[...], w_ref[...])

    @pl.when((k > 0) & (k < nk - 1))
    def _():
        of_ref[...] += _dot(x_ref[...], w_ref[...])

    @pl.when(k == nk - 1)
    def _():
        w = w_ref[...]
        for r in range(x_ref.shape[0] // LN_SUB_ROWS):
            rows = slice(r * LN_SUB_ROWS, (r + 1) * LN_SUB_ROWS)
            y = of_ref[rows, :] + _dot(x_ref[rows, :], w)
            _layer_norm_store(y, g_ref, b_ref, of_ref, None, rows)


def _ffn_ln(name, lidx, x, w, h, g, b, tk):
    t, kdim = x.shape
    d = h.shape[1]
    tm = 1024
    nk = kdim // tk
    assert nk >= 2
    return _call(functools.partial(_ffn_ln_kernel, nk=nk), name, lidx, [x, w, h, g, b],
                 grid=(t // tm, nk),
                 in_specs=[pl.BlockSpec((tm, tk), lambda i, k, l: (i, k)),
                           pl.BlockSpec((None, tk, d), lambda i, k, l: (l[0], k, 0)),
                           pl.BlockSpec((tm, d), lambda i, k, l: (i, 0)),
                           pl.BlockSpec((None, 1, d), lambda i, k, l: (l[0], 0, 0)),
                           pl.BlockSpec((None, 1, d), lambda i, k, l: (l[0], 0, 0))],
                 out_specs=pl.BlockSpec((tm, d), lambda i, k, l: (i, 0)),
                 out_shape=jax.ShapeDtypeStruct((t, d), F32),
                 vmem_limit_bytes=VMEM_LIMIT_LARGE_BYTES)


def _swiglu_kernel(l_ref, x_ref, wa_ref, wb_ref, o_ref, wab_ref):
    @pl.when(pl.program_id(1) == 0)
    def _():
        wab_ref[0] = wa_ref[...].astype(BF16)
        wab_ref[1] = wb_ref[...].astype(BF16)

    x = x_ref[...]
    a = _dot(x, wab_ref[0])
    o_ref[...] = (jax.nn.silu(a) * _dot(x, wab_ref[1])).astype(o_ref.dtype)


def _ple_kernel(l_ref, h_ref, p_ref, wpi_ref, wpg_ref, of_ref, ob_ref):
    wpi = wpi_ref[...]
    wpg = wpg_ref[...]
    for r in range(h_ref.shape[0] // LN_SUB_ROWS):
        rows = slice(r * LN_SUB_ROWS, (r + 1) * LN_SUB_ROWS)
        h = h_ref[rows, :]
        p_in = _dot(p_ref[rows, :].astype(BF16), wpi)
        out = h + p_in * jax.nn.sigmoid(_dot(h.astype(BF16), wpg))
        of_ref[rows, :] = out
        ob_ref[rows, :] = out.astype(ob_ref.dtype)


def _rope_tables(n, dim):
    inv = 1.0 / (ROPE_THETA ** (jnp.arange(0, dim, 2, dtype=F32) / dim))
    ang = jnp.arange(n, dtype=F32)[:, None] * inv[None, :]
    return jnp.cos(ang), jnp.sin(ang)


def _layer(lidx, h, hb, consts, weights, bsz, seq):
    (cos128, sin128, cos64, sin64a, sin64b, cmp_cos, cmp_sin, overlap, block_of_key) = consts
    (w_proj, fox_bias, pe_k, pe_v, wk1, wk2, wv1, wv2, kv_norm, kv_up,
     w_br_fox, w_br_nsa, w_br_dsa, w_out, ln1_g, ln1_b, w_ffn_in, w_ffn_out, ln2_g, ln2_b, p, w_ple_in, w_ple_gate) = weights
    t = bsz * seq
    nq = seq // TQ
    d = D_MODEL

    z_a = _proj("proj_plain", lidx, hb, w_proj, COL_PLAIN, N_PLAIN, 2048, 512, BF16, seq)
    z_b = _proj("proj_rope128", lidx, hb, w_proj, COL_ROPE128, N_ROPE128, 2048, 512, BF16, seq,
                rope=((HEAD_DIM // 2,), cos128, sin128))
    z_c = _proj("proj_rope64", lidx, hb, w_proj, COL_ROPE64, N_ROPE64, 2048, N_ROPE64 // 2, BF16, seq,
                rope=((IDX_DIM // 2, LANE - IDX_DIM // 2), cos64, sin64a, sin64b))
    z_d = _proj("proj_f32", lidx, hb, w_proj, COL_F32, N_F32, 1024, N_F32, F32, seq)

    cum = _call(_cum_kernel, "fox_cum", lidx, [z_d, fox_bias], grid=(bsz,),
                in_specs=[pl.BlockSpec((seq, LANE), lambda b, l: (b, ZD_FOX_F)),
                          pl.BlockSpec((None, 1, LANE), lambda b, l: (l[0], 0, 0))],
                out_specs=pl.BlockSpec((seq, LANE), lambda b, l: (b, 0)),
                out_shape=jax.ShapeDtypeStruct((t, LANE), F32))
    cum_rows = jnp.transpose(cum.reshape(bsz, seq, LANE)[:, :, :8], (0, 2, 1)).reshape(bsz, 8 * (seq // TK), TK)
    fq = FOX_HEADS * HEAD_DIM
    o_fox = _call(_fox_kernel, "fox_attn", lidx, [z_a, z_a, z_a, cum, cum_rows], grid=(bsz, nq),
                  in_specs=[pl.BlockSpec((TQ, fq), lambda b, i, l: (b * nq + i, 0)),
                            pl.BlockSpec((seq, fq), lambda b, i, l: (b, 1)),
                            pl.BlockSpec((seq, fq), lambda b, i, l: (b, 2)),
                            pl.BlockSpec((TQ, LANE), lambda b, i, l: (b * nq + i, 0)),
                            pl.BlockSpec((None, 8 * (seq // TK), TK), lambda b, i, l: (b, 0, 0))],
                  out_specs=pl.BlockSpec((TQ, fq), lambda b, i, l: (b * nq + i, 0)),
                  out_shape=jax.ShapeDtypeStruct((t, fq), BF16),
                  scratch_shapes=[pltpu.VMEM((FOX_HEADS, seq // TK, TQ, TK), F32),
                                  pltpu.VMEM((FOX_HEADS, TQ, LANE), F32), pltpu.VMEM((FOX_HEADS, TQ, LANE), F32),
                                  pltpu.VMEM((FOX_HEADS, TQ, 2 * HEAD_DIM), F32)])

    nchunk = seq // CMP_STRIDE
    half = CMP_STRIDE * HEAD_DIM
    r_k = z_d[:, ZD_NSA_KC * LANE:(ZD_NSA_KC + 1) * LANE].reshape(bsz * nchunk, half)
    r_v = z_d[:, ZD_NSA_VC * LANE:(ZD_NSA_VC + 1) * LANE].reshape(bsz * nchunk, half)
    k_cmp, v_cmp = _call(
        _cmp_kernel, "nsa_compress", lidx, [r_k, r_v, pe_k, pe_v, wk1, wk2, wv1, wv2, cmp_cos, cmp_sin], grid=(bsz,),
        in_specs=[pl.BlockSpec((nchunk, half), lambda b, l: (b, 0)),
                  pl.BlockSpec((nchunk, half), lambda b, l: (b, 0)),
                  pl.BlockSpec((None, 2, half), lambda b, l: (l[0], 0, 0)),
                  pl.BlockSpec((None, 2, half), lambda b, l: (l[0], 0, 0)),
                  pl.BlockSpec((None, 2 * half, CMP_HIDDEN), lambda b, l: (l[0], 0, 0)),
                  pl.BlockSpec((None, CMP_HIDDEN, HEAD_DIM), lambda b, l: (l[0], 0, 0)),
                  pl.BlockSpec((None, 2 * half, CMP_HIDDEN), lambda b, l: (l[0], 0, 0)),
                  pl.BlockSpec((None, CMP_HIDDEN, HEAD_DIM), lambda b, l: (l[0], 0, 0)),
                  pl.BlockSpec((nchunk, HEAD_DIM), lambda b, l: (0, 0)),
                  pl.BlockSpec((nchunk, HEAD_DIM), lambda b, l: (0, 0))],
        out_specs=[pl.BlockSpec((nchunk, HEAD_DIM), lambda b, l: (b, 0)),
                   pl.BlockSpec((nchunk, HEAD_DIM), lambda b, l: (b, 0))],
        out_shape=[jax.ShapeDtypeStruct((bsz * nchunk, HEAD_DIM), BF16)] * 2)
    nsq = NSA_HEADS * HEAD_DIM
    o_nsa = _call(
        _nsa_kernel, "nsa_attn", lidx, [z_b, z_b, z_b, z_a, z_a, k_cmp, v_cmp, z_d, overlap, block_of_key], grid=(bsz, nq),
        in_specs=[pl.BlockSpec((TQ, nsq), lambda b, i, l: (b * nq + i, 0)),
                  pl.BlockSpec((seq, LANE), lambda b, i, l: (b, 4)),
                  pl.BlockSpec((seq, LANE), lambda b, i, l: (b, 5)),
                  pl.BlockSpec((seq, LANE), lambda b, i, l: (b, 18)),
                  pl.BlockSpec((seq, LANE), lambda b, i, l: (b, 19)),
                  pl.BlockSpec((nchunk, HEAD_DIM), lambda b, i, l: (b, 0)),
                  pl.BlockSpec((nchunk, HEAD_DIM), lambda b, i, l: (b, 0)),
                  pl.BlockSpec((TQ, LANE), lambda b, i, l: (b * nq + i, ZD_NSA_G)),
                  pl.BlockSpec((N_CMP_PAD, LANE), lambda b, i, l: (0, 0)),
                  pl.BlockSpec((seq, LANE), lambda b, i, l: (0, 0))],
        out_specs=pl.BlockSpec((TQ, nsq), lambda b, i, l: (b * nq + i, 0)),
        out_shape=jax.ShapeDtypeStruct((t, nsq), BF16),
        scratch_shapes=[pltpu.VMEM((NSA_HEADS * TQ, HEAD_DIM), BF16),
                        pltpu.VMEM((NSA_HEADS * TQ, HEAD_DIM), F32),
                        pltpu.VMEM((seq // TK, NSA_HEADS * TQ, TK), F32),
                        pltpu.VMEM((NSA_HEADS * TQ, LANE), F32),
                        pltpu.VMEM((NSA_HEADS * TQ, 2 * HEAD_DIM), F32)])

    tm_kv = 1024
    k_d, v_d = _call(
        _dsa_kv_kernel, "dsa_kv", lidx, [z_d, kv_norm, kv_up, cos128, sin128], grid=(t // tm_kv,),
        in_specs=[pl.BlockSpec((tm_kv, DSA_KV_RANK), lambda i, l: (i, 0)),
                  pl.BlockSpec((None, 1, DSA_KV_RANK), lambda i, l: (l[0], 0, 0)),
                  pl.BlockSpec((None, DSA_KV_RANK, 2 * HEAD_DIM), lambda i, l: (l[0], 0, 0)),
                  pl.BlockSpec((tm_kv, LANE), lambda i, l: (i % (seq // tm_kv), 0)),
                  pl.BlockSpec((tm_kv, LANE), lambda i, l: (i % (seq // tm_kv), 0))],
        out_specs=[pl.BlockSpec((tm_kv, HEAD_DIM), lambda i, l: (i, 0)),
                   pl.BlockSpec((tm_kv, HEAD_DIM), lambda i, l: (i, 0))],
        out_shape=[jax.ShapeDtypeStruct((t, HEAD_DIM), BF16)] * 2)
    dq = DSA_HEADS * HEAD_DIM
    iqw = IDX_HEADS * IDX_DIM
    o_dsa = _call(
        _dsa_kernel, "dsa_attn", lidx, [z_b, k_d, v_d, z_c, z_c, z_c, z_d], grid=(bsz, nq),
        in_specs=[pl.BlockSpec((TQ, dq), lambda b, i, l: (b * nq + i, 1)),
                  pl.BlockSpec((seq, HEAD_DIM), lambda b, i, l: (b, 0)),
                  pl.BlockSpec((seq, HEAD_DIM), lambda b, i, l: (b, 0)),
                  pl.BlockSpec((TQ, iqw), lambda b, i, l: (b * nq + i, 0)),
                  pl.BlockSpec((seq, LANE), lambda b, i, l: (b, iqw // LANE)),
                  pl.BlockSpec((seq, LANE), lambda b, i, l: (b, iqw // LANE + 1)),
                  pl.BlockSpec((TQ, LANE), lambda b, i, l: (b * nq + i, ZD_IDX_W))],
        out_specs=pl.BlockSpec((TQ, dq), lambda b, i, l: (b * nq + i, 0)),
        out_shape=jax.ShapeDtypeStruct((t, dq), BF16),
        scratch_shapes=[pltpu.VMEM((seq // TK, TQ, TK), F32), pltpu.VMEM((seq // TK, TK, TQ), F32),
                        pltpu.VMEM((seq // TK, TK, TQ), BF16), pltpu.VMEM((LANE, TQ), F32), pltpu.VMEM((TQ, LANE), F32), pltpu.VMEM((TQ, LANE), I32),
                        pltpu.VMEM((DSA_HEADS * TQ, HEAD_DIM), BF16),
                        pltpu.VMEM((seq // TK, DSA_HEADS * TQ, TK), F32),
                        pltpu.VMEM((DSA_HEADS * TQ, LANE), F32),
                        pltpu.VMEM((DSA_HEADS * TQ, 2 * HEAD_DIM), F32)])

    tm, tn = 1024, 512
    ncol = d // tn
    gate0 = COL_GATE // tn
    assert gate0 * tn == COL_GATE
    mixed = _call(
        _merge_kernel, "merge", lidx, [hb, o_fox, o_nsa, o_dsa, w_proj, w_proj, w_proj, w_br_fox, w_br_nsa, w_br_dsa],
        grid=(t // tm, ncol),
        in_specs=[pl.BlockSpec((tm, d), lambda i, j, l: (i, 0)),
                  pl.BlockSpec((tm, fq), lambda i, j, l: (i, 0)),
                  pl.BlockSpec((tm, nsq), lambda i, j, l: (i, 0)),
                  pl.BlockSpec((tm, dq), lambda i, j, l: (i, 0)),
                  pl.BlockSpec((None, d, tn), lambda i, j, l: (l[0], 0, gate0 + j)),
                  pl.BlockSpec((None, d, tn), lambda i, j, l: (l[0], 0, gate0 + ncol + j)),
                  pl.BlockSpec((None, d, tn), lambda i, j, l: (l[0], 0, gate0 + 2 * ncol + j)),
                  pl.BlockSpec((None, fq, tn), lambda i, j, l: (l[0], 0, j)),
                  pl.BlockSpec((None, nsq, tn), lambda i, j, l: (l[0], 0, j)),
                  pl.BlockSpec((None, dq, tn), lambda i, j, l: (l[0], 0, j))],
        out_specs=pl.BlockSpec((tm, tn), lambda i, j, l: (i, j)),
        out_shape=jax.ShapeDtypeStruct((t, d), BF16))

    h, hb = _out_ln("out_ln", lidx, mixed, w_out, h, ln1_g, ln1_b)

    nff = D_FF // tn
    act = _call(
        _swiglu_kernel, "swiglu", lidx, [hb, w_ffn_in, w_ffn_in], grid=(nff, t // tm),
        in_specs=[pl.BlockSpec((tm, d), lambda j, i, l: (i, 0)),
                  pl.BlockSpec((None, d, tn), lambda j, i, l: (l[0], 0, j)),
                  pl.BlockSpec((None, d, tn), lambda j, i, l: (l[0], 0, nff + j))],
        out_specs=pl.BlockSpec((tm, tn), lambda j, i, l: (i, j)),
        out_shape=jax.ShapeDtypeStruct((t, D_FF), BF16),
        scratch_shapes=[pltpu.VMEM((2, d, tn), BF16)])
    h = _ffn_ln("ffn_ln", lidx, act, w_ffn_out, h, ln2_g, ln2_b, tk=D_FF // 4)

    tmp = 512
    h, hb = _call(
        _ple_kernel, "ple", lidx, [h, p, w_ple_in, w_ple_gate], grid=(t // tmp,),
        in_specs=[pl.BlockSpec((tmp, d), lambda i, l: (i, 0)),
                  pl.BlockSpec((None, tmp, PLE_DIM), lambda i, l: (l[0], i, 0)),
                  pl.BlockSpec((None, PLE_DIM, d), lambda i, l: (l[0], 0, 0)),
                  pl.BlockSpec((None, d, d), lambda i, l: (l[0], 0, 0))],
        out_specs=[pl.BlockSpec((tmp, d), lambda i, l: (i, 0)),
                   pl.BlockSpec((tmp, d), lambda i, l: (i, 0))],
        out_shape=[jax.ShapeDtypeStruct((t, d), F32), jax.ShapeDtypeStruct((t, d), BF16)])
    return h, hb


def kernel(x, p, w_in, fox_f_bias, nsa_pe_k, nsa_pe_v, nsa_cmp_k1, nsa_cmp_k2, nsa_cmp_v1, nsa_cmp_v2, dsa_kv_norm, dsa_kv_up, w_br_fox, w_br_nsa, w_br_dsa, w_out, ln1_g, ln1_b, w_ffn_in, w_ffn_out, ln2_g, ln2_b, w_ple_in, w_ple_gate):
    bsz, seq, d = x.shape
    depth = w_in.shape[0]
    t = bsz * seq
    assert d == D_MODEL and seq % 1024 == 0 and depth == DEPTH

    cat = functools.partial(jnp.concatenate, axis=-1)
    w_proj = _regroup(w_in)

    fox_bias = jnp.pad(fox_f_bias, ((0, 0), (0, LANE - FOX_HEADS))).reshape(depth, 1, LANE)
    half = CMP_STRIDE * HEAD_DIM
    weights = (
        w_proj, fox_bias,
        nsa_pe_k.reshape(depth, 2, half), nsa_pe_v.reshape(depth, 2, half),
        nsa_cmp_k1.astype(BF16), nsa_cmp_k2.astype(BF16), nsa_cmp_v1.astype(BF16), nsa_cmp_v2.astype(BF16),
        dsa_kv_norm.reshape(depth, 1, DSA_KV_RANK), dsa_kv_up.astype(BF16),
        w_br_fox.astype(BF16), w_br_nsa.astype(BF16), w_br_dsa.astype(BF16), w_out.astype(BF16),
        ln1_g.reshape(depth, 1, d), ln1_b.reshape(depth, 1, d),
        w_ffn_in, w_ffn_out.astype(BF16),
        ln2_g.reshape(depth, 1, d), ln2_b.reshape(depth, 1, d),
        p.reshape(depth, t, PLE_DIM), w_ple_in.astype(BF16), w_ple_gate.astype(BF16),
    )

    cos, sin = _rope_tables(seq, HEAD_DIM)
    cos128 = cat([cos, cos])
    sin128 = cat([-sin, sin])
    cos_i, sin_i = _rope_tables(seq, IDX_DIM)
    zi = jnp.zeros_like(sin_i)
    cos64 = cat([cos_i, cos_i, cos_i, cos_i])
    sin64a = cat([zi, sin_i, zi, sin_i])
    sin64b = cat([-sin_i, zi, -sin_i, zi])
    n_cmp = (seq - CMP_LEN) // CMP_STRIDE + 1
    c_end = jnp.minimum(jnp.arange(N_CMP_PAD) * CMP_STRIDE + CMP_LEN - 1, seq - 1)
    cmp_cos, cmp_sin = cos128[c_end], sin128[c_end]
    n_slc = seq // SLC_LEN
    c_start = jnp.arange(N_CMP_PAD) * CMP_STRIDE
    s_start = jnp.arange(LANE) * SLC_LEN
    overlap = jnp.maximum(jnp.minimum(c_start[:, None] + CMP_LEN - 1, s_start[None, :] + SLC_LEN - 1)
                          - jnp.maximum(c_start[:, None], s_start[None, :]) + 1, 0).astype(F32) / CMP_LEN
    overlap = jnp.where((jnp.arange(N_CMP_PAD)[:, None] < n_cmp) & (jnp.arange(LANE)[None, :] < n_slc), overlap, 0.0)
    overlap = overlap.T.astype(BF16)
    block_of_key = (jnp.arange(seq)[:, None] // SLC_LEN == jnp.arange(LANE)[None, :]).astype(BF16)
    consts = (cos128, sin128, cos64, sin64a, sin64b, cmp_cos, cmp_sin, overlap, block_of_key)

    h = x.reshape(t, d)
    hb = h.astype(BF16)
    for layer in range(depth):
        lidx = jnp.full((1,), layer, I32)
        h, hb = _layer(lidx, h, hb, consts, weights, bsz, seq)
    return h.reshape(bsz, seq, d)
```
